```python
import math
import jax, jax.numpy as jnp
from jax import lax
import numpy as np

D_MODEL = 1024
BATCH = 8
SEQ = 2048
DEPTH = 2

N_MEM = 256
D_MIX = D_MODEL
D_ATTN = D_MIX // 2
D_CONV = D_MIX - D_ATTN
HEAD_DIM = 64
N_ATTN_HEADS = D_ATTN // HEAD_DIM
CONV_WIDTH = 31
Q_BLOCK = 128
N_XATTN_HEADS = 4
XATTN_HEAD_DIM = D_MODEL // N_XATTN_HEADS
D_FF = 2816
D_IN = 3 * D_ATTN + N_ATTN_HEADS + 2 * D_CONV
EPS = 1e-6
NEG_INF = -1e30

kernel_name = "fox_conformer_macaron_hybrid"


def rmsnorm(x, g):
    xf = x.astype(jnp.float32)
    y = xf * lax.rsqrt(jnp.mean(xf * xf, axis=-1, keepdims=True) + EPS)
    return (y * g.astype(jnp.float32)).astype(x.dtype)


def layernorm(x, g, b):
    xf = x.astype(jnp.float32)
    mu = jnp.mean(xf, axis=-1, keepdims=True)
    xc = xf - mu
    y = xc * lax.rsqrt(jnp.mean(xc * xc, axis=-1, keepdims=True) + EPS)
    return (y * g.astype(jnp.float32) + b.astype(jnp.float32)).astype(x.dtype)


def swiglu(h, w_gate, w_up, w_down):
    return (jax.nn.silu(h @ w_gate) * (h @ w_up)) @ w_down


def fox_attention(q, k, v, logf):
    B, S, H, Dh = q.shape
    scale = 1.0 / math.sqrt(Dh)
    c = jnp.transpose(jnp.cumsum(logf, axis=1), (0, 2, 1))
    outs = []
    for i in range(S // Q_BLOCK):
        q0, q1 = i * Q_BLOCK, (i + 1) * Q_BLOCK
        qb = q[:, q0:q1]
        kb = k[:, :q1]
        vb = v[:, :q1]
        s = jnp.einsum('bqhd,bkhd->bhqk', qb, kb).astype(jnp.float32) * scale
        s = s + c[:, :, q0:q1, None] - c[:, :, None, :q1]
        mask = (q0 + jnp.arange(Q_BLOCK))[:, None] >= jnp.arange(q1)[None, :]
        s = jnp.where(mask[None, None], s, NEG_INF)
        p = jax.nn.softmax(s, axis=-1).astype(v.dtype)
        outs.append(jnp.einsum('bhqk,bkhd->bqhd', p, vb))
    return jnp.concatenate(outs, axis=1)


def causal_depthwise_conv(u, w, b):
    C = u.shape[-1]
    kern = w.astype(u.dtype)[:, None, :]
    y = lax.conv_general_dilated(
        u, kern, window_strides=(1,), padding=[(CONV_WIDTH - 1, 0)],
        dimension_numbers=('NWC', 'WIO', 'NWC'), feature_group_count=C)
    return y + b.astype(u.dtype)


def hybrid_mix(h, w_in, b_f, conv_w, conv_b, ln_g, ln_b, attn_g, conv_g, w_out):
    B, S, _ = h.shape
    proj = h @ w_in
    splits = [D_ATTN, 2 * D_ATTN, 3 * D_ATTN, 3 * D_ATTN + N_ATTN_HEADS,
              3 * D_ATTN + N_ATTN_HEADS + D_CONV]
    q, k, v, f_logit, a, g = jnp.split(proj, splits, axis=-1)
    q = q.reshape(B, S, N_ATTN_HEADS, HEAD_DIM)
    k = k.reshape(B, S, N_ATTN_HEADS, HEAD_DIM)
    v = v.reshape(B, S, N_ATTN_HEADS, HEAD_DIM)
    logf = jax.nn.log_sigmoid((f_logit + b_f).astype(jnp.float32))
    attn = fox_attention(q, k, v, logf).reshape(B, S, D_ATTN)
    u = a * jax.nn.sigmoid(g)
    u = causal_depthwise_conv(u, conv_w, conv_b)
    u = jax.nn.silu(layernorm(u, ln_g, ln_b))
    y = jnp.concatenate([rmsnorm(attn, attn_g), rmsnorm(u, conv_g)], axis=-1)
    return y @ w_out


def memory_cross_attention(h, mem_n, w_q, w_kv, w_o):
    B, S, _ = h.shape
    q = (h @ w_q).reshape(B, S, N_XATTN_HEADS, XATTN_HEAD_DIM)
    kv = mem_n @ w_kv
    k, v = jnp.split(kv, 2, axis=-1)
    k = k.reshape(B, -1, N_XATTN_HEADS, XATTN_HEAD_DIM)
    v = v.reshape(B, -1, N_XATTN_HEADS, XATTN_HEAD_DIM)
    s = jnp.einsum('bqhd,bmhd->bhqm', q, k).astype(jnp.float32) / math.sqrt(XATTN_HEAD_DIM)
    p = jax.nn.softmax(s, axis=-1).astype(v.dtype)
    o = jnp.einsum('bhqm,bmhd->bqhd', p, v).reshape(B, S, D_MODEL)
    return o @ w_o


def _fwd_setup_inputs(seed: int = 0) -> dict:
    key = jax.random.key(seed)
    ks = iter(jax.random.split(key, 32))
    L = DEPTH

    def w(shape, fan_in):
        return jax.random.normal(next(ks), shape, jnp.float32) * (fan_in ** -0.5)

    def gain(shape):
        return 1.0 + 0.02 * jax.random.normal(next(ks), shape, jnp.float32)

    def bias(shape):
        return 0.02 * jax.random.normal(next(ks), shape, jnp.float32)

    return {
        "x": jax.random.normal(next(ks), (BATCH, SEQ, D_MODEL), jnp.float32),
        "mem": jax.random.normal(next(ks), (BATCH, N_MEM, D_MODEL), jnp.float32),
        "ffn1_norm_g": gain((L, D_MODEL)),
        "ffn1_w_gate": w((L, D_MODEL, D_FF), D_MODEL),
        "ffn1_w_up": w((L, D_MODEL, D_FF), D_MODEL),
        "ffn1_w_down": w((L, D_FF, D_MODEL), D_FF),
        "mix_norm_g": gain((L, D_MODEL)),
        "w_in": w((L, D_MODEL, D_IN), D_MODEL),
        "b_f": jax.random.uniform(next(ks), (L, N_ATTN_HEADS), jnp.float32, 1.0, 6.0),
        "conv_w": w((L, CONV_WIDTH, D_CONV), CONV_WIDTH),
        "conv_b": bias((L, D_CONV)),
        "conv_ln_g": gain((L, D_CONV)),
        "conv_ln_b": bias((L, D_CONV)),
        "attn_out_g": gain((L, D_ATTN)),
        "conv_out_g": gain((L, D_CONV)),
        "w_out": w((L, D_MIX, D_MODEL), D_MIX),
        "xattn_norm_g": gain((L, D_MODEL)),
        "mem_norm_g": gain((L, D_MODEL)),
        "xattn_w_q": w((L, D_MODEL, D_MODEL), D_MODEL),
        "xattn_w_kv": w((L, D_MODEL, 2 * D_MODEL), D_MODEL),
        "xattn_w_o": w((L, D_MODEL, D_MODEL), D_MODEL),
        "ffn2_norm_g": gain((L, D_MODEL)),
        "ffn2_w_gate": w((L, D_MODEL, D_FF), D_MODEL),
        "ffn2_w_up": w((L, D_MODEL, D_FF), D_MODEL),
        "ffn2_w_down": w((L, D_FF, D_MODEL), D_FF),
        "final_norm_g": gain((D_MODEL,)),
    }


def _fwd_reference(x, mem, ffn1_norm_g, ffn1_w_gate, ffn1_w_up, ffn1_w_down, mix_norm_g, w_in, b_f,
              conv_w, conv_b, conv_ln_g, conv_ln_b, attn_out_g, conv_out_g, w_out,
              xattn_norm_g, mem_norm_g, xattn_w_q, xattn_w_kv, xattn_w_o,
              ffn2_norm_g, ffn2_w_gate, ffn2_w_up, ffn2_w_down, final_norm_g):
    for l in range(DEPTH):
        x = x + 0.5 * swiglu(rmsnorm(x, ffn1_norm_g[l]), ffn1_w_gate[l], ffn1_w_up[l], ffn1_w_down[l])
        x = x + hybrid_mix(rmsnorm(x, mix_norm_g[l]), w_in[l], b_f[l], conv_w[l], conv_b[l],
                           conv_ln_g[l], conv_ln_b[l], attn_out_g[l], conv_out_g[l], w_out[l])
        x = x + memory_cross_attention(rmsnorm(x, xattn_norm_g[l]), rmsnorm(mem, mem_norm_g[l]),
                                       xattn_w_q[l], xattn_w_kv[l], xattn_w_o[l])
        x = x + 0.5 * swiglu(rmsnorm(x, ffn2_norm_g[l]), ffn2_w_gate[l], ffn2_w_up[l], ffn2_w_down[l])
    return rmsnorm(x, final_norm_g)


import jax as _jax
import jax.numpy as _jnp

TWIN_FORMAT = 'train_step'
FWD_PARAMS = ['x', 'mem', 'ffn1_norm_g', 'ffn1_w_gate', 'ffn1_w_up', 'ffn1_w_down', 'mix_norm_g', 'w_in', 'b_f', 'conv_w', 'conv_b', 'conv_ln_g', 'conv_ln_b', 'attn_out_g', 'conv_out_g', 'w_out', 'xattn_norm_g', 'mem_norm_g', 'xattn_w_q', 'xattn_w_kv', 'xattn_w_o', 'ffn2_norm_g', 'ffn2_w_gate', 'ffn2_w_up', 'ffn2_w_down', 'final_norm_g']
TWIN_WEIGHTS = ['ffn1_norm_g', 'ffn1_w_gate', 'ffn1_w_up', 'ffn1_w_down', 'mix_norm_g', 'w_in', 'b_f', 'conv_w', 'conv_b', 'conv_ln_g', 'conv_ln_b', 'attn_out_g', 'conv_out_g', 'w_out', 'xattn_norm_g', 'mem_norm_g', 'xattn_w_q', 'xattn_w_kv', 'xattn_w_o', 'ffn2_norm_g', 'ffn2_w_gate', 'ffn2_w_up', 'ffn2_w_down', 'final_norm_g']
TWIN_DIFF_INPUT = 'x'
TWIN_INPUTS = ['x', 'mem', 'ffn1_norm_g', 'ffn1_w_gate', 'ffn1_w_up', 'ffn1_w_down', 'mix_norm_g', 'w_in', 'b_f', 'conv_w', 'conv_b', 'conv_ln_g', 'conv_ln_b', 'attn_out_g', 'conv_out_g', 'w_out', 'xattn_norm_g', 'mem_norm_g', 'xattn_w_q', 'xattn_w_kv', 'xattn_w_o', 'ffn2_norm_g', 'ffn2_w_gate', 'ffn2_w_up', 'ffn2_w_down', 'final_norm_g', 'loss_target', 'm_ffn1_norm_g', 'm_ffn1_w_gate', 'm_ffn1_w_up', 'm_ffn1_w_down', 'm_mix_norm_g', 'm_w_in', 'm_b_f', 'm_conv_w', 'm_conv_b', 'm_conv_ln_g', 'm_conv_ln_b', 'm_attn_out_g', 'm_conv_out_g', 'm_w_out', 'm_xattn_norm_g', 'm_mem_norm_g', 'm_xattn_w_q', 'm_xattn_w_kv', 'm_xattn_w_o', 'm_ffn2_norm_g', 'm_ffn2_w_gate', 'm_ffn2_w_up', 'm_ffn2_w_down', 'm_final_norm_g', 'v_ffn1_norm_g', 'v_ffn1_w_gate', 'v_ffn1_w_up', 'v_ffn1_w_down', 'v_mix_norm_g', 'v_w_in', 'v_b_f', 'v_conv_w', 'v_conv_b', 'v_conv_ln_g', 'v_conv_ln_b', 'v_attn_out_g', 'v_conv_out_g', 'v_w_out', 'v_xattn_norm_g', 'v_mem_norm_g', 'v_xattn_w_q', 'v_xattn_w_kv', 'v_xattn_w_o', 'v_ffn2_norm_g', 'v_ffn2_w_gate', 'v_ffn2_w_up', 'v_ffn2_w_down', 'v_final_norm_g']
TWIN_OUTPUTS = ['loss', 'grad_x', 'grad_ffn1_norm_g', 'grad_ffn1_w_gate', 'grad_ffn1_w_up', 'grad_ffn1_w_down', 'grad_mix_norm_g', 'grad_w_in', 'grad_b_f', 'grad_conv_w', 'grad_conv_b', 'grad_conv_ln_g', 'grad_conv_ln_b', 'grad_attn_out_g', 'grad_conv_out_g', 'grad_w_out', 'grad_xattn_norm_g', 'grad_mem_norm_g', 'grad_xattn_w_q', 'grad_xattn_w_kv', 'grad_xattn_w_o', 'grad_ffn2_norm_g', 'grad_ffn2_w_gate', 'grad_ffn2_w_up', 'grad_ffn2_w_down', 'grad_final_norm_g', 'delta_ffn1_norm_g', 'delta_ffn1_w_gate', 'delta_ffn1_w_up', 'delta_ffn1_w_down', 'delta_mix_norm_g', 'delta_w_in', 'delta_b_f', 'delta_conv_w', 'delta_conv_b', 'delta_conv_ln_g', 'delta_conv_ln_b', 'delta_attn_out_g', 'delta_conv_out_g', 'delta_w_out', 'delta_xattn_norm_g', 'delta_mem_norm_g', 'delta_xattn_w_q', 'delta_xattn_w_kv', 'delta_xattn_w_o', 'delta_ffn2_norm_g', 'delta_ffn2_w_gate', 'delta_ffn2_w_up', 'delta_ffn2_w_down', 'delta_final_norm_g', 'new_m_ffn1_norm_g', 'new_m_ffn1_w_gate', 'new_m_ffn1_w_up', 'new_m_ffn1_w_down', 'new_m_mix_norm_g', 'new_m_w_in', 'new_m_b_f', 'new_m_conv_w', 'new_m_conv_b', 'new_m_conv_ln_g', 'new_m_conv_ln_b', 'new_m_attn_out_g', 'new_m_conv_out_g', 'new_m_w_out', 'new_m_xattn_norm_g', 'new_m_mem_norm_g', 'new_m_xattn_w_q', 'new_m_xattn_w_kv', 'new_m_xattn_w_o', 'new_m_ffn2_norm_g', 'new_m_ffn2_w_gate', 'new_m_ffn2_w_up', 'new_m_ffn2_w_down', 'new_m_final_norm_g', 'new_v_ffn1_norm_g', 'new_v_ffn1_w_gate', 'new_v_ffn1_w_up', 'new_v_ffn1_w_down', 'new_v_mix_norm_g', 'new_v_w_in', 'new_v_b_f', 'new_v_conv_w', 'new_v_conv_b', 'new_v_conv_ln_g', 'new_v_conv_ln_b', 'new_v_attn_out_g', 'new_v_conv_out_g', 'new_v_w_out', 'new_v_xattn_norm_g', 'new_v_mem_norm_g', 'new_v_xattn_w_q', 'new_v_xattn_w_kv', 'new_v_xattn_w_o', 'new_v_ffn2_norm_g', 'new_v_ffn2_w_gate', 'new_v_ffn2_w_up', 'new_v_ffn2_w_down', 'new_v_final_norm_g']
TWIN_LEAF_KINDS = {'loss': 'loss', 'grad_x': 'grad_x', 'grad_ffn1_norm_g': 'grad_w', 'grad_ffn1_w_gate': 'grad_w', 'grad_ffn1_w_up': 'grad_w', 'grad_ffn1_w_down': 'grad_w', 'grad_mix_norm_g': 'grad_w', 'grad_w_in': 'grad_w', 'grad_b_f': 'grad_w', 'grad_conv_w': 'grad_w', 'grad_conv_b': 'grad_w', 'grad_conv_ln_g': 'grad_w', 'grad_conv_ln_b': 'grad_w', 'grad_attn_out_g': 'grad_w', 'grad_conv_out_g': 'grad_w', 'grad_w_out': 'grad_w', 'grad_xattn_norm_g': 'grad_w', 'grad_mem_norm_g': 'grad_w', 'grad_xattn_w_q': 'grad_w', 'grad_xattn_w_kv': 'grad_w', 'grad_xattn_w_o': 'grad_w', 'grad_ffn2_norm_g': 'grad_w', 'grad_ffn2_w_gate': 'grad_w', 'grad_ffn2_w_up': 'grad_w', 'grad_ffn2_w_down': 'grad_w', 'grad_final_norm_g': 'grad_w', 'delta_ffn1_norm_g': 'delta_w', 'delta_ffn1_w_gate': 'delta_w', 'delta_ffn1_w_up': 'delta_w', 'delta_ffn1_w_down': 'delta_w', 'delta_mix_norm_g': 'delta_w', 'delta_w_in': 'delta_w', 'delta_b_f': 'delta_w', 'delta_conv_w': 'delta_w', 'delta_conv_b': 'delta_w', 'delta_conv_ln_g': 'delta_w', 'delta_conv_ln_b': 'delta_w', 'delta_attn_out_g': 'delta_w', 'delta_conv_out_g': 'delta_w', 'delta_w_out': 'delta_w', 'delta_xattn_norm_g': 'delta_w', 'delta_mem_norm_g': 'delta_w', 'delta_xattn_w_q': 'delta_w', 'delta_xattn_w_kv': 'delta_w', 'delta_xattn_w_o': 'delta_w', 'delta_ffn2_norm_g': 'delta_w', 'delta_ffn2_w_gate': 'delta_w', 'delta_ffn2_w_up': 'delta_w', 'delta_ffn2_w_down': 'delta_w', 'delta_final_norm_g': 'delta_w', 'new_m_ffn1_norm_g': 'new_m', 'new_m_ffn1_w_gate': 'new_m', 'new_m_ffn1_w_up': 'new_m', 'new_m_ffn1_w_down': 'new_m', 'new_m_mix_norm_g': 'new_m', 'new_m_w_in': 'new_m', 'new_m_b_f': 'new_m', 'new_m_conv_w': 'new_m', 'new_m_conv_b': 'new_m', 'new_m_conv_ln_g': 'new_m', 'new_m_conv_ln_b': 'new_m', 'new_m_attn_out_g': 'new_m', 'new_m_conv_out_g': 'new_m', 'new_m_w_out': 'new_m', 'new_m_xattn_norm_g': 'new_m', 'new_m_mem_norm_g': 'new_m', 'new_m_xattn_w_q': 'new_m', 'new_m_xattn_w_kv': 'new_m', 'new_m_xattn_w_o': 'new_m', 'new_m_ffn2_norm_g': 'new_m', 'new_m_ffn2_w_gate': 'new_m', 'new_m_ffn2_w_up': 'new_m', 'new_m_ffn2_w_down': 'new_m', 'new_m_final_norm_g': 'new_m', 'new_v_ffn1_norm_g': 'new_v', 'new_v_ffn1_w_gate': 'new_v', 'new_v_ffn1_w_up': 'new_v', 'new_v_ffn1_w_down': 'new_v', 'new_v_mix_norm_g': 'new_v', 'new_v_w_in': 'new_v', 'new_v_b_f': 'new_v', 'new_v_conv_w': 'new_v', 'new_v_conv_b': 'new_v', 'new_v_conv_ln_g': 'new_v', 'new_v_conv_ln_b': 'new_v', 'new_v_attn_out_g': 'new_v', 'new_v_conv_out_g': 'new_v', 'new_v_w_out': 'new_v', 'new_v_xattn_norm_g': 'new_v', 'new_v_mem_norm_g': 'new_v', 'new_v_xattn_w_q': 'new_v', 'new_v_xattn_w_kv': 'new_v', 'new_v_xattn_w_o': 'new_v', 'new_v_ffn2_norm_g': 'new_v', 'new_v_ffn2_w_gate': 'new_v', 'new_v_ffn2_w_up': 'new_v', 'new_v_ffn2_w_down': 'new_v', 'new_v_final_norm_g': 'new_v'}


def _forward(args):
    return _fwd_reference(*[args[k] for k in FWD_PARAMS])


def _output_shape():
    out = _jax.eval_shape(lambda: _forward(_fwd_setup_inputs(0)))
    return out.shape, out.dtype

N_MICROBATCH = 1
ADAM_LR = 0.001
ADAM_B1 = 0.9
ADAM_B2 = 0.999
ADAM_EPS = 1e-08
ADAM_WD = 0.01
ADAM_STEP = 10
PER_EXAMPLE_BATCH_AXIS = {'x': 0, 'mem': 0, 'loss_target': 0}
SHARED_INPUTS = []
_WEIGHT_DTYPES = {'ffn1_norm_g': _jnp.float32, 'ffn1_w_gate': _jnp.float32, 'ffn1_w_up': _jnp.float32, 'ffn1_w_down': _jnp.float32, 'mix_norm_g': _jnp.float32, 'w_in': _jnp.float32, 'b_f': _jnp.float32, 'conv_w': _jnp.float32, 'conv_b': _jnp.float32, 'conv_ln_g': _jnp.float32, 'conv_ln_b': _jnp.float32, 'attn_out_g': _jnp.float32, 'conv_out_g': _jnp.float32, 'w_out': _jnp.float32, 'xattn_norm_g': _jnp.float32, 'mem_norm_g': _jnp.float32, 'xattn_w_q': _jnp.float32, 'xattn_w_kv': _jnp.float32, 'xattn_w_o': _jnp.float32, 'ffn2_norm_g': _jnp.float32, 'ffn2_w_gate': _jnp.float32, 'ffn2_w_up': _jnp.float32, 'ffn2_w_down': _jnp.float32, 'final_norm_g': _jnp.float32}
MOMENT_SCALE = {'ffn1_norm_g': 5.601219e-02, 'ffn1_w_gate': 2.347179e-02, 'ffn1_w_up': 2.280527e-02, 'ffn1_w_down': 3.771966e-02, 'mix_norm_g': 1.125651e-01, 'w_in': 6.967692e-02, 'b_f': 3.600815e-01, 'conv_w': 8.795984e-02, 'conv_b': 2.197376e-01, 'conv_ln_g': 1.202853e-01, 'conv_ln_b': 1.303483e-01, 'attn_out_g': 1.056653e-01, 'conv_out_g': 9.258461e-02, 'w_out': 9.065832e-02, 'xattn_norm_g': 8.800743e-03, 'mem_norm_g': 1.379033e-02, 'xattn_w_q': 8.886864e-03, 'xattn_w_kv': 9.479651e-03, 'xattn_w_o': 1.018874e-02, 'ffn2_norm_g': 3.523832e-02, 'ffn2_w_gate': 1.515223e-02, 'ffn2_w_up': 1.470727e-02, 'ffn2_w_down': 2.443651e-02, 'final_norm_g': 1.601783e+01}


def _to_microbatches(a, axis):
    t = _jnp.moveaxis(a, axis, 0)
    t = t.reshape((N_MICROBATCH, t.shape[0] // N_MICROBATCH) + t.shape[1:])
    return _jnp.moveaxis(t, 1, axis + 1)


def setup_inputs(seed: int = 0) -> dict:
    inp = _fwd_setup_inputs(seed)
    key = _jax.random.fold_in(_jax.random.key(seed), 7919)
    shape, _ = _output_shape()
    out = dict(inp)
    out["loss_target"] = _jax.random.normal(_jax.random.fold_in(key, 0), shape, _jnp.float32)
    for i, name in enumerate(TWIN_WEIGHTS):
        w = inp[name].astype(_jnp.float32)
        if MOMENT_SCALE is None:
            s = _jnp.sqrt(_jnp.mean(_jnp.square(w)) + 1e-30)
        else:
            s = MOMENT_SCALE[name]
        km, kv = _jax.random.split(_jax.random.fold_in(key, i + 1))
        out[name] = w
        out["m_" + name] = s * _jax.random.normal(km, w.shape, _jnp.float32)
        out["v_" + name] = (s * s) * _jax.random.uniform(kv, w.shape, _jnp.float32, 0.5, 1.5)
    if N_MICROBATCH > 1:
        for name, axis in PER_EXAMPLE_BATCH_AXIS.items():
            out[name] = _to_microbatches(out[name], axis)
    return {'x': out['x'], 'mem': out['mem'], 'ffn1_norm_g': out['ffn1_norm_g'], 'ffn1_w_gate': out['ffn1_w_gate'], 'ffn1_w_up': out['ffn1_w_up'], 'ffn1_w_down': out['ffn1_w_down'], 'mix_norm_g': out['mix_norm_g'], 'w_in': out['w_in'], 'b_f': out['b_f'], 'conv_w': out['conv_w'], 'conv_b': out['conv_b'], 'conv_ln_g': out['conv_ln_g'], 'conv_ln_b': out['conv_ln_b'], 'attn_out_g': out['attn_out_g'], 'conv_out_g': out['conv_out_g'], 'w_out': out['w_out'], 'xattn_norm_g': out['xattn_norm_g'], 'mem_norm_g': out['mem_norm_g'], 'xattn_w_q': out['xattn_w_q'], 'xattn_w_kv': out['xattn_w_kv'], 'xattn_w_o': out['xattn_w_o'], 'ffn2_norm_g': out['ffn2_norm_g'], 'ffn2_w_gate': out['ffn2_w_gate'], 'ffn2_w_up': out['ffn2_w_up'], 'ffn2_w_down': out['ffn2_w_down'], 'final_norm_g': out['final_norm_g'], 'loss_target': out['loss_target'], 'm_ffn1_norm_g': out['m_ffn1_norm_g'], 'm_ffn1_w_gate': out['m_ffn1_w_gate'], 'm_ffn1_w_up': out['m_ffn1_w_up'], 'm_ffn1_w_down': out['m_ffn1_w_down'], 'm_mix_norm_g': out['m_mix_norm_g'], 'm_w_in': out['m_w_in'], 'm_b_f': out['m_b_f'], 'm_conv_w': out['m_conv_w'], 'm_conv_b': out['m_conv_b'], 'm_conv_ln_g': out['m_conv_ln_g'], 'm_conv_ln_b': out['m_conv_ln_b'], 'm_attn_out_g': out['m_attn_out_g'], 'm_conv_out_g': out['m_conv_out_g'], 'm_w_out': out['m_w_out'], 'm_xattn_norm_g': out['m_xattn_norm_g'], 'm_mem_norm_g': out['m_mem_norm_g'], 'm_xattn_w_q': out['m_xattn_w_q'], 'm_xattn_w_kv': out['m_xattn_w_kv'], 'm_xattn_w_o': out['m_xattn_w_o'], 'm_ffn2_norm_g': out['m_ffn2_norm_g'], 'm_ffn2_w_gate': out['m_ffn2_w_gate'], 'm_ffn2_w_up': out['m_ffn2_w_up'], 'm_ffn2_w_down': out['m_ffn2_w_down'], 'm_final_norm_g': out['m_final_norm_g'], 'v_ffn1_norm_g': out['v_ffn1_norm_g'], 'v_ffn1_w_gate': out['v_ffn1_w_gate'], 'v_ffn1_w_up': out['v_ffn1_w_up'], 'v_ffn1_w_down': out['v_ffn1_w_down'], 'v_mix_norm_g': out['v_mix_norm_g'], 'v_w_in': out['v_w_in'], 'v_b_f': out['v_b_f'], 'v_conv_w': out['v_conv_w'], 'v_conv_b': out['v_conv_b'], 'v_conv_ln_g': out['v_conv_ln_g'], 'v_conv_ln_b': out['v_conv_ln_b'], 'v_attn_out_g': out['v_attn_out_g'], 'v_conv_out_g': out['v_conv_out_g'], 'v_w_out': out['v_w_out'], 'v_xattn_norm_g': out['v_xattn_norm_g'], 'v_mem_norm_g': out['v_mem_norm_g'], 'v_xattn_w_q': out['v_xattn_w_q'], 'v_xattn_w_kv': out['v_xattn_w_kv'], 'v_xattn_w_o': out['v_xattn_w_o'], 'v_ffn2_norm_g': out['v_ffn2_norm_g'], 'v_ffn2_w_gate': out['v_ffn2_w_gate'], 'v_ffn2_w_up': out['v_ffn2_w_up'], 'v_ffn2_w_down': out['v_ffn2_w_down'], 'v_final_norm_g': out['v_final_norm_g']}


def _loss(weights, diff, rest, loss_target):
    with _jax.named_scope("forward"):
        args = {**rest, TWIN_DIFF_INPUT: diff, **{k: w.astype(_WEIGHT_DTYPES[k]) for k, w in weights.items()}}
        y = _forward(args)
    with _jax.named_scope("loss_head"):
        err = _jnp.square(y.astype(_jnp.float32) - loss_target)
        return 0.5 * _jnp.sum(_jnp.mean(err, axis=-1)) if err.ndim else 0.5 * err


def _adamw(w, g, m, v):
    m = ADAM_B1 * m + (1.0 - ADAM_B1) * g
    v = ADAM_B2 * v + (1.0 - ADAM_B2) * _jnp.square(g)
    m_hat = m / (1.0 - ADAM_B1 ** ADAM_STEP)
    v_hat = v / (1.0 - ADAM_B2 ** ADAM_STEP)
    delta = -ADAM_LR * (m_hat / (_jnp.sqrt(v_hat) + ADAM_EPS) + ADAM_WD * w)
    return delta, m, v


def reference(x, mem, ffn1_norm_g, ffn1_w_gate, ffn1_w_up, ffn1_w_down, mix_norm_g, w_in, b_f, conv_w, conv_b, conv_ln_g, conv_ln_b, attn_out_g, conv_out_g, w_out, xattn_norm_g, mem_norm_g, xattn_w_q, xattn_w_kv, xattn_w_o, ffn2_norm_g, ffn2_w_gate, ffn2_w_up, ffn2_w_down, final_norm_g, loss_target, m_ffn1_norm_g, m_ffn1_w_gate, m_ffn1_w_up, m_ffn1_w_down, m_mix_norm_g, m_w_in, m_b_f, m_conv_w, m_conv_b, m_conv_ln_g, m_conv_ln_b, m_attn_out_g, m_conv_out_g, m_w_out, m_xattn_norm_g, m_mem_norm_g, m_xattn_w_q, m_xattn_w_kv, m_xattn_w_o, m_ffn2_norm_g, m_ffn2_w_gate, m_ffn2_w_up, m_ffn2_w_down, m_final_norm_g, v_ffn1_norm_g, v_ffn1_w_gate, v_ffn1_w_up, v_ffn1_w_down, v_mix_norm_g, v_w_in, v_b_f, v_conv_w, v_conv_b, v_conv_ln_g, v_conv_ln_b, v_attn_out_g, v_conv_out_g, v_w_out, v_xattn_norm_g, v_mem_norm_g, v_xattn_w_q, v_xattn_w_kv, v_xattn_w_o, v_ffn2_norm_g, v_ffn2_w_gate, v_ffn2_w_up, v_ffn2_w_down, v_final_norm_g):
    given = dict(x=x, mem=mem, ffn1_norm_g=ffn1_norm_g, ffn1_w_gate=ffn1_w_gate, ffn1_w_up=ffn1_w_up, ffn1_w_down=ffn1_w_down, mix_norm_g=mix_norm_g, w_in=w_in, b_f=b_f, conv_w=conv_w, conv_b=conv_b, conv_ln_g=conv_ln_g, conv_ln_b=conv_ln_b, attn_out_g=attn_out_g, conv_out_g=conv_out_g, w_out=w_out, xattn_norm_g=xattn_norm_g, mem_norm_g=mem_norm_g, xattn_w_q=xattn_w_q, xattn_w_kv=xattn_w_kv, xattn_w_o=xattn_w_o, ffn2_norm_g=ffn2_norm_g, ffn2_w_gate=ffn2_w_gate, ffn2_w_up=ffn2_w_up, ffn2_w_down=ffn2_w_down, final_norm_g=final_norm_g, loss_target=loss_target, m_ffn1_norm_g=m_ffn1_norm_g, m_ffn1_w_gate=m_ffn1_w_gate, m_ffn1_w_up=m_ffn1_w_up, m_ffn1_w_down=m_ffn1_w_down, m_mix_norm_g=m_mix_norm_g, m_w_in=m_w_in, m_b_f=m_b_f, m_conv_w=m_conv_w, m_conv_b=m_conv_b, m_conv_ln_g=m_conv_ln_g, m_conv_ln_b=m_conv_ln_b, m_attn_out_g=m_attn_out_g, m_conv_out_g=m_conv_out_g, m_w_out=m_w_out, m_xattn_norm_g=m_xattn_norm_g, m_mem_norm_g=m_mem_norm_g, m_xattn_w_q=m_xattn_w_q, m_xattn_w_kv=m_xattn_w_kv, m_xattn_w_o=m_xattn_w_o, m_ffn2_norm_g=m_ffn2_norm_g, m_ffn2_w_gate=m_ffn2_w_gate, m_ffn2_w_up=m_ffn2_w_up, m_ffn2_w_down=m_ffn2_w_down, m_final_norm_g=m_final_norm_g, v_ffn1_norm_g=v_ffn1_norm_g, v_ffn1_w_gate=v_ffn1_w_gate, v_ffn1_w_up=v_ffn1_w_up, v_ffn1_w_down=v_ffn1_w_down, v_mix_norm_g=v_mix_norm_g, v_w_in=v_w_in, v_b_f=v_b_f, v_conv_w=v_conv_w, v_conv_b=v_conv_b, v_conv_ln_g=v_conv_ln_g, v_conv_ln_b=v_conv_ln_b, v_attn_out_g=v_attn_out_g, v_conv_out_g=v_conv_out_g, v_w_out=v_w_out, v_xattn_norm_g=v_xattn_norm_g, v_mem_norm_g=v_mem_norm_g, v_xattn_w_q=v_xattn_w_q, v_xattn_w_kv=v_xattn_w_kv, v_xattn_w_o=v_xattn_w_o, v_ffn2_norm_g=v_ffn2_norm_g, v_ffn2_w_gate=v_ffn2_w_gate, v_ffn2_w_up=v_ffn2_w_up, v_ffn2_w_down=v_ffn2_w_down, v_final_norm_g=v_final_norm_g)
    weights = {n: given[n] for n in TWIN_WEIGHTS}
    shared = {n: given[n] for n in SHARED_INPUTS}
    per_example = {n: given[n] for n in ['x', 'mem']}
    grad_fn = _jax.value_and_grad(_loss, argnums=(0, 1))

    def one_microbatch(ex, loss_target):
        ex = dict(ex)
        diff = ex.pop(TWIN_DIFF_INPUT)
        return grad_fn(weights, diff, {**shared, **ex}, loss_target)

    if N_MICROBATCH == 1:
        loss, (grad_w, grad_x) = one_microbatch(per_example, given["loss_target"])
    else:
        def body(carry, xs):
            loss_sum, grad_sum = carry
            l_k, (gw_k, gx_k) = one_microbatch(xs[0], xs[1])
            with _jax.named_scope("update"):
                return (loss_sum + l_k, _jax.tree.map(_jnp.add, grad_sum, gw_k)), gx_k

        init = (_jnp.zeros((), _jnp.float32), _jax.tree.map(_jnp.zeros_like, weights))
        (loss, grad_w), grad_x = _jax.lax.scan(body, init, (per_example, given["loss_target"]))
    with _jax.named_scope("update"):
        delta_w, new_m, new_v = {}, {}, {}
        for n in TWIN_WEIGHTS:
            delta_w[n], new_m[n], new_v[n] = _adamw(weights[n], grad_w[n], given["m_" + n], given["v_" + n])
    return (loss, grad_x, *[grad_w[n] for n in TWIN_WEIGHTS], *[delta_w[n] for n in TWIN_WEIGHTS],
            *[new_m[n] for n in TWIN_WEIGHTS], *[new_v[n] for n in TWIN_WEIGHTS])
```

```python
import math

import jax
import jax.numpy as jnp
from jax import lax
from jax.experimental import pallas as pl
from jax.experimental.pallas import tpu as pltpu

F32, BF16 = jnp.float32, jnp.bfloat16
S = jax.ShapeDtypeStruct
MESH = pl.DeviceIdType.MESH

EPS = 1e-6
NEG_INF = -1e30
HEAD_DIM = 64
N_XATTN_HEADS = 4
CONV_WIDTH = 31
CONV_PAD = 32
LANES = 128
ADAM_LR, ADAM_B1, ADAM_B2, ADAM_EPS, ADAM_WD, ADAM_STEP = 0.001, 0.9, 0.999, 1e-08, 0.01, 10
N_DEV = 8
VMEM_LIMIT_BYTES = 56 * 1024 * 1024
ROW_TILE = 512
PACK_COLS = 1024
SMALL_COLS = 512

NN = ((1,), (0,))
NT = ((1,), (1,))
TN = ((0,), (0,))

NAMES = ['x', 'mem', 'ffn1_norm_g', 'ffn1_w_gate', 'ffn1_w_up', 'ffn1_w_down', 'mix_norm_g', 'w_in', 'b_f', 'conv_w', 'conv_b',
         'conv_ln_g', 'conv_ln_b', 'attn_out_g', 'conv_out_g', 'w_out', 'xattn_norm_g', 'mem_norm_g', 'xattn_w_q', 'xattn_w_kv',
         'xattn_w_o', 'ffn2_norm_g', 'ffn2_w_gate', 'ffn2_w_up', 'ffn2_w_down', 'final_norm_g']
WEIGHTS = NAMES[2:]
BIG = [('ffn1_w_gate', 'col'), ('ffn1_w_up', 'col'), ('ffn1_w_down', 'row'), ('w_in', 'col'), ('w_out', 'row'),
       ('xattn_w_q', 'row'), ('xattn_w_kv', 'col'), ('xattn_w_o', 'row'), ('ffn2_w_gate', 'col'), ('ffn2_w_up', 'col'),
       ('ffn2_w_down', 'row')]
SMALL = ['ffn1_norm_g', 'mix_norm_g', 'xattn_norm_g', 'mem_norm_g', 'ffn2_norm_g', 'conv_b', 'conv_ln_g', 'conv_ln_b',
         'attn_out_g', 'conv_out_g', 'b_f', 'final_norm_g']


def _dot(a, b, dims):
    return lax.dot_general(a, b, (dims, ((), ())), preferred_element_type=F32)


def _full(shape):
    nd = len(shape)
    return pl.BlockSpec(shape, lambda *_: (0,) * nd)


def _tile(n, pref):
    for t in (pref, 512, 384, 256, 128, 64, 32, 16, 8):
        if t <= n and n % t == 0:
            return t
    return n


def _pcall(name, body, grid, in_specs, out_specs, out_shape, scratch=(), aliases=None):
    return pl.pallas_call(
        body, grid=grid, in_specs=in_specs, out_specs=out_specs, out_shape=out_shape, scratch_shapes=list(scratch), name=name,
        input_output_aliases=aliases or {},
        compiler_params=pltpu.CompilerParams(dimension_semantics=("arbitrary",) * len(grid), vmem_limit_bytes=VMEM_LIMIT_BYTES))


def _arr(w):
    return w[0] if isinstance(w, tuple) else w


def _wshape(w):
    return w[0].shape[1:] if isinstance(w, tuple) else w.shape


def _wspec(w, block, imap):
    if isinstance(w, tuple):
        layer = w[1]
        return pl.BlockSpec((None,) + block, lambda *g: (layer,) + imap(*g))
    return pl.BlockSpec(block, imap)


def _wfull(w):
    shape = _wshape(w)
    return _wspec(w, shape, lambda *_: (0,) * len(shape))


def _sigmoid(z):
    return jax.nn.sigmoid(z)


def _rstd(x):
    return lax.rsqrt(jnp.mean(x * x, axis=-1, keepdims=True) + EPS)


def _rms_bwd(dy, x, g):
    r = _rstd(x)
    xh = x * r
    u = dy * g
    dx = r * (u - xh * jnp.mean(u * xh, axis=-1, keepdims=True))
    return dx, dy * xh


def _colsum(v):
    return jnp.sum(v, axis=0, keepdims=True)


def _rms_fwd(x, g, name):
    T, D = x.shape
    tm = _tile(T, ROW_TILE)

    def body(x_ref, g_ref, h_ref):
        xv = x_ref[...]
        h_ref[...] = (xv * _rstd(xv) * g_ref[...]).astype(BF16)

    row = pl.BlockSpec((tm, D), lambda i: (i, 0))
    return _pcall(name, body, (T // tm,), [row, _full((1, D))], row, S((T, D), BF16))(x, g.reshape(1, D))


def _mm(a, w, out_dtype, name):
    M, K = a.shape
    N = _wshape(w)[1]
    tm = _tile(M, ROW_TILE)
    tn = N if N <= 1536 else N // 2

    def body(a_ref, w_ref, o_ref):
        o_ref[...] = _dot(a_ref[...], w_ref[...], NN).astype(out_dtype)

    return _pcall(name, body, (N // tn, M // tm),
                  [pl.BlockSpec((tm, K), lambda j, i: (i, 0)), _wspec(w, (K, tn), lambda j, i: (0, j))],
                  pl.BlockSpec((tm, tn), lambda j, i: (i, j)), S((M, N), out_dtype))(a, _arr(w))


def _mm_res(a, w, res, scale, name):
    M, K = a.shape
    N = _wshape(w)[1]
    tm = _tile(M, ROW_TILE)

    def body(a_ref, w_ref, r_ref, o_ref):
        o_ref[...] = r_ref[...] + scale * _dot(a_ref[...], w_ref[...], NN)

    row = pl.BlockSpec((tm, N), lambda i: (i, 0))
    return _pcall(name, body, (M // tm,), [pl.BlockSpec((tm, K), lambda i: (i, 0)), _wfull(w), row], row,
                  S((M, N), F32))(a, _arr(w), res)


def _mm_nt(a, w, out_dtype, name):
    M, K = a.shape
    N = _wshape(w)[0]
    tm = _tile(M, ROW_TILE)
    tn = N if N <= 1536 else N // 2

    def body(a_ref, w_ref, o_ref):
        o_ref[...] = _dot(a_ref[...].astype(BF16), w_ref[...], NT).astype(out_dtype)

    return _pcall(name, body, (N // tn, M // tm),
                  [pl.BlockSpec((tm, K), lambda j, i: (i, 0)), _wspec(w, (tn, K), lambda j, i: (j, 0))],
                  pl.BlockSpec((tm, tn), lambda j, i: (i, j)), S((M, N), out_dtype))(a, _arr(w))


def _wgrad(a, dy, scale, name, into):
    buf, layer, L = into
    T, M = a.shape
    N = dy.shape[1]
    tm = _tile(M, 256)

    def body(a_ref, dy_ref, *rest):
        rest[-1][...] = (scale * _dot(a_ref[...], dy_ref[...].astype(BF16), TN)).astype(BF16)

    in_specs = [pl.BlockSpec((T, tm), lambda i: (0, i)), _full((T, N))]
    args = [a, dy]
    if buf is not None:
        in_specs.append(pl.BlockSpec(memory_space=pl.ANY))
        args.append(buf)
    return _pcall(name, body, (M // tm,), in_specs, pl.BlockSpec((None, tm, N), lambda i: (layer, i, 0)), S((L, M, N), BF16),
                  aliases={2: 0} if buf is not None else None)(*args)


def _bwd_h(dots, x, g, dres, name):
    T, D = x.shape
    tm = _tile(T, 256)
    n = len(dots)
    modes = [m for _, _, m in dots]

    def body(*refs):
        x_ref, g_ref, r_ref, dx_ref, dg_ref = refs[2 * n:]
        dh = None
        for k in range(n):
            part = _dot(refs[2 * k][...], refs[2 * k + 1][...], NT if modes[k] == 'nt' else TN)
            dh = part if dh is None else dh + part
        dx, dgrow = _rms_bwd(dh, x_ref[...], g_ref[...])
        dx_ref[...] = r_ref[...] + dx

        @pl.when(pl.program_id(0) == 0)
        def _():
            dg_ref[...] = jnp.zeros_like(dg_ref)

        dg_ref[...] += _colsum(dgrow)

    in_specs, args = [], []
    for lhs, w, mode in dots:
        if mode == 'nt':
            in_specs.append(pl.BlockSpec((tm, lhs.shape[1]), lambda i: (i, 0)))
        else:
            in_specs.append(pl.BlockSpec((lhs.shape[0], tm), lambda i: (0, i)))
        in_specs.append(_wfull(w))
        args += [lhs, _arr(w)]
    row = pl.BlockSpec((tm, D), lambda i: (i, 0))
    in_specs += [row, _full((1, D)), row]
    return _pcall(name, body, (T // tm,), in_specs, [row, _full((1, D))], [S((T, D), F32), S((1, D), F32)])(
        *args, x, g.reshape(1, D), dres)


def _rms_gain_grad(dy, x, g, name):
    T, D = x.shape

    def body(dy_ref, x_ref, g_ref, dg_ref):
        _, dgrow = _rms_bwd(dy_ref[...], x_ref[...], g_ref[...])
        dg_ref[...] = _colsum(dgrow)

    return _pcall(name, body, (), [_full((T, D)), _full((T, D)), _full((1, D))], _full((1, D)), S((1, D), F32))(
        dy, x, g.reshape(1, D))


def _ffn_up(h, wg, wu, name):
    T, D = h.shape
    Fh = _wshape(wg)[1]
    tm = _tile(T, ROW_TILE)
    tn = Fh if Fh <= 1536 else Fh // 2

    def body(h_ref, wg_ref, wu_ref, g_ref, u_ref, a_ref):
        hv = h_ref[...]
        gv = _dot(hv, wg_ref[...], NN)
        uv = _dot(hv, wu_ref[...], NN)
        g_ref[...] = gv.astype(BF16)
        u_ref[...] = uv.astype(BF16)
        a_ref[...] = (gv * _sigmoid(gv) * uv).astype(BF16)

    tile = pl.BlockSpec((tm, tn), lambda j, i: (i, j))
    return _pcall(name, body, (Fh // tn, T // tm),
                  [pl.BlockSpec((tm, D), lambda j, i: (i, 0)), _wspec(wg, (D, tn), lambda j, i: (0, j)),
                   _wspec(wu, (D, tn), lambda j, i: (0, j))],
                  [tile, tile, tile], [S((T, Fh), BF16)] * 3)(h, _arr(wg), _arr(wu))


def _ffn_bwd_act(dout, wd, gate, up, scale, name):
    T, D = dout.shape
    Fh = _wshape(wd)[0]
    tm = _tile(T, ROW_TILE)
    tn = Fh if Fh <= 1536 else Fh // 2

    def body(d_ref, w_ref, g_ref, u_ref, dg_ref, du_ref):
        da = scale * _dot(d_ref[...].astype(BF16), w_ref[...], NT)
        gv = g_ref[...].astype(F32)
        uv = u_ref[...].astype(F32)
        sg = _sigmoid(gv)
        dg_ref[...] = (da * uv * (sg * (1.0 + gv * (1.0 - sg)))).astype(BF16)
        du_ref[...] = (da * (gv * sg)).astype(BF16)

    tile = pl.BlockSpec((tm, tn), lambda j, i: (i, j))
    return _pcall(name, body, (Fh // tn, T // tm),
                  [pl.BlockSpec((tm, D), lambda j, i: (i, 0)), _wspec(wd, (tn, D), lambda j, i: (j, 0)), tile, tile],
                  [tile, tile], [S((T, Fh), BF16)] * 2)(dout, _arr(wd), gate, up)


def _loss_head(x, g, tgt, name):
    T, D = x.shape
    tm = _tile(T, ROW_TILE)

    def body(x_ref, g_ref, t_ref, loss_ref, dx_ref, dg_ref):
        xv = x_ref[...]
        gv = g_ref[...]
        r = _rstd(xv)
        xh = xv * r
        e = xh * gv - t_ref[...]
        dy = e * (1.0 / D)
        u = dy * gv
        dx_ref[...] = r * (u - xh * jnp.mean(u * xh, axis=-1, keepdims=True))

        @pl.when(pl.program_id(0) == 0)
        def _():
            dg_ref[...] = jnp.zeros_like(dg_ref)
            loss_ref[...] = jnp.zeros_like(loss_ref)

        dg_ref[...] += _colsum(dy * xh)
        loss_ref[...] += 0.5 * _colsum(jnp.mean(e * e, axis=-1, keepdims=True))

    row = pl.BlockSpec((tm, D), lambda i: (i, 0))
    return _pcall(name, body, (T // tm,), [row, _full((1, D)), row], [_full((1, 1)), row, _full((1, D))],
                  [S((1, 1), F32), S((T, D), F32), S((1, D), F32)])(x, g.reshape(1, D), tgt)


def _split3(xb):
    hi = xb.astype(BF16)
    r1 = xb - hi.astype(F32)
    mid = r1.astype(BF16)
    lo = (r1 - mid.astype(F32)).astype(BF16)
    return hi, mid, lo


def _fox_prep(h, wft, bft, name):
    T, D = h.shape
    blk = _tile(T, 256)

    def body(h_ref, w_ref, b_ref, ct_ref, sg_ref):
        z = _dot(w_ref[...], h_ref[...], NT) + b_ref[...]
        sg_ref[...] = 1.0 - _sigmoid(z)
        logf = jnp.minimum(z, 0.0) - jnp.log1p(jnp.exp(-jnp.abs(z)))
        upper = (lax.broadcasted_iota(jnp.int32, (blk, blk), 0) <= lax.broadcasted_iota(jnp.int32, (blk, blk), 1)).astype(BF16)
        carry = jnp.zeros((16, 1), F32)
        for b in range(T // blk):
            hi, mid, lo = _split3(logf[:, b * blk:(b + 1) * blk])
            cb = _dot(hi, upper, NN) + _dot(mid, upper, NN) + _dot(lo, upper, NN) + carry
            ct_ref[:, b * blk:(b + 1) * blk] = cb
            carry = cb[:, blk - 1:blk]

    return _pcall(name, body, (), [_full((T, D)), _full((16, D)), _full((16, 1))], [_full((16, T)), _full((16, T))],
                  [S((16, T), F32), S((16, T), F32)])(h, wft, bft)


def _fox_prep_bwd(dcs, sg, h, name):
    T, D = h.shape
    blk = _tile(T, 256)
    nb = T // blk

    def body(dcs_ref, sg_ref, h_ref, dfl_ref, dw_ref, db_ref):
        lower = (lax.broadcasted_iota(jnp.int32, (blk, blk), 0) >= lax.broadcasted_iota(jnp.int32, (blk, blk), 1)).astype(BF16)
        carry = jnp.zeros((16, 1), F32)
        db = jnp.zeros((16, 1), F32)
        for b in range(nb - 1, -1, -1):
            cols = slice(b * blk, (b + 1) * blk)
            hi, mid, lo = _split3(-dcs_ref[:, cols])
            dlogf = _dot(hi, lower, NN) + _dot(mid, lower, NN) + _dot(lo, lower, NN) + carry
            carry = dlogf[:, 0:1]
            dfl = dlogf * sg_ref[:, cols]
            db = db + jnp.sum(dfl, axis=-1, keepdims=True)
            dfl_ref[:, cols] = dfl.astype(BF16)
        db_ref[...] = db
        dw_ref[...] = _dot(dfl_ref[...], h_ref[...], NN)

    return _pcall(name, body, (), [_full((16, T)), _full((16, T)), _full((T, D))],
                  [_full((16, T)), _full((16, D)), _full((16, 1))],
                  [S((16, T), BF16), S((16, D), F32), S((16, 1), F32)])(dcs, sg, h)


def _fox_logits(q, k, c_col, c_row, row0):
    s = _dot(q, k, NT) * (1.0 / math.sqrt(HEAD_DIM)) + (c_col - c_row)
    row = lax.broadcasted_iota(jnp.int32, s.shape, 0) + row0
    col = lax.broadcasted_iota(jnp.int32, s.shape, 1)
    return jnp.where(row >= col, s, NEG_INF)


def _fox_specs(T, n_pairs):
    qs = pl.BlockSpec((T, LANES), lambda p: (0, p))
    ks = pl.BlockSpec((T, LANES), lambda p: (0, n_pairs + p))
    vs = pl.BlockSpec((T, LANES), lambda p: (0, 2 * n_pairs + p))
    col = pl.BlockSpec((2, T, 1), lambda p: (p, 0, 0))
    rowv = pl.BlockSpec((None, 2, T), lambda p: (p, 0, 0))
    return qs, ks, vs, col, rowv


def _fox_fwd(qkv, c_col, c_row, name):
    T = qkv.shape[0]
    DA = qkv.shape[1] // 3
    n_pairs = DA // LANES
    tq = _tile(T, 256)

    def body(q_ref, k_ref, v_ref, c_ref, ct_ref, o_ref, lse_ref):
        for hh in range(2):
            sl = slice(hh * HEAD_DIM, (hh + 1) * HEAD_DIM)
            for i in range(T // tq):
                rows = slice(i * tq, (i + 1) * tq)
                kp = (i + 1) * tq
                s = _fox_logits(q_ref[rows, sl], k_ref[0:kp, sl], c_ref[hh, rows, :], ct_ref[hh:hh + 1, 0:kp], i * tq)
                m = jnp.max(s, axis=-1, keepdims=True)
                p = jnp.exp(s - m)
                l = jnp.sum(p, axis=-1, keepdims=True)
                o_ref[rows, sl] = _dot(p.astype(BF16), v_ref[0:kp, sl], NN) / l
                lse_ref[hh, rows, :] = m + jnp.log(l)

    qs, ks, vs, col, rowv = _fox_specs(T, n_pairs)
    return _pcall(name, body, (n_pairs,), [qs, ks, vs, col, rowv], [qs, col],
                  [S((T, DA), F32), S((2 * n_pairs, T, 1), F32)])(qkv, qkv, qkv, c_col, c_row)


def _fox_bwd(qkv, c_col, c_row, lse, do, name):
    T = qkv.shape[0]
    DA = qkv.shape[1] // 3
    n_pairs = DA // LANES
    tq = _tile(T, 256)
    scale = 1.0 / math.sqrt(HEAD_DIM)

    def body(q_ref, k_ref, v_ref, c_ref, ct_ref, lse_ref, do_ref, dq_ref, dk_ref, dv_ref, dcs_ref, dk_acc, dv_acc):
        dk_acc[...] = jnp.zeros_like(dk_acc)
        dv_acc[...] = jnp.zeros_like(dv_acc)
        dcs_ref[...] = jnp.zeros_like(dcs_ref)
        for hh in range(2):
            sl = slice(hh * HEAD_DIM, (hh + 1) * HEAD_DIM)
            for i in range(T // tq):
                rows = slice(i * tq, (i + 1) * tq)
                kp = (i + 1) * tq
                q = q_ref[rows, sl]
                k = k_ref[0:kp, sl]
                dob = do_ref[rows, sl]
                s = _fox_logits(q, k, c_ref[hh, rows, :], ct_ref[hh:hh + 1, 0:kp], i * tq)
                p = jnp.exp(s - lse_ref[hh, rows, :])
                dp = _dot(dob, v_ref[0:kp, sl], NT)
                ds = p * (dp - jnp.sum(p * dp, axis=-1, keepdims=True))
                dsb = ds.astype(BF16)
                dq_ref[rows, sl] = (_dot(dsb, k, NN) * scale).astype(BF16)
                dk_acc[0:kp, sl] += _dot(dsb, q, TN) * scale
                dv_acc[0:kp, sl] += _dot(p.astype(BF16), dob, TN)
                dcs_ref[hh:hh + 1, 0:kp] += _colsum(ds)
        dk_ref[...] = dk_acc[...].astype(BF16)
        dv_ref[...] = dv_acc[...].astype(BF16)

    qs, ks, vs, col, rowv = _fox_specs(T, n_pairs)
    return _pcall(name, body, (n_pairs,), [qs, ks, vs, col, rowv, col, qs], [qs, qs, qs, rowv],
                  [S((T, DA), BF16)] * 3 + [S((n_pairs, 2, T), F32)],
                  scratch=[pltpu.VMEM((T, LANES), F32), pltpu.VMEM((T, LANES), F32)])(qkv, qkv, qkv, c_col, c_row, lse, do)


def _conv_fwd(ag, w, b, name):
    T = ag.shape[0]
    DC = ag.shape[1] // 2
    nb = DC // LANES
    tr = _tile(T, 256)

    def body(a_ref, g_ref, w_ref, b_ref, y_ref, pad):
        pad[0:CONV_PAD, :] = jnp.zeros((CONV_PAD, LANES), F32)
        pad[CONV_PAD:CONV_PAD + T, :] = a_ref[...] * _sigmoid(g_ref[...])
        for r in range(T // tr):
            acc = jnp.zeros((tr, LANES), F32) + b_ref[...]
            for j in range(CONV_WIDTH):
                o = r * tr + CONV_PAD - (CONV_WIDTH - 1) + j
                acc = acc + w_ref[j:j + 1, :] * pad[o:o + tr, :]
            y_ref[r * tr:(r + 1) * tr, :] = acc

    blk = pl.BlockSpec((T, LANES), lambda c: (0, c))
    return _pcall(name, body, (nb,), [blk, pl.BlockSpec((T, LANES), lambda c: (0, nb + c)),
                                      pl.BlockSpec((CONV_PAD, LANES), lambda c: (0, c)), pl.BlockSpec((1, LANES), lambda c: (0, c))],
                  blk, S((T, DC), F32), scratch=[pltpu.VMEM((T + CONV_PAD, LANES), F32)])(ag, ag, w, b)


def _conv_bwd(dy, ag, w, name):
    T = ag.shape[0]
    DC = ag.shape[1] // 2
    nb = DC // LANES
    tr = _tile(T, 256)

    def body(dy_ref, a_ref, g_ref, w_ref, da_ref, dg_ref, dw_ref, db_ref, pad, dpad):
        av = a_ref[...]
        sg = _sigmoid(g_ref[...])
        pad[0:CONV_PAD, :] = jnp.zeros((CONV_PAD, LANES), F32)
        pad[CONV_PAD:CONV_PAD + T, :] = av * sg
        dpad[0:T, :] = dy_ref[...]
        dpad[T:T + CONV_PAD, :] = jnp.zeros((CONV_PAD, LANES), F32)
        db_ref[...] = _colsum(dy_ref[...])
        dw_ref[...] = jnp.zeros_like(dw_ref)
        for j in range(CONV_WIDTH):
            acc = jnp.zeros((tr, LANES), F32)
            for r in range(T // tr):
                o = r * tr + CONV_PAD - (CONV_WIDTH - 1) + j
                acc = acc + dpad[r * tr:(r + 1) * tr, :] * pad[o:o + tr, :]
            dw_ref[j:j + 1, :] = _colsum(acc)
        for r in range(T // tr):
            acc = jnp.zeros((tr, LANES), F32)
            for j in range(CONV_WIDTH):
                o = r * tr + (CONV_WIDTH - 1) - j
                acc = acc + w_ref[j:j + 1, :] * dpad[o:o + tr, :]
            rows = slice(r * tr, (r + 1) * tr)
            sgr = sg[rows, :]
            da_ref[rows, :] = (acc * sgr).astype(BF16)
            dg_ref[rows, :] = (acc * av[rows, :] * sgr * (1.0 - sgr)).astype(BF16)

    blk = pl.BlockSpec((T, LANES), lambda c: (0, c))
    wblk = pl.BlockSpec((CONV_PAD, LANES), lambda c: (0, c))
    return _pcall(name, body, (nb,), [blk, blk, pl.BlockSpec((T, LANES), lambda c: (0, nb + c)), wblk],
                  [blk, blk, wblk, pl.BlockSpec((1, LANES), lambda c: (0, c))],
                  [S((T, DC), BF16), S((T, DC), BF16), S((CONV_PAD, DC), F32), S((1, DC), F32)],
                  scratch=[pltpu.VMEM((T + CONV_PAD, LANES), F32), pltpu.VMEM((T + CONV_PAD, LANES), F32)])(dy, ag, ag, w)


def _conv_norms(yc, lg, lb):
    mu = jnp.mean(yc, axis=-1, keepdims=True)
    xc = yc - mu
    rs = lax.rsqrt(jnp.mean(xc * xc, axis=-1, keepdims=True) + EPS)
    xh = xc * rs
    z = xh * lg + lb
    sg = _sigmoid(z)
    return rs, xh, z, sg, z * sg


def _mix_post(attn, yc, ag, cg, lg, lb, name):
    T, DA = attn.shape
    DC = yc.shape[1]
    tm = _tile(T, ROW_TILE)

    def body(at_ref, yc_ref, ag_ref, cg_ref, lg_ref, lb_ref, y_ref):
        at = at_ref[...]
        y_ref[:, 0:DA] = (at * _rstd(at) * ag_ref[...]).astype(BF16)
        _, _, _, _, sv = _conv_norms(yc_ref[...], lg_ref[...], lb_ref[...])
        y_ref[:, DA:DA + DC] = (sv * _rstd(sv) * cg_ref[...]).astype(BF16)

    return _pcall(name, body, (T // tm,),
                  [pl.BlockSpec((tm, DA), lambda i: (i, 0)), pl.BlockSpec((tm, DC), lambda i: (i, 0)), _full((1, DA)),
                   _full((1, DC)), _full((1, DC)), _full((1, DC))],
                  pl.BlockSpec((tm, DA + DC), lambda i: (i, 0)), S((T, DA + DC), BF16))(attn, yc, ag, cg, lg, lb)


def _mix_post_bwd(dy, attn, yc, ag, cg, lg, lb, name):
    T, DA = attn.shape
    DC = yc.shape[1]
    tm = _tile(T, ROW_TILE)

    def body(dy_ref, at_ref, yc_ref, ag_ref, cg_ref, lg_ref, lb_ref, dat_ref, dyc_ref, dag_ref, dcg_ref, dlg_ref, dlb_ref):
        dat, dag_rows = _rms_bwd(dy_ref[:, 0:DA], at_ref[...], ag_ref[...])
        dat_ref[...] = dat.astype(BF16)
        lgv = lg_ref[...]
        rs, xh, z, sg, sv = _conv_norms(yc_ref[...], lgv, lb_ref[...])
        dsv, dcg_rows = _rms_bwd(dy_ref[:, DA:DA + DC], sv, cg_ref[...])
        dz = dsv * (sg * (1.0 + z * (1.0 - sg)))
        dxh = dz * lgv
        dyc_ref[...] = rs * (dxh - jnp.mean(dxh, axis=-1, keepdims=True) - xh * jnp.mean(dxh * xh, axis=-1, keepdims=True))

        @pl.when(pl.program_id(0) == 0)
        def _():
            for r in (dag_ref, dcg_ref, dlg_ref, dlb_ref):
                r[...] = jnp.zeros_like(r)

        dag_ref[...] += _colsum(dag_rows)
        dcg_ref[...] += _colsum(dcg_rows)
        dlg_ref[...] += _colsum(dz * xh)
        dlb_ref[...] += _colsum(dz)

    ra = pl.BlockSpec((tm, DA), lambda i: (i, 0))
    rc = pl.BlockSpec((tm, DC), lambda i: (i, 0))
    return _pcall(name, body, (T // tm,),
                  [pl.BlockSpec((tm, DA + DC), lambda i: (i, 0)), ra, rc, _full((1, DA)), _full((1, DC)), _full((1, DC)),
                   _full((1, DC))],
                  [ra, rc, _full((1, DA)), _full((1, DC)), _full((1, DC)), _full((1, DC))],
                  [S((T, DA), BF16), S((T, DC), F32), S((1, DA), F32), S((1, DC), F32), S((1, DC), F32), S((1, DC), F32)])(
        dy, attn, yc, ag, cg, lg, lb)


def _xattn_probs(q, k, xd):
    s = _dot(q, k, NT) * (1.0 / math.sqrt(xd))
    p = jnp.exp(s - jnp.max(s, axis=-1, keepdims=True))
    return p / jnp.sum(p, axis=-1, keepdims=True)


def _xattn_fwd(q, kv, name):
    T, D = q.shape
    M = kv.shape[0]
    xd = D // N_XATTN_HEADS
    tq = _tile(T, ROW_TILE)

    def body(q_ref, kv_ref, o_ref):
        for h in range(N_XATTN_HEADS):
            sl = slice(h * xd, (h + 1) * xd)
            p = _xattn_probs(q_ref[:, sl], kv_ref[:, sl], xd)
            o_ref[:, sl] = _dot(p.astype(BF16), kv_ref[:, D + h * xd:D + (h + 1) * xd], NN).astype(BF16)

    row = pl.BlockSpec((tq, D), lambda i: (i, 0))
    return _pcall(name, body, (T // tq,), [row, _full((M, 2 * D))], row, S((T, D), BF16))(q, kv)


def _xattn_bwd(q, kv, do, name):
    T, D = q.shape
    M = kv.shape[0]
    xd = D // N_XATTN_HEADS
    tq = _tile(T, ROW_TILE)
    scale = 1.0 / math.sqrt(xd)

    def body(q_ref, kv_ref, do_ref, dq_ref, dkv_ref):
        @pl.when(pl.program_id(0) == 0)
        def _():
            dkv_ref[...] = jnp.zeros_like(dkv_ref)

        for h in range(N_XATTN_HEADS):
            sl = slice(h * xd, (h + 1) * xd)
            vsl = slice(D + h * xd, D + (h + 1) * xd)
            qh = q_ref[:, sl]
            kh = kv_ref[:, sl]
            doh = do_ref[:, sl]
            p = _xattn_probs(qh, kh, xd)
            dp = _dot(doh, kv_ref[:, vsl], NT)
            ds = (p * (dp - jnp.sum(p * dp, axis=-1, keepdims=True)) * scale).astype(BF16)
            dq_ref[:, sl] = _dot(ds, kh, NN).astype(BF16)
            dkv_ref[:, sl] += _dot(ds, qh, TN)
            dkv_ref[:, vsl] += _dot(p.astype(BF16), doh, TN)

    row = pl.BlockSpec((tq, D), lambda i: (i, 0))
    return _pcall(name, body, (T // tq,), [row, _full((M, 2 * D)), row], [row, _full((M, 2 * D))],
                  [S((T, D), BF16), S((M, 2 * D), F32)])(q, kv, do)


def _adamw(w, m, v, g, name):
    shape = w.shape
    C = shape[-1]
    R = w.size // C
    tr = R if R <= 512 else _tile(R, 512)

    def body(w_ref, m_ref, v_ref, g_ref, d_ref, nm_ref, nv_ref):
        gv = g_ref[...]
        mv = ADAM_B1 * m_ref[...] + (1.0 - ADAM_B1) * gv
        vv = ADAM_B2 * v_ref[...] + (1.0 - ADAM_B2) * (gv * gv)
        m_hat = mv / (1.0 - ADAM_B1 ** ADAM_STEP)
        v_hat = vv / (1.0 - ADAM_B2 ** ADAM_STEP)
        d_ref[...] = -ADAM_LR * (m_hat / (jnp.sqrt(v_hat) + ADAM_EPS) + ADAM_WD * w_ref[...])
        nm_ref[...] = mv
        nv_ref[...] = vv

    blk = pl.BlockSpec((tr, C), lambda i: (i, 0))
    outs = _pcall(name, body, (R // tr,), [blk] * 4, [blk] * 3, [S((R, C), F32)] * 3)(
        w.reshape(R, C), m.reshape(R, C), v.reshape(R, C), g.reshape(R, C))
    return [o.reshape(shape) for o in outs]


def _adamw_sum(w, m, v, parts, name):
    L, p, q = w.shape
    qq = parts.shape[3]
    tr = _tile(p, 256)

    def body(w_ref, m_ref, v_ref, p_ref, g_ref, d_ref, nm_ref, nv_ref):
        gs = ((p_ref[0].astype(F32) + p_ref[1].astype(F32)) + p_ref[2].astype(F32)) + p_ref[3].astype(F32)
        gv = gs[:, 0:q]
        mv = ADAM_B1 * m_ref[...] + (1.0 - ADAM_B1) * gv
        vv = ADAM_B2 * v_ref[...] + (1.0 - ADAM_B2) * (gv * gv)
        m_hat = mv / (1.0 - ADAM_B1 ** ADAM_STEP)
        v_hat = vv / (1.0 - ADAM_B2 ** ADAM_STEP)
        g_ref[...] = gv
        d_ref[...] = -ADAM_LR * (m_hat / (jnp.sqrt(v_hat) + ADAM_EPS) + ADAM_WD * w_ref[...])
        nm_ref[...] = mv
        nv_ref[...] = vv

    blk = pl.BlockSpec((None, tr, q), lambda l, i: (l, i, 0))
    return _pcall(name, body, (L, p // tr), [blk, blk, blk, pl.BlockSpec((4, None, tr, qq), lambda l, i: (0, l, i, 0))],
                  [blk] * 4, [S((L, p, q), F32)] * 4)(w, m, v, parts)


def _pair_add(g, recv, axis, name):
    _, L, p, q = recv.shape
    my_c = lax.axis_index("c").astype(jnp.int32).reshape(1)

    def body(c_ref, g_ref, r_ref, o_ref):
        del c_ref
        o_ref[...] = (g_ref[...].astype(F32) + r_ref[...].astype(F32)).astype(BF16)

    if axis == 1:
        gspec = pl.BlockSpec((None, p, q), lambda k, l, c: (l, 2 * k + c[0], 0))
    else:
        gspec = pl.BlockSpec((None, p, q), lambda k, l, c: (l, 0, 2 * k + c[0]))
    part = pl.BlockSpec((None, None, p, q), lambda k, l, c: (k, l, 0, 0))
    gs = pltpu.PrefetchScalarGridSpec(num_scalar_prefetch=1, grid=(4, L), in_specs=[gspec, part], out_specs=part)
    return pl.pallas_call(body, grid_spec=gs, out_shape=S((4, L, p, q), BF16), name=name,
                          compiler_params=pltpu.CompilerParams(dimension_semantics=("arbitrary", "arbitrary")))(my_c, g, recv)


def _win_pieces(n_attn, n_heads, n_conv, shard, chunk):
    bounds = [0, 3 * n_attn, 3 * n_attn + n_heads, 3 * n_attn + n_heads + 2 * n_conv]
    pieces = []
    for j in range(N_DEV):
        lo, hi = shard * j, shard * (j + 1)
        for r in range(3):
            a, b = max(lo, bounds[r]), min(hi, bounds[r + 1])
            if a < b:
                pieces.append((r, a - bounds[r], b - bounds[r], chunk * j + a - lo))
    return pieces


def _win_split(w_in, pieces, widths, name):
    L, D, C = w_in.shape
    tr = _tile(D, 256)

    def body(x_ref, *outs):
        outs[1][...] = jnp.zeros_like(outs[1])
        for r, d0, d1, s0 in pieces:
            outs[r][:, d0:d1] = x_ref[:, s0:s0 + d1 - d0]

    return _pcall(name, body, (L, D // tr), [pl.BlockSpec((None, tr, C), lambda l, i: (l, i, 0))],
                  [pl.BlockSpec((None, tr, wd), lambda l, i: (l, i, 0)) for wd in widths],
                  [S((L, D, wd), BF16) for wd in widths])(w_in)


def _win_merge(parts, pieces, chunked_cols, name):
    L, D, _ = parts[0].shape
    tr = _tile(D, 256)

    def body(a_ref, b_ref, c_ref, o_ref):
        ins = (a_ref, b_ref, c_ref)
        o_ref[...] = jnp.zeros_like(o_ref)
        for r, d0, d1, s0 in pieces:
            o_ref[:, s0:s0 + d1 - d0] = ins[r][:, d0:d1]

    return _pcall(name, body, (L, D // tr), [pl.BlockSpec((None, tr, x.shape[2]), lambda l, i: (l, i, 0)) for x in parts],
                  pl.BlockSpec((None, tr, chunked_cols), lambda l, i: (l, i, 0)), S((L, D, chunked_cols), BF16))(*parts)


def _place():
    return lax.axis_index("x"), lax.axis_index("y"), lax.axis_index("c")


def _flip(v, f):
    return 1 - v if f else v


def _window(ref, axis, size, dev):
    start = dev * size if isinstance(dev, int) else pl.multiple_of(dev * size, LANES if axis == 2 else 16)
    return ref.at[:, pl.ds(start, size), :] if axis == 1 else ref.at[:, :, pl.ds(start, size)]


def _allgather_weights(shards, axes, name):
    n = len(shards)
    sizes = [s.shape[ax] for s, ax in zip(shards, axes)]
    fulls = [S(tuple(N_DEV * d if i == ax else d for i, d in enumerate(s.shape)), s.dtype) for s, ax in zip(shards, axes)]

    def body(*refs):
        xs, outs = refs[:n], refs[n:2 * n]
        send_sems, recv_sems, local_sems = refs[2 * n:]
        x, y, c = _place()
        me, sibling = (x, y, c), (x, y, 1 - c)
        chips = [(1 - x, y), (x, 1 - y), (1 - x, 1 - y)]

        def win(w, px, py, pc):
            return _window(outs[w], axes[w], sizes[w], 4 * px + 2 * py + pc)

        def copy(w, k, block, to, src=None):
            return pltpu.make_async_remote_copy(
                src_ref=win(w, *block) if src is None else src, dst_ref=win(w, *block), send_sem=send_sems.at[7 * w + k],
                recv_sem=recv_sems.at[7 * w + k], device_id=to, device_id_type=MESH)

        mines = [pltpu.make_async_copy(xs[w], win(w, *me), local_sems.at[w]) for w in range(n)]
        started = []
        for w in range(n):
            mines[w].start()
            first = [copy(w, 0, me, sibling, src=xs[w])] + [copy(w, 1 + j, me, (*chip, c), src=xs[w]) for j, chip in enumerate(chips)]
            for cp in first:
                cp.start()
            started += first
        for w in range(n):
            for j, chip in enumerate(chips):
                copy(w, 1 + j, (*chip, c), me).wait_recv()
                fwd = copy(w, 4 + j, (*chip, c), sibling)
                fwd.start()
                started.append(fwd)
        for w in range(n):
            copy(w, 0, sibling, me).wait_recv()
            for j, chip in enumerate(chips):
                copy(w, 4 + j, (*chip, 1 - c), me).wait_recv()
        for cp in started:
            cp.wait_send()
        for cp in mines:
            cp.wait()

    any_spec = pl.BlockSpec(memory_space=pl.ANY)
    return pl.pallas_call(
        body, out_shape=fulls, in_specs=[any_spec] * n, out_specs=[any_spec] * n, name=name,
        scratch_shapes=[pltpu.SemaphoreType.DMA((7 * n,)), pltpu.SemaphoreType.DMA((7 * n,)), pltpu.SemaphoreType.DMA((n,))])(*shards)


def _to_sibling(grads, axes, sizes, name):
    n = len(grads)
    outs = []
    for g, ax, sz in zip(grads, axes, sizes):
        L, K, N = g.shape
        outs.append(S((4, L, sz, N) if ax == 1 else (4, L, K, sz), g.dtype))

    def body(*refs):
        g_refs, out_refs = refs[:n], refs[n:2 * n]
        send_sems, recv_sems = refs[2 * n:]
        x, y, c = _place()
        copies = []
        for w in range(n):
            for k in range(4):
                copies.append(pltpu.make_async_remote_copy(
                    src_ref=_window(g_refs[w], axes[w], sizes[w], 2 * k + 1 - c), dst_ref=out_refs[w].at[k],
                    send_sem=send_sems.at[4 * w + k], recv_sem=recv_sems.at[4 * w + k], device_id=(x, y, 1 - c),
                    device_id_type=MESH))
        for cp in copies:
            cp.start()
        for cp in copies:
            cp.wait()

    any_spec = pl.BlockSpec(memory_space=pl.ANY)
    return pl.pallas_call(
        body, out_shape=outs, in_specs=[any_spec] * n, out_specs=[any_spec] * n, name=name,
        scratch_shapes=[pltpu.SemaphoreType.DMA((4 * n,)), pltpu.SemaphoreType.DMA((4 * n,))])(*grads)


def _to_chips(parts, name):
    n = len(parts)

    def body(*refs):
        p_refs, out_refs = refs[:n], refs[n:2 * n]
        send_sems, recv_sems, local_sems = refs[2 * n:]
        x, y, c = _place()
        me = 2 * x + y
        mines = [pltpu.make_async_copy(p_refs[w].at[me], out_refs[w].at[me], local_sems.at[w]) for w in range(n)]
        copies = []
        for w in range(n):
            mines[w].start()
            for rel in (1, 2, 3):
                tx, ty = _flip(x, rel & 2), _flip(y, rel & 1)
                copies.append(pltpu.make_async_remote_copy(
                    src_ref=p_refs[w].at[2 * tx + ty], dst_ref=out_refs[w].at[me], send_sem=send_sems.at[3 * w + rel - 1],
                    recv_sem=recv_sems.at[3 * w + rel - 1], device_id=(tx, ty, c), device_id_type=MESH))
        for cp in copies:
            cp.start()
        for cp in copies:
            cp.wait()
        for cp in mines:
            cp.wait()

    any_spec = pl.BlockSpec(memory_space=pl.ANY)
    return pl.pallas_call(
        body, out_shape=[S(p.shape, p.dtype) for p in parts], in_specs=[any_spec] * n, out_specs=[any_spec] * n, name=name,
        scratch_shapes=[pltpu.SemaphoreType.DMA((3 * n,)), pltpu.SemaphoreType.DMA((3 * n,)), pltpu.SemaphoreType.DMA((n,))])(*parts)


def _exchange_small(v, reduce, name):
    R, C = v.shape

    def body(v_ref, out_ref, gath, send_sems, recv_sems):
        x, y, c = _place()
        me = 4 * x + 2 * y + c
        buf = gath if reduce else out_ref
        buf[me] = v_ref[...]
        copies = []
        for rel in range(1, N_DEV):
            peer = (_flip(x, rel & 4), _flip(y, rel & 2), _flip(c, rel & 1))
            copies.append(pltpu.make_async_remote_copy(
                src_ref=v_ref, dst_ref=buf.at[me], send_sem=send_sems.at[rel - 1], recv_sem=recv_sems.at[rel - 1],
                device_id=peer, device_id_type=MESH))
        for cp in copies:
            cp.start()
        for cp in copies:
            cp.wait()
        if reduce:
            acc = gath[0]
            for d in range(1, N_DEV):
                acc = acc + gath[d]
            out_ref[...] = acc

    vm = pl.BlockSpec(memory_space=pltpu.VMEM)
    return pl.pallas_call(
        body, out_shape=S((R, C) if reduce else (N_DEV, R, C), F32), in_specs=[vm], out_specs=vm, name=name,
        scratch_shapes=[pltpu.VMEM((N_DEV, R, C) if reduce else (8, LANES), F32), pltpu.SemaphoreType.DMA((N_DEV - 1,)),
                        pltpu.SemaphoreType.DMA((N_DEV - 1,))])(v)


def _pad_rows(flat, cols, mult):
    n = flat.shape[-1]
    rows = -(-n // cols)
    rows = -(-rows // mult) * mult
    pad = [(0, 0)] * (flat.ndim - 1) + [(0, rows * cols - n)]
    return jnp.pad(flat, pad).reshape(flat.shape[:-1] + (rows, cols))


def _round_up(n, m):
    return -(-n // m) * m


def _shard_axes(a):
    return [(2, _round_up(a[n].shape[2], LANES)) if kind == 'col' else (1, _round_up(a[n].shape[1], LANES)) for n, kind in BIG]


def _padded_shards(a):
    out = []
    for (n, _), (ax, size) in zip(BIG, _shard_axes(a)):
        pad = [(0, 0)] * 3
        pad[ax] = (0, size - a[n].shape[ax])
        out.append(jnp.pad(a[n].astype(BF16), pad))
    return out


def _pack_small(vals):
    rows = [_pad_rows(vals[n].astype(F32).reshape(-1), SMALL_COLS, 1) for n in SMALL]
    m = jnp.concatenate(rows, axis=0)
    return jnp.pad(m, ((0, -m.shape[0] % 8), (0, 0)))


def _unpack_small(m, a):
    out, r = {}, 0
    for n in SMALL:
        nr = -(-a[n].size // SMALL_COLS)
        out[n] = m[r:r + nr].reshape(-1)[:a[n].size].reshape(a[n].shape)
        r += nr
    return out, r


def _layer_weights(W, a, conv_w_full, l):
    H = a['b_f'].shape[1]
    return dict(
        wg1=(W['ffn1_w_gate'], l), wu1=(W['ffn1_w_up'], l), wd1=(W['ffn1_w_down'], l),
        wqkv=(W['wqkv'], l), wft=W['wf'][l, :, :16].T, wag=(W['wag'], l), wout=(W['w_out'], l),
        bft=jnp.pad(a['b_f'][l].reshape(H, 1), ((0, 16 - H), (0, 0))),
        cw=jnp.pad(conv_w_full[l], ((0, CONV_PAD - CONV_WIDTH), (0, 0))), cb=a['conv_b'][l].reshape(1, -1),
        lg=a['conv_ln_g'][l].reshape(1, -1), lb=a['conv_ln_b'][l].reshape(1, -1),
        ag=a['attn_out_g'][l].reshape(1, -1), cg=a['conv_out_g'][l].reshape(1, -1),
        wq=(W['xattn_w_q'], l), wkv=(W['xattn_w_kv'], l), wo=(W['xattn_w_o'], l),
        wg2=(W['ffn2_w_gate'], l), wu2=(W['ffn2_w_up'], l), wd2=(W['ffn2_w_down'], l),
        g1=a['ffn1_norm_g'][l], gm=a['mix_norm_g'][l], gx=a['xattn_norm_g'][l], gmem=a['mem_norm_g'][l], g2=a['ffn2_norm_g'][l])


def _layer_fwd(x0, mem, w, H, l):
    T = x0.shape[0]
    sv = {'x0': x0}
    sv['h1'] = _rms_fwd(x0, w['g1'], f"l{l}_ffn1_norm")
    sv['G1'], sv['U1'], sv['A1'] = _ffn_up(sv['h1'], w['wg1'], w['wu1'], f"l{l}_ffn1_up")
    x1 = sv['x1'] = _mm_res(sv['A1'], w['wd1'], x0, 0.5, f"l{l}_ffn1_down")
    h2 = sv['h2'] = _rms_fwd(x1, w['gm'], f"l{l}_mix_norm")
    sv['qkv'] = _mm(h2, w['wqkv'], BF16, f"l{l}_qkv_proj")
    sv['agv'] = _mm(h2, w['wag'], F32, f"l{l}_glu_proj")
    ct, sv['sg'] = _fox_prep(h2, w['wft'], w['bft'], f"l{l}_fox_prep")
    sv['c_col'] = ct[:H].reshape(H, T, 1)
    sv['c_row'] = ct[:H].reshape(H // 2, 2, T)
    sv['attn'], sv['lse'] = _fox_fwd(sv['qkv'], sv['c_col'], sv['c_row'], f"l{l}_fox_fwd")
    sv['yc'] = _conv_fwd(sv['agv'], w['cw'], w['cb'], f"l{l}_conv_fwd")
    sv['ycat'] = _mix_post(sv['attn'], sv['yc'], w['ag'], w['cg'], w['lg'], w['lb'], f"l{l}_mix_post")
    x2 = sv['x2'] = _mm_res(sv['ycat'], w['wout'], x1, 1.0, f"l{l}_out_proj")
    sv['h3'] = _rms_fwd(x2, w['gx'], f"l{l}_xattn_norm")
    sv['memn'] = _rms_fwd(mem, w['gmem'], f"l{l}_mem_norm")
    sv['q'] = _mm(sv['h3'], w['wq'], BF16, f"l{l}_xattn_q")
    sv['kv'] = _mm(sv['memn'], w['wkv'], BF16, f"l{l}_xattn_kv")
    sv['o'] = _xattn_fwd(sv['q'], sv['kv'], f"l{l}_xattn_fwd")
    x3 = sv['x3'] = _mm_res(sv['o'], w['wo'], x2, 1.0, f"l{l}_xattn_out")
    sv['h4'] = _rms_fwd(x3, w['g2'], f"l{l}_ffn2_norm")
    sv['G2'], sv['U2'], sv['A2'] = _ffn_up(sv['h4'], w['wg2'], w['wu2'], f"l{l}_ffn2_up")
    return _mm_res(sv['A2'], w['wd2'], x3, 0.5, f"l{l}_ffn2_down"), sv


def _ffn_bwd(dout, x_in, h, G, U, A, wg, wu, wd, g, tag, put, which):
    dG, dU = _ffn_bwd_act(dout, wd, G, U, 0.5, tag + "_bwd_act")
    put(which + '_w_down', A, dout, 0.5, tag + "_dwd")
    put(which + '_w_gate', h, dG, 1.0, tag + "_dwg")
    put(which + '_w_up', h, dU, 1.0, tag + "_dwu")
    return _bwd_h([(dG, wg, 'nt'), (dU, wu, 'nt')], x_in, g, dout, tag + "_bwd_h")


def _layer_bwd(dx4, mem, w, sv, H, l, L, gbuf):
    small = {}
    T = dx4.shape[0]

    def put(key, act, dy, scale, name):
        gbuf[key] = _wgrad(act, dy, scale, name, (gbuf.get(key), l, L))

    dx3, small['ffn2_norm_g'] = _ffn_bwd(dx4, sv['x3'], sv['h4'], sv['G2'], sv['U2'], sv['A2'], w['wg2'], w['wu2'], w['wd2'],
                                         w['g2'], f"l{l}_ffn2", put, 'ffn2')
    do = _mm_nt(dx3, w['wo'], BF16, f"l{l}_xattn_do")
    put('xattn_w_o', sv['o'], dx3, 1.0, f"l{l}_dwo")
    dq, dkv = _xattn_bwd(sv['q'], sv['kv'], do, f"l{l}_xattn_bwd")
    put('xattn_w_q', sv['h3'], dq, 1.0, f"l{l}_dwq")
    dx2, small['xattn_norm_g'] = _bwd_h([(dq, w['wq'], 'nt')], sv['x2'], w['gx'], dx3, f"l{l}_xattn_bwd_h")
    dmemn = _mm_nt(dkv, w['wkv'], F32, f"l{l}_dmemn")
    put('xattn_w_kv', sv['memn'], dkv, 1.0, f"l{l}_dwkv")
    small['mem_norm_g'] = _rms_gain_grad(dmemn, mem, w['gmem'], f"l{l}_dgmem")
    dycat = _mm_nt(dx2, w['wout'], F32, f"l{l}_dycat")
    put('w_out', sv['ycat'], dx2, 1.0, f"l{l}_dwout")
    dattn, dyc, small['attn_out_g'], small['conv_out_g'], small['conv_ln_g'], small['conv_ln_b'] = _mix_post_bwd(
        dycat, sv['attn'], sv['yc'], w['ag'], w['cg'], w['lg'], w['lb'], f"l{l}_mix_post_bwd")
    dva, dga, dcw, small['conv_b'] = _conv_bwd(dyc, sv['agv'], w['cw'], f"l{l}_conv_bwd")
    dq_, dk_, dv_, dcs = _fox_bwd(sv['qkv'], sv['c_col'], sv['c_row'], sv['lse'], dattn, f"l{l}_fox_bwd")
    dcs16 = jnp.pad(dcs.reshape(H, T), ((0, 16 - H), (0, 0)))
    dflt, dwft, dbf = _fox_prep_bwd(dcs16, sv['sg'], sv['h2'], f"l{l}_fox_prep_bwd")
    small['b_f'] = dbf[:H].reshape(H)
    dqkv = jnp.concatenate([dq_, dk_, dv_], axis=1)
    dag = jnp.concatenate([dva, dga], axis=1)
    put('wqkv', sv['h2'], dqkv, 1.0, f"l{l}_dwqkv")
    put('wag', sv['h2'], dag, 1.0, f"l{l}_dwag")
    dx1, small['mix_norm_g'] = _bwd_h([(dqkv, w['wqkv'], 'nt'), (dag, w['wag'], 'nt'), (dflt, w['wft'], 'tn')],
                                      sv['x1'], w['gm'], dx2, f"l{l}_mix_bwd_h")
    dx0, small['ffn1_norm_g'] = _ffn_bwd(dx1, sv['x0'], sv['h1'], sv['G1'], sv['U1'], sv['A1'], w['wg1'], w['wu1'], w['wd1'],
                                         w['g1'], f"l{l}_ffn1", put, 'ffn1')
    small = {k: v.reshape(-1) for k, v in small.items()}
    return dx0, small, dwft, dcw[:CONV_WIDTH]


def _local_step(x, mem, tgt, W, a, conv_w_full):
    L, H = a['b_f'].shape
    ws = [_layer_weights(W, a, conv_w_full, l) for l in range(L)]
    saved = []
    for l in range(L):
        x, sv = _layer_fwd(x, mem, ws[l], H, l)
        saved.append(sv)
    loss, dx, dgf = _loss_head(x, a['final_norm_g'], tgt, "loss_head")
    gbuf, smalls, dwfts, dcws = {}, [None] * L, [None] * L, [None] * L
    for l in range(L - 1, -1, -1):
        dx, smalls[l], dwfts[l], dcws[l] = _layer_bwd(dx, mem, ws[l], saved[l], H, l, L, gbuf)
    small = {n: jnp.stack([smalls[l][n] for l in range(L)]) for n in SMALL if n != 'final_norm_g'}
    small['final_norm_g'] = dgf.reshape(-1)
    dwf = jnp.stack([jnp.pad(t[:H].T, ((0, 0), (0, LANES - H))) for t in dwfts]).astype(BF16)
    return loss, dx, gbuf, dwf, small, jnp.stack(dcws)


def kernel(x, mem, ffn1_norm_g, ffn1_w_gate, ffn1_w_up, ffn1_w_down, mix_norm_g, w_in, b_f, conv_w, conv_b, conv_ln_g, conv_ln_b, attn_out_g, conv_out_g, w_out, xattn_norm_g, mem_norm_g, xattn_w_q, xattn_w_kv, xattn_w_o, ffn2_norm_g, ffn2_w_gate, ffn2_w_up, ffn2_w_down, final_norm_g, loss_target, m_ffn1_norm_g, m_ffn1_w_gate, m_ffn1_w_up, m_ffn1_w_down, m_mix_norm_g, m_w_in, m_b_f, m_conv_w, m_conv_b, m_conv_ln_g, m_conv_ln_b, m_attn_out_g, m_conv_out_g, m_w_out, m_xattn_norm_g, m_mem_norm_g, m_xattn_w_q, m_xattn_w_kv, m_xattn_w_o, m_ffn2_norm_g, m_ffn2_w_gate, m_ffn2_w_up, m_ffn2_w_down, m_final_norm_g, v_ffn1_norm_g, v_ffn1_w_gate, v_ffn1_w_up, v_ffn1_w_down, v_mix_norm_g, v_w_in, v_b_f, v_conv_w, v_conv_b, v_conv_ln_g, v_conv_ln_b, v_attn_out_g, v_conv_out_g, v_w_out, v_xattn_norm_g, v_mem_norm_g, v_xattn_w_q, v_xattn_w_kv, v_xattn_w_o, v_ffn2_norm_g, v_ffn2_w_gate, v_ffn2_w_up, v_ffn2_w_down, v_final_norm_g):
    args = (x, mem, ffn1_norm_g, ffn1_w_gate, ffn1_w_up, ffn1_w_down, mix_norm_g, w_in, b_f, conv_w, conv_b, conv_ln_g, conv_ln_b, attn_out_g, conv_out_g, w_out, xattn_norm_g, mem_norm_g, xattn_w_q, xattn_w_kv, xattn_w_o, ffn2_norm_g, ffn2_w_gate, ffn2_w_up, ffn2_w_down, final_norm_g)
    moments_m = (m_ffn1_norm_g, m_ffn1_w_gate, m_ffn1_w_up, m_ffn1_w_down, m_mix_norm_g, m_w_in, m_b_f, m_conv_w, m_conv_b, m_conv_ln_g, m_conv_ln_b, m_attn_out_g, m_conv_out_g, m_w_out, m_xattn_norm_g, m_mem_norm_g, m_xattn_w_q, m_xattn_w_kv, m_xattn_w_o, m_ffn2_norm_g, m_ffn2_w_gate, m_ffn2_w_up, m_ffn2_w_down, m_final_norm_g)
    moments_v = (v_ffn1_norm_g, v_ffn1_w_gate, v_ffn1_w_up, v_ffn1_w_down, v_mix_norm_g, v_w_in, v_b_f, v_conv_w, v_conv_b, v_conv_ln_g, v_conv_ln_b, v_attn_out_g, v_conv_out_g, v_w_out, v_xattn_norm_g, v_mem_norm_g, v_xattn_w_q, v_xattn_w_kv, v_xattn_w_o, v_ffn2_norm_g, v_ffn2_w_gate, v_ffn2_w_up, v_ffn2_w_down, v_final_norm_g)
    a = dict(zip(NAMES, args))
    am = dict(zip(WEIGHTS, moments_m))
    av = dict(zip(WEIGHTS, moments_v))
    L, taps, cshard = conv_w.shape
    dev = 4 * lax.axis_index("x") + 2 * lax.axis_index("y") + lax.axis_index("c")

    big_names = [n for n, _ in BIG]
    geometry = _shard_axes(a)
    axes = [ax for ax, _ in geometry]
    sizes = [sz for _, sz in geometry]
    W = dict(zip(big_names, _allgather_weights(_padded_shards(a), axes, "allgather_weights")))
    n_attn, n_heads, n_conv = attn_out_g.shape[1], b_f.shape[1], conv_out_g.shape[1]
    chunk = sizes[big_names.index('w_in')]
    pieces = _win_pieces(n_attn, n_heads, n_conv, w_in.shape[2], chunk)
    W['wqkv'], W['wf'], W['wag'] = _win_split(W.pop('w_in'), pieces, (3 * n_attn, LANES, 2 * n_conv), "w_in_split")
    cw_rows = _pad_rows(conv_w.reshape(-1), LANES, 8)
    cw_all = _exchange_small(cw_rows, False, "allgather_conv_w").reshape(N_DEV, -1)[:, :conv_w.size]
    conv_w_full = cw_all.reshape(N_DEV, L, taps, cshard).transpose(1, 2, 0, 3).reshape(L, taps, N_DEV * cshard)

    loss, grad_x, gbuf, dwf, gsmall, dcw = _local_step(x[0], mem[0], loss_target[0], W, a, conv_w_full)

    gbuf['w_in'] = _win_merge((gbuf.pop('wqkv'), dwf, gbuf.pop('wag')), pieces, N_DEV * chunk, "w_in_merge")
    gfull = [gbuf[n] for n in big_names]
    recv = _to_sibling(gfull, axes, sizes, "reduce_to_sibling")
    chip_parts = [_pair_add(g, r, ax, "reduce_pair_add_" + n) for g, r, ax, n in zip(gfull, recv, axes, big_names)]
    by_chip = _to_chips(chip_parts, "reduce_to_chips")

    small_rows = _pack_small(gsmall)
    n_small = small_rows.shape[0]
    dcw_rows = jnp.pad(dcw, ((0, 0), (0, CONV_PAD - taps), (0, 0))).reshape(-1, SMALL_COLS)
    summed = _exchange_small(jnp.concatenate([small_rows, dcw_rows], axis=0), True, "allreduce_small")
    g_small, _ = _unpack_small(summed[:n_small], a)
    dcw_sum = summed[n_small:].reshape(L, CONV_PAD, N_DEV * cshard)[:, :taps]

    grads = dict(g_small)
    grads['conv_w'] = lax.dynamic_slice_in_dim(dcw_sum, dev * cshard, cshard, axis=2)

    delta, new_m, new_v = {}, {}, {}
    for n, parts in zip(big_names, by_chip):
        grads[n], delta[n], new_m[n], new_v[n] = _adamw_sum(a[n], am[n], av[n], parts, "adamw_" + n)
    delta['conv_w'], new_m['conv_w'], new_v['conv_w'] = _adamw(conv_w, am['conv_w'], av['conv_w'], grads['conv_w'], "adamw_conv_w")
    pw, pm, pv, pg = (_pack_small(d) for d in (a, am, av, g_small))
    for dst, packed in zip((delta, new_m, new_v), _adamw(pw, pm, pv, pg, "adamw_small")):
        dst.update(_unpack_small(packed, a)[0])

    total = lax.psum(loss.reshape(()), ("x", "y", "c"))
    return (total, grad_x[None], *[grads[n] for n in WEIGHTS], *[delta[n] for n in WEIGHTS], *[new_m[n] for n in WEIGHTS],
            *[new_v[n] for n in WEIGHTS])
```

```python
import math

import jax
import jax.numpy as jnp
from jax import lax
from jax.experimental import pallas as pl
from jax.experimental.pallas import tpu as pltpu

F32, BF16 = jnp.float32, jnp.bfloat16
S = jax.ShapeDtypeStruct
MESH = pl.DeviceIdType.MESH

EPS = 1e-6
NEG_INF = -1e30
HEAD_DIM = 64
N_XATTN_HEADS = 4
CONV_WIDTH = 31
CONV_PAD = 32
LANES = 128
ADAM_LR, ADAM_B1, ADAM_B2, ADAM_EPS, ADAM_WD, ADAM_STEP = 0.001, 0.9, 0.999, 1e-08, 0.01, 10
N_DEV = 8
VMEM_LIMIT_BYTES = 56 * 1024 * 1024
ROW_TILE = 512
SMALL_COLS = 512

NN = ((1,), (0,))
NT = ((1,), (1,))
TN = ((0,), (0,))

NAMES = ['x', 'mem', 'ffn1_norm_g', 'ffn1_w_gate', 'ffn1_w_up', 'ffn1_w_down', 'mix_norm_g', 'w_in', 'b_f', 'conv_w', 'conv_b',
         'conv_ln_g', 'conv_ln_b', 'attn_out_g', 'conv_out_g', 'w_out', 'xattn_norm_g', 'mem_norm_g', 'xattn_w_q', 'xattn_w_kv',
         'xattn_w_o', 'ffn2_norm_g', 'ffn2_w_gate', 'ffn2_w_up', 'ffn2_w_down', 'final_norm_g']
WEIGHTS = NAMES[2:]
BIG = [('ffn1_w_gate', 'col'), ('ffn1_w_up', 'col'), ('ffn1_w_down', 'row'), ('w_in', 'col'), ('w_out', 'row'),
       ('xattn_w_q', 'row'), ('xattn_w_kv', 'col'), ('xattn_w_o', 'row'), ('ffn2_w_gate', 'col'), ('ffn2_w_up', 'col'),
       ('ffn2_w_down', 'row')]
SMALL = ['ffn1_norm_g', 'mix_norm_g', 'xattn_norm_g', 'mem_norm_g', 'ffn2_norm_g', 'conv_b', 'conv_ln_g', 'conv_ln_b',
         'attn_out_g', 'conv_out_g', 'b_f', 'final_norm_g']


def _dot(a, b, dims):
    return lax.dot_general(a, b, (dims, ((), ())), preferred_element_type=F32)


def _full(shape):
    nd = len(shape)
    return pl.BlockSpec(shape, lambda *_: (0,) * nd)


def _tile(n, pref):
    for t in (pref, 512, 384, 256, 128, 64, 32, 16, 8):
        if t <= n and n % t == 0:
            return t
    return n


def _pcall(name, body, grid, in_specs, out_specs, out_shape, scratch=(), aliases=None, dep=None):
    n_in = len(in_specs)
    kernel_body = body
    if dep is not None:
        in_specs = list(in_specs) + [pl.BlockSpec(memory_space=pl.ANY)]

        def kernel_body(*refs):
            return body(*refs[:n_in], *refs[n_in + 1:])

    call = pl.pallas_call(
        kernel_body, grid=grid, in_specs=in_specs, out_specs=out_specs, out_shape=out_shape, scratch_shapes=list(scratch),
        name=name, input_output_aliases=aliases or {},
        compiler_params=pltpu.CompilerParams(dimension_semantics=("arbitrary",) * len(grid), vmem_limit_bytes=VMEM_LIMIT_BYTES))
    return call if dep is None else (lambda *args: call(*args, dep))


def _arr(w):
    return w[0] if isinstance(w, tuple) else w


def _wshape(w):
    return w[0].shape[1:] if isinstance(w, tuple) else w.shape


def _wspec(w, block, imap):
    if isinstance(w, tuple):
        layer = w[1]
        return pl.BlockSpec((None,) + block, lambda *g: (layer,) + imap(*g))
    return pl.BlockSpec(block, imap)


def _wfull(w):
    shape = _wshape(w)
    return _wspec(w, shape, lambda *_: (0,) * len(shape))


def _sigmoid(z):
    return jax.nn.sigmoid(z)


def _rstd(x):
    return lax.rsqrt(jnp.mean(x * x, axis=-1, keepdims=True) + EPS)


def _rms_bwd(dy, x, g):
    r = _rstd(x)
    xh = x * r
    u = dy * g
    dx = r * (u - xh * jnp.mean(u * xh, axis=-1, keepdims=True))
    return dx, dy * xh


def _colsum(v):
    return jnp.sum(v, axis=0, keepdims=True)


def _rms_fwd(x, g, name):
    T, D = x.shape
    tm = _tile(T, ROW_TILE)

    def body(x_ref, g_ref, h_ref):
        xv = x_ref[...]
        h_ref[...] = (xv * _rstd(xv) * g_ref[...]).astype(BF16)

    row = pl.BlockSpec((tm, D), lambda i: (i, 0))
    return _pcall(name, body, (T // tm,), [row, _full((1, D))], row, S((T, D), BF16))(x, g.reshape(1, D))


def _mm(a, w, out_dtype, name):
    M, K = a.shape
    N = _wshape(w)[1]
    tm = _tile(M, ROW_TILE)
    tn = N if N <= 1536 else N // 2

    def body(a_ref, w_ref, o_ref):
        o_ref[...] = _dot(a_ref[...], w_ref[...], NN).astype(out_dtype)

    return _pcall(name, body, (N // tn, M // tm),
                  [pl.BlockSpec((tm, K), lambda j, i: (i, 0)), _wspec(w, (K, tn), lambda j, i: (0, j))],
                  pl.BlockSpec((tm, tn), lambda j, i: (i, j)), S((M, N), out_dtype))(a, _arr(w))


def _mm_res(a, w, res, scale, name):
    M, K = a.shape
    N = _wshape(w)[1]
    tm = _tile(M, ROW_TILE)

    def body(a_ref, w_ref, r_ref, o_ref):
        o_ref[...] = r_ref[...] + scale * _dot(a_ref[...], w_ref[...], NN)

    row = pl.BlockSpec((tm, N), lambda i: (i, 0))
    return _pcall(name, body, (M // tm,), [pl.BlockSpec((tm, K), lambda i: (i, 0)), _wfull(w), row], row,
                  S((M, N), F32))(a, _arr(w), res)


def _mm_nt(a, w, out_dtype, name, dep=None):
    M, K = a.shape
    N = _wshape(w)[0]
    tm = _tile(M, ROW_TILE)
    tn = N if N <= 1536 else N // 2

    def body(a_ref, w_ref, o_ref):
        o_ref[...] = _dot(a_ref[...].astype(BF16), w_ref[...], NT).astype(out_dtype)

    return _pcall(name, body, (N // tn, M // tm),
                  [pl.BlockSpec((tm, K), lambda j, i: (i, 0)), _wspec(w, (tn, K), lambda j, i: (j, 0))],
                  pl.BlockSpec((tm, tn), lambda j, i: (i, j)), S((M, N), out_dtype), dep=dep)(a, _arr(w))


def _wgrad(a, dy, scale, name, into):
    buf, layer, L = into
    T, M = a.shape
    N = dy.shape[1]
    tm = _tile(M, 256)

    def body(a_ref, dy_ref, *rest):
        rest[-1][...] = (scale * _dot(a_ref[...], dy_ref[...].astype(BF16), TN)).astype(BF16)

    in_specs = [pl.BlockSpec((T, tm), lambda i: (0, i)), _full((T, N))]
    args = [a, dy]
    if buf is not None:
        in_specs.append(pl.BlockSpec(memory_space=pl.ANY))
        args.append(buf)
    return _pcall(name, body, (M // tm,), in_specs, pl.BlockSpec((None, tm, N), lambda i: (layer, i, 0)), S((L, M, N), BF16),
                  aliases={2: 0} if buf is not None else None)(*args)


def _bwd_h(dots, x, g, dres, name):
    T, D = x.shape
    tm = _tile(T, 256)
    n = len(dots)
    modes = [m for _, _, m in dots]

    def body(*refs):
        x_ref, g_ref, r_ref, dx_ref, dg_ref = refs[2 * n:]
        dh = None
        for k in range(n):
            part = _dot(refs[2 * k][...], refs[2 * k + 1][...], NT if modes[k] == 'nt' else TN)
            dh = part if dh is None else dh + part
        dx, dgrow = _rms_bwd(dh, x_ref[...], g_ref[...])
        dx_ref[...] = r_ref[...] + dx

        @pl.when(pl.program_id(0) == 0)
        def _():
            dg_ref[...] = jnp.zeros_like(dg_ref)

        dg_ref[...] += _colsum(dgrow)

    in_specs, args = [], []
    for lhs, w, mode in dots:
        if mode == 'nt':
            in_specs.append(pl.BlockSpec((tm, lhs.shape[1]), lambda i: (i, 0)))
        else:
            in_specs.append(pl.BlockSpec((lhs.shape[0], tm), lambda i: (0, i)))
        in_specs.append(_wfull(w))
        args += [lhs, _arr(w)]
    row = pl.BlockSpec((tm, D), lambda i: (i, 0))
    in_specs += [row, _full((1, D)), row]
    return _pcall(name, body, (T // tm,), in_specs, [row, _full((1, D))], [S((T, D), F32), S((1, D), F32)])(
        *args, x, g.reshape(1, D), dres)


def _rms_gain_grad(dy, x, g, name):
    T, D = x.shape

    def body(dy_ref, x_ref, g_ref, dg_ref):
        _, dgrow = _rms_bwd(dy_ref[...], x_ref[...], g_ref[...])
        dg_ref[...] = _colsum(dgrow)

    return _pcall(name, body, (), [_full((T, D)), _full((T, D)), _full((1, D))], _full((1, D)), S((1, D), F32))(
        dy, x, g.reshape(1, D))


def _ffn_up(h, wg, wu, name):
    T, D = h.shape
    Fh = _wshape(wg)[1]
    tm = _tile(T, ROW_TILE)
    tn = Fh if Fh <= 1536 else Fh // 2

    def body(h_ref, wg_ref, wu_ref, g_ref, u_ref, a_ref):
        hv = h_ref[...]
        gv = _dot(hv, wg_ref[...], NN)
        uv = _dot(hv, wu_ref[...], NN)
        g_ref[...] = gv.astype(BF16)
        u_ref[...] = uv.astype(BF16)
        a_ref[...] = (gv * _sigmoid(gv) * uv).astype(BF16)

    tile = pl.BlockSpec((tm, tn), lambda j, i: (i, j))
    return _pcall(name, body, (Fh // tn, T // tm),
                  [pl.BlockSpec((tm, D), lambda j, i: (i, 0)), _wspec(wg, (D, tn), lambda j, i: (0, j)),
                   _wspec(wu, (D, tn), lambda j, i: (0, j))],
                  [tile, tile, tile], [S((T, Fh), BF16)] * 3)(h, _arr(wg), _arr(wu))


def _ffn_bwd_act(dout, wd, gate, up, scale, name, dep=None):
    T, D = dout.shape
    Fh = _wshape(wd)[0]
    tm = _tile(T, ROW_TILE)
    tn = Fh if Fh <= 1536 else Fh // 2

    def body(d_ref, w_ref, g_ref, u_ref, dg_ref, du_ref):
        da = scale * _dot(d_ref[...].astype(BF16), w_ref[...], NT)
        gv = g_ref[...].astype(F32)
        uv = u_ref[...].astype(F32)
        sg = _sigmoid(gv)
        dg_ref[...] = (da * uv * (sg * (1.0 + gv * (1.0 - sg)))).astype(BF16)
        du_ref[...] = (da * (gv * sg)).astype(BF16)

    tile = pl.BlockSpec((tm, tn), lambda j, i: (i, j))
    return _pcall(name, body, (Fh // tn, T // tm),
                  [pl.BlockSpec((tm, D), lambda j, i: (i, 0)), _wspec(wd, (tn, D), lambda j, i: (j, 0)), tile, tile],
                  [tile, tile], [S((T, Fh), BF16)] * 2, dep=dep)(dout, _arr(wd), gate, up)


def _loss_head(x, g, tgt, name):
    T, D = x.shape
    tm = _tile(T, ROW_TILE)

    def body(x_ref, g_ref, t_ref, loss_ref, dx_ref, dg_ref):
        xv = x_ref[...]
        gv = g_ref[...]
        r = _rstd(xv)
        xh = xv * r
        e = xh * gv - t_ref[...]
        dy = e * (1.0 / D)
        u = dy * gv
        dx_ref[...] = r * (u - xh * jnp.mean(u * xh, axis=-1, keepdims=True))

        @pl.when(pl.program_id(0) == 0)
        def _():
            dg_ref[...] = jnp.zeros_like(dg_ref)
            loss_ref[...] = jnp.zeros_like(loss_ref)

        dg_ref[...] += _colsum(dy * xh)
        loss_ref[...] += 0.5 * _colsum(jnp.mean(e * e, axis=-1, keepdims=True))

    row = pl.BlockSpec((tm, D), lambda i: (i, 0))
    return _pcall(name, body, (T // tm,), [row, _full((1, D)), row], [_full((1, 1)), row, _full((1, D))],
                  [S((1, 1), F32), S((T, D), F32), S((1, D), F32)])(x, g.reshape(1, D), tgt)


def _split3(xb):
    hi = xb.astype(BF16)
    r1 = xb - hi.astype(F32)
    mid = r1.astype(BF16)
    lo = (r1 - mid.astype(F32)).astype(BF16)
    return hi, mid, lo


def _fox_prep(h, wft, bft, name):
    T, D = h.shape
    blk = _tile(T, 256)

    def body(h_ref, w_ref, b_ref, ct_ref, sg_ref):
        z = _dot(w_ref[...], h_ref[...], NT) + b_ref[...]
        sg_ref[...] = 1.0 - _sigmoid(z)
        logf = jnp.minimum(z, 0.0) - jnp.log1p(jnp.exp(-jnp.abs(z)))
        upper = (lax.broadcasted_iota(jnp.int32, (blk, blk), 0) <= lax.broadcasted_iota(jnp.int32, (blk, blk), 1)).astype(BF16)
        carry = jnp.zeros((16, 1), F32)
        for b in range(T // blk):
            hi, mid, lo = _split3(logf[:, b * blk:(b + 1) * blk])
            cb = _dot(hi, upper, NN) + _dot(mid, upper, NN) + _dot(lo, upper, NN) + carry
            ct_ref[:, b * blk:(b + 1) * blk] = cb
            carry = cb[:, blk - 1:blk]

    return _pcall(name, body, (), [_full((T, D)), _full((16, D)), _full((16, 1))], [_full((16, T)), _full((16, T))],
                  [S((16, T), F32), S((16, T), F32)])(h, wft, bft)


def _fox_prep_bwd(dcs, sg, h, name):
    T, D = h.shape
    blk = _tile(T, 256)
    nb = T // blk

    def body(dcs_ref, sg_ref, h_ref, dfl_ref, dw_ref, db_ref):
        lower = (lax.broadcasted_iota(jnp.int32, (blk, blk), 0) >= lax.broadcasted_iota(jnp.int32, (blk, blk), 1)).astype(BF16)
        carry = jnp.zeros((16, 1), F32)
        db = jnp.zeros((16, 1), F32)
        for b in range(nb - 1, -1, -1):
            cols = slice(b * blk, (b + 1) * blk)
            hi, mid, lo = _split3(-dcs_ref[:, cols])
            dlogf = _dot(hi, lower, NN) + _dot(mid, lower, NN) + _dot(lo, lower, NN) + carry
            carry = dlogf[:, 0:1]
            dfl = dlogf * sg_ref[:, cols]
            db = db + jnp.sum(dfl, axis=-1, keepdims=True)
            dfl_ref[:, cols] = dfl.astype(BF16)
        db_ref[...] = db
        dw_ref[...] = _dot(dfl_ref[...], h_ref[...], NN)

    return _pcall(name, body, (), [_full((16, T)), _full((16, T)), _full((T, D))],
                  [_full((16, T)), _full((16, D)), _full((16, 1))],
                  [S((16, T), BF16), S((16, D), F32), S((16, 1), F32)])(dcs, sg, h)


def _fox_logits(q, k, c_col, c_row, row0):
    s = _dot(q, k, NT) * (1.0 / math.sqrt(HEAD_DIM)) + (c_col - c_row)
    row = lax.broadcasted_iota(jnp.int32, s.shape, 0) + row0
    col = lax.broadcasted_iota(jnp.int32, s.shape, 1)
    return jnp.where(row >= col, s, NEG_INF)


def _fox_specs(T, n_pairs):
    qs = pl.BlockSpec((T, LANES), lambda p: (0, p))
    ks = pl.BlockSpec((T, LANES), lambda p: (0, n_pairs + p))
    vs = pl.BlockSpec((T, LANES), lambda p: (0, 2 * n_pairs + p))
    col = pl.BlockSpec((2, T, 1), lambda p: (p, 0, 0))
    rowv = pl.BlockSpec((None, 2, T), lambda p: (p, 0, 0))
    return qs, ks, vs, col, rowv


def _fox_fwd(qkv, c_col, c_row, name):
    T = qkv.shape[0]
    DA = qkv.shape[1] // 3
    n_pairs = DA // LANES
    tq = _tile(T, 256)

    def body(q_ref, k_ref, v_ref, c_ref, ct_ref, o_ref, lse_ref):
        for hh in range(2):
            sl = slice(hh * HEAD_DIM, (hh + 1) * HEAD_DIM)
            for i in range(T // tq):
                rows = slice(i * tq, (i + 1) * tq)
                kp = (i + 1) * tq
                s = _fox_logits(q_ref[rows, sl], k_ref[0:kp, sl], c_ref[hh, rows, :], ct_ref[hh:hh + 1, 0:kp], i * tq)
                m = jnp.max(s, axis=-1, keepdims=True)
                p = jnp.exp(s - m)
                l = jnp.sum(p, axis=-1, keepdims=True)
                o_ref[rows, sl] = _dot(p.astype(BF16), v_ref[0:kp, sl], NN) / l
                lse_ref[hh, rows, :] = m + jnp.log(l)

    qs, ks, vs, col, rowv = _fox_specs(T, n_pairs)
    return _pcall(name, body, (n_pairs,), [qs, ks, vs, col, rowv], [qs, col],
                  [S((T, DA), F32), S((2 * n_pairs, T, 1), F32)])(qkv, qkv, qkv, c_col, c_row)


def _fox_bwd(qkv, c_col, c_row, lse, do, name):
    T = qkv.shape[0]
    DA = qkv.shape[1] // 3
    n_pairs = DA // LANES
    tq = _tile(T, 256)
    scale = 1.0 / math.sqrt(HEAD_DIM)

    def body(q_ref, k_ref, v_ref, c_ref, ct_ref, lse_ref, do_ref, dq_ref, dk_ref, dv_ref, dcs_ref, dk_acc, dv_acc):
        dk_acc[...] = jnp.zeros_like(dk_acc)
        dv_acc[...] = jnp.zeros_like(dv_acc)
        dcs_ref[...] = jnp.zeros_like(dcs_ref)
        for hh in range(2):
            sl = slice(hh * HEAD_DIM, (hh + 1) * HEAD_DIM)
            for i in range(T // tq):
                rows = slice(i * tq, (i + 1) * tq)
                kp = (i + 1) * tq
                q = q_ref[rows, sl]
                k = k_ref[0:kp, sl]
                dob = do_ref[rows, sl]
                s = _fox_logits(q, k, c_ref[hh, rows, :], ct_ref[hh:hh + 1, 0:kp], i * tq)
                p = jnp.exp(s - lse_ref[hh, rows, :])
                dp = _dot(dob, v_ref[0:kp, sl], NT)
                ds = p * (dp - jnp.sum(p * dp, axis=-1, keepdims=True))
                dsb = ds.astype(BF16)
                dq_ref[rows, sl] = (_dot(dsb, k, NN) * scale).astype(BF16)
                dk_acc[0:kp, sl] += _dot(dsb, q, TN) * scale
                dv_acc[0:kp, sl] += _dot(p.astype(BF16), dob, TN)
                dcs_ref[hh:hh + 1, 0:kp] += _colsum(ds)
        dk_ref[...] = dk_acc[...].astype(BF16)
        dv_ref[...] = dv_acc[...].astype(BF16)

    qs, ks, vs, col, rowv = _fox_specs(T, n_pairs)
    return _pcall(name, body, (n_pairs,), [qs, ks, vs, col, rowv, col, qs], [qs, qs, qs, rowv],
                  [S((T, DA), BF16)] * 3 + [S((n_pairs, 2, T), F32)],
                  scratch=[pltpu.VMEM((T, LANES), F32), pltpu.VMEM((T, LANES), F32)])(qkv, qkv, qkv, c_col, c_row, lse, do)


def _conv_fwd(ag, w, b, name):
    T = ag.shape[0]
    DC = ag.shape[1] // 2
    nb = DC // LANES
    tr = _tile(T, 256)

    def body(a_ref, g_ref, w_ref, b_ref, y_ref, pad):
        pad[0:CONV_PAD, :] = jnp.zeros((CONV_PAD, LANES), F32)
        pad[CONV_PAD:CONV_PAD + T, :] = a_ref[...] * _sigmoid(g_ref[...])
        for r in range(T // tr):
            acc = jnp.zeros((tr, LANES), F32) + b_ref[...]
            for j in range(CONV_WIDTH):
                o = r * tr + CONV_PAD - (CONV_WIDTH - 1) + j
                acc = acc + w_ref[j:j + 1, :] * pad[o:o + tr, :]
            y_ref[r * tr:(r + 1) * tr, :] = acc

    blk = pl.BlockSpec((T, LANES), lambda c: (0, c))
    return _pcall(name, body, (nb,), [blk, pl.BlockSpec((T, LANES), lambda c: (0, nb + c)),
                                      pl.BlockSpec((CONV_PAD, LANES), lambda c: (0, c)), pl.BlockSpec((1, LANES), lambda c: (0, c))],
                  blk, S((T, DC), F32), scratch=[pltpu.VMEM((T + CONV_PAD, LANES), F32)])(ag, ag, w, b)


def _conv_bwd(dy, ag, w, name):
    T = ag.shape[0]
    DC = ag.shape[1] // 2
    nb = DC // LANES
    tr = _tile(T, 256)

    def body(dy_ref, a_ref, g_ref, w_ref, da_ref, dg_ref, dw_ref, db_ref, pad, dpad):
        av = a_ref[...]
        sg = _sigmoid(g_ref[...])
        pad[0:CONV_PAD, :] = jnp.zeros((CONV_PAD, LANES), F32)
        pad[CONV_PAD:CONV_PAD + T, :] = av * sg
        dpad[0:T, :] = dy_ref[...]
        dpad[T:T + CONV_PAD, :] = jnp.zeros((CONV_PAD, LANES), F32)
        db_ref[...] = _colsum(dy_ref[...])
        dw_ref[...] = jnp.zeros_like(dw_ref)
        for j in range(CONV_WIDTH):
            acc = jnp.zeros((tr, LANES), F32)
            for r in range(T // tr):
                o = r * tr + CONV_PAD - (CONV_WIDTH - 1) + j
                acc = acc + dpad[r * tr:(r + 1) * tr, :] * pad[o:o + tr, :]
            dw_ref[j:j + 1, :] = _colsum(acc)
        for r in range(T // tr):
            acc = jnp.zeros((tr, LANES), F32)
            for j in range(CONV_WIDTH):
                o = r * tr + (CONV_WIDTH - 1) - j
                acc = acc + w_ref[j:j + 1, :] * dpad[o:o + tr, :]
            rows = slice(r * tr, (r + 1) * tr)
            sgr = sg[rows, :]
            da_ref[rows, :] = (acc * sgr).astype(BF16)
            dg_ref[rows, :] = (acc * av[rows, :] * sgr * (1.0 - sgr)).astype(BF16)

    blk = pl.BlockSpec((T, LANES), lambda c: (0, c))
    wblk = pl.BlockSpec((CONV_PAD, LANES), lambda c: (0, c))
    return _pcall(name, body, (nb,), [blk, blk, pl.BlockSpec((T, LANES), lambda c: (0, nb + c)), wblk],
                  [blk, blk, wblk, pl.BlockSpec((1, LANES), lambda c: (0, c))],
                  [S((T, DC), BF16), S((T, DC), BF16), S((CONV_PAD, DC), F32), S((1, DC), F32)],
                  scratch=[pltpu.VMEM((T + CONV_PAD, LANES), F32), pltpu.VMEM((T + CONV_PAD, LANES), F32)])(dy, ag, ag, w)


def _conv_norms(yc, lg, lb):
    mu = jnp.mean(yc, axis=-1, keepdims=True)
    xc = yc - mu
    rs = lax.rsqrt(jnp.mean(xc * xc, axis=-1, keepdims=True) + EPS)
    xh = xc * rs
    z = xh * lg + lb
    sg = _sigmoid(z)
    return rs, xh, z, sg, z * sg


def _mix_post(attn, yc, ag, cg, lg, lb, name):
    T, DA = attn.shape
    DC = yc.shape[1]
    tm = _tile(T, ROW_TILE)

    def body(at_ref, yc_ref, ag_ref, cg_ref, lg_ref, lb_ref, y_ref):
        at = at_ref[...]
        y_ref[:, 0:DA] = (at * _rstd(at) * ag_ref[...]).astype(BF16)
        _, _, _, _, sv = _conv_norms(yc_ref[...], lg_ref[...], lb_ref[...])
        y_ref[:, DA:DA + DC] = (sv * _rstd(sv) * cg_ref[...]).astype(BF16)

    return _pcall(name, body, (T // tm,),
                  [pl.BlockSpec((tm, DA), lambda i: (i, 0)), pl.BlockSpec((tm, DC), lambda i: (i, 0)), _full((1, DA)),
                   _full((1, DC)), _full((1, DC)), _full((1, DC))],
                  pl.BlockSpec((tm, DA + DC), lambda i: (i, 0)), S((T, DA + DC), BF16))(attn, yc, ag, cg, lg, lb)


def _mix_post_bwd(dy, attn, yc, ag, cg, lg, lb, name):
    T, DA = attn.shape
    DC = yc.shape[1]
    tm = _tile(T, ROW_TILE)

    def body(dy_ref, at_ref, yc_ref, ag_ref, cg_ref, lg_ref, lb_ref, dat_ref, dyc_ref, dag_ref, dcg_ref, dlg_ref, dlb_ref):
        dat, dag_rows = _rms_bwd(dy_ref[:, 0:DA], at_ref[...], ag_ref[...])
        dat_ref[...] = dat.astype(BF16)
        lgv = lg_ref[...]
        rs, xh, z, sg, sv = _conv_norms(yc_ref[...], lgv, lb_ref[...])
        dsv, dcg_rows = _rms_bwd(dy_ref[:, DA:DA + DC], sv, cg_ref[...])
        dz = dsv * (sg * (1.0 + z * (1.0 - sg)))
        dxh = dz * lgv
        dyc_ref[...] = rs * (dxh - jnp.mean(dxh, axis=-1, keepdims=True) - xh * jnp.mean(dxh * xh, axis=-1, keepdims=True))

        @pl.when(pl.program_id(0) == 0)
        def _():
            for r in (dag_ref, dcg_ref, dlg_ref, dlb_ref):
                r[...] = jnp.zeros_like(r)

        dag_ref[...] += _colsum(dag_rows)
        dcg_ref[...] += _colsum(dcg_rows)
        dlg_ref[...] += _colsum(dz * xh)
        dlb_ref[...] += _colsum(dz)

    ra = pl.BlockSpec((tm, DA), lambda i: (i, 0))
    rc = pl.BlockSpec((tm, DC), lambda i: (i, 0))
    return _pcall(name, body, (T // tm,),
                  [pl.BlockSpec((tm, DA + DC), lambda i: (i, 0)), ra, rc, _full((1, DA)), _full((1, DC)), _full((1, DC)),
                   _full((1, DC))],
                  [ra, rc, _full((1, DA)), _full((1, DC)), _full((1, DC)), _full((1, DC))],
                  [S((T, DA), BF16), S((T, DC), F32), S((1, DA), F32), S((1, DC), F32), S((1, DC), F32), S((1, DC), F32)])(
        dy, attn, yc, ag, cg, lg, lb)


def _xattn_probs(q, k, xd):
    s = _dot(q, k, NT) * (1.0 / math.sqrt(xd))
    p = jnp.exp(s - jnp.max(s, axis=-1, keepdims=True))
    return p / jnp.sum(p, axis=-1, keepdims=True)


def _xattn_fwd(q, kv, name):
    T, D = q.shape
    M = kv.shape[0]
    xd = D // N_XATTN_HEADS
    tq = _tile(T, ROW_TILE)

    def body(q_ref, kv_ref, o_ref):
        for h in range(N_XATTN_HEADS):
            sl = slice(h * xd, (h + 1) * xd)
            p = _xattn_probs(q_ref[:, sl], kv_ref[:, sl], xd)
            o_ref[:, sl] = _dot(p.astype(BF16), kv_ref[:, D + h * xd:D + (h + 1) * xd], NN).astype(BF16)

    row = pl.BlockSpec((tq, D), lambda i: (i, 0))
    return _pcall(name, body, (T // tq,), [row, _full((M, 2 * D))], row, S((T, D), BF16))(q, kv)


def _xattn_bwd(q, kv, do, name):
    T, D = q.shape
    M = kv.shape[0]
    xd = D // N_XATTN_HEADS
    tq = _tile(T, ROW_TILE)
    scale = 1.0 / math.sqrt(xd)

    def body(q_ref, kv_ref, do_ref, dq_ref, dkv_ref):
        @pl.when(pl.program_id(0) == 0)
        def _():
            dkv_ref[...] = jnp.zeros_like(dkv_ref)

        for h in range(N_XATTN_HEADS):
            sl = slice(h * xd, (h + 1) * xd)
            vsl = slice(D + h * xd, D + (h + 1) * xd)
            qh = q_ref[:, sl]
            kh = kv_ref[:, sl]
            doh = do_ref[:, sl]
            p = _xattn_probs(qh, kh, xd)
            dp = _dot(doh, kv_ref[:, vsl], NT)
            ds = (p * (dp - jnp.sum(p * dp, axis=-1, keepdims=True)) * scale).astype(BF16)
            dq_ref[:, sl] = _dot(ds, kh, NN).astype(BF16)
            dkv_ref[:, sl] += _dot(ds, qh, TN)
            dkv_ref[:, vsl] += _dot(p.astype(BF16), doh, TN)

    row = pl.BlockSpec((tq, D), lambda i: (i, 0))
    return _pcall(name, body, (T // tq,), [row, _full((M, 2 * D)), row], [row, _full((M, 2 * D))],
                  [S((T, D), BF16), S((M, 2 * D), F32)])(q, kv, do)


def _adamw(w, m, v, g, name):
    shape = w.shape
    C = shape[-1]
    R = w.size // C
    tr = R if R <= 512 else _tile(R, 512)

    def body(w_ref, m_ref, v_ref, g_ref, d_ref, nm_ref, nv_ref):
        gv = g_ref[...]
        mv = ADAM_B1 * m_ref[...] + (1.0 - ADAM_B1) * gv
        vv = ADAM_B2 * v_ref[...] + (1.0 - ADAM_B2) * (gv * gv)
        m_hat = mv / (1.0 - ADAM_B1 ** ADAM_STEP)
        v_hat = vv / (1.0 - ADAM_B2 ** ADAM_STEP)
        d_ref[...] = -ADAM_LR * (m_hat / (jnp.sqrt(v_hat) + ADAM_EPS) + ADAM_WD * w_ref[...])
        nm_ref[...] = mv
        nv_ref[...] = vv

    blk = pl.BlockSpec((tr, C), lambda i: (i, 0))
    outs = _pcall(name, body, (R // tr,), [blk] * 4, [blk] * 3, [S((R, C), F32)] * 3)(
        w.reshape(R, C), m.reshape(R, C), v.reshape(R, C), g.reshape(R, C))
    return [o.reshape(shape) for o in outs]


def _place_scalars():
    return jnp.stack([lax.axis_index("c"), 2 * lax.axis_index("x") + lax.axis_index("y")]).astype(jnp.int32)


def _adamw_sum(w, m, v, owns, landed, name):
    L, p, q = w.shape
    qq = owns[0].shape[2]
    tr = _tile(p, 256)

    def body(place_ref, w_ref, m_ref, v_ref, *rest):
        own_refs, land_refs = rest[:L], rest[L:2 * L]
        g_ref, d_ref, nm_ref, nv_ref = rest[2 * L:]
        chip = place_ref[1]

        def update(l):
            own = own_refs[l][...].astype(F32)
            gs = None
            for k in range(4):
                term = jnp.where(chip == k, own, land_refs[l][k].astype(F32))
                gs = term if gs is None else gs + term
            gv = gs[:, 0:q]
            mv = ADAM_B1 * m_ref[...] + (1.0 - ADAM_B1) * gv
            vv = ADAM_B2 * v_ref[...] + (1.0 - ADAM_B2) * (gv * gv)
            m_hat = mv / (1.0 - ADAM_B1 ** ADAM_STEP)
            v_hat = vv / (1.0 - ADAM_B2 ** ADAM_STEP)
            g_ref[...] = gv
            d_ref[...] = -ADAM_LR * (m_hat / (jnp.sqrt(v_hat) + ADAM_EPS) + ADAM_WD * w_ref[...])
            nm_ref[...] = mv
            nv_ref[...] = vv

        for l in range(L):
            pl.when(pl.program_id(0) == l)(lambda l=l: update(l))

    def rows_of(layer):
        return lambda l, i, place: jnp.where(l == layer, i, 0)

    blk = pl.BlockSpec((None, tr, q), lambda l, i, place: (l, i, 0))
    in_specs = [blk, blk, blk]
    in_specs += [pl.BlockSpec((None, tr, qq), lambda l, i, place, r=rows_of(k): (0, r(l, i, place), 0)) for k in range(L)]
    in_specs += [pl.BlockSpec((4, None, tr, qq), lambda l, i, place, r=rows_of(k): (0, 0, r(l, i, place), 0)) for k in range(L)]
    gs = pltpu.PrefetchScalarGridSpec(num_scalar_prefetch=1, grid=(L, p // tr), in_specs=in_specs, out_specs=[blk] * 4)
    return pl.pallas_call(body, grid_spec=gs, out_shape=[S((L, p, q), F32)] * 4, name=name,
                          compiler_params=pltpu.CompilerParams(dimension_semantics=("arbitrary", "arbitrary")))(
        _place_scalars(), w, m, v, *owns, *landed)


def _pair_add(g, recv, axis, name):
    _, _, p, q = recv.shape

    def body(place_ref, g_ref, r_ref, o_ref, own_ref):
        s = (g_ref[...].astype(F32) + r_ref[...].astype(F32)).astype(BF16)
        o_ref[...] = s

        @pl.when(pl.program_id(0) == place_ref[1])
        def _():
            own_ref[...] = s

    if axis == 1:
        gspec = pl.BlockSpec((None, p, q), lambda k, place: (0, 2 * k + place[0], 0))
    else:
        gspec = pl.BlockSpec((None, p, q), lambda k, place: (0, 0, 2 * k + place[0]))
    part = pl.BlockSpec((None, None, p, q), lambda k, place: (k, 0, 0, 0))
    own = pl.BlockSpec((None, p, q), lambda k, place: (0, 0, 0))
    gs = pltpu.PrefetchScalarGridSpec(num_scalar_prefetch=1, grid=(4,), in_specs=[gspec, part], out_specs=[part, own])
    return pl.pallas_call(body, grid_spec=gs, out_shape=[S((4, 1, p, q), BF16), S((1, p, q), BF16)], name=name,
                          compiler_params=pltpu.CompilerParams(dimension_semantics=("arbitrary",)))(_place_scalars(), g, recv)


def _win_pieces(n_attn, n_heads, n_conv, shard, chunk):
    bounds = [0, 3 * n_attn, 3 * n_attn + n_heads, 3 * n_attn + n_heads + 2 * n_conv]
    pieces = []
    for j in range(N_DEV):
        lo, hi = shard * j, shard * (j + 1)
        for r in range(3):
            a, b = max(lo, bounds[r]), min(hi, bounds[r + 1])
            if a < b:
                pieces.append((r, a - bounds[r], b - bounds[r], chunk * j + a - lo))
    return pieces


def _win_split(w_in, pieces, widths, name):
    L, D, C = w_in.shape
    tr = _tile(D, 256)

    def body(x_ref, *outs):
        outs[1][...] = jnp.zeros_like(outs[1])
        for r, d0, d1, s0 in pieces:
            outs[r][:, d0:d1] = x_ref[:, s0:s0 + d1 - d0]

    return _pcall(name, body, (L, D // tr), [pl.BlockSpec((None, tr, C), lambda l, i: (l, i, 0))],
                  [pl.BlockSpec((None, tr, wd), lambda l, i: (l, i, 0)) for wd in widths],
                  [S((L, D, wd), BF16) for wd in widths])(w_in)


def _win_merge(parts, pieces, chunked_cols, name):
    L, D, _ = parts[0].shape
    tr = _tile(D, 256)

    def body(a_ref, b_ref, c_ref, o_ref):
        ins = (a_ref, b_ref, c_ref)
        o_ref[...] = jnp.zeros_like(o_ref)
        for r, d0, d1, s0 in pieces:
            o_ref[:, s0:s0 + d1 - d0] = ins[r][:, d0:d1]

    return _pcall(name, body, (L, D // tr), [pl.BlockSpec((None, tr, x.shape[2]), lambda l, i: (l, i, 0)) for x in parts],
                  pl.BlockSpec((None, tr, chunked_cols), lambda l, i: (l, i, 0)), S((L, D, chunked_cols), BF16))(*parts)


def _place():
    return lax.axis_index("x"), lax.axis_index("y"), lax.axis_index("c")


def _flip(v, f):
    return 1 - v if f else v


def _window(ref, axis, size, dev):
    start = dev * size if isinstance(dev, int) else pl.multiple_of(dev * size, LANES if axis == 2 else 16)
    return ref.at[:, pl.ds(start, size), :] if axis == 1 else ref.at[:, :, pl.ds(start, size)]


HBM_SPEC = pl.BlockSpec(memory_space=pltpu.HBM)
SEM_SPEC = pl.BlockSpec(memory_space=pltpu.SEMAPHORE)
SPLIT_COPY_PARAMS = dict(has_side_effects=pltpu.SideEffectType.DATAFLOW_SIDE_EFFECTING)


def _hbm(v):
    return pltpu.with_memory_space_constraint(v, pltpu.HBM)


def _full_shape(shard, axis):
    return tuple(N_DEV * d if i == axis else d for i, d in enumerate(shard.shape))


def _ag_peers(x, y, c):
    return [(x, y, 1 - c), (1 - x, y, c), (x, 1 - y, c), (1 - x, 1 - y, c)]


def _ag_start(shards, axes, groups, name):
    n, ng = len(shards), len(groups)
    sizes = [s.shape[ax] for s, ax in zip(shards, axes)]
    where = {w: (g, i) for g, members in enumerate(groups) for i, w in enumerate(members)}

    def body(*refs):
        xs, fulls = refs[:n], refs[n:2 * n]
        send, recv = refs[3 * n:3 * n + ng], refs[3 * n + ng:]
        x, y, c = _place()
        for members in groups:
            for w in members:
                g, i = where[w]
                for k, to in enumerate(_ag_peers(x, y, c)):
                    pltpu.make_async_remote_copy(
                        src_ref=xs[w], dst_ref=_window(fulls[w], axes[w], sizes[w], 4 * x + 2 * y + c),
                        send_sem=send[g].at[4 * i + k], recv_sem=recv[g].at[4 * i + k], device_id=to, device_id_type=MESH).start()

    sems = [pltpu.SemaphoreType.DMA((4 * len(m),)) for m in groups]
    outs = pl.pallas_call(
        body, name=name,
        out_shape=[pltpu.HBM(_full_shape(s, ax), s.dtype) for s, ax in zip(shards, axes)] + [pltpu.HBM(s.shape, s.dtype) for s in shards]
        + sems + sems,
        in_specs=[HBM_SPEC] * n, out_specs=[HBM_SPEC] * (2 * n) + [SEM_SPEC] * (2 * ng),
        input_output_aliases={w: n + w for w in range(n)},
        compiler_params=pltpu.CompilerParams(**SPLIT_COPY_PARAMS))(*[_hbm(s) for s in shards])
    return outs[:n], outs[n:2 * n], outs[2 * n:2 * n + ng], outs[2 * n + ng:]


def _ag_wait(shards, fulls, send_sems, recv_sems, axes, after, name):
    n = len(shards)
    sizes = [s.shape[ax] for s, ax in zip(shards, axes)]

    def body(*refs):
        xs = refs[:n]
        send, recv = refs[2 * n], refs[2 * n + 1]
        landed = refs[3 * n + 3:]
        x, y, c = _place()
        for w in range(n):
            for k, frm in enumerate(_ag_peers(x, y, c)):
                copy = pltpu.make_async_remote_copy(
                    src_ref=xs[w], dst_ref=_window(landed[w], axes[w], sizes[w], 4 * frm[0] + 2 * frm[1] + frm[2]),
                    send_sem=send.at[4 * w + k], recv_sem=recv.at[4 * w + k], device_id=frm, device_id_type=MESH)
                copy.wait_send()
                copy.wait_recv()

    outs = pl.pallas_call(
        body, name=name, out_shape=[pltpu.HBM(v.shape, v.dtype) for v in list(shards) + list(fulls)],
        in_specs=[HBM_SPEC] * (2 * n) + [SEM_SPEC, SEM_SPEC, pl.BlockSpec(memory_space=pl.ANY)], out_specs=[HBM_SPEC] * (2 * n),
        input_output_aliases={i: i for i in range(2 * n)},
        compiler_params=pltpu.CompilerParams(**SPLIT_COPY_PARAMS))(*shards, *fulls, send_sems, recv_sems, after)
    return outs[:n], outs[n:]


def _ag_forward(fulls, shards, axes, name):
    n = len(fulls)
    sizes = [s.shape[ax] for s, ax in zip(shards, axes)]

    def body(*refs):
        xs, full_refs = refs[:n], refs[2 * n:3 * n]
        send_sems, recv_sems, local_sems = refs[3 * n:]
        x, y, c = _place()
        chips = [(1 - x, y), (x, 1 - y), (1 - x, 1 - y)]
        copies = []
        for w in range(n):
            mine = pltpu.make_async_copy(xs[w], _window(full_refs[w], axes[w], sizes[w], 4 * x + 2 * y + c), local_sems.at[w])
            mine.start()
            copies.append(mine)
            for j, (px, py) in enumerate(chips):
                sent = _window(full_refs[w], axes[w], sizes[w], 4 * px + 2 * py + c)
                got = _window(full_refs[w], axes[w], sizes[w], 4 * px + 2 * py + 1 - c)
                out = pltpu.make_async_remote_copy(src_ref=sent, dst_ref=sent, send_sem=send_sems.at[3 * w + j],
                                                   recv_sem=recv_sems.at[3 * w + j], device_id=(x, y, 1 - c), device_id_type=MESH)
                out.start()
                back = pltpu.make_async_remote_copy(src_ref=got, dst_ref=got, send_sem=send_sems.at[3 * w + j],
                                                    recv_sem=recv_sems.at[3 * w + j], device_id=(x, y, 1 - c), device_id_type=MESH)
                copies.append((out, back))
        for cp in copies:
            if isinstance(cp, tuple):
                cp[0].wait_send()
                cp[1].wait_recv()
            else:
                cp.wait()

    any_spec = pl.BlockSpec(memory_space=pl.ANY)
    outs = pl.pallas_call(
        body, name=name, out_shape=[S(f.shape, f.dtype) for f in fulls], in_specs=[any_spec] * (2 * n), out_specs=[any_spec] * n,
        input_output_aliases={n + w: w for w in range(n)},
        scratch_shapes=[pltpu.SemaphoreType.DMA((3 * n,)), pltpu.SemaphoreType.DMA((3 * n,)), pltpu.SemaphoreType.DMA((n,))])(
        *shards, *fulls)
    return outs


def _to_sibling(grads, axes, sizes, name):
    n = len(grads)
    outs = []
    for g, ax, sz in zip(grads, axes, sizes):
        L, K, N = g.shape
        outs.append(S((4, L, sz, N) if ax == 1 else (4, L, K, sz), g.dtype))

    def body(*refs):
        g_refs, out_refs = refs[:n], refs[n:2 * n]
        send_sems, recv_sems = refs[2 * n:]
        x, y, c = _place()
        copies = []
        for w in range(n):
            for k in range(4):
                copies.append(pltpu.make_async_remote_copy(
                    src_ref=_window(g_refs[w], axes[w], sizes[w], 2 * k + 1 - c), dst_ref=out_refs[w].at[k],
                    send_sem=send_sems.at[4 * w + k], recv_sem=recv_sems.at[4 * w + k], device_id=(x, y, 1 - c),
                    device_id_type=MESH))
        for cp in copies:
            cp.start()
        for cp in copies:
            cp.wait()

    any_spec = pl.BlockSpec(memory_space=pl.ANY)
    return pl.pallas_call(
        body, out_shape=outs, in_specs=[any_spec] * n, out_specs=[any_spec] * n, name=name,
        scratch_shapes=[pltpu.SemaphoreType.DMA((4 * n,)), pltpu.SemaphoreType.DMA((4 * n,))])(*grads)


def _rs_copy(p_ref, out_ref, send_sems, recv_sems, w, rel, x, y, c):
    tx, ty = _flip(x, rel & 2), _flip(y, rel & 1)
    return pltpu.make_async_remote_copy(
        src_ref=p_ref.at[2 * tx + ty], dst_ref=out_ref.at[2 * x + y], send_sem=send_sems.at[3 * w + rel - 1],
        recv_sem=recv_sems.at[3 * w + rel - 1], device_id=(tx, ty, c), device_id_type=MESH)


def _rs_start(parts, name):
    n = len(parts)

    def body(*refs):
        p_refs, out_refs = refs[:n], refs[n:2 * n]
        send_sems, recv_sems, token = refs[3 * n:]
        x, y, c = _place()
        for w in range(n):
            for rel in (1, 2, 3):
                _rs_copy(p_refs[w], out_refs[w], send_sems, recv_sems, w, rel, x, y, c).start()
        token[...] = jnp.zeros_like(token)

    sems = pltpu.SemaphoreType.DMA((3 * n,))
    outs = pl.pallas_call(
        body, name=name,
        out_shape=[pltpu.HBM(p.shape, p.dtype) for p in parts] * 2 + [sems, sems, S((8, LANES), F32)],
        in_specs=[HBM_SPEC] * n, out_specs=[HBM_SPEC] * (2 * n) + [SEM_SPEC, SEM_SPEC, pl.BlockSpec(memory_space=pltpu.VMEM)],
        input_output_aliases={w: n + w for w in range(n)},
        compiler_params=pltpu.CompilerParams(**SPLIT_COPY_PARAMS))(*[_hbm(p) for p in parts])
    return outs[:n], outs[n:2 * n], outs[2 * n], outs[2 * n + 1], outs[2 * n + 2]


def _rs_wait(parts, landing, send_sems, recv_sems, after, name):
    n = len(parts)

    def body(*refs):
        p_refs = refs[:n]
        send, recv = refs[2 * n], refs[2 * n + 1]
        landed = refs[3 * n + 3:]
        x, y, c = _place()
        for w in range(n):
            for rel in (1, 2, 3):
                copy = _rs_copy(p_refs[w], landed[w], send, recv, w, rel, x, y, c)
                copy.wait_send()
                copy.wait_recv()

    outs = pl.pallas_call(
        body, name=name, out_shape=[pltpu.HBM(v.shape, v.dtype) for v in list(parts) + list(landing)],
        in_specs=[HBM_SPEC] * (2 * n) + [SEM_SPEC, SEM_SPEC, pl.BlockSpec(memory_space=pl.ANY)], out_specs=[HBM_SPEC] * (2 * n),
        input_output_aliases={i: i for i in range(2 * n)},
        compiler_params=pltpu.CompilerParams(**SPLIT_COPY_PARAMS))(*parts, *landing, send_sems, recv_sems, after)
    return outs[n:]


def _exchange_small(v, reduce, name):
    R, C = v.shape

    def body(v_ref, out_ref, gath, send_sems, recv_sems):
        x, y, c = _place()
        me = 4 * x + 2 * y + c
        buf = gath if reduce else out_ref
        buf[me] = v_ref[...]
        copies = []
        for rel in range(1, N_DEV):
            peer = (_flip(x, rel & 4), _flip(y, rel & 2), _flip(c, rel & 1))
            copies.append(pltpu.make_async_remote_copy(
                src_ref=v_ref, dst_ref=buf.at[me], send_sem=send_sems.at[rel - 1], recv_sem=recv_sems.at[rel - 1],
                device_id=peer, device_id_type=MESH))
        for cp in copies:
            cp.start()
        for cp in copies:
            cp.wait()
        if reduce:
            acc = gath[0]
            for d in range(1, N_DEV):
                acc = acc + gath[d]
            out_ref[...] = acc

    vm = pl.BlockSpec(memory_space=pltpu.VMEM)
    return pl.pallas_call(
        body, out_shape=S((R, C) if reduce else (N_DEV, R, C), F32), in_specs=[vm], out_specs=vm, name=name,
        scratch_shapes=[pltpu.VMEM((N_DEV, R, C) if reduce else (8, LANES), F32), pltpu.SemaphoreType.DMA((N_DEV - 1,)),
                        pltpu.SemaphoreType.DMA((N_DEV - 1,))])(v)


def _pad_rows(flat, cols, mult):
    n = flat.shape[-1]
    rows = -(-n // cols)
    rows = -(-rows // mult) * mult
    pad = [(0, 0)] * (flat.ndim - 1) + [(0, rows * cols - n)]
    return jnp.pad(flat, pad).reshape(flat.shape[:-1] + (rows, cols))


def _round_up(n, m):
    return -(-n // m) * m


def _shard_axes(a):
    return [(2, _round_up(a[n].shape[2], LANES)) if kind == 'col' else (1, _round_up(a[n].shape[1], LANES)) for n, kind in BIG]


def _pack_small(vals):
    rows = [_pad_rows(vals[n].astype(F32).reshape(-1), SMALL_COLS, 1) for n in SMALL]
    m = jnp.concatenate(rows, axis=0)
    return jnp.pad(m, ((0, -m.shape[0] % 8), (0, 0)))


def _unpack_small(m, a):
    out, r = {}, 0
    for n in SMALL:
        nr = -(-a[n].size // SMALL_COLS)
        out[n] = m[r:r + nr].reshape(-1)[:a[n].size].reshape(a[n].shape)
        r += nr
    return out, r


GROUPS = (('ffn1_w_gate', 'ffn1_w_up', 'ffn1_w_down'), ('w_in', 'w_out', 'xattn_w_q', 'xattn_w_kv', 'xattn_w_o'),
          ('ffn2_w_gate', 'ffn2_w_up', 'ffn2_w_down'))


def _layer_small(a, conv_w_full, l):
    H = a['b_f'].shape[1]
    return dict(
        bft=jnp.pad(a['b_f'][l].reshape(H, 1), ((0, 16 - H), (0, 0))),
        cw=jnp.pad(conv_w_full[l], ((0, CONV_PAD - CONV_WIDTH), (0, 0))), cb=a['conv_b'][l].reshape(1, -1),
        lg=a['conv_ln_g'][l].reshape(1, -1), lb=a['conv_ln_b'][l].reshape(1, -1),
        ag=a['attn_out_g'][l].reshape(1, -1), cg=a['conv_out_g'][l].reshape(1, -1),
        g1=a['ffn1_norm_g'][l], gm=a['mix_norm_g'][l], gx=a['xattn_norm_g'][l], gmem=a['mem_norm_g'][l], g2=a['ffn2_norm_g'][l])


def _layer_fwd(x0, mem, w, fetch, cfg, l):
    T = x0.shape[0]
    H = cfg['heads']
    sv = {'x0': x0}
    m = fetch(l, 0, x0)
    w.update(wg1=(m['ffn1_w_gate'], 0), wu1=(m['ffn1_w_up'], 0), wd1=(m['ffn1_w_down'], 0))
    sv['h1'] = _rms_fwd(x0, w['g1'], f"l{l}_ffn1_norm")
    sv['G1'], sv['U1'], sv['A1'] = _ffn_up(sv['h1'], w['wg1'], w['wu1'], f"l{l}_ffn1_up")
    x1 = sv['x1'] = _mm_res(sv['A1'], w['wd1'], x0, 0.5, f"l{l}_ffn1_down")
    m = fetch(l, 1, x1)
    wqkv, wf, wag = _win_split(m['w_in'], cfg['pieces'], cfg['widths'], f"l{l}_w_in_split")
    w.update(wqkv=(wqkv, 0), wft=wf[0, :, :16].T, wag=(wag, 0), wout=(m['w_out'], 0), wq=(m['xattn_w_q'], 0),
             wkv=(m['xattn_w_kv'], 0), wo=(m['xattn_w_o'], 0))
    h2 = sv['h2'] = _rms_fwd(x1, w['gm'], f"l{l}_mix_norm")
    sv['qkv'] = _mm(h2, w['wqkv'], BF16, f"l{l}_qkv_proj")
    sv['agv'] = _mm(h2, w['wag'], F32, f"l{l}_glu_proj")
    ct, sv['sg'] = _fox_prep(h2, w['wft'], w['bft'], f"l{l}_fox_prep")
    sv['c_col'] = ct[:H].reshape(H, T, 1)
    sv['c_row'] = ct[:H].reshape(H // 2, 2, T)
    sv['attn'], sv['lse'] = _fox_fwd(sv['qkv'], sv['c_col'], sv['c_row'], f"l{l}_fox_fwd")
    sv['yc'] = _conv_fwd(sv['agv'], w['cw'], w['cb'], f"l{l}_conv_fwd")
    sv['ycat'] = _mix_post(sv['attn'], sv['yc'], w['ag'], w['cg'], w['lg'], w['lb'], f"l{l}_mix_post")
    x2 = sv['x2'] = _mm_res(sv['ycat'], w['wout'], x1, 1.0, f"l{l}_out_proj")
    sv['h3'] = _rms_fwd(x2, w['gx'], f"l{l}_xattn_norm")
    sv['memn'] = _rms_fwd(mem, w['gmem'], f"l{l}_mem_norm")
    sv['q'] = _mm(sv['h3'], w['wq'], BF16, f"l{l}_xattn_q")
    sv['kv'] = _mm(sv['memn'], w['wkv'], BF16, f"l{l}_xattn_kv")
    sv['o'] = _xattn_fwd(sv['q'], sv['kv'], f"l{l}_xattn_fwd")
    x3 = sv['x3'] = _mm_res(sv['o'], w['wo'], x2, 1.0, f"l{l}_xattn_out")
    m = fetch(l, 2, x3)
    w.update(wg2=(m['ffn2_w_gate'], 0), wu2=(m['ffn2_w_up'], 0), wd2=(m['ffn2_w_down'], 0))
    sv['h4'] = _rms_fwd(x3, w['g2'], f"l{l}_ffn2_norm")
    sv['G2'], sv['U2'], sv['A2'] = _ffn_up(sv['h4'], w['wg2'], w['wu2'], f"l{l}_ffn2_up")
    return _mm_res(sv['A2'], w['wd2'], x3, 0.5, f"l{l}_ffn2_down"), sv


def _ffn_bwd(dout, x_in, h, G, U, A, wg, wu, wd, g, tag, put, which, dep):
    dG, dU = _ffn_bwd_act(dout, wd, G, U, 0.5, tag + "_bwd_act", dep)
    put(which + '_w_down', A, dout, 0.5, tag + "_dwd")
    put(which + '_w_gate', h, dG, 1.0, tag + "_dwg")
    put(which + '_w_up', h, dU, 1.0, tag + "_dwu")
    return _bwd_h([(dG, wg, 'nt'), (dU, wu, 'nt')], x_in, g, dout, tag + "_bwd_h")


def _layer_bwd(dx4, mem, w, sv, reduce, cfg, l, dep):
    small, grads = {}, {}
    T = dx4.shape[0]
    H = cfg['heads']

    def put(key, act, dy, scale, name):
        grads[key] = _wgrad(act, dy, scale, name, (None, 0, 1))

    dx3, small['ffn2_norm_g'] = _ffn_bwd(dx4, sv['x3'], sv['h4'], sv['G2'], sv['U2'], sv['A2'], w['wg2'], w['wu2'], w['wd2'],
                                         w['g2'], f"l{l}_ffn2", put, 'ffn2', dep)
    dep = reduce(l, 2, {n: grads.pop(n) for n in GROUPS[2]})
    do = _mm_nt(dx3, w['wo'], BF16, f"l{l}_xattn_do", dep)
    put('xattn_w_o', sv['o'], dx3, 1.0, f"l{l}_dwo")
    dq, dkv = _xattn_bwd(sv['q'], sv['kv'], do, f"l{l}_xattn_bwd")
    put('xattn_w_q', sv['h3'], dq, 1.0, f"l{l}_dwq")
    dx2, small['xattn_norm_g'] = _bwd_h([(dq, w['wq'], 'nt')], sv['x2'], w['gx'], dx3, f"l{l}_xattn_bwd_h")
    dmemn = _mm_nt(dkv, w['wkv'], F32, f"l{l}_dmemn")
    put('xattn_w_kv', sv['memn'], dkv, 1.0, f"l{l}_dwkv")
    small['mem_norm_g'] = _rms_gain_grad(dmemn, mem, w['gmem'], f"l{l}_dgmem")
    dycat = _mm_nt(dx2, w['wout'], F32, f"l{l}_dycat")
    put('w_out', sv['ycat'], dx2, 1.0, f"l{l}_dwout")
    dattn, dyc, small['attn_out_g'], small['conv_out_g'], small['conv_ln_g'], small['conv_ln_b'] = _mix_post_bwd(
        dycat, sv['attn'], sv['yc'], w['ag'], w['cg'], w['lg'], w['lb'], f"l{l}_mix_post_bwd")
    dva, dga, dcw, small['conv_b'] = _conv_bwd(dyc, sv['agv'], w['cw'], f"l{l}_conv_bwd")
    dq_, dk_, dv_, dcs = _fox_bwd(sv['qkv'], sv['c_col'], sv['c_row'], sv['lse'], dattn, f"l{l}_fox_bwd")
    dcs16 = jnp.pad(dcs.reshape(H, T), ((0, 16 - H), (0, 0)))
    dflt, dwft, dbf = _fox_prep_bwd(dcs16, sv['sg'], sv['h2'], f"l{l}_fox_prep_bwd")
    small['b_f'] = dbf[:H].reshape(H)
    dqkv = jnp.concatenate([dq_, dk_, dv_], axis=1)
    dag = jnp.concatenate([dva, dga], axis=1)
    put('wqkv', sv['h2'], dqkv, 1.0, f"l{l}_dwqkv")
    put('wag', sv['h2'], dag, 1.0, f"l{l}_dwag")
    dx1, small['mix_norm_g'] = _bwd_h([(dqkv, w['wqkv'], 'nt'), (dag, w['wag'], 'nt'), (dflt, w['wft'], 'tn')],
                                      sv['x1'], w['gm'], dx2, f"l{l}_mix_bwd_h")
    dwf = jnp.pad(dwft[:H].T, ((0, 0), (0, LANES - H)))[None].astype(BF16)
    grads['w_in'] = _win_merge((grads.pop('wqkv'), dwf, grads.pop('wag')), cfg['pieces'], cfg['chunked_cols'], f"l{l}_w_in_merge")
    dep = reduce(l, 1, {n: grads.pop(n) for n in GROUPS[1]})
    dx0, small['ffn1_norm_g'] = _ffn_bwd(dx1, sv['x0'], sv['h1'], sv['G1'], sv['U1'], sv['A1'], w['wg1'], w['wu1'], w['wd1'],
                                         w['g1'], f"l{l}_ffn1", put, 'ffn1', dep)
    dep = reduce(l, 0, {n: grads.pop(n) for n in GROUPS[0]})
    small = {k: v.reshape(-1) for k, v in small.items()}
    return dx0, small, dcw[:CONV_WIDTH], dep


def _local_step(x, mem, tgt, a, conv_w_full, fetch, reduce, cfg):
    L = a['b_f'].shape[0]
    ws = [_layer_small(a, conv_w_full, l) for l in range(L)]
    saved = []
    for l in range(L):
        x, sv = _layer_fwd(x, mem, ws[l], fetch, cfg, l)
        saved.append(sv)
    loss, dx, dgf = _loss_head(x, a['final_norm_g'], tgt, "loss_head")
    smalls, dcws, dep = [None] * L, [None] * L, None
    for l in range(L - 1, -1, -1):
        dx, smalls[l], dcws[l], dep = _layer_bwd(dx, mem, ws[l], saved[l], reduce, cfg, l, dep)
    small = {n: jnp.stack([smalls[l][n] for l in range(L)]) for n in SMALL if n != 'final_norm_g'}
    small['final_norm_g'] = dgf.reshape(-1)
    return loss, dx, small, jnp.stack(dcws)


def kernel(x, mem, ffn1_norm_g, ffn1_w_gate, ffn1_w_up, ffn1_w_down, mix_norm_g, w_in, b_f, conv_w, conv_b, conv_ln_g, conv_ln_b, attn_out_g, conv_out_g, w_out, xattn_norm_g, mem_norm_g, xattn_w_q, xattn_w_kv, xattn_w_o, ffn2_norm_g, ffn2_w_gate, ffn2_w_up, ffn2_w_down, final_norm_g, loss_target, m_ffn1_norm_g, m_ffn1_w_gate, m_ffn1_w_up, m_ffn1_w_down, m_mix_norm_g, m_w_in, m_b_f, m_conv_w, m_conv_b, m_conv_ln_g, m_conv_ln_b, m_attn_out_g, m_conv_out_g, m_w_out, m_xattn_norm_g, m_mem_norm_g, m_xattn_w_q, m_xattn_w_kv, m_xattn_w_o, m_ffn2_norm_g, m_ffn2_w_gate, m_ffn2_w_up, m_ffn2_w_down, m_final_norm_g, v_ffn1_norm_g, v_ffn1_w_gate, v_ffn1_w_up, v_ffn1_w_down, v_mix_norm_g, v_w_in, v_b_f, v_conv_w, v_conv_b, v_conv_ln_g, v_conv_ln_b, v_attn_out_g, v_conv_out_g, v_w_out, v_xattn_norm_g, v_mem_norm_g, v_xattn_w_q, v_xattn_w_kv, v_xattn_w_o, v_ffn2_norm_g, v_ffn2_w_gate, v_ffn2_w_up, v_ffn2_w_down, v_final_norm_g):
    args = (x, mem, ffn1_norm_g, ffn1_w_gate, ffn1_w_up, ffn1_w_down, mix_norm_g, w_in, b_f, conv_w, conv_b, conv_ln_g, conv_ln_b, attn_out_g, conv_out_g, w_out, xattn_norm_g, mem_norm_g, xattn_w_q, xattn_w_kv, xattn_w_o, ffn2_norm_g, ffn2_w_gate, ffn2_w_up, ffn2_w_down, final_norm_g)
    moments_m = (m_ffn1_norm_g, m_ffn1_w_gate, m_ffn1_w_up, m_ffn1_w_down, m_mix_norm_g, m_w_in, m_b_f, m_conv_w, m_conv_b, m_conv_ln_g, m_conv_ln_b, m_attn_out_g, m_conv_out_g, m_w_out, m_xattn_norm_g, m_mem_norm_g, m_xattn_w_q, m_xattn_w_kv, m_xattn_w_o, m_ffn2_norm_g, m_ffn2_w_gate, m_ffn2_w_up, m_ffn2_w_down, m_final_norm_g)
    moments_v = (v_ffn1_norm_g, v_ffn1_w_gate, v_ffn1_w_up, v_ffn1_w_down, v_mix_norm_g, v_w_in, v_b_f, v_conv_w, v_conv_b, v_conv_ln_g, v_conv_ln_b, v_attn_out_g, v_conv_out_g, v_w_out, v_xattn_norm_g, v_mem_norm_g, v_xattn_w_q, v_xattn_w_kv, v_xattn_w_o, v_ffn2_norm_g, v_ffn2_w_gate, v_ffn2_w_up, v_ffn2_w_down, v_final_norm_g)
    a = dict(zip(NAMES, args))
    am = dict(zip(WEIGHTS, moments_m))
    av = dict(zip(WEIGHTS, moments_v))
    L, taps, cshard = conv_w.shape
    dev = 4 * lax.axis_index("x") + 2 * lax.axis_index("y") + lax.axis_index("c")

    big_names = [n for n, _ in BIG]
    geometry = dict(zip(big_names, _shard_axes(a)))
    n_attn, n_heads, n_conv = attn_out_g.shape[1], b_f.shape[1], conv_out_g.shape[1]
    chunk = geometry['w_in'][1]
    cfg = dict(heads=n_heads, pieces=_win_pieces(n_attn, n_heads, n_conv, w_in.shape[2], chunk),
               widths=(3 * n_attn, LANES, 2 * n_conv), chunked_cols=N_DEV * chunk)

    keys = [(l, n) for l in range(L) for names in GROUPS for n in names]
    members = [[keys.index((l, n)) for n in names] for l in range(L) for names in GROUPS]
    shards = []
    for l, n in keys:
        ax, size = geometry[n]
        pad = [(0, 0)] * 3
        pad[ax] = (0, size - a[n].shape[ax])
        shards.append(jnp.pad(a[n][l:l + 1].astype(BF16), pad))
    key_axes = [geometry[n][0] for _, n in keys]
    fulls, thru, ag_send, ag_recv = _ag_start(shards, key_axes, members, "allgather_start")

    def fetch(l, gi, after):
        g = l * len(GROUPS) + gi
        axs = [key_axes[i] for i in members[g]]
        own, landed = _ag_wait([thru[i] for i in members[g]], [fulls[i] for i in members[g]], ag_send[g], ag_recv[g], axs, after,
                               f"allgather_wait_l{l}g{gi}")
        return dict(zip(GROUPS[gi], _ag_forward(landed, own, axs, f"allgather_forward_l{l}g{gi}")))

    pending, own_part, landed_part = [], {}, {}

    def reduce(l, gi, grads):
        names = GROUPS[gi]
        axs = [geometry[n][0] for n in names]
        recv = _to_sibling([grads[n] for n in names], axs, [geometry[n][1] for n in names], f"reduce_to_sibling_l{l}g{gi}")
        parts = []
        for n, r, ax in zip(names, recv, axs):
            part, own_part[(l, n)] = _pair_add(grads[n], r, ax, f"reduce_pair_add_l{l}_{n}")
            parts.append(part)
        landing, parts_thru, send, recv_sems, token = _rs_start(parts, f"reduce_start_l{l}g{gi}")
        pending.append((l, gi, parts_thru, landing, send, recv_sems))
        return token

    cw_rows = _pad_rows(conv_w.reshape(-1), LANES, 8)
    cw_all = _exchange_small(cw_rows, False, "allgather_conv_w").reshape(N_DEV, -1)[:, :conv_w.size]
    conv_w_full = cw_all.reshape(N_DEV, L, taps, cshard).transpose(1, 2, 0, 3).reshape(L, taps, N_DEV * cshard)

    loss, grad_x, gsmall, dcw = _local_step(x[0], mem[0], loss_target[0], a, conv_w_full, fetch, reduce, cfg)

    for l, gi, parts_thru, landing, send, recv_sems in pending:
        for n, arr in zip(GROUPS[gi], _rs_wait(parts_thru, landing, send, recv_sems, grad_x, f"reduce_wait_l{l}g{gi}")):
            landed_part[(l, n)] = arr

    small_rows = _pack_small(gsmall)
    n_small = small_rows.shape[0]
    dcw_rows = jnp.pad(dcw, ((0, 0), (0, CONV_PAD - taps), (0, 0))).reshape(-1, SMALL_COLS)
    summed = _exchange_small(jnp.concatenate([small_rows, dcw_rows], axis=0), True, "allreduce_small")
    g_small, _ = _unpack_small(summed[:n_small], a)
    dcw_sum = summed[n_small:].reshape(L, CONV_PAD, N_DEV * cshard)[:, :taps]

    grads = dict(g_small)
    grads['conv_w'] = lax.dynamic_slice_in_dim(dcw_sum, dev * cshard, cshard, axis=2)

    delta, new_m, new_v = {}, {}, {}
    for n in big_names:
        grads[n], delta[n], new_m[n], new_v[n] = _adamw_sum(
            a[n], am[n], av[n], [own_part[(l, n)] for l in range(L)], [landed_part[(l, n)] for l in range(L)], "adamw_" + n)
    delta['conv_w'], new_m['conv_w'], new_v['conv_w'] = _adamw(conv_w, am['conv_w'], av['conv_w'], grads['conv_w'], "adamw_conv_w")
    pw, pm, pv, pg = (_pack_small(d) for d in (a, am, av, g_small))
    for dst, packed in zip((delta, new_m, new_v), _adamw(pw, pm, pv, pg, "adamw_small")):
        dst.update(_unpack_small(packed, a)[0])

    total = lax.psum(loss.reshape(()), ("x", "y", "c"))
    return (total, grad_x[None], *[grads[n] for n in WEIGHTS], *[delta[n] for n in WEIGHTS], *[new_m[n] for n in WEIGHTS],
            *[new_v[n] for n in WEIGHTS])
```

```python
import math

import jax
import jax.numpy as jnp
from jax import lax
from jax.experimental import pallas as pl
from jax.experimental.pallas import tpu as pltpu

F32, BF16 = jnp.float32, jnp.bfloat16
S = jax.ShapeDtypeStruct
MESH = pl.DeviceIdType.MESH

EPS = 1e-6
NEG_INF = -1e30
HEAD_DIM = 64
N_XATTN_HEADS = 4
CONV_WIDTH = 31
CONV_PAD = 32
LANES = 128
ADAM_LR, ADAM_B1, ADAM_B2, ADAM_EPS, ADAM_WD, ADAM_STEP = 0.001, 0.9, 0.999, 1e-08, 0.01, 10
N_DEV = 8
VMEM_LIMIT_BYTES = 56 * 1024 * 1024
ROW_TILE = 512
SMALL_COLS = 512

NN = ((1,), (0,))
NT = ((1,), (1,))
TN = ((0,), (0,))

NAMES = ['x', 'mem', 'ffn1_norm_g', 'ffn1_w_gate', 'ffn1_w_up', 'ffn1_w_down', 'mix_norm_g', 'w_in', 'b_f', 'conv_w', 'conv_b',
         'conv_ln_g', 'conv_ln_b', 'attn_out_g', 'conv_out_g', 'w_out', 'xattn_norm_g', 'mem_norm_g', 'xattn_w_q', 'xattn_w_kv',
         'xattn_w_o', 'ffn2_norm_g', 'ffn2_w_gate', 'ffn2_w_up', 'ffn2_w_down', 'final_norm_g']
WEIGHTS = NAMES[2:]
BIG = [('ffn1_w_gate', 'col'), ('ffn1_w_up', 'col'), ('ffn1_w_down', 'row'), ('w_in', 'col'), ('w_out', 'row'),
       ('xattn_w_q', 'row'), ('xattn_w_kv', 'col'), ('xattn_w_o', 'row'), ('ffn2_w_gate', 'col'), ('ffn2_w_up', 'col'),
       ('ffn2_w_down', 'row')]
SMALL = ['ffn1_norm_g', 'mix_norm_g', 'xattn_norm_g', 'mem_norm_g', 'ffn2_norm_g', 'conv_b', 'conv_ln_g', 'conv_ln_b',
         'attn_out_g', 'conv_out_g', 'b_f', 'final_norm_g']


def _dot(a, b, dims):
    return lax.dot_general(a, b, (dims, ((), ())), preferred_element_type=F32)


def _full(shape):
    nd = len(shape)
    return pl.BlockSpec(shape, lambda *_: (0,) * nd)


def _tile(n, pref):
    for t in (pref, 512, 384, 256, 128, 64, 32, 16, 8):
        if t <= n and n % t == 0:
            return t
    return n


def _pcall(name, body, grid, in_specs, out_specs, out_shape, scratch=(), aliases=None, dep=None):
    n_in = len(in_specs)
    kernel_body = body
    if dep is not None:
        in_specs = list(in_specs) + [pl.BlockSpec(memory_space=pl.ANY)]

        def kernel_body(*refs):
            return body(*refs[:n_in], *refs[n_in + 1:])

    call = pl.pallas_call(
        kernel_body, grid=grid, in_specs=in_specs, out_specs=out_specs, out_shape=out_shape, scratch_shapes=list(scratch),
        name=name, input_output_aliases=aliases or {},
        compiler_params=pltpu.CompilerParams(dimension_semantics=("arbitrary",) * len(grid), vmem_limit_bytes=VMEM_LIMIT_BYTES))
    return call if dep is None else (lambda *args: call(*args, dep))


def _arr(w):
    return w[0] if isinstance(w, tuple) else w


def _wshape(w):
    return w[0].shape[1:] if isinstance(w, tuple) else w.shape


def _wspec(w, block, imap):
    if isinstance(w, tuple):
        layer = w[1]
        return pl.BlockSpec((None,) + block, lambda *g: (layer,) + imap(*g))
    return pl.BlockSpec(block, imap)


def _wfull(w):
    shape = _wshape(w)
    return _wspec(w, shape, lambda *_: (0,) * len(shape))


def _sigmoid(z):
    return jax.nn.sigmoid(z)


def _rstd(x):
    return lax.rsqrt(jnp.mean(x * x, axis=-1, keepdims=True) + EPS)


def _rms_bwd(dy, x, g):
    r = _rstd(x)
    xh = x * r
    u = dy * g
    dx = r * (u - xh * jnp.mean(u * xh, axis=-1, keepdims=True))
    return dx, dy * xh


def _colsum(v):
    return jnp.sum(v, axis=0, keepdims=True)


def _rms_fwd(x, g, name):
    T, D = x.shape
    tm = _tile(T, ROW_TILE)

    def body(x_ref, g_ref, h_ref):
        xv = x_ref[...]
        h_ref[...] = (xv * _rstd(xv) * g_ref[...]).astype(BF16)

    row = pl.BlockSpec((tm, D), lambda i: (i, 0))
    return _pcall(name, body, (T // tm,), [row, _full((1, D))], row, S((T, D), BF16))(x, g.reshape(1, D))


def _mm(a, w, out_dtype, name):
    M, K = a.shape
    N = _wshape(w)[1]
    tm = _tile(M, ROW_TILE)
    tn = N if N <= 1536 else N // 2

    def body(a_ref, w_ref, o_ref):
        o_ref[...] = _dot(a_ref[...], w_ref[...], NN).astype(out_dtype)

    return _pcall(name, body, (N // tn, M // tm),
                  [pl.BlockSpec((tm, K), lambda j, i: (i, 0)), _wspec(w, (K, tn), lambda j, i: (0, j))],
                  pl.BlockSpec((tm, tn), lambda j, i: (i, j)), S((M, N), out_dtype))(a, _arr(w))


def _mm_res(a, w, res, scale, name):
    M, K = a.shape
    N = _wshape(w)[1]
    tm = _tile(M, ROW_TILE)

    def body(a_ref, w_ref, r_ref, o_ref):
        o_ref[...] = r_ref[...] + scale * _dot(a_ref[...], w_ref[...], NN)

    row = pl.BlockSpec((tm, N), lambda i: (i, 0))
    return _pcall(name, body, (M // tm,), [pl.BlockSpec((tm, K), lambda i: (i, 0)), _wfull(w), row], row,
                  S((M, N), F32))(a, _arr(w), res)


def _mm_nt(a, w, out_dtype, name, dep=None):
    M, K = a.shape
    N = _wshape(w)[0]
    tm = _tile(M, ROW_TILE)
    tn = N if N <= 1536 else N // 2

    def body(a_ref, w_ref, o_ref):
        o_ref[...] = _dot(a_ref[...].astype(BF16), w_ref[...], NT).astype(out_dtype)

    return _pcall(name, body, (N // tn, M // tm),
                  [pl.BlockSpec((tm, K), lambda j, i: (i, 0)), _wspec(w, (tn, K), lambda j, i: (j, 0))],
                  pl.BlockSpec((tm, tn), lambda j, i: (i, j)), S((M, N), out_dtype), dep=dep)(a, _arr(w))


def _wgrad(a, dy, scale, name, into):
    buf, layer, L = into
    T, M = a.shape
    N = dy.shape[1]
    tm = _tile(M, 256)

    def body(a_ref, dy_ref, *rest):
        rest[-1][...] = (scale * _dot(a_ref[...], dy_ref[...].astype(BF16), TN)).astype(BF16)

    in_specs = [pl.BlockSpec((T, tm), lambda i: (0, i)), _full((T, N))]
    args = [a, dy]
    if buf is not None:
        in_specs.append(pl.BlockSpec(memory_space=pl.ANY))
        args.append(buf)
    return _pcall(name, body, (M // tm,), in_specs, pl.BlockSpec((None, tm, N), lambda i: (layer, i, 0)), S((L, M, N), BF16),
                  aliases={2: 0} if buf is not None else None)(*args)


def _bwd_h(dots, x, g, dres, name, dep=None):
    T, D = x.shape
    tm = _tile(T, 256)
    n = len(dots)
    modes = [m for _, _, m in dots]

    def body(*refs):
        x_ref, g_ref, r_ref, dx_ref, dg_ref = refs[2 * n:]
        dh = None
        for k in range(n):
            part = _dot(refs[2 * k][...], refs[2 * k + 1][...], NT if modes[k] == 'nt' else TN)
            dh = part if dh is None else dh + part
        dx, dgrow = _rms_bwd(dh, x_ref[...], g_ref[...])
        dx_ref[...] = r_ref[...] + dx

        @pl.when(pl.program_id(0) == 0)
        def _():
            dg_ref[...] = jnp.zeros_like(dg_ref)

        dg_ref[...] += _colsum(dgrow)

    in_specs, args = [], []
    for lhs, w, mode in dots:
        if mode == 'nt':
            in_specs.append(pl.BlockSpec((tm, lhs.shape[1]), lambda i: (i, 0)))
        else:
            in_specs.append(pl.BlockSpec((lhs.shape[0], tm), lambda i: (0, i)))
        in_specs.append(_wfull(w))
        args += [lhs, _arr(w)]
    row = pl.BlockSpec((tm, D), lambda i: (i, 0))
    in_specs += [row, _full((1, D)), row]
    return _pcall(name, body, (T // tm,), in_specs, [row, _full((1, D))], [S((T, D), F32), S((1, D), F32)], dep=dep)(
        *args, x, g.reshape(1, D), dres)


def _rms_gain_grad(dy, x, g, name):
    T, D = x.shape

    def body(dy_ref, x_ref, g_ref, dg_ref):
        _, dgrow = _rms_bwd(dy_ref[...], x_ref[...], g_ref[...])
        dg_ref[...] = _colsum(dgrow)

    return _pcall(name, body, (), [_full((T, D)), _full((T, D)), _full((1, D))], _full((1, D)), S((1, D), F32))(
        dy, x, g.reshape(1, D))


def _ffn_up(h, wg, wu, name):
    T, D = h.shape
    Fh = _wshape(wg)[1]
    tm = _tile(T, ROW_TILE)
    tn = Fh if Fh <= 1536 else Fh // 2

    def body(h_ref, wg_ref, wu_ref, g_ref, u_ref, a_ref):
        hv = h_ref[...]
        gv = _dot(hv, wg_ref[...], NN)
        uv = _dot(hv, wu_ref[...], NN)
        g_ref[...] = gv.astype(BF16)
        u_ref[...] = uv.astype(BF16)
        a_ref[...] = (gv * _sigmoid(gv) * uv).astype(BF16)

    tile = pl.BlockSpec((tm, tn), lambda j, i: (i, j))
    return _pcall(name, body, (Fh // tn, T // tm),
                  [pl.BlockSpec((tm, D), lambda j, i: (i, 0)), _wspec(wg, (D, tn), lambda j, i: (0, j)),
                   _wspec(wu, (D, tn), lambda j, i: (0, j))],
                  [tile, tile, tile], [S((T, Fh), BF16)] * 3)(h, _arr(wg), _arr(wu))


def _ffn_bwd_act(dout, wd, gate, up, scale, name, dep=None):
    T, D = dout.shape
    Fh = _wshape(wd)[0]
    tm = _tile(T, ROW_TILE)
    tn = Fh if Fh <= 1536 else Fh // 2

    def body(d_ref, w_ref, g_ref, u_ref, dg_ref, du_ref):
        da = scale * _dot(d_ref[...].astype(BF16), w_ref[...], NT)
        gv = g_ref[...].astype(F32)
        uv = u_ref[...].astype(F32)
        sg = _sigmoid(gv)
        dg_ref[...] = (da * uv * (sg * (1.0 + gv * (1.0 - sg)))).astype(BF16)
        du_ref[...] = (da * (gv * sg)).astype(BF16)

    tile = pl.BlockSpec((tm, tn), lambda j, i: (i, j))
    return _pcall(name, body, (Fh // tn, T // tm),
                  [pl.BlockSpec((tm, D), lambda j, i: (i, 0)), _wspec(wd, (tn, D), lambda j, i: (j, 0)), tile, tile],
                  [tile, tile], [S((T, Fh), BF16)] * 2, dep=dep)(dout, _arr(wd), gate, up)


def _loss_head(x, g, tgt, name):
    T, D = x.shape
    tm = _tile(T, ROW_TILE)

    def body(x_ref, g_ref, t_ref, loss_ref, dx_ref, dg_ref):
        xv = x_ref[...]
        gv = g_ref[...]
        r = _rstd(xv)
        xh = xv * r
        e = xh * gv - t_ref[...]
        dy = e * (1.0 / D)
        u = dy * gv
        dx_ref[...] = r * (u - xh * jnp.mean(u * xh, axis=-1, keepdims=True))

        @pl.when(pl.program_id(0) == 0)
        def _():
            dg_ref[...] = jnp.zeros_like(dg_ref)
            loss_ref[...] = jnp.zeros_like(loss_ref)

        dg_ref[...] += _colsum(dy * xh)
        loss_ref[...] += 0.5 * _colsum(jnp.mean(e * e, axis=-1, keepdims=True))

    row = pl.BlockSpec((tm, D), lambda i: (i, 0))
    return _pcall(name, body, (T // tm,), [row, _full((1, D)), row], [_full((1, 1)), row, _full((1, D))],
                  [S((1, 1), F32), S((T, D), F32), S((1, D), F32)])(x, g.reshape(1, D), tgt)


def _split3(xb):
    hi = xb.astype(BF16)
    r1 = xb - hi.astype(F32)
    mid = r1.astype(BF16)
    lo = (r1 - mid.astype(F32)).astype(BF16)
    return hi, mid, lo


def _fox_prep(h, wft, bft, name):
    T, D = h.shape
    blk = _tile(T, 256)

    def body(h_ref, w_ref, b_ref, ct_ref, sg_ref):
        z = _dot(w_ref[...], h_ref[...], NT) + b_ref[...]
        sg_ref[...] = 1.0 - _sigmoid(z)
        logf = jnp.minimum(z, 0.0) - jnp.log1p(jnp.exp(-jnp.abs(z)))
        upper = (lax.broadcasted_iota(jnp.int32, (blk, blk), 0) <= lax.broadcasted_iota(jnp.int32, (blk, blk), 1)).astype(BF16)
        carry = jnp.zeros((16, 1), F32)
        for b in range(T // blk):
            hi, mid, lo = _split3(logf[:, b * blk:(b + 1) * blk])
            cb = _dot(hi, upper, NN) + _dot(mid, upper, NN) + _dot(lo, upper, NN) + carry
            ct_ref[:, b * blk:(b + 1) * blk] = cb
            carry = cb[:, blk - 1:blk]

    return _pcall(name, body, (), [_full((T, D)), _full((16, D)), _full((16, 1))], [_full((16, T)), _full((16, T))],
                  [S((16, T), F32), S((16, T), F32)])(h, wft, bft)


def _fox_prep_bwd(dcs, sg, h, name):
    T, D = h.shape
    blk = _tile(T, 256)
    nb = T // blk

    def body(dcs_ref, sg_ref, h_ref, dfl_ref, dw_ref, db_ref):
        lower = (lax.broadcasted_iota(jnp.int32, (blk, blk), 0) >= lax.broadcasted_iota(jnp.int32, (blk, blk), 1)).astype(BF16)
        carry = jnp.zeros((16, 1), F32)
        db = jnp.zeros((16, 1), F32)
        for b in range(nb - 1, -1, -1):
            cols = slice(b * blk, (b + 1) * blk)
            hi, mid, lo = _split3(-dcs_ref[:, cols])
            dlogf = _dot(hi, lower, NN) + _dot(mid, lower, NN) + _dot(lo, lower, NN) + carry
            carry = dlogf[:, 0:1]
            dfl = dlogf * sg_ref[:, cols]
            db = db + jnp.sum(dfl, axis=-1, keepdims=True)
            dfl_ref[:, cols] = dfl.astype(BF16)
        db_ref[...] = db
        dw_ref[...] = _dot(dfl_ref[...], h_ref[...], NN)

    return _pcall(name, body, (), [_full((16, T)), _full((16, T)), _full((T, D))],
                  [_full((16, T)), _full((16, D)), _full((16, 1))],
                  [S((16, T), BF16), S((16, D), F32), S((16, 1), F32)])(dcs, sg, h)


def _fox_logits(q, k, c_col, c_row, row0):
    s = _dot(q, k, NT) * (1.0 / math.sqrt(HEAD_DIM)) + (c_col - c_row)
    row = lax.broadcasted_iota(jnp.int32, s.shape, 0) + row0
    col = lax.broadcasted_iota(jnp.int32, s.shape, 1)
    return jnp.where(row >= col, s, NEG_INF)


def _fox_specs(T, n_pairs):
    qs = pl.BlockSpec((T, LANES), lambda p: (0, p))
    ks = pl.BlockSpec((T, LANES), lambda p: (0, n_pairs + p))
    vs = pl.BlockSpec((T, LANES), lambda p: (0, 2 * n_pairs + p))
    col = pl.BlockSpec((2, T, 1), lambda p: (p, 0, 0))
    rowv = pl.BlockSpec((None, 2, T), lambda p: (p, 0, 0))
    return qs, ks, vs, col, rowv


def _fox_fwd(qkv, c_col, c_row, name):
    T = qkv.shape[0]
    DA = qkv.shape[1] // 3
    n_pairs = DA // LANES
    tq = _tile(T, 256)

    def body(q_ref, k_ref, v_ref, c_ref, ct_ref, o_ref, lse_ref):
        for hh in range(2):
            sl = slice(hh * HEAD_DIM, (hh + 1) * HEAD_DIM)
            for i in range(T // tq):
                rows = slice(i * tq, (i + 1) * tq)
                kp = (i + 1) * tq
                s = _fox_logits(q_ref[rows, sl], k_ref[0:kp, sl], c_ref[hh, rows, :], ct_ref[hh:hh + 1, 0:kp], i * tq)
                m = jnp.max(s, axis=-1, keepdims=True)
                p = jnp.exp(s - m)
                l = jnp.sum(p, axis=-1, keepdims=True)
                o_ref[rows, sl] = _dot(p.astype(BF16), v_ref[0:kp, sl], NN) / l
                lse_ref[hh, rows, :] = m + jnp.log(l)

    qs, ks, vs, col, rowv = _fox_specs(T, n_pairs)
    return _pcall(name, body, (n_pairs,), [qs, ks, vs, col, rowv], [qs, col],
                  [S((T, DA), F32), S((2 * n_pairs, T, 1), F32)])(qkv, qkv, qkv, c_col, c_row)


def _fox_bwd(qkv, c_col, c_row, lse, do, name):
    T = qkv.shape[0]
    DA = qkv.shape[1] // 3
    n_pairs = DA // LANES
    tq = _tile(T, 256)
    scale = 1.0 / math.sqrt(HEAD_DIM)

    def body(q_ref, k_ref, v_ref, c_ref, ct_ref, lse_ref, do_ref, dq_ref, dk_ref, dv_ref, dcs_ref, dk_acc, dv_acc):
        dk_acc[...] = jnp.zeros_like(dk_acc)
        dv_acc[...] = jnp.zeros_like(dv_acc)
        dcs_ref[...] = jnp.zeros_like(dcs_ref)
        for hh in range(2):
            sl = slice(hh * HEAD_DIM, (hh + 1) * HEAD_DIM)
            for i in range(T // tq):
                rows = slice(i * tq, (i + 1) * tq)
                kp = (i + 1) * tq
                q = q_ref[rows, sl]
                k = k_ref[0:kp, sl]
                dob = do_ref[rows, sl]
                s = _fox_logits(q, k, c_ref[hh, rows, :], ct_ref[hh:hh + 1, 0:kp], i * tq)
                p = jnp.exp(s - lse_ref[hh, rows, :])
                dp = _dot(dob, v_ref[0:kp, sl], NT)
                ds = p * (dp - jnp.sum(p * dp, axis=-1, keepdims=True))
                dsb = ds.astype(BF16)
                dq_ref[rows, sl] = (_dot(dsb, k, NN) * scale).astype(BF16)
                dk_acc[0:kp, sl] += _dot(dsb, q, TN) * scale
                dv_acc[0:kp, sl] += _dot(p.astype(BF16), dob, TN)
                dcs_ref[hh:hh + 1, 0:kp] += _colsum(ds)
        dk_ref[...] = dk_acc[...].astype(BF16)
        dv_ref[...] = dv_acc[...].astype(BF16)

    qs, ks, vs, col, rowv = _fox_specs(T, n_pairs)
    return _pcall(name, body, (n_pairs,), [qs, ks, vs, col, rowv, col, qs], [qs, qs, qs, rowv],
                  [S((T, DA), BF16)] * 3 + [S((n_pairs, 2, T), F32)],
                  scratch=[pltpu.VMEM((T, LANES), F32), pltpu.VMEM((T, LANES), F32)])(qkv, qkv, qkv, c_col, c_row, lse, do)


def _conv_fwd(ag, w, b, name):
    T = ag.shape[0]
    DC = ag.shape[1] // 2
    nb = DC // LANES
    tr = _tile(T, 256)

    def body(a_ref, g_ref, w_ref, b_ref, y_ref, pad):
        pad[0:CONV_PAD, :] = jnp.zeros((CONV_PAD, LANES), F32)
        pad[CONV_PAD:CONV_PAD + T, :] = a_ref[...] * _sigmoid(g_ref[...])
        for r in range(T // tr):
            acc = jnp.zeros((tr, LANES), F32) + b_ref[...]
            for j in range(CONV_WIDTH):
                o = r * tr + CONV_PAD - (CONV_WIDTH - 1) + j
                acc = acc + w_ref[j:j + 1, :] * pad[o:o + tr, :]
            y_ref[r * tr:(r + 1) * tr, :] = acc

    blk = pl.BlockSpec((T, LANES), lambda c: (0, c))
    return _pcall(name, body, (nb,), [blk, pl.BlockSpec((T, LANES), lambda c: (0, nb + c)),
                                      pl.BlockSpec((CONV_PAD, LANES), lambda c: (0, c)), pl.BlockSpec((1, LANES), lambda c: (0, c))],
                  blk, S((T, DC), F32), scratch=[pltpu.VMEM((T + CONV_PAD, LANES), F32)])(ag, ag, w, b)


def _conv_bwd(dy, ag, w, name):
    T = ag.shape[0]
    DC = ag.shape[1] // 2
    nb = DC // LANES
    tr = _tile(T, 256)

    def body(dy_ref, a_ref, g_ref, w_ref, da_ref, dg_ref, dw_ref, db_ref, pad, dpad):
        av = a_ref[...]
        sg = _sigmoid(g_ref[...])
        pad[0:CONV_PAD, :] = jnp.zeros((CONV_PAD, LANES), F32)
        pad[CONV_PAD:CONV_PAD + T, :] = av * sg
        dpad[0:T, :] = dy_ref[...]
        dpad[T:T + CONV_PAD, :] = jnp.zeros((CONV_PAD, LANES), F32)
        db_ref[...] = _colsum(dy_ref[...])
        dw_ref[...] = jnp.zeros_like(dw_ref)
        for j in range(CONV_WIDTH):
            acc = jnp.zeros((tr, LANES), F32)
            for r in range(T // tr):
                o = r * tr + CONV_PAD - (CONV_WIDTH - 1) + j
                acc = acc + dpad[r * tr:(r + 1) * tr, :] * pad[o:o + tr, :]
            dw_ref[j:j + 1, :] = _colsum(acc)
        for r in range(T // tr):
            acc = jnp.zeros((tr, LANES), F32)
            for j in range(CONV_WIDTH):
                o = r * tr + (CONV_WIDTH - 1) - j
                acc = acc + w_ref[j:j + 1, :] * dpad[o:o + tr, :]
            rows = slice(r * tr, (r + 1) * tr)
            sgr = sg[rows, :]
            da_ref[rows, :] = (acc * sgr).astype(BF16)
            dg_ref[rows, :] = (acc * av[rows, :] * sgr * (1.0 - sgr)).astype(BF16)

    blk = pl.BlockSpec((T, LANES), lambda c: (0, c))
    wblk = pl.BlockSpec((CONV_PAD, LANES), lambda c: (0, c))
    return _pcall(name, body, (nb,), [blk, blk, pl.BlockSpec((T, LANES), lambda c: (0, nb + c)), wblk],
                  [blk, blk, wblk, pl.BlockSpec((1, LANES), lambda c: (0, c))],
                  [S((T, DC), BF16), S((T, DC), BF16), S((CONV_PAD, DC), F32), S((1, DC), F32)],
                  scratch=[pltpu.VMEM((T + CONV_PAD, LANES), F32), pltpu.VMEM((T + CONV_PAD, LANES), F32)])(dy, ag, ag, w)


def _conv_norms(yc, lg, lb):
    mu = jnp.mean(yc, axis=-1, keepdims=True)
    xc = yc - mu
    rs = lax.rsqrt(jnp.mean(xc * xc, axis=-1, keepdims=True) + EPS)
    xh = xc * rs
    z = xh * lg + lb
    sg = _sigmoid(z)
    return rs, xh, z, sg, z * sg


def _mix_post(attn, yc, ag, cg, lg, lb, name):
    T, DA = attn.shape
    DC = yc.shape[1]
    tm = _tile(T, ROW_TILE)

    def body(at_ref, yc_ref, ag_ref, cg_ref, lg_ref, lb_ref, y_ref):
        at = at_ref[...]
        y_ref[:, 0:DA] = (at * _rstd(at) * ag_ref[...]).astype(BF16)
        _, _, _, _, sv = _conv_norms(yc_ref[...], lg_ref[...], lb_ref[...])
        y_ref[:, DA:DA + DC] = (sv * _rstd(sv) * cg_ref[...]).astype(BF16)

    return _pcall(name, body, (T // tm,),
                  [pl.BlockSpec((tm, DA), lambda i: (i, 0)), pl.BlockSpec((tm, DC), lambda i: (i, 0)), _full((1, DA)),
                   _full((1, DC)), _full((1, DC)), _full((1, DC))],
                  pl.BlockSpec((tm, DA + DC), lambda i: (i, 0)), S((T, DA + DC), BF16))(attn, yc, ag, cg, lg, lb)


def _mix_post_bwd(dy, attn, yc, ag, cg, lg, lb, name):
    T, DA = attn.shape
    DC = yc.shape[1]
    tm = _tile(T, ROW_TILE)

    def body(dy_ref, at_ref, yc_ref, ag_ref, cg_ref, lg_ref, lb_ref, dat_ref, dyc_ref, dag_ref, dcg_ref, dlg_ref, dlb_ref):
        dat, dag_rows = _rms_bwd(dy_ref[:, 0:DA], at_ref[...], ag_ref[...])
        dat_ref[...] = dat.astype(BF16)
        lgv = lg_ref[...]
        rs, xh, z, sg, sv = _conv_norms(yc_ref[...], lgv, lb_ref[...])
        dsv, dcg_rows = _rms_bwd(dy_ref[:, DA:DA + DC], sv, cg_ref[...])
        dz = dsv * (sg * (1.0 + z * (1.0 - sg)))
        dxh = dz * lgv
        dyc_ref[...] = rs * (dxh - jnp.mean(dxh, axis=-1, keepdims=True) - xh * jnp.mean(dxh * xh, axis=-1, keepdims=True))

        @pl.when(pl.program_id(0) == 0)
        def _():
            for r in (dag_ref, dcg_ref, dlg_ref, dlb_ref):
                r[...] = jnp.zeros_like(r)

        dag_ref[...] += _colsum(dag_rows)
        dcg_ref[...] += _colsum(dcg_rows)
        dlg_ref[...] += _colsum(dz * xh)
        dlb_ref[...] += _colsum(dz)

    ra = pl.BlockSpec((tm, DA), lambda i: (i, 0))
    rc = pl.BlockSpec((tm, DC), lambda i: (i, 0))
    return _pcall(name, body, (T // tm,),
                  [pl.BlockSpec((tm, DA + DC), lambda i: (i, 0)), ra, rc, _full((1, DA)), _full((1, DC)), _full((1, DC)),
                   _full((1, DC))],
                  [ra, rc, _full((1, DA)), _full((1, DC)), _full((1, DC)), _full((1, DC))],
                  [S((T, DA), BF16), S((T, DC), F32), S((1, DA), F32), S((1, DC), F32), S((1, DC), F32), S((1, DC), F32)])(
        dy, attn, yc, ag, cg, lg, lb)


def _xattn_probs(q, k, xd):
    s = _dot(q, k, NT) * (1.0 / math.sqrt(xd))
    p = jnp.exp(s - jnp.max(s, axis=-1, keepdims=True))
    return p / jnp.sum(p, axis=-1, keepdims=True)


def _xattn_fwd(q, kv, name):
    T, D = q.shape
    M = kv.shape[0]
    xd = D // N_XATTN_HEADS
    tq = _tile(T, ROW_TILE)

    def body(q_ref, kv_ref, o_ref):
        for h in range(N_XATTN_HEADS):
            sl = slice(h * xd, (h + 1) * xd)
            p = _xattn_probs(q_ref[:, sl], kv_ref[:, sl], xd)
            o_ref[:, sl] = _dot(p.astype(BF16), kv_ref[:, D + h * xd:D + (h + 1) * xd], NN).astype(BF16)

    row = pl.BlockSpec((tq, D), lambda i: (i, 0))
    return _pcall(name, body, (T // tq,), [row, _full((M, 2 * D))], row, S((T, D), BF16))(q, kv)


def _xattn_bwd(q, kv, do, name):
    T, D = q.shape
    M = kv.shape[0]
    xd = D // N_XATTN_HEADS
    tq = _tile(T, ROW_TILE)
    scale = 1.0 / math.sqrt(xd)

    def body(q_ref, kv_ref, do_ref, dq_ref, dkv_ref):
        @pl.when(pl.program_id(0) == 0)
        def _():
            dkv_ref[...] = jnp.zeros_like(dkv_ref)

        for h in range(N_XATTN_HEADS):
            sl = slice(h * xd, (h + 1) * xd)
            vsl = slice(D + h * xd, D + (h + 1) * xd)
            qh = q_ref[:, sl]
            kh = kv_ref[:, sl]
            doh = do_ref[:, sl]
            p = _xattn_probs(qh, kh, xd)
            dp = _dot(doh, kv_ref[:, vsl], NT)
            ds = (p * (dp - jnp.sum(p * dp, axis=-1, keepdims=True)) * scale).astype(BF16)
            dq_ref[:, sl] = _dot(ds, kh, NN).astype(BF16)
            dkv_ref[:, sl] += _dot(ds, qh, TN)
            dkv_ref[:, vsl] += _dot(p.astype(BF16), doh, TN)

    row = pl.BlockSpec((tq, D), lambda i: (i, 0))
    return _pcall(name, body, (T // tq,), [row, _full((M, 2 * D)), row], [row, _full((M, 2 * D))],
                  [S((T, D), BF16), S((M, 2 * D), F32)])(q, kv, do)


def _adamw(w, m, v, g, name):
    shape = w.shape
    C = shape[-1]
    R = w.size // C
    tr = R if R <= 512 else _tile(R, 512)

    def body(w_ref, m_ref, v_ref, g_ref, d_ref, nm_ref, nv_ref):
        gv = g_ref[...]
        mv = ADAM_B1 * m_ref[...] + (1.0 - ADAM_B1) * gv
        vv = ADAM_B2 * v_ref[...] + (1.0 - ADAM_B2) * (gv * gv)
        m_hat = mv / (1.0 - ADAM_B1 ** ADAM_STEP)
        v_hat = vv / (1.0 - ADAM_B2 ** ADAM_STEP)
        d_ref[...] = -ADAM_LR * (m_hat / (jnp.sqrt(v_hat) + ADAM_EPS) + ADAM_WD * w_ref[...])
        nm_ref[...] = mv
        nv_ref[...] = vv

    blk = pl.BlockSpec((tr, C), lambda i: (i, 0))
    outs = _pcall(name, body, (R // tr,), [blk] * 4, [blk] * 3, [S((R, C), F32)] * 3)(
        w.reshape(R, C), m.reshape(R, C), v.reshape(R, C), g.reshape(R, C))
    return [o.reshape(shape) for o in outs]


def _place_scalars():
    return jnp.stack([lax.axis_index("c"), 2 * lax.axis_index("x") + lax.axis_index("y")]).astype(jnp.int32)


def _adamw_sum(w, m, v, owns, landed, name):
    L, p, q = w.shape
    qq = owns[0].shape[2]
    tr = _tile(p, 256)

    def body(place_ref, w_ref, m_ref, v_ref, *rest):
        own_refs, land_refs = rest[:L], rest[L:2 * L]
        g_ref, d_ref, nm_ref, nv_ref = rest[2 * L:]
        chip = place_ref[1]

        def update(l):
            own = own_refs[l][...].astype(F32)
            gs = None
            for k in range(4):
                term = jnp.where(chip == k, own, land_refs[l][k].astype(F32))
                gs = term if gs is None else gs + term
            gv = gs[:, 0:q]
            mv = ADAM_B1 * m_ref[...] + (1.0 - ADAM_B1) * gv
            vv = ADAM_B2 * v_ref[...] + (1.0 - ADAM_B2) * (gv * gv)
            m_hat = mv / (1.0 - ADAM_B1 ** ADAM_STEP)
            v_hat = vv / (1.0 - ADAM_B2 ** ADAM_STEP)
            g_ref[...] = gv
            d_ref[...] = -ADAM_LR * (m_hat / (jnp.sqrt(v_hat) + ADAM_EPS) + ADAM_WD * w_ref[...])
            nm_ref[...] = mv
            nv_ref[...] = vv

        for l in range(L):
            pl.when(pl.program_id(0) == l)(lambda l=l: update(l))

    def rows_of(layer):
        return lambda l, i, place: jnp.where(l == layer, i, 0)

    blk = pl.BlockSpec((None, tr, q), lambda l, i, place: (l, i, 0))
    in_specs = [blk, blk, blk]
    in_specs += [pl.BlockSpec((None, tr, qq), lambda l, i, place, r=rows_of(k): (0, r(l, i, place), 0)) for k in range(L)]
    in_specs += [pl.BlockSpec((4, None, tr, qq), lambda l, i, place, r=rows_of(k): (0, 0, r(l, i, place), 0)) for k in range(L)]
    gs = pltpu.PrefetchScalarGridSpec(num_scalar_prefetch=1, grid=(L, p // tr), in_specs=in_specs, out_specs=[blk] * 4)
    return pl.pallas_call(body, grid_spec=gs, out_shape=[S((L, p, q), F32)] * 4, name=name,
                          compiler_params=pltpu.CompilerParams(dimension_semantics=("arbitrary", "arbitrary")))(
        _place_scalars(), w, m, v, *owns, *landed)


def _pair_add(g, recv, axis, name):
    _, _, p, q = recv.shape

    def body(place_ref, g_ref, r_ref, o_ref, own_ref):
        s = (g_ref[...].astype(F32) + r_ref[...].astype(F32)).astype(BF16)
        o_ref[...] = s

        @pl.when(pl.program_id(0) == place_ref[1])
        def _():
            own_ref[...] = s

    if axis == 1:
        gspec = pl.BlockSpec((None, p, q), lambda k, place: (0, 2 * k + place[0], 0))
    else:
        gspec = pl.BlockSpec((None, p, q), lambda k, place: (0, 0, 2 * k + place[0]))
    part = pl.BlockSpec((None, None, p, q), lambda k, place: (k, 0, 0, 0))
    own = pl.BlockSpec((None, p, q), lambda k, place: (0, 0, 0))
    gs = pltpu.PrefetchScalarGridSpec(num_scalar_prefetch=1, grid=(4,), in_specs=[gspec, part], out_specs=[part, own])
    return pl.pallas_call(body, grid_spec=gs, out_shape=[S((4, 1, p, q), BF16), S((1, p, q), BF16)], name=name,
                          compiler_params=pltpu.CompilerParams(dimension_semantics=("arbitrary",)))(_place_scalars(), g, recv)


def _win_pieces(n_attn, n_heads, n_conv, shard, chunk):
    bounds = [0, 3 * n_attn, 3 * n_attn + n_heads, 3 * n_attn + n_heads + 2 * n_conv]
    pieces = []
    for j in range(N_DEV):
        lo, hi = shard * j, shard * (j + 1)
        for r in range(3):
            a, b = max(lo, bounds[r]), min(hi, bounds[r + 1])
            if a < b:
                pieces.append((r, a - bounds[r], b - bounds[r], chunk * j + a - lo))
    return pieces


def _win_split(w_in, pieces, widths, name):
    L, D, C = w_in.shape
    tr = _tile(D, 256)

    def body(x_ref, *outs):
        outs[1][...] = jnp.zeros_like(outs[1])
        for r, d0, d1, s0 in pieces:
            outs[r][:, d0:d1] = x_ref[:, s0:s0 + d1 - d0]

    return _pcall(name, body, (L, D // tr), [pl.BlockSpec((None, tr, C), lambda l, i: (l, i, 0))],
                  [pl.BlockSpec((None, tr, wd), lambda l, i: (l, i, 0)) for wd in widths],
                  [S((L, D, wd), BF16) for wd in widths])(w_in)


def _win_merge(parts, pieces, chunked_cols, name):
    L, D, _ = parts[0].shape
    tr = _tile(D, 256)

    def body(a_ref, b_ref, c_ref, o_ref):
        ins = (a_ref, b_ref, c_ref)
        o_ref[...] = jnp.zeros_like(o_ref)
        for r, d0, d1, s0 in pieces:
            o_ref[:, s0:s0 + d1 - d0] = ins[r][:, d0:d1]

    return _pcall(name, body, (L, D // tr), [pl.BlockSpec((None, tr, x.shape[2]), lambda l, i: (l, i, 0)) for x in parts],
                  pl.BlockSpec((None, tr, chunked_cols), lambda l, i: (l, i, 0)), S((L, D, chunked_cols), BF16))(*parts)


def _place():
    return lax.axis_index("x"), lax.axis_index("y"), lax.axis_index("c")


def _flip(v, f):
    return 1 - v if f else v


def _window(ref, axis, size, dev):
    start = dev * size if isinstance(dev, int) else pl.multiple_of(dev * size, LANES if axis == 2 else 16)
    return ref.at[:, pl.ds(start, size), :] if axis == 1 else ref.at[:, :, pl.ds(start, size)]


HBM_SPEC = pl.BlockSpec(memory_space=pltpu.HBM)
SEM_SPEC = pl.BlockSpec(memory_space=pltpu.SEMAPHORE)
SPLIT_COPY_PARAMS = dict(has_side_effects=pltpu.SideEffectType.DATAFLOW_SIDE_EFFECTING)


def _hbm(v):
    return pltpu.with_memory_space_constraint(v, pltpu.HBM)


def _full_shape(shard, axis):
    return tuple(N_DEV * d if i == axis else d for i, d in enumerate(shard.shape))


def _ag_peers(x, y, c):
    return [(x, y, 1 - c), (1 - x, y, c), (x, 1 - y, c), (1 - x, 1 - y, c)]


SIBLING_COLLECTIVE_ID = 0


def _sibling_handshake(x, y, c):
    barrier = pltpu.get_barrier_semaphore()
    pl.semaphore_signal(barrier, inc=1, device_id=(x, y, 1 - c), device_id_type=MESH)
    pl.semaphore_wait(barrier, 1)


def _ag_start(shards, axes, groups, after, name):
    n, ng = len(shards), len(groups)
    sizes = [s.shape[ax] for s, ax in zip(shards, axes)]
    where = {w: (g, i) for g, members in enumerate(groups) for i, w in enumerate(members)}

    def body(*refs):
        xs, fulls = refs[:n], refs[n + 1:2 * n + 1]
        send, recv = refs[3 * n + 1:3 * n + 1 + ng], refs[3 * n + 1 + ng:]
        x, y, c = _place()
        for members in groups:
            for w in members:
                g, i = where[w]
                for k, to in enumerate(_ag_peers(x, y, c)):
                    pltpu.make_async_remote_copy(
                        src_ref=xs[w], dst_ref=_window(fulls[w], axes[w], sizes[w], 4 * x + 2 * y + c),
                        send_sem=send[g].at[4 * i + k], recv_sem=recv[g].at[4 * i + k], device_id=to, device_id_type=MESH).start()

    sems = [pltpu.SemaphoreType.DMA((4 * len(m),)) for m in groups]
    outs = pl.pallas_call(
        body, name=name,
        out_shape=[pltpu.HBM(_full_shape(s, ax), s.dtype) for s, ax in zip(shards, axes)] + [pltpu.HBM(s.shape, s.dtype) for s in shards]
        + sems + sems,
        in_specs=[HBM_SPEC] * n + [pl.BlockSpec(memory_space=pl.ANY)], out_specs=[HBM_SPEC] * (2 * n) + [SEM_SPEC] * (2 * ng),
        input_output_aliases={w: n + w for w in range(n)},
        compiler_params=pltpu.CompilerParams(**SPLIT_COPY_PARAMS))(*[_hbm(s) for s in shards], after)
    return outs[:n], outs[n:2 * n], outs[2 * n:2 * n + ng], outs[2 * n + ng:]


def _ag_wait(shards, fulls, send_sems, recv_sems, axes, after, name):
    n = len(shards)
    sizes = [s.shape[ax] for s, ax in zip(shards, axes)]

    def body(*refs):
        xs = refs[:n]
        send, recv = refs[2 * n], refs[2 * n + 1]
        landed = refs[3 * n + 3:]
        x, y, c = _place()
        for w in range(n):
            for k, frm in enumerate(_ag_peers(x, y, c)):
                copy = pltpu.make_async_remote_copy(
                    src_ref=xs[w], dst_ref=_window(landed[w], axes[w], sizes[w], 4 * frm[0] + 2 * frm[1] + frm[2]),
                    send_sem=send.at[4 * w + k], recv_sem=recv.at[4 * w + k], device_id=frm, device_id_type=MESH)
                copy.wait_send()
                copy.wait_recv()

    outs = pl.pallas_call(
        body, name=name, out_shape=[pltpu.HBM(v.shape, v.dtype) for v in list(shards) + list(fulls)],
        in_specs=[HBM_SPEC] * (2 * n) + [SEM_SPEC, SEM_SPEC, pl.BlockSpec(memory_space=pl.ANY)], out_specs=[HBM_SPEC] * (2 * n),
        input_output_aliases={i: i for i in range(2 * n)},
        compiler_params=pltpu.CompilerParams(**SPLIT_COPY_PARAMS))(*shards, *fulls, send_sems, recv_sems, after)
    return outs[:n], outs[n:]


def _ag_forward(fulls, shards, axes, name):
    n = len(fulls)
    sizes = [s.shape[ax] for s, ax in zip(shards, axes)]

    def body(*refs):
        xs, full_refs = refs[:n], refs[2 * n:3 * n]
        send_sems, recv_sems, local_sems = refs[3 * n:]
        x, y, c = _place()
        _sibling_handshake(x, y, c)
        chips = [(1 - x, y), (x, 1 - y), (1 - x, 1 - y)]
        copies = []
        for w in range(n):
            mine = pltpu.make_async_copy(xs[w], _window(full_refs[w], axes[w], sizes[w], 4 * x + 2 * y + c), local_sems.at[w])
            mine.start()
            copies.append(mine)
            for j, (px, py) in enumerate(chips):
                sent = _window(full_refs[w], axes[w], sizes[w], 4 * px + 2 * py + c)
                got = _window(full_refs[w], axes[w], sizes[w], 4 * px + 2 * py + 1 - c)
                out = pltpu.make_async_remote_copy(src_ref=sent, dst_ref=sent, send_sem=send_sems.at[3 * w + j],
                                                   recv_sem=recv_sems.at[3 * w + j], device_id=(x, y, 1 - c), device_id_type=MESH)
                out.start()
                back = pltpu.make_async_remote_copy(src_ref=got, dst_ref=got, send_sem=send_sems.at[3 * w + j],
                                                    recv_sem=recv_sems.at[3 * w + j], device_id=(x, y, 1 - c), device_id_type=MESH)
                copies.append((out, back))
        for cp in copies:
            if isinstance(cp, tuple):
                cp[0].wait_send()
                cp[1].wait_recv()
            else:
                cp.wait()

    any_spec = pl.BlockSpec(memory_space=pl.ANY)
    outs = pl.pallas_call(
        body, name=name, out_shape=[S(f.shape, f.dtype) for f in fulls], in_specs=[any_spec] * (2 * n), out_specs=[any_spec] * n,
        input_output_aliases={n + w: w for w in range(n)},
        scratch_shapes=[pltpu.SemaphoreType.DMA((3 * n,)), pltpu.SemaphoreType.DMA((3 * n,)), pltpu.SemaphoreType.DMA((n,))],
        compiler_params=pltpu.CompilerParams(collective_id=SIBLING_COLLECTIVE_ID))(*shards, *fulls)
    return outs


def _to_sibling(grads, axes, sizes, name):
    n = len(grads)
    outs = []
    for g, ax, sz in zip(grads, axes, sizes):
        L, K, N = g.shape
        outs.append(S((4, L, sz, N) if ax == 1 else (4, L, K, sz), g.dtype))

    def body(*refs):
        g_refs, out_refs = refs[:n], refs[n:2 * n]
        send_sems, recv_sems = refs[2 * n:]
        x, y, c = _place()
        _sibling_handshake(x, y, c)
        copies = []
        for w in range(n):
            for k in range(4):
                copies.append(pltpu.make_async_remote_copy(
                    src_ref=_window(g_refs[w], axes[w], sizes[w], 2 * k + 1 - c), dst_ref=out_refs[w].at[k],
                    send_sem=send_sems.at[4 * w + k], recv_sem=recv_sems.at[4 * w + k], device_id=(x, y, 1 - c),
                    device_id_type=MESH))
        for cp in copies:
            cp.start()
        for cp in copies:
            cp.wait()

    any_spec = pl.BlockSpec(memory_space=pl.ANY)
    return pl.pallas_call(
        body, out_shape=outs, in_specs=[any_spec] * n, out_specs=[any_spec] * n, name=name,
        scratch_shapes=[pltpu.SemaphoreType.DMA((4 * n,)), pltpu.SemaphoreType.DMA((4 * n,))],
        compiler_params=pltpu.CompilerParams(collective_id=SIBLING_COLLECTIVE_ID))(*grads)


def _rs_copy(p_ref, out_ref, send_sems, recv_sems, w, rel, x, y, c):
    tx, ty = _flip(x, rel & 2), _flip(y, rel & 1)
    return pltpu.make_async_remote_copy(
        src_ref=p_ref.at[2 * tx + ty], dst_ref=out_ref.at[2 * x + y], send_sem=send_sems.at[3 * w + rel - 1],
        recv_sem=recv_sems.at[3 * w + rel - 1], device_id=(tx, ty, c), device_id_type=MESH)


def _rs_start(parts, name):
    n = len(parts)

    def body(*refs):
        p_refs, out_refs = refs[:n], refs[n:2 * n]
        send_sems, recv_sems, token = refs[3 * n:]
        x, y, c = _place()
        for w in range(n):
            for rel in (1, 2, 3):
                _rs_copy(p_refs[w], out_refs[w], send_sems, recv_sems, w, rel, x, y, c).start()
        token[...] = jnp.zeros_like(token)

    sems = pltpu.SemaphoreType.DMA((3 * n,))
    outs = pl.pallas_call(
        body, name=name,
        out_shape=[pltpu.HBM(p.shape, p.dtype) for p in parts] * 2 + [sems, sems, S((8, LANES), F32)],
        in_specs=[HBM_SPEC] * n, out_specs=[HBM_SPEC] * (2 * n) + [SEM_SPEC, SEM_SPEC, pl.BlockSpec(memory_space=pltpu.VMEM)],
        input_output_aliases={w: n + w for w in range(n)},
        compiler_params=pltpu.CompilerParams(**SPLIT_COPY_PARAMS))(*[_hbm(p) for p in parts])
    return outs[:n], outs[n:2 * n], outs[2 * n], outs[2 * n + 1], outs[2 * n + 2]


def _rs_wait(parts, landing, send_sems, recv_sems, after, name):
    n = len(parts)

    def body(*refs):
        p_refs = refs[:n]
        send, recv = refs[2 * n], refs[2 * n + 1]
        landed = refs[3 * n + 3:]
        x, y, c = _place()
        for w in range(n):
            for rel in (1, 2, 3):
                copy = _rs_copy(p_refs[w], landed[w], send, recv, w, rel, x, y, c)
                copy.wait_send()
                copy.wait_recv()

    outs = pl.pallas_call(
        body, name=name, out_shape=[pltpu.HBM(v.shape, v.dtype) for v in list(parts) + list(landing)],
        in_specs=[HBM_SPEC] * (2 * n) + [SEM_SPEC, SEM_SPEC, pl.BlockSpec(memory_space=pl.ANY)], out_specs=[HBM_SPEC] * (2 * n),
        input_output_aliases={i: i for i in range(2 * n)},
        compiler_params=pltpu.CompilerParams(**SPLIT_COPY_PARAMS))(*parts, *landing, send_sems, recv_sems, after)
    return outs[n:]


def _exchange_small(v, reduce, name):
    R, C = v.shape

    def body(v_ref, out_ref, gath, send_sems, recv_sems):
        x, y, c = _place()
        me = 4 * x + 2 * y + c
        buf = gath if reduce else out_ref
        buf[me] = v_ref[...]
        copies = []
        for rel in range(1, N_DEV):
            peer = (_flip(x, rel & 4), _flip(y, rel & 2), _flip(c, rel & 1))
            copies.append(pltpu.make_async_remote_copy(
                src_ref=v_ref, dst_ref=buf.at[me], send_sem=send_sems.at[rel - 1], recv_sem=recv_sems.at[rel - 1],
                device_id=peer, device_id_type=MESH))
        for cp in copies:
            cp.start()
        for cp in copies:
            cp.wait()
        if reduce:
            acc = gath[0]
            for d in range(1, N_DEV):
                acc = acc + gath[d]
            out_ref[...] = acc

    vm = pl.BlockSpec(memory_space=pltpu.VMEM)
    return pl.pallas_call(
        body, out_shape=S((R, C) if reduce else (N_DEV, R, C), F32), in_specs=[vm], out_specs=vm, name=name,
        scratch_shapes=[pltpu.VMEM((N_DEV, R, C) if reduce else (8, LANES), F32), pltpu.SemaphoreType.DMA((N_DEV - 1,)),
                        pltpu.SemaphoreType.DMA((N_DEV - 1,))])(v)


def _pad_rows(flat, cols, mult):
    n = flat.shape[-1]
    rows = -(-n // cols)
    rows = -(-rows // mult) * mult
    pad = [(0, 0)] * (flat.ndim - 1) + [(0, rows * cols - n)]
    return jnp.pad(flat, pad).reshape(flat.shape[:-1] + (rows, cols))


def _round_up(n, m):
    return -(-n // m) * m


def _shard_axes(a):
    return [(2, _round_up(a[n].shape[2], LANES)) if kind == 'col' else (1, _round_up(a[n].shape[1], LANES)) for n, kind in BIG]


def _pack_small(vals):
    rows = [_pad_rows(vals[n].astype(F32).reshape(-1), SMALL_COLS, 1) for n in SMALL]
    m = jnp.concatenate(rows, axis=0)
    return jnp.pad(m, ((0, -m.shape[0] % 8), (0, 0)))


def _unpack_small(m, a):
    out, r = {}, 0
    for n in SMALL:
        nr = -(-a[n].size // SMALL_COLS)
        out[n] = m[r:r + nr].reshape(-1)[:a[n].size].reshape(a[n].shape)
        r += nr
    return out, r


GROUPS = (('ffn1_w_gate', 'ffn1_w_up', 'ffn1_w_down'), ('w_in', 'w_out', 'xattn_w_q', 'xattn_w_kv', 'xattn_w_o'),
          ('ffn2_w_gate', 'ffn2_w_up', 'ffn2_w_down'))


def _layer_small(a, conv_w_full, l):
    H = a['b_f'].shape[1]
    return dict(
        bft=jnp.pad(a['b_f'][l].reshape(H, 1), ((0, 16 - H), (0, 0))),
        cw=jnp.pad(conv_w_full[l], ((0, CONV_PAD - CONV_WIDTH), (0, 0))), cb=a['conv_b'][l].reshape(1, -1),
        lg=a['conv_ln_g'][l].reshape(1, -1), lb=a['conv_ln_b'][l].reshape(1, -1),
        ag=a['attn_out_g'][l].reshape(1, -1), cg=a['conv_out_g'][l].reshape(1, -1),
        g1=a['ffn1_norm_g'][l], gm=a['mix_norm_g'][l], gx=a['xattn_norm_g'][l], gmem=a['mem_norm_g'][l], g2=a['ffn2_norm_g'][l])


def _layer_fwd(x0, mem, w, fetch, cfg, l):
    T = x0.shape[0]
    H = cfg['heads']
    sv = {'x0': x0}
    m = fetch(l, 0, x0)
    w.update(wg1=(m['ffn1_w_gate'], 0), wu1=(m['ffn1_w_up'], 0), wd1=(m['ffn1_w_down'], 0))
    sv['h1'] = _rms_fwd(x0, w['g1'], f"l{l}_ffn1_norm")
    sv['G1'], sv['U1'], sv['A1'] = _ffn_up(sv['h1'], w['wg1'], w['wu1'], f"l{l}_ffn1_up")
    x1 = sv['x1'] = _mm_res(sv['A1'], w['wd1'], x0, 0.5, f"l{l}_ffn1_down")
    m = fetch(l, 1, x1)
    wqkv, wf, wag = _win_split(m['w_in'], cfg['pieces'], cfg['widths'], f"l{l}_w_in_split")
    w.update(wqkv=(wqkv, 0), wft=wf[0, :, :16].T, wag=(wag, 0), wout=(m['w_out'], 0), wq=(m['xattn_w_q'], 0),
             wkv=(m['xattn_w_kv'], 0), wo=(m['xattn_w_o'], 0))
    h2 = sv['h2'] = _rms_fwd(x1, w['gm'], f"l{l}_mix_norm")
    sv['qkv'] = _mm(h2, w['wqkv'], BF16, f"l{l}_qkv_proj")
    sv['agv'] = _mm(h2, w['wag'], F32, f"l{l}_glu_proj")
    ct, sv['sg'] = _fox_prep(h2, w['wft'], w['bft'], f"l{l}_fox_prep")
    sv['c_col'] = ct[:H].reshape(H, T, 1)
    sv['c_row'] = ct[:H].reshape(H // 2, 2, T)
    sv['attn'], sv['lse'] = _fox_fwd(sv['qkv'], sv['c_col'], sv['c_row'], f"l{l}_fox_fwd")
    sv['yc'] = _conv_fwd(sv['agv'], w['cw'], w['cb'], f"l{l}_conv_fwd")
    sv['ycat'] = _mix_post(sv['attn'], sv['yc'], w['ag'], w['cg'], w['lg'], w['lb'], f"l{l}_mix_post")
    x2 = sv['x2'] = _mm_res(sv['ycat'], w['wout'], x1, 1.0, f"l{l}_out_proj")
    sv['h3'] = _rms_fwd(x2, w['gx'], f"l{l}_xattn_norm")
    sv['memn'] = _rms_fwd(mem, w['gmem'], f"l{l}_mem_norm")
    sv['q'] = _mm(sv['h3'], w['wq'], BF16, f"l{l}_xattn_q")
    sv['kv'] = _mm(sv['memn'], w['wkv'], BF16, f"l{l}_xattn_kv")
    sv['o'] = _xattn_fwd(sv['q'], sv['kv'], f"l{l}_xattn_fwd")
    x3 = sv['x3'] = _mm_res(sv['o'], w['wo'], x2, 1.0, f"l{l}_xattn_out")
    m = fetch(l, 2, x3)
    w.update(wg2=(m['ffn2_w_gate'], 0), wu2=(m['ffn2_w_up'], 0), wd2=(m['ffn2_w_down'], 0))
    sv['h4'] = _rms_fwd(x3, w['g2'], f"l{l}_ffn2_norm")
    sv['G2'], sv['U2'], sv['A2'] = _ffn_up(sv['h4'], w['wg2'], w['wu2'], f"l{l}_ffn2_up")
    return _mm_res(sv['A2'], w['wd2'], x3, 0.5, f"l{l}_ffn2_down"), sv


def _ffn_bwd(dout, x_in, h, G, U, A, wg, wu, wd, g, tag, put, which, dep, flush):
    dG, dU = _ffn_bwd_act(dout, wd, G, U, 0.5, tag + "_bwd_act", dep)
    put(which + '_w_down', A, dout, 0.5, tag + "_dwd")
    put(which + '_w_gate', h, dG, 1.0, tag + "_dwg")
    put(which + '_w_up', h, dU, 1.0, tag + "_dwu")
    dep = flush()
    dx, dg = _bwd_h([(dG, wg, 'nt'), (dU, wu, 'nt')], x_in, g, dout, tag + "_bwd_h", dep)
    return dx, dg, dep


def _layer_bwd(dx4, mem, w, sv, reduce, cfg, l, dep):
    small, grads = {}, {}
    T = dx4.shape[0]
    H = cfg['heads']

    def put(key, act, dy, scale, name):
        grads[key] = _wgrad(act, dy, scale, name, (None, 0, 1))

    dx3, small['ffn2_norm_g'], dep = _ffn_bwd(
        dx4, sv['x3'], sv['h4'], sv['G2'], sv['U2'], sv['A2'], w['wg2'], w['wu2'], w['wd2'], w['g2'], f"l{l}_ffn2", put, 'ffn2',
        dep, lambda: reduce(l, 2, {n: grads.pop(n) for n in GROUPS[2]}))
    do = _mm_nt(dx3, w['wo'], BF16, f"l{l}_xattn_do", dep)
    put('xattn_w_o', sv['o'], dx3, 1.0, f"l{l}_dwo")
    dq, dkv = _xattn_bwd(sv['q'], sv['kv'], do, f"l{l}_xattn_bwd")
    put('xattn_w_q', sv['h3'], dq, 1.0, f"l{l}_dwq")
    dx2, small['xattn_norm_g'] = _bwd_h([(dq, w['wq'], 'nt')], sv['x2'], w['gx'], dx3, f"l{l}_xattn_bwd_h")
    dmemn = _mm_nt(dkv, w['wkv'], F32, f"l{l}_dmemn")
    put('xattn_w_kv', sv['memn'], dkv, 1.0, f"l{l}_dwkv")
    small['mem_norm_g'] = _rms_gain_grad(dmemn, mem, w['gmem'], f"l{l}_dgmem")
    dycat = _mm_nt(dx2, w['wout'], F32, f"l{l}_dycat")
    put('w_out', sv['ycat'], dx2, 1.0, f"l{l}_dwout")
    dattn, dyc, small['attn_out_g'], small['conv_out_g'], small['conv_ln_g'], small['conv_ln_b'] = _mix_post_bwd(
        dycat, sv['attn'], sv['yc'], w['ag'], w['cg'], w['lg'], w['lb'], f"l{l}_mix_post_bwd")
    dva, dga, dcw, small['conv_b'] = _conv_bwd(dyc, sv['agv'], w['cw'], f"l{l}_conv_bwd")
    dq_, dk_, dv_, dcs = _fox_bwd(sv['qkv'], sv['c_col'], sv['c_row'], sv['lse'], dattn, f"l{l}_fox_bwd")
    dcs16 = jnp.pad(dcs.reshape(H, T), ((0, 16 - H), (0, 0)))
    dflt, dwft, dbf = _fox_prep_bwd(dcs16, sv['sg'], sv['h2'], f"l{l}_fox_prep_bwd")
    small['b_f'] = dbf[:H].reshape(H)
    dqkv = jnp.concatenate([dq_, dk_, dv_], axis=1)
    dag = jnp.concatenate([dva, dga], axis=1)
    put('wqkv', sv['h2'], dqkv, 1.0, f"l{l}_dwqkv")
    put('wag', sv['h2'], dag, 1.0, f"l{l}_dwag")
    dx1, small['mix_norm_g'] = _bwd_h([(dqkv, w['wqkv'], 'nt'), (dag, w['wag'], 'nt'), (dflt, w['wft'], 'tn')],
                                      sv['x1'], w['gm'], dx2, f"l{l}_mix_bwd_h")
    dwf = jnp.pad(dwft[:H].T, ((0, 0), (0, LANES - H)))[None].astype(BF16)
    grads['w_in'] = _win_merge((grads.pop('wqkv'), dwf, grads.pop('wag')), cfg['pieces'], cfg['chunked_cols'], f"l{l}_w_in_merge")
    dep = reduce(l, 1, {n: grads.pop(n) for n in GROUPS[1]})
    dx0, small['ffn1_norm_g'], dep = _ffn_bwd(
        dx1, sv['x0'], sv['h1'], sv['G1'], sv['U1'], sv['A1'], w['wg1'], w['wu1'], w['wd1'], w['g1'], f"l{l}_ffn1", put, 'ffn1',
        dep, lambda: reduce(l, 0, {n: grads.pop(n) for n in GROUPS[0]}))
    small = {k: v.reshape(-1) for k, v in small.items()}
    return dx0, small, dcw[:CONV_WIDTH], dep


def _local_step(x, mem, tgt, a, conv_w_full, fetch, reduce, cfg):
    L = a['b_f'].shape[0]
    ws = [_layer_small(a, conv_w_full, l) for l in range(L)]
    saved = []
    for l in range(L):
        x, sv = _layer_fwd(x, mem, ws[l], fetch, cfg, l)
        saved.append(sv)
    loss, dx, dgf = _loss_head(x, a['final_norm_g'], tgt, "loss_head")
    smalls, dcws, dep = [None] * L, [None] * L, None
    for l in range(L - 1, -1, -1):
        dx, smalls[l], dcws[l], dep = _layer_bwd(dx, mem, ws[l], saved[l], reduce, cfg, l, dep)
    small = {n: jnp.stack([smalls[l][n] for l in range(L)]) for n in SMALL if n != 'final_norm_g'}
    small['final_norm_g'] = dgf.reshape(-1)
    return loss, dx, small, jnp.stack(dcws)


def kernel(x, mem, ffn1_norm_g, ffn1_w_gate, ffn1_w_up, ffn1_w_down, mix_norm_g, w_in, b_f, conv_w, conv_b, conv_ln_g, conv_ln_b, attn_out_g, conv_out_g, w_out, xattn_norm_g, mem_norm_g, xattn_w_q, xattn_w_kv, xattn_w_o, ffn2_norm_g, ffn2_w_gate, ffn2_w_up, ffn2_w_down, final_norm_g, loss_target, m_ffn1_norm_g, m_ffn1_w_gate, m_ffn1_w_up, m_ffn1_w_down, m_mix_norm_g, m_w_in, m_b_f, m_conv_w, m_conv_b, m_conv_ln_g, m_conv_ln_b, m_attn_out_g, m_conv_out_g, m_w_out, m_xattn_norm_g, m_mem_norm_g, m_xattn_w_q, m_xattn_w_kv, m_xattn_w_o, m_ffn2_norm_g, m_ffn2_w_gate, m_ffn2_w_up, m_ffn2_w_down, m_final_norm_g, v_ffn1_norm_g, v_ffn1_w_gate, v_ffn1_w_up, v_ffn1_w_down, v_mix_norm_g, v_w_in, v_b_f, v_conv_w, v_conv_b, v_conv_ln_g, v_conv_ln_b, v_attn_out_g, v_conv_out_g, v_w_out, v_xattn_norm_g, v_mem_norm_g, v_xattn_w_q, v_xattn_w_kv, v_xattn_w_o, v_ffn2_norm_g, v_ffn2_w_gate, v_ffn2_w_up, v_ffn2_w_down, v_final_norm_g):
    args = (x, mem, ffn1_norm_g, ffn1_w_gate, ffn1_w_up, ffn1_w_down, mix_norm_g, w_in, b_f, conv_w, conv_b, conv_ln_g, conv_ln_b, attn_out_g, conv_out_g, w_out, xattn_norm_g, mem_norm_g, xattn_w_q, xattn_w_kv, xattn_w_o, ffn2_norm_g, ffn2_w_gate, ffn2_w_up, ffn2_w_down, final_norm_g)
    moments_m = (m_ffn1_norm_g, m_ffn1_w_gate, m_ffn1_w_up, m_ffn1_w_down, m_mix_norm_g, m_w_in, m_b_f, m_conv_w, m_conv_b, m_conv_ln_g, m_conv_ln_b, m_attn_out_g, m_conv_out_g, m_w_out, m_xattn_norm_g, m_mem_norm_g, m_xattn_w_q, m_xattn_w_kv, m_xattn_w_o, m_ffn2_norm_g, m_ffn2_w_gate, m_ffn2_w_up, m_ffn2_w_down, m_final_norm_g)
    moments_v = (v_ffn1_norm_g, v_ffn1_w_gate, v_ffn1_w_up, v_ffn1_w_down, v_mix_norm_g, v_w_in, v_b_f, v_conv_w, v_conv_b, v_conv_ln_g, v_conv_ln_b, v_attn_out_g, v_conv_out_g, v_w_out, v_xattn_norm_g, v_mem_norm_g, v_xattn_w_q, v_xattn_w_kv, v_xattn_w_o, v_ffn2_norm_g, v_ffn2_w_gate, v_ffn2_w_up, v_ffn2_w_down, v_final_norm_g)
    a = dict(zip(NAMES, args))
    am = dict(zip(WEIGHTS, moments_m))
    av = dict(zip(WEIGHTS, moments_v))
    L, taps, cshard = conv_w.shape
    dev = 4 * lax.axis_index("x") + 2 * lax.axis_index("y") + lax.axis_index("c")

    big_names = [n for n, _ in BIG]
    geometry = dict(zip(big_names, _shard_axes(a)))
    n_attn, n_heads, n_conv = attn_out_g.shape[1], b_f.shape[1], conv_out_g.shape[1]
    chunk = geometry['w_in'][1]
    cfg = dict(heads=n_heads, pieces=_win_pieces(n_attn, n_heads, n_conv, w_in.shape[2], chunk),
               widths=(3 * n_attn, LANES, 2 * n_conv), chunked_cols=N_DEV * chunk)

    cw_rows = _pad_rows(conv_w.reshape(-1), LANES, 8)
    cw_all = _exchange_small(cw_rows, False, "allgather_conv_w")
    conv_w_full = cw_all.reshape(N_DEV, -1)[:, :conv_w.size].reshape(N_DEV, L, taps, cshard).transpose(1, 2, 0, 3).reshape(
        L, taps, N_DEV * cshard)

    keys = [(l, n) for l in range(L) for names in GROUPS for n in names]
    members = [[keys.index((l, n)) for n in names] for l in range(L) for names in GROUPS]
    shards = []
    for l, n in keys:
        ax, size = geometry[n]
        pad = [(0, 0)] * 3
        pad[ax] = (0, size - a[n].shape[ax])
        shards.append(jnp.pad(a[n][l:l + 1].astype(BF16), pad))
    key_axes = [geometry[n][0] for _, n in keys]
    fulls, thru, ag_send, ag_recv = _ag_start(shards, key_axes, members, cw_all, "allgather_start")

    def fetch(l, gi, after):
        g = l * len(GROUPS) + gi
        axs = [key_axes[i] for i in members[g]]
        own, landed = _ag_wait([thru[i] for i in members[g]], [fulls[i] for i in members[g]], ag_send[g], ag_recv[g], axs, after,
                               f"allgather_wait_l{l}g{gi}")
        return dict(zip(GROUPS[gi], _ag_forward(landed, own, axs, f"allgather_forward_l{l}g{gi}")))

    pending, own_part, landed_part = [], {}, {}

    def reduce(l, gi, grads):
        names = GROUPS[gi]
        axs = [geometry[n][0] for n in names]
        recv = _to_sibling([grads[n] for n in names], axs, [geometry[n][1] for n in names], f"reduce_to_sibling_l{l}g{gi}")
        parts = []
        for n, r, ax in zip(names, recv, axs):
            part, own_part[(l, n)] = _pair_add(grads[n], r, ax, f"reduce_pair_add_l{l}_{n}")
            parts.append(part)
        landing, parts_thru, send, recv_sems, token = _rs_start(parts, f"reduce_start_l{l}g{gi}")
        pending.append((l, gi, parts_thru, landing, send, recv_sems))
        return token

    loss, grad_x, gsmall, dcw = _local_step(x[0], mem[0], loss_target[0], a, conv_w_full, fetch, reduce, cfg)

    def wait_group(entry, after):
        l, gi, parts_thru, landing, send, recv_sems = entry
        for n, arr in zip(GROUPS[gi], _rs_wait(parts_thru, landing, send, recv_sems, after, f"reduce_wait_l{l}g{gi}")):
            landed_part[(l, n)] = arr

    for entry in pending[:-1]:
        wait_group(entry, grad_x)

    small_rows = _pack_small(gsmall)
    n_small = small_rows.shape[0]
    dcw_rows = jnp.pad(dcw, ((0, 0), (0, CONV_PAD - taps), (0, 0))).reshape(-1, SMALL_COLS)
    summed = _exchange_small(jnp.concatenate([small_rows, dcw_rows], axis=0), True, "allreduce_small")
    g_small, _ = _unpack_small(summed[:n_small], a)
    dcw_sum = summed[n_small:].reshape(L, CONV_PAD, N_DEV * cshard)[:, :taps]

    grads = dict(g_small)
    grads['conv_w'] = lax.dynamic_slice_in_dim(dcw_sum, dev * cshard, cshard, axis=2)

    delta, new_m, new_v = {}, {}, {}
    def update(n):
        grads[n], delta[n], new_m[n], new_v[n] = _adamw_sum(
            a[n], am[n], av[n], [own_part[(l, n)] for l in range(L)], [landed_part[(l, n)] for l in range(L)], "adamw_" + n)

    last_names = GROUPS[pending[-1][1]]
    early = [n for n in big_names if n not in last_names]
    for n in early:
        update(n)
    wait_group(pending[-1], delta[early[-1]])
    for n in last_names:
        update(n)
    delta['conv_w'], new_m['conv_w'], new_v['conv_w'] = _adamw(conv_w, am['conv_w'], av['conv_w'], grads['conv_w'], "adamw_conv_w")
    pw, pm, pv, pg = (_pack_small(d) for d in (a, am, av, g_small))
    for dst, packed in zip((delta, new_m, new_v), _adamw(pw, pm, pv, pg, "adamw_small")):
        dst.update(_unpack_small(packed, a)[0])

    total = lax.psum(loss.reshape(()), ("x", "y", "c"))
    return (total, grad_x[None], *[grads[n] for n in WEIGHTS], *[delta[n] for n in WEIGHTS], *[new_m[n] for n in WEIGHTS],
            *[new_v[n] for n in WEIGHTS])
```

```python
import math

import jax
import jax.numpy as jnp
from jax import lax
from jax.experimental import pallas as pl
from jax.experimental.pallas import tpu as pltpu

F32, BF16 = jnp.float32, jnp.bfloat16
S = jax.ShapeDtypeStruct
MESH = pl.DeviceIdType.MESH

EPS = 1e-6
NEG_INF = -1e30
HEAD_DIM = 64
N_XATTN_HEADS = 4
CONV_WIDTH = 31
CONV_PAD = 32
LANES = 128
ADAM_LR, ADAM_B1, ADAM_B2, ADAM_EPS, ADAM_WD, ADAM_STEP = 0.001, 0.9, 0.999, 1e-08, 0.01, 10
N_DEV = 8
VMEM_LIMIT_BYTES = 56 * 1024 * 1024
ROW_TILE = 512
SMALL_COLS = 512

NN = ((1,), (0,))
NT = ((1,), (1,))
TN = ((0,), (0,))

NAMES = ['x', 'mem', 'ffn1_norm_g', 'ffn1_w_gate', 'ffn1_w_up', 'ffn1_w_down', 'mix_norm_g', 'w_in', 'b_f', 'conv_w', 'conv_b',
         'conv_ln_g', 'conv_ln_b', 'attn_out_g', 'conv_out_g', 'w_out', 'xattn_norm_g', 'mem_norm_g', 'xattn_w_q', 'xattn_w_kv',
         'xattn_w_o', 'ffn2_norm_g', 'ffn2_w_gate', 'ffn2_w_up', 'ffn2_w_down', 'final_norm_g']
WEIGHTS = NAMES[2:]
BIG = [('ffn1_w_gate', 'colT'), ('ffn1_w_up', 'colT'), ('ffn1_w_down', 'row'), ('w_in', 'col'), ('w_out', 'row'),
       ('xattn_w_q', 'row'), ('xattn_w_kv', 'colT'), ('xattn_w_o', 'row'), ('ffn2_w_gate', 'colT'), ('ffn2_w_up', 'colT'),
       ('ffn2_w_down', 'row')]
TRANSPOSED = tuple(n for n, kind in BIG if kind == 'colT')


def _as_handled(n, v):
    return jnp.swapaxes(v, 1, 2) if n in TRANSPOSED else v
SMALL = ['ffn1_norm_g', 'mix_norm_g', 'xattn_norm_g', 'mem_norm_g', 'ffn2_norm_g', 'conv_b', 'conv_ln_g', 'conv_ln_b',
         'attn_out_g', 'conv_out_g', 'b_f', 'final_norm_g']


def _dot(a, b, dims):
    return lax.dot_general(a, b, (dims, ((), ())), preferred_element_type=F32)


def _full(shape):
    nd = len(shape)
    return pl.BlockSpec(shape, lambda *_: (0,) * nd)


def _tile(n, pref):
    for t in (pref, 512, 384, 256, 128, 64, 32, 16, 8):
        if t <= n and n % t == 0:
            return t
    return n


def _pcall(name, body, grid, in_specs, out_specs, out_shape, scratch=(), aliases=None, dep=None):
    n_in = len(in_specs)
    kernel_body = body
    if dep is not None:
        in_specs = list(in_specs) + [pl.BlockSpec(memory_space=pl.ANY)]

        def kernel_body(*refs):
            return body(*refs[:n_in], *refs[n_in + 1:])

    call = pl.pallas_call(
        kernel_body, grid=grid, in_specs=in_specs, out_specs=out_specs, out_shape=out_shape, scratch_shapes=list(scratch),
        name=name, input_output_aliases=aliases or {},
        compiler_params=pltpu.CompilerParams(dimension_semantics=("arbitrary",) * len(grid), vmem_limit_bytes=VMEM_LIMIT_BYTES))
    return call if dep is None else (lambda *args: call(*args, dep))


def _arr(w):
    return w[0] if isinstance(w, tuple) else w


def _wshape(w):
    return w[0].shape[1:] if isinstance(w, tuple) else w.shape


def _wspec(w, block, imap):
    if isinstance(w, tuple):
        layer = w[1]
        return pl.BlockSpec((None,) + block, lambda *g: (layer,) + imap(*g))
    return pl.BlockSpec(block, imap)


def _wfull(w):
    shape = _wshape(w)
    return _wspec(w, shape, lambda *_: (0,) * len(shape))


def _sigmoid(z):
    return jax.nn.sigmoid(z)


def _rstd(x):
    return lax.rsqrt(jnp.mean(x * x, axis=-1, keepdims=True) + EPS)


def _rms_bwd(dy, x, g):
    r = _rstd(x)
    xh = x * r
    u = dy * g
    dx = r * (u - xh * jnp.mean(u * xh, axis=-1, keepdims=True))
    return dx, dy * xh


def _colsum(v):
    return jnp.sum(v, axis=0, keepdims=True)


def _rms_fwd(x, g, name):
    T, D = x.shape
    tm = _tile(T, ROW_TILE)

    def body(x_ref, g_ref, h_ref):
        xv = x_ref[...]
        h_ref[...] = (xv * _rstd(xv) * g_ref[...]).astype(BF16)

    row = pl.BlockSpec((tm, D), lambda i: (i, 0))
    return _pcall(name, body, (T // tm,), [row, _full((1, D))], row, S((T, D), BF16))(x, g.reshape(1, D))


def _mm(a, w, out_dtype, name):
    M, K = a.shape
    N = _wshape(w)[1]
    tm = _tile(M, ROW_TILE)
    tn = N if N <= 1536 else N // 2

    def body(a_ref, w_ref, o_ref):
        o_ref[...] = _dot(a_ref[...].astype(BF16), w_ref[...], NN).astype(out_dtype)

    return _pcall(name, body, (N // tn, M // tm),
                  [pl.BlockSpec((tm, K), lambda j, i: (i, 0)), _wspec(w, (K, tn), lambda j, i: (0, j))],
                  pl.BlockSpec((tm, tn), lambda j, i: (i, j)), S((M, N), out_dtype))(a, _arr(w))


def _mm_res(a, w, res, scale, name):
    M, K = a.shape
    N = _wshape(w)[1]
    tm = _tile(M, ROW_TILE)

    def body(a_ref, w_ref, r_ref, o_ref):
        o_ref[...] = r_ref[...] + scale * _dot(a_ref[...], w_ref[...], NN)

    row = pl.BlockSpec((tm, N), lambda i: (i, 0))
    return _pcall(name, body, (M // tm,), [pl.BlockSpec((tm, K), lambda i: (i, 0)), _wfull(w), row], row,
                  S((M, N), F32))(a, _arr(w), res)


def _mm_nt(a, w, out_dtype, name, dep=None):
    M, K = a.shape
    N = _wshape(w)[0]
    tm = _tile(M, ROW_TILE)
    tn = N if N <= 1536 else N // 2

    def body(a_ref, w_ref, o_ref):
        o_ref[...] = _dot(a_ref[...].astype(BF16), w_ref[...], NT).astype(out_dtype)

    return _pcall(name, body, (N // tn, M // tm),
                  [pl.BlockSpec((tm, K), lambda j, i: (i, 0)), _wspec(w, (tn, K), lambda j, i: (j, 0))],
                  pl.BlockSpec((tm, tn), lambda j, i: (i, j)), S((M, N), out_dtype), dep=dep)(a, _arr(w))


def _wgrad(a, dy, scale, name, into):
    buf, layer, L = into
    T, M = a.shape
    N = dy.shape[1]
    tm = _tile(M, 256)

    def body(a_ref, dy_ref, *rest):
        rest[-1][...] = (scale * _dot(a_ref[...].astype(BF16), dy_ref[...].astype(BF16), TN)).astype(BF16)

    in_specs = [pl.BlockSpec((T, tm), lambda i: (0, i)), _full((T, N))]
    args = [a, dy]
    if buf is not None:
        in_specs.append(pl.BlockSpec(memory_space=pl.ANY))
        args.append(buf)
    return _pcall(name, body, (M // tm,), in_specs, pl.BlockSpec((None, tm, N), lambda i: (layer, i, 0)), S((L, M, N), BF16),
                  aliases={2: 0} if buf is not None else None)(*args)


def _bwd_h(dots, x, g, dres, name, dep=None):
    T, D = x.shape
    tm = _tile(T, 256)
    n = len(dots)
    dims = [{'nt': NT, 'nn': NN, 'tn': TN}[m] for _, _, m in dots]

    def body(*refs):
        x_ref, g_ref, r_ref, dx_ref, dg_ref = refs[2 * n:]
        dh = None
        for k in range(n):
            part = _dot(refs[2 * k][...], refs[2 * k + 1][...], dims[k])
            dh = part if dh is None else dh + part
        dx, dgrow = _rms_bwd(dh, x_ref[...], g_ref[...])
        dx_ref[...] = r_ref[...] + dx

        @pl.when(pl.program_id(0) == 0)
        def _():
            dg_ref[...] = jnp.zeros_like(dg_ref)

        dg_ref[...] += _colsum(dgrow)

    in_specs, args = [], []
    for lhs, w, mode in dots:
        if mode == 'tn':
            in_specs.append(pl.BlockSpec((lhs.shape[0], tm), lambda i: (0, i)))
        else:
            in_specs.append(pl.BlockSpec((tm, lhs.shape[1]), lambda i: (i, 0)))
        in_specs.append(_wfull(w))
        args += [lhs, _arr(w)]
    row = pl.BlockSpec((tm, D), lambda i: (i, 0))
    in_specs += [row, _full((1, D)), row]
    return _pcall(name, body, (T // tm,), in_specs, [row, _full((1, D))], [S((T, D), F32), S((1, D), F32)], dep=dep)(
        *args, x, g.reshape(1, D), dres)


def _rms_gain_grad(dy, x, g, name):
    T, D = x.shape

    def body(dy_ref, x_ref, g_ref, dg_ref):
        _, dgrow = _rms_bwd(dy_ref[...], x_ref[...], g_ref[...])
        dg_ref[...] = _colsum(dgrow)

    return _pcall(name, body, (), [_full((T, D)), _full((T, D)), _full((1, D))], _full((1, D)), S((1, D), F32))(
        dy, x, g.reshape(1, D))


def _ffn_up(h, wg, wu, name):
    T, D = h.shape
    Fh = _wshape(wg)[0]
    tm = _tile(T, ROW_TILE)
    tn = Fh if Fh <= 1536 else Fh // 2

    def body(h_ref, wg_ref, wu_ref, g_ref, u_ref, a_ref):
        hv = h_ref[...]
        gv = _dot(hv, wg_ref[...], NT)
        uv = _dot(hv, wu_ref[...], NT)
        g_ref[...] = gv.astype(BF16)
        u_ref[...] = uv.astype(BF16)
        a_ref[...] = (gv * _sigmoid(gv) * uv).astype(BF16)

    tile = pl.BlockSpec((tm, tn), lambda j, i: (i, j))
    return _pcall(name, body, (Fh // tn, T // tm),
                  [pl.BlockSpec((tm, D), lambda j, i: (i, 0)), _wspec(wg, (tn, D), lambda j, i: (j, 0)),
                   _wspec(wu, (tn, D), lambda j, i: (j, 0))],
                  [tile, tile, tile], [S((T, Fh), BF16)] * 3)(h, _arr(wg), _arr(wu))


def _ffn_bwd_act(dout, wd, gate, up, scale, name, dep=None):
    T, D = dout.shape
    Fh = _wshape(wd)[0]
    tm = _tile(T, ROW_TILE)
    tn = Fh if Fh <= 1536 else Fh // 2

    def body(d_ref, w_ref, g_ref, u_ref, dg_ref, du_ref):
        da = scale * _dot(d_ref[...].astype(BF16), w_ref[...], NT)
        gv = g_ref[...].astype(F32)
        uv = u_ref[...].astype(F32)
        sg = _sigmoid(gv)
        dg_ref[...] = (da * uv * (sg * (1.0 + gv * (1.0 - sg)))).astype(BF16)
        du_ref[...] = (da * (gv * sg)).astype(BF16)

    tile = pl.BlockSpec((tm, tn), lambda j, i: (i, j))
    return _pcall(name, body, (Fh // tn, T // tm),
                  [pl.BlockSpec((tm, D), lambda j, i: (i, 0)), _wspec(wd, (tn, D), lambda j, i: (j, 0)), tile, tile],
                  [tile, tile], [S((T, Fh), BF16)] * 2, dep=dep)(dout, _arr(wd), gate, up)


def _loss_head(x, g, tgt, name):
    T, D = x.shape
    tm = _tile(T, ROW_TILE)

    def body(x_ref, g_ref, t_ref, loss_ref, dx_ref, dg_ref):
        xv = x_ref[...]
        gv = g_ref[...]
        r = _rstd(xv)
        xh = xv * r
        e = xh * gv - t_ref[...]
        dy = e * (1.0 / D)
        u = dy * gv
        dx_ref[...] = r * (u - xh * jnp.mean(u * xh, axis=-1, keepdims=True))

        @pl.when(pl.program_id(0) == 0)
        def _():
            dg_ref[...] = jnp.zeros_like(dg_ref)
            loss_ref[...] = jnp.zeros_like(loss_ref)

        dg_ref[...] += _colsum(dy * xh)
        loss_ref[...] += 0.5 * _colsum(jnp.mean(e * e, axis=-1, keepdims=True))

    row = pl.BlockSpec((tm, D), lambda i: (i, 0))
    return _pcall(name, body, (T // tm,), [row, _full((1, D)), row], [_full((1, 1)), row, _full((1, D))],
                  [S((1, 1), F32), S((T, D), F32), S((1, D), F32)])(x, g.reshape(1, D), tgt)


def _split3(xb):
    hi = xb.astype(BF16)
    r1 = xb - hi.astype(F32)
    mid = r1.astype(BF16)
    lo = (r1 - mid.astype(F32)).astype(BF16)
    return hi, mid, lo


def _fox_prep(h, wft, bft, name):
    T, D = h.shape
    blk = _tile(T, 256)

    def body(h_ref, w_ref, b_ref, ct_ref, sg_ref):
        z = _dot(w_ref[...], h_ref[...], NT) + b_ref[...]
        sg_ref[...] = 1.0 - _sigmoid(z)
        logf = jnp.minimum(z, 0.0) - jnp.log1p(jnp.exp(-jnp.abs(z)))
        upper = (lax.broadcasted_iota(jnp.int32, (blk, blk), 0) <= lax.broadcasted_iota(jnp.int32, (blk, blk), 1)).astype(BF16)
        carry = jnp.zeros((16, 1), F32)
        for b in range(T // blk):
            hi, mid, lo = _split3(logf[:, b * blk:(b + 1) * blk])
            cb = _dot(hi, upper, NN) + _dot(mid, upper, NN) + _dot(lo, upper, NN) + carry
            ct_ref[:, b * blk:(b + 1) * blk] = cb
            carry = cb[:, blk - 1:blk]

    return _pcall(name, body, (), [_full((T, D)), _full((16, D)), _full((16, 1))], [_full((16, T)), _full((16, T))],
                  [S((16, T), F32), S((16, T), F32)])(h, wft, bft)


def _fox_prep_bwd(dcs, sg, h, name):
    T, D = h.shape
    blk = _tile(T, 256)
    nb = T // blk

    def body(dcs_ref, sg_ref, h_ref, dfl_ref, dw_ref, db_ref):
        lower = (lax.broadcasted_iota(jnp.int32, (blk, blk), 0) >= lax.broadcasted_iota(jnp.int32, (blk, blk), 1)).astype(BF16)
        carry = jnp.zeros((16, 1), F32)
        db = jnp.zeros((16, 1), F32)
        for b in range(nb - 1, -1, -1):
            cols = slice(b * blk, (b + 1) * blk)
            hi, mid, lo = _split3(-dcs_ref[:, cols])
            dlogf = _dot(hi, lower, NN) + _dot(mid, lower, NN) + _dot(lo, lower, NN) + carry
            carry = dlogf[:, 0:1]
            dfl = dlogf * sg_ref[:, cols]
            db = db + jnp.sum(dfl, axis=-1, keepdims=True)
            dfl_ref[:, cols] = dfl.astype(BF16)
        db_ref[...] = db
        dw_ref[...] = _dot(dfl_ref[...], h_ref[...], NN)

    return _pcall(name, body, (), [_full((16, T)), _full((16, T)), _full((T, D))],
                  [_full((16, T)), _full((16, D)), _full((16, 1))],
                  [S((16, T), BF16), S((16, D), F32), S((16, 1), F32)])(dcs, sg, h)


def _fox_logits(q, k, c_col, c_row, row0):
    s = _dot(q, k, NT) * (1.0 / math.sqrt(HEAD_DIM)) + (c_col - c_row)
    row = lax.broadcasted_iota(jnp.int32, s.shape, 0) + row0
    col = lax.broadcasted_iota(jnp.int32, s.shape, 1)
    return jnp.where(row >= col, s, NEG_INF)


def _fox_specs(T, n_pairs):
    qs = pl.BlockSpec((T, LANES), lambda p: (0, p))
    ks = pl.BlockSpec((T, LANES), lambda p: (0, n_pairs + p))
    vs = pl.BlockSpec((T, LANES), lambda p: (0, 2 * n_pairs + p))
    col = pl.BlockSpec((2, T, 1), lambda p: (p, 0, 0))
    rowv = pl.BlockSpec((None, 2, T), lambda p: (p, 0, 0))
    return qs, ks, vs, col, rowv


def _fox_fwd(qkv, c_col, c_row, name):
    T = qkv.shape[0]
    DA = qkv.shape[1] // 3
    n_pairs = DA // LANES
    tq = _tile(T, 256)

    def body(q_ref, k_ref, v_ref, c_ref, ct_ref, o_ref, lse_ref):
        for hh in range(2):
            sl = slice(hh * HEAD_DIM, (hh + 1) * HEAD_DIM)
            for i in range(T // tq):
                rows = slice(i * tq, (i + 1) * tq)
                kp = (i + 1) * tq
                s = _fox_logits(q_ref[rows, sl], k_ref[0:kp, sl], c_ref[hh, rows, :], ct_ref[hh:hh + 1, 0:kp], i * tq)
                m = jnp.max(s, axis=-1, keepdims=True)
                p = jnp.exp(s - m)
                l = jnp.sum(p, axis=-1, keepdims=True)
                o_ref[rows, sl] = _dot(p.astype(BF16), v_ref[0:kp, sl], NN) / l
                lse_ref[hh, rows, :] = m + jnp.log(l)

    qs, ks, vs, col, rowv = _fox_specs(T, n_pairs)
    return _pcall(name, body, (n_pairs,), [qs, ks, vs, col, rowv], [qs, col],
                  [S((T, DA), F32), S((2 * n_pairs, T, 1), F32)])(qkv, qkv, qkv, c_col, c_row)


def _fox_bwd(qkv, c_col, c_row, lse, do, name):
    T = qkv.shape[0]
    DA = qkv.shape[1] // 3
    n_pairs = DA // LANES
    tq = _tile(T, 256)
    scale = 1.0 / math.sqrt(HEAD_DIM)

    def body(q_ref, k_ref, v_ref, c_ref, ct_ref, lse_ref, do_ref, dq_ref, dk_ref, dv_ref, dcs_ref, dk_acc, dv_acc):
        dk_acc[...] = jnp.zeros_like(dk_acc)
        dv_acc[...] = jnp.zeros_like(dv_acc)
        dcs_ref[...] = jnp.zeros_like(dcs_ref)
        for hh in range(2):
            sl = slice(hh * HEAD_DIM, (hh + 1) * HEAD_DIM)
            for i in range(T // tq):
                rows = slice(i * tq, (i + 1) * tq)
                kp = (i + 1) * tq
                q = q_ref[rows, sl]
                k = k_ref[0:kp, sl]
                dob = do_ref[rows, sl]
                s = _fox_logits(q, k, c_ref[hh, rows, :], ct_ref[hh:hh + 1, 0:kp], i * tq)
                p = jnp.exp(s - lse_ref[hh, rows, :])
                dp = _dot(dob, v_ref[0:kp, sl], NT)
                ds = p * (dp - jnp.sum(p * dp, axis=-1, keepdims=True))
                dsb = ds.astype(BF16)
                dq_ref[rows, sl] = (_dot(dsb, k, NN) * scale).astype(BF16)
                dk_acc[0:kp, sl] += _dot(dsb, q, TN) * scale
                dv_acc[0:kp, sl] += _dot(p.astype(BF16), dob, TN)
                dcs_ref[hh:hh + 1, 0:kp] += _colsum(ds)
        dk_ref[...] = dk_acc[...].astype(BF16)
        dv_ref[...] = dv_acc[...].astype(BF16)

    qs, ks, vs, col, rowv = _fox_specs(T, n_pairs)
    return _pcall(name, body, (n_pairs,), [qs, ks, vs, col, rowv, col, qs], [qs, qs, qs, rowv],
                  [S((T, DA), BF16)] * 3 + [S((n_pairs, 2, T), F32)],
                  scratch=[pltpu.VMEM((T, LANES), F32), pltpu.VMEM((T, LANES), F32)])(qkv, qkv, qkv, c_col, c_row, lse, do)


def _conv_fwd(ag, w, b, name):
    T = ag.shape[0]
    DC = ag.shape[1] // 2
    nb = DC // LANES
    tr = _tile(T, 256)

    def body(a_ref, g_ref, w_ref, b_ref, y_ref, pad):
        pad[0:CONV_PAD, :] = jnp.zeros((CONV_PAD, LANES), F32)
        pad[CONV_PAD:CONV_PAD + T, :] = a_ref[...] * _sigmoid(g_ref[...])
        for r in range(T // tr):
            acc = jnp.zeros((tr, LANES), F32) + b_ref[...]
            for j in range(CONV_WIDTH):
                o = r * tr + CONV_PAD - (CONV_WIDTH - 1) + j
                acc = acc + w_ref[j:j + 1, :] * pad[o:o + tr, :]
            y_ref[r * tr:(r + 1) * tr, :] = acc

    blk = pl.BlockSpec((T, LANES), lambda c: (0, c))
    return _pcall(name, body, (nb,), [blk, pl.BlockSpec((T, LANES), lambda c: (0, nb + c)),
                                      pl.BlockSpec((CONV_PAD, LANES), lambda c: (0, c)), pl.BlockSpec((1, LANES), lambda c: (0, c))],
                  blk, S((T, DC), F32), scratch=[pltpu.VMEM((T + CONV_PAD, LANES), F32)])(ag, ag, w, b)


def _conv_bwd(dy, ag, w, name):
    T = ag.shape[0]
    DC = ag.shape[1] // 2
    nb = DC // LANES
    tr = _tile(T, 256)

    def body(dy_ref, a_ref, g_ref, w_ref, da_ref, dg_ref, dw_ref, db_ref, pad, dpad):
        av = a_ref[...]
        sg = _sigmoid(g_ref[...])
        pad[0:CONV_PAD, :] = jnp.zeros((CONV_PAD, LANES), F32)
        pad[CONV_PAD:CONV_PAD + T, :] = av * sg
        dpad[0:T, :] = dy_ref[...]
        dpad[T:T + CONV_PAD, :] = jnp.zeros((CONV_PAD, LANES), F32)
        db_ref[...] = _colsum(dy_ref[...])
        dw_ref[...] = jnp.zeros_like(dw_ref)
        for j in range(CONV_WIDTH):
            acc = jnp.zeros((tr, LANES), F32)
            for r in range(T // tr):
                o = r * tr + CONV_PAD - (CONV_WIDTH - 1) + j
                acc = acc + dpad[r * tr:(r + 1) * tr, :] * pad[o:o + tr, :]
            dw_ref[j:j + 1, :] = _colsum(acc)
        for r in range(T // tr):
            acc = jnp.zeros((tr, LANES), F32)
            for j in range(CONV_WIDTH):
                o = r * tr + (CONV_WIDTH - 1) - j
                acc = acc + w_ref[j:j + 1, :] * dpad[o:o + tr, :]
            rows = slice(r * tr, (r + 1) * tr)
            sgr = sg[rows, :]
            da_ref[rows, :] = (acc * sgr).astype(BF16)
            dg_ref[rows, :] = (acc * av[rows, :] * sgr * (1.0 - sgr)).astype(BF16)

    blk = pl.BlockSpec((T, LANES), lambda c: (0, c))
    wblk = pl.BlockSpec((CONV_PAD, LANES), lambda c: (0, c))
    return _pcall(name, body, (nb,), [blk, blk, pl.BlockSpec((T, LANES), lambda c: (0, nb + c)), wblk],
                  [blk, blk, wblk, pl.BlockSpec((1, LANES), lambda c: (0, c))],
                  [S((T, DC), BF16), S((T, DC), BF16), S((CONV_PAD, DC), F32), S((1, DC), F32)],
                  scratch=[pltpu.VMEM((T + CONV_PAD, LANES), F32), pltpu.VMEM((T + CONV_PAD, LANES), F32)])(dy, ag, ag, w)


def _conv_norms(yc, lg, lb):
    mu = jnp.mean(yc, axis=-1, keepdims=True)
    xc = yc - mu
    rs = lax.rsqrt(jnp.mean(xc * xc, axis=-1, keepdims=True) + EPS)
    xh = xc * rs
    z = xh * lg + lb
    sg = _sigmoid(z)
    return rs, xh, z, sg, z * sg


def _mix_post(attn, yc, ag, cg, lg, lb, name):
    T, DA = attn.shape
    DC = yc.shape[1]
    tm = _tile(T, ROW_TILE)

    def body(at_ref, yc_ref, ag_ref, cg_ref, lg_ref, lb_ref, y_ref):
        at = at_ref[...]
        y_ref[:, 0:DA] = (at * _rstd(at) * ag_ref[...]).astype(BF16)
        _, _, _, _, sv = _conv_norms(yc_ref[...], lg_ref[...], lb_ref[...])
        y_ref[:, DA:DA + DC] = (sv * _rstd(sv) * cg_ref[...]).astype(BF16)

    return _pcall(name, body, (T // tm,),
                  [pl.BlockSpec((tm, DA), lambda i: (i, 0)), pl.BlockSpec((tm, DC), lambda i: (i, 0)), _full((1, DA)),
                   _full((1, DC)), _full((1, DC)), _full((1, DC))],
                  pl.BlockSpec((tm, DA + DC), lambda i: (i, 0)), S((T, DA + DC), BF16))(attn, yc, ag, cg, lg, lb)


def _mix_post_bwd(dy, attn, yc, ag, cg, lg, lb, name):
    T, DA = attn.shape
    DC = yc.shape[1]
    tm = _tile(T, ROW_TILE)

    def body(dy_ref, at_ref, yc_ref, ag_ref, cg_ref, lg_ref, lb_ref, dat_ref, dyc_ref, dag_ref, dcg_ref, dlg_ref, dlb_ref):
        dat, dag_rows = _rms_bwd(dy_ref[:, 0:DA], at_ref[...], ag_ref[...])
        dat_ref[...] = dat.astype(BF16)
        lgv = lg_ref[...]
        rs, xh, z, sg, sv = _conv_norms(yc_ref[...], lgv, lb_ref[...])
        dsv, dcg_rows = _rms_bwd(dy_ref[:, DA:DA + DC], sv, cg_ref[...])
        dz = dsv * (sg * (1.0 + z * (1.0 - sg)))
        dxh = dz * lgv
        dyc_ref[...] = rs * (dxh - jnp.mean(dxh, axis=-1, keepdims=True) - xh * jnp.mean(dxh * xh, axis=-1, keepdims=True))

        @pl.when(pl.program_id(0) == 0)
        def _():
            for r in (dag_ref, dcg_ref, dlg_ref, dlb_ref):
                r[...] = jnp.zeros_like(r)

        dag_ref[...] += _colsum(dag_rows)
        dcg_ref[...] += _colsum(dcg_rows)
        dlg_ref[...] += _colsum(dz * xh)
        dlb_ref[...] += _colsum(dz)

    ra = pl.BlockSpec((tm, DA), lambda i: (i, 0))
    rc = pl.BlockSpec((tm, DC), lambda i: (i, 0))
    return _pcall(name, body, (T // tm,),
                  [pl.BlockSpec((tm, DA + DC), lambda i: (i, 0)), ra, rc, _full((1, DA)), _full((1, DC)), _full((1, DC)),
                   _full((1, DC))],
                  [ra, rc, _full((1, DA)), _full((1, DC)), _full((1, DC)), _full((1, DC))],
                  [S((T, DA), BF16), S((T, DC), F32), S((1, DA), F32), S((1, DC), F32), S((1, DC), F32), S((1, DC), F32)])(
        dy, attn, yc, ag, cg, lg, lb)


def _xattn_probs(q, k, xd):
    s = _dot(q, k, NT) * (1.0 / math.sqrt(xd))
    p = jnp.exp(s - jnp.max(s, axis=-1, keepdims=True))
    return p / jnp.sum(p, axis=-1, keepdims=True)


def _xattn_fwd(q, kv, name):
    T, D = q.shape
    M = kv.shape[0]
    xd = D // N_XATTN_HEADS
    tq = _tile(T, ROW_TILE)

    def body(q_ref, kv_ref, o_ref):
        for h in range(N_XATTN_HEADS):
            sl = slice(h * xd, (h + 1) * xd)
            p = _xattn_probs(q_ref[:, sl], kv_ref[:, sl], xd)
            o_ref[:, sl] = _dot(p.astype(BF16), kv_ref[:, D + h * xd:D + (h + 1) * xd], NN).astype(BF16)

    row = pl.BlockSpec((tq, D), lambda i: (i, 0))
    return _pcall(name, body, (T // tq,), [row, _full((M, 2 * D))], row, S((T, D), BF16))(q, kv)


def _xattn_bwd(q, kv, do, name):
    T, D = q.shape
    M = kv.shape[0]
    xd = D // N_XATTN_HEADS
    tq = _tile(T, ROW_TILE)
    scale = 1.0 / math.sqrt(xd)

    def body(q_ref, kv_ref, do_ref, dq_ref, dkv_ref):
        @pl.when(pl.program_id(0) == 0)
        def _():
            dkv_ref[...] = jnp.zeros_like(dkv_ref)

        for h in range(N_XATTN_HEADS):
            sl = slice(h * xd, (h + 1) * xd)
            vsl = slice(D + h * xd, D + (h + 1) * xd)
            qh = q_ref[:, sl]
            kh = kv_ref[:, sl]
            doh = do_ref[:, sl]
            p = _xattn_probs(qh, kh, xd)
            dp = _dot(doh, kv_ref[:, vsl], NT)
            ds = (p * (dp - jnp.sum(p * dp, axis=-1, keepdims=True)) * scale).astype(BF16)
            dq_ref[:, sl] = _dot(ds, kh, NN).astype(BF16)
            dkv_ref[:, sl] += _dot(ds, qh, TN)
            dkv_ref[:, vsl] += _dot(p.astype(BF16), doh, TN)

    row = pl.BlockSpec((tq, D), lambda i: (i, 0))
    return _pcall(name, body, (T // tq,), [row, _full((M, 2 * D)), row], [row, _full((M, 2 * D))],
                  [S((T, D), BF16), S((M, 2 * D), F32)])(q, kv, do)


def _adamw(w, m, v, g, name):
    shape = w.shape
    C = shape[-1]
    R = w.size // C
    tr = R if R <= 512 else _tile(R, 512)

    def body(w_ref, m_ref, v_ref, g_ref, d_ref, nm_ref, nv_ref):
        gv = g_ref[...]
        mv = ADAM_B1 * m_ref[...] + (1.0 - ADAM_B1) * gv
        vv = ADAM_B2 * v_ref[...] + (1.0 - ADAM_B2) * (gv * gv)
        m_hat = mv / (1.0 - ADAM_B1 ** ADAM_STEP)
        v_hat = vv / (1.0 - ADAM_B2 ** ADAM_STEP)
        d_ref[...] = -ADAM_LR * (m_hat / (jnp.sqrt(v_hat) + ADAM_EPS) + ADAM_WD * w_ref[...])
        nm_ref[...] = mv
        nv_ref[...] = vv

    blk = pl.BlockSpec((tr, C), lambda i: (i, 0))
    outs = _pcall(name, body, (R // tr,), [blk] * 4, [blk] * 3, [S((R, C), F32)] * 3)(
        w.reshape(R, C), m.reshape(R, C), v.reshape(R, C), g.reshape(R, C))
    return [o.reshape(shape) for o in outs]


def _place_scalars():
    return jnp.stack([lax.axis_index("c"), 2 * lax.axis_index("x") + lax.axis_index("y")]).astype(jnp.int32)


def _adamw_sum(w, m, v, owns, landed, name):
    L, p, q = w.shape
    qq = owns[0].shape[2]
    tr = _tile(p, 256)

    def body(place_ref, w_ref, m_ref, v_ref, *rest):
        own_refs, land_refs = rest[:L], rest[L:2 * L]
        g_ref, d_ref, nm_ref, nv_ref = rest[2 * L:]
        chip = place_ref[1]

        def update(l):
            own = own_refs[l][...].astype(F32)
            gs = None
            for k in range(4):
                term = jnp.where(chip == k, own, land_refs[l][k].astype(F32))
                gs = term if gs is None else gs + term
            gv = gs[:, 0:q]
            mv = ADAM_B1 * m_ref[...] + (1.0 - ADAM_B1) * gv
            vv = ADAM_B2 * v_ref[...] + (1.0 - ADAM_B2) * (gv * gv)
            m_hat = mv / (1.0 - ADAM_B1 ** ADAM_STEP)
            v_hat = vv / (1.0 - ADAM_B2 ** ADAM_STEP)
            g_ref[...] = gv
            d_ref[...] = -ADAM_LR * (m_hat / (jnp.sqrt(v_hat) + ADAM_EPS) + ADAM_WD * w_ref[...])
            nm_ref[...] = mv
            nv_ref[...] = vv

        for l in range(L):
            pl.when(pl.program_id(0) == l)(lambda l=l: update(l))

    def rows_of(layer):
        return lambda l, i, place: jnp.where(l == layer, i, 0)

    blk = pl.BlockSpec((None, tr, q), lambda l, i, place: (l, i, 0))
    in_specs = [blk, blk, blk]
    in_specs += [pl.BlockSpec((None, tr, qq), lambda l, i, place, r=rows_of(k): (0, r(l, i, place), 0)) for k in range(L)]
    in_specs += [pl.BlockSpec((4, None, tr, qq), lambda l, i, place, r=rows_of(k): (0, 0, r(l, i, place), 0)) for k in range(L)]
    gs = pltpu.PrefetchScalarGridSpec(num_scalar_prefetch=1, grid=(L, p // tr), in_specs=in_specs, out_specs=[blk] * 4)
    return pl.pallas_call(body, grid_spec=gs, out_shape=[S((L, p, q), F32)] * 4, name=name,
                          compiler_params=pltpu.CompilerParams(dimension_semantics=("arbitrary", "arbitrary")))(
        _place_scalars(), w, m, v, *owns, *landed)


def _pair_add(g, recv, axis, name):
    _, _, p, q = recv.shape

    def body(place_ref, g_ref, r_ref, o_ref, own_ref):
        s = (g_ref[...].astype(F32) + r_ref[...].astype(F32)).astype(BF16)
        o_ref[...] = s

        @pl.when(pl.program_id(0) == place_ref[1])
        def _():
            own_ref[...] = s

    if axis == 1:
        gspec = pl.BlockSpec((None, p, q), lambda k, place: (0, 2 * k + place[0], 0))
    else:
        gspec = pl.BlockSpec((None, p, q), lambda k, place: (0, 0, 2 * k + place[0]))
    part = pl.BlockSpec((None, None, p, q), lambda k, place: (k, 0, 0, 0))
    own = pl.BlockSpec((None, p, q), lambda k, place: (0, 0, 0))
    gs = pltpu.PrefetchScalarGridSpec(num_scalar_prefetch=1, grid=(4,), in_specs=[gspec, part], out_specs=[part, own])
    return pl.pallas_call(body, grid_spec=gs, out_shape=[S((4, 1, p, q), BF16), S((1, p, q), BF16)], name=name,
                          compiler_params=pltpu.CompilerParams(dimension_semantics=("arbitrary",)))(_place_scalars(), g, recv)


def _win_pieces(n_attn, n_heads, n_conv, shard, chunk):
    bounds = [0, 3 * n_attn, 3 * n_attn + n_heads, 3 * n_attn + n_heads + 2 * n_conv]
    pieces = []
    for j in range(N_DEV):
        lo, hi = shard * j, shard * (j + 1)
        for r in range(3):
            a, b = max(lo, bounds[r]), min(hi, bounds[r + 1])
            if a < b:
                pieces.append((r, a - bounds[r], b - bounds[r], chunk * j + a - lo))
    return pieces


def _win_split(w_in, pieces, widths, name):
    L, D, C = w_in.shape
    tr = _tile(D, 256)

    def body(x_ref, *outs):
        outs[1][...] = jnp.zeros_like(outs[1])
        for r, d0, d1, s0 in pieces:
            outs[r][:, d0:d1] = x_ref[:, s0:s0 + d1 - d0]

    return _pcall(name, body, (L, D // tr), [pl.BlockSpec((None, tr, C), lambda l, i: (l, i, 0))],
                  [pl.BlockSpec((None, tr, wd), lambda l, i: (l, i, 0)) for wd in widths],
                  [S((L, D, wd), BF16) for wd in widths])(w_in)


def _win_merge(parts, pieces, chunked_cols, name):
    L, D, _ = parts[0].shape
    tr = _tile(D, 256)

    def body(a_ref, b_ref, c_ref, o_ref):
        ins = (a_ref, b_ref, c_ref)
        o_ref[...] = jnp.zeros_like(o_ref)
        for r, d0, d1, s0 in pieces:
            o_ref[:, s0:s0 + d1 - d0] = ins[r][:, d0:d1]

    return _pcall(name, body, (L, D // tr), [pl.BlockSpec((None, tr, x.shape[2]), lambda l, i: (l, i, 0)) for x in parts],
                  pl.BlockSpec((None, tr, chunked_cols), lambda l, i: (l, i, 0)), S((L, D, chunked_cols), BF16))(*parts)


def _place():
    return lax.axis_index("x"), lax.axis_index("y"), lax.axis_index("c")


def _flip(v, f):
    return 1 - v if f else v


def _window(ref, axis, size, dev):
    start = dev * size if isinstance(dev, int) else pl.multiple_of(dev * size, LANES if axis == 2 else 16)
    return ref.at[:, pl.ds(start, size), :] if axis == 1 else ref.at[:, :, pl.ds(start, size)]


HBM_SPEC = pl.BlockSpec(memory_space=pltpu.HBM)
SEM_SPEC = pl.BlockSpec(memory_space=pltpu.SEMAPHORE)
SPLIT_COPY_PARAMS = dict(has_side_effects=pltpu.SideEffectType.DATAFLOW_SIDE_EFFECTING)


def _hbm(v):
    return pltpu.with_memory_space_constraint(v, pltpu.HBM)


def _full_shape(shard, axis):
    return tuple(N_DEV * d if i == axis else d for i, d in enumerate(shard.shape))


def _ag_peers(x, y, c):
    return [(x, y, 1 - c), (1 - x, y, c), (x, 1 - y, c), (1 - x, 1 - y, c)]


SIBLING_COLLECTIVE_ID = 0


def _sibling_handshake(x, y, c):
    barrier = pltpu.get_barrier_semaphore()
    pl.semaphore_signal(barrier, inc=1, device_id=(x, y, 1 - c), device_id_type=MESH)
    pl.semaphore_wait(barrier, 1)


def _ag_start(shards, axes, groups, after, name):
    n, ng = len(shards), len(groups)
    sizes = [s.shape[ax] for s, ax in zip(shards, axes)]
    where = {w: (g, i) for g, members in enumerate(groups) for i, w in enumerate(members)}

    def body(*refs):
        xs, fulls = refs[:n], refs[n + 1:2 * n + 1]
        send, recv = refs[3 * n + 1:3 * n + 1 + ng], refs[3 * n + 1 + ng:]
        x, y, c = _place()
        for members in groups:
            for w in members:
                g, i = where[w]
                for k, to in enumerate(_ag_peers(x, y, c)):
                    pltpu.make_async_remote_copy(
                        src_ref=xs[w], dst_ref=_window(fulls[w], axes[w], sizes[w], 4 * x + 2 * y + c),
                        send_sem=send[g].at[4 * i + k], recv_sem=recv[g].at[4 * i + k], device_id=to, device_id_type=MESH).start()

    sems = [pltpu.SemaphoreType.DMA((4 * len(m),)) for m in groups]
    outs = pl.pallas_call(
        body, name=name,
        out_shape=[pltpu.HBM(_full_shape(s, ax), s.dtype) for s, ax in zip(shards, axes)] + [pltpu.HBM(s.shape, s.dtype) for s in shards]
        + sems + sems,
        in_specs=[HBM_SPEC] * n + [pl.BlockSpec(memory_space=pl.ANY)], out_specs=[HBM_SPEC] * (2 * n) + [SEM_SPEC] * (2 * ng),
        input_output_aliases={w: n + w for w in range(n)},
        compiler_params=pltpu.CompilerParams(**SPLIT_COPY_PARAMS))(*[_hbm(s) for s in shards], after)
    return outs[:n], outs[n:2 * n], outs[2 * n:2 * n + ng], outs[2 * n + ng:]


def _ag_wait(shards, fulls, send_sems, recv_sems, axes, after, name):
    n = len(shards)
    sizes = [s.shape[ax] for s, ax in zip(shards, axes)]

    def body(*refs):
        xs = refs[:n]
        send, recv = refs[2 * n], refs[2 * n + 1]
        landed = refs[3 * n + 3:]
        x, y, c = _place()
        for w in range(n):
            for k, frm in enumerate(_ag_peers(x, y, c)):
                copy = pltpu.make_async_remote_copy(
                    src_ref=xs[w], dst_ref=_window(landed[w], axes[w], sizes[w], 4 * frm[0] + 2 * frm[1] + frm[2]),
                    send_sem=send.at[4 * w + k], recv_sem=recv.at[4 * w + k], device_id=frm, device_id_type=MESH)
                copy.wait_send()
                copy.wait_recv()

    outs = pl.pallas_call(
        body, name=name, out_shape=[pltpu.HBM(v.shape, v.dtype) for v in list(shards) + list(fulls)],
        in_specs=[HBM_SPEC] * (2 * n) + [SEM_SPEC, SEM_SPEC, pl.BlockSpec(memory_space=pl.ANY)], out_specs=[HBM_SPEC] * (2 * n),
        input_output_aliases={i: i for i in range(2 * n)},
        compiler_params=pltpu.CompilerParams(**SPLIT_COPY_PARAMS))(*shards, *fulls, send_sems, recv_sems, after)
    return outs[:n], outs[n:]


def _ag_forward(fulls, shards, axes, name):
    n = len(fulls)
    sizes = [s.shape[ax] for s, ax in zip(shards, axes)]

    def body(*refs):
        xs, full_refs = refs[:n], refs[2 * n:3 * n]
        send_sems, recv_sems, local_sems = refs[3 * n:]
        x, y, c = _place()
        _sibling_handshake(x, y, c)
        chips = [(1 - x, y), (x, 1 - y), (1 - x, 1 - y)]
        copies = []
        for w in range(n):
            mine = pltpu.make_async_copy(xs[w], _window(full_refs[w], axes[w], sizes[w], 4 * x + 2 * y + c), local_sems.at[w])
            mine.start()
            copies.append(mine)
            for j, (px, py) in enumerate(chips):
                sent = _window(full_refs[w], axes[w], sizes[w], 4 * px + 2 * py + c)
                got = _window(full_refs[w], axes[w], sizes[w], 4 * px + 2 * py + 1 - c)
                out = pltpu.make_async_remote_copy(src_ref=sent, dst_ref=sent, send_sem=send_sems.at[3 * w + j],
                                                   recv_sem=recv_sems.at[3 * w + j], device_id=(x, y, 1 - c), device_id_type=MESH)
                out.start()
                back = pltpu.make_async_remote_copy(src_ref=got, dst_ref=got, send_sem=send_sems.at[3 * w + j],
                                                    recv_sem=recv_sems.at[3 * w + j], device_id=(x, y, 1 - c), device_id_type=MESH)
                copies.append((out, back))
        for cp in copies:
            if isinstance(cp, tuple):
                cp[0].wait_send()
                cp[1].wait_recv()
            else:
                cp.wait()

    any_spec = pl.BlockSpec(memory_space=pl.ANY)
    outs = pl.pallas_call(
        body, name=name, out_shape=[S(f.shape, f.dtype) for f in fulls], in_specs=[any_spec] * (2 * n), out_specs=[any_spec] * n,
        input_output_aliases={n + w: w for w in range(n)},
        scratch_shapes=[pltpu.SemaphoreType.DMA((3 * n,)), pltpu.SemaphoreType.DMA((3 * n,)), pltpu.SemaphoreType.DMA((n,))],
        compiler_params=pltpu.CompilerParams(collective_id=SIBLING_COLLECTIVE_ID))(*shards, *fulls)
    return outs


def _to_sibling(grads, axes, sizes, name):
    n = len(grads)
    outs = []
    for g, ax, sz in zip(grads, axes, sizes):
        L, K, N = g.shape
        outs.append(S((4, L, sz, N) if ax == 1 else (4, L, K, sz), g.dtype))

    def body(*refs):
        g_refs, out_refs = refs[:n], refs[n:2 * n]
        send_sems, recv_sems = refs[2 * n:]
        x, y, c = _place()
        _sibling_handshake(x, y, c)
        copies = []
        for w in range(n):
            for k in range(4):
                copies.append(pltpu.make_async_remote_copy(
                    src_ref=_window(g_refs[w], axes[w], sizes[w], 2 * k + 1 - c), dst_ref=out_refs[w].at[k],
                    send_sem=send_sems.at[4 * w + k], recv_sem=recv_sems.at[4 * w + k], device_id=(x, y, 1 - c),
                    device_id_type=MESH))
        for cp in copies:
            cp.start()
        for cp in copies:
            cp.wait()

    any_spec = pl.BlockSpec(memory_space=pl.ANY)
    return pl.pallas_call(
        body, out_shape=outs, in_specs=[any_spec] * n, out_specs=[any_spec] * n, name=name,
        scratch_shapes=[pltpu.SemaphoreType.DMA((4 * n,)), pltpu.SemaphoreType.DMA((4 * n,))],
        compiler_params=pltpu.CompilerParams(collective_id=SIBLING_COLLECTIVE_ID))(*grads)


def _rs_copy(p_ref, out_ref, send_sems, recv_sems, w, rel, x, y, c):
    tx, ty = _flip(x, rel & 2), _flip(y, rel & 1)
    return pltpu.make_async_remote_copy(
        src_ref=p_ref.at[2 * tx + ty], dst_ref=out_ref.at[2 * x + y], send_sem=send_sems.at[3 * w + rel - 1],
        recv_sem=recv_sems.at[3 * w + rel - 1], device_id=(tx, ty, c), device_id_type=MESH)


def _rs_start(parts, name):
    n = len(parts)

    def body(*refs):
        p_refs, out_refs = refs[:n], refs[n:2 * n]
        send_sems, recv_sems, token = refs[3 * n:]
        x, y, c = _place()
        for w in range(n):
            for rel in (1, 2, 3):
                _rs_copy(p_refs[w], out_refs[w], send_sems, recv_sems, w, rel, x, y, c).start()
        token[...] = jnp.zeros_like(token)

    sems = pltpu.SemaphoreType.DMA((3 * n,))
    outs = pl.pallas_call(
        body, name=name,
        out_shape=[pltpu.HBM(p.shape, p.dtype) for p in parts] * 2 + [sems, sems, S((8, LANES), F32)],
        in_specs=[HBM_SPEC] * n, out_specs=[HBM_SPEC] * (2 * n) + [SEM_SPEC, SEM_SPEC, pl.BlockSpec(memory_space=pltpu.VMEM)],
        input_output_aliases={w: n + w for w in range(n)},
        compiler_params=pltpu.CompilerParams(**SPLIT_COPY_PARAMS))(*[_hbm(p) for p in parts])
    return outs[:n], outs[n:2 * n], outs[2 * n], outs[2 * n + 1], outs[2 * n + 2]


def _rs_wait(parts, landing, send_sems, recv_sems, after, name):
    n = len(parts)

    def body(*refs):
        p_refs = refs[:n]
        send, recv = refs[2 * n], refs[2 * n + 1]
        landed = refs[3 * n + 3:]
        x, y, c = _place()
        for w in range(n):
            for rel in (1, 2, 3):
                copy = _rs_copy(p_refs[w], landed[w], send, recv, w, rel, x, y, c)
                copy.wait_send()
                copy.wait_recv()

    outs = pl.pallas_call(
        body, name=name, out_shape=[pltpu.HBM(v.shape, v.dtype) for v in list(parts) + list(landing)],
        in_specs=[HBM_SPEC] * (2 * n) + [SEM_SPEC, SEM_SPEC, pl.BlockSpec(memory_space=pl.ANY)], out_specs=[HBM_SPEC] * (2 * n),
        input_output_aliases={i: i for i in range(2 * n)},
        compiler_params=pltpu.CompilerParams(**SPLIT_COPY_PARAMS))(*parts, *landing, send_sems, recv_sems, after)
    return outs[n:]


def _exchange_small(v, reduce, name, after=None):
    R, C = v.shape

    def body(v_ref, *rest):
        out_ref, gath, send_sems, recv_sems = rest[-4:]
        x, y, c = _place()
        me = 4 * x + 2 * y + c
        buf = gath if reduce else out_ref
        buf[me] = v_ref[...]
        copies = []
        for rel in range(1, N_DEV):
            peer = (_flip(x, rel & 4), _flip(y, rel & 2), _flip(c, rel & 1))
            copies.append(pltpu.make_async_remote_copy(
                src_ref=v_ref, dst_ref=buf.at[me], send_sem=send_sems.at[rel - 1], recv_sem=recv_sems.at[rel - 1],
                device_id=peer, device_id_type=MESH))
        for cp in copies:
            cp.start()
        for cp in copies:
            cp.wait()
        if reduce:
            acc = gath[0]
            for d in range(1, N_DEV):
                acc = acc + gath[d]
            out_ref[...] = acc

    vm = pl.BlockSpec(memory_space=pltpu.VMEM)
    extra = [] if after is None else [after]
    return pl.pallas_call(
        body, out_shape=S((R, C) if reduce else (N_DEV, R, C), F32), in_specs=[vm] + [pl.BlockSpec(memory_space=pl.ANY)] * len(extra),
        out_specs=vm, name=name,
        scratch_shapes=[pltpu.VMEM((N_DEV, R, C) if reduce else (8, LANES), F32), pltpu.SemaphoreType.DMA((N_DEV - 1,)),
                        pltpu.SemaphoreType.DMA((N_DEV - 1,))])(v, *extra)


def _pad_rows(flat, cols, mult):
    n = flat.shape[-1]
    rows = -(-n // cols)
    rows = -(-rows // mult) * mult
    pad = [(0, 0)] * (flat.ndim - 1) + [(0, rows * cols - n)]
    return jnp.pad(flat, pad).reshape(flat.shape[:-1] + (rows, cols))


def _round_up(n, m):
    return -(-n // m) * m


def _shard_axes(a):
    return [(2, _round_up(a[n].shape[2], LANES)) if kind == 'col' else (1, _round_up(a[n].shape[1 if kind == 'row' else 2], LANES))
            for n, kind in BIG]


def _pack_small(vals):
    rows = [_pad_rows(vals[n].astype(F32).reshape(-1), SMALL_COLS, 1) for n in SMALL]
    m = jnp.concatenate(rows, axis=0)
    return jnp.pad(m, ((0, -m.shape[0] % 8), (0, 0)))


def _unpack_small(m, a):
    out, r = {}, 0
    for n in SMALL:
        nr = -(-a[n].size // SMALL_COLS)
        out[n] = m[r:r + nr].reshape(-1)[:a[n].size].reshape(a[n].shape)
        r += nr
    return out, r


GROUPS = (('ffn1_w_gate', 'ffn1_w_up', 'ffn1_w_down'), ('w_in', 'w_out', 'xattn_w_q', 'xattn_w_kv', 'xattn_w_o'),
          ('ffn2_w_gate', 'ffn2_w_up', 'ffn2_w_down'))


def _layer_small(a, conv_w_full, l):
    H = a['b_f'].shape[1]
    return dict(
        bft=jnp.pad(a['b_f'][l].reshape(H, 1), ((0, 16 - H), (0, 0))),
        cw=jnp.pad(conv_w_full[l], ((0, CONV_PAD - CONV_WIDTH), (0, 0))), cb=a['conv_b'][l].reshape(1, -1),
        lg=a['conv_ln_g'][l].reshape(1, -1), lb=a['conv_ln_b'][l].reshape(1, -1),
        ag=a['attn_out_g'][l].reshape(1, -1), cg=a['conv_out_g'][l].reshape(1, -1),
        g1=a['ffn1_norm_g'][l], gm=a['mix_norm_g'][l], gx=a['xattn_norm_g'][l], gmem=a['mem_norm_g'][l], g2=a['ffn2_norm_g'][l])


def _layer_fwd(x0, mem, w, fetch, cfg, l):
    T = x0.shape[0]
    H = cfg['heads']
    sv = {'x0': x0}
    m = fetch(l, 0, x0)
    w.update(wg1=(m['ffn1_w_gate'], 0), wu1=(m['ffn1_w_up'], 0), wd1=(m['ffn1_w_down'], 0))
    sv['h1'] = _rms_fwd(x0, w['g1'], f"l{l}_ffn1_norm")
    sv['G1'], sv['U1'], sv['A1'] = _ffn_up(sv['h1'], w['wg1'], w['wu1'], f"l{l}_ffn1_up")
    x1 = sv['x1'] = _mm_res(sv['A1'], w['wd1'], x0, 0.5, f"l{l}_ffn1_down")
    m = fetch(l, 1, x1)
    wqkv, wf, wag = _win_split(m['w_in'], cfg['pieces'], cfg['widths'], f"l{l}_w_in_split")
    w.update(wqkv=(wqkv, 0), wft=wf[0, :, :16].T, wag=(wag, 0), wout=(m['w_out'], 0), wq=(m['xattn_w_q'], 0),
             wkv=(m['xattn_w_kv'], 0), wo=(m['xattn_w_o'], 0))
    h2 = sv['h2'] = _rms_fwd(x1, w['gm'], f"l{l}_mix_norm")
    sv['qkv'] = _mm(h2, w['wqkv'], BF16, f"l{l}_qkv_proj")
    sv['agv'] = _mm(h2, w['wag'], F32, f"l{l}_glu_proj")
    ct, sv['sg'] = _fox_prep(h2, w['wft'], w['bft'], f"l{l}_fox_prep")
    sv['c_col'] = ct[:H].reshape(H, T, 1)
    sv['c_row'] = ct[:H].reshape(H // 2, 2, T)
    sv['attn'], sv['lse'] = _fox_fwd(sv['qkv'], sv['c_col'], sv['c_row'], f"l{l}_fox_fwd")
    sv['yc'] = _conv_fwd(sv['agv'], w['cw'], w['cb'], f"l{l}_conv_fwd")
    sv['ycat'] = _mix_post(sv['attn'], sv['yc'], w['ag'], w['cg'], w['lg'], w['lb'], f"l{l}_mix_post")
    x2 = sv['x2'] = _mm_res(sv['ycat'], w['wout'], x1, 1.0, f"l{l}_out_proj")
    sv['h3'] = _rms_fwd(x2, w['gx'], f"l{l}_xattn_norm")
    sv['memn'] = _rms_fwd(mem, w['gmem'], f"l{l}_mem_norm")
    sv['q'] = _mm(sv['h3'], w['wq'], BF16, f"l{l}_xattn_q")
    sv['kv'] = _mm_nt(sv['memn'], w['wkv'], BF16, f"l{l}_xattn_kv")
    sv['o'] = _xattn_fwd(sv['q'], sv['kv'], f"l{l}_xattn_fwd")
    x3 = sv['x3'] = _mm_res(sv['o'], w['wo'], x2, 1.0, f"l{l}_xattn_out")
    m = fetch(l, 2, x3)
    w.update(wg2=(m['ffn2_w_gate'], 0), wu2=(m['ffn2_w_up'], 0), wd2=(m['ffn2_w_down'], 0))
    sv['h4'] = _rms_fwd(x3, w['g2'], f"l{l}_ffn2_norm")
    sv['G2'], sv['U2'], sv['A2'] = _ffn_up(sv['h4'], w['wg2'], w['wu2'], f"l{l}_ffn2_up")
    return _mm_res(sv['A2'], w['wd2'], x3, 0.5, f"l{l}_ffn2_down"), sv


def _ffn_bwd(dout, x_in, h, G, U, A, wg, wu, wd, g, tag, put, which, dep, flush):
    dG, dU = _ffn_bwd_act(dout, wd, G, U, 0.5, tag + "_bwd_act", dep)
    put(which + '_w_down', A, dout, 0.5, tag + "_dwd")
    put(which + '_w_gate', dG, h, 1.0, tag + "_dwg")
    put(which + '_w_up', dU, h, 1.0, tag + "_dwu")
    dep = flush()
    dx, dg = _bwd_h([(dG, wg, 'nn'), (dU, wu, 'nn')], x_in, g, dout, tag + "_bwd_h", dep)
    return dx, dg, dep


def _layer_bwd(dx4, mem, w, sv, reduce, cfg, l, dep):
    small, grads = {}, {}
    T = dx4.shape[0]
    H = cfg['heads']

    def put(key, act, dy, scale, name):
        grads[key] = _wgrad(act, dy, scale, name, (None, 0, 1))

    dx3, small['ffn2_norm_g'], dep = _ffn_bwd(
        dx4, sv['x3'], sv['h4'], sv['G2'], sv['U2'], sv['A2'], w['wg2'], w['wu2'], w['wd2'], w['g2'], f"l{l}_ffn2", put, 'ffn2',
        dep, lambda: reduce(l, 2, {n: grads.pop(n) for n in GROUPS[2]}))
    do = _mm_nt(dx3, w['wo'], BF16, f"l{l}_xattn_do", dep)
    put('xattn_w_o', sv['o'], dx3, 1.0, f"l{l}_dwo")
    dq, dkv = _xattn_bwd(sv['q'], sv['kv'], do, f"l{l}_xattn_bwd")
    put('xattn_w_q', sv['h3'], dq, 1.0, f"l{l}_dwq")
    dx2, small['xattn_norm_g'] = _bwd_h([(dq, w['wq'], 'nt')], sv['x2'], w['gx'], dx3, f"l{l}_xattn_bwd_h")
    dmemn = _mm(dkv, w['wkv'], F32, f"l{l}_dmemn")
    put('xattn_w_kv', dkv, sv['memn'], 1.0, f"l{l}_dwkv")
    small['mem_norm_g'] = _rms_gain_grad(dmemn, mem, w['gmem'], f"l{l}_dgmem")
    dycat = _mm_nt(dx2, w['wout'], F32, f"l{l}_dycat")
    put('w_out', sv['ycat'], dx2, 1.0, f"l{l}_dwout")
    dattn, dyc, small['attn_out_g'], small['conv_out_g'], small['conv_ln_g'], small['conv_ln_b'] = _mix_post_bwd(
        dycat, sv['attn'], sv['yc'], w['ag'], w['cg'], w['lg'], w['lb'], f"l{l}_mix_post_bwd")
    dva, dga, dcw, small['conv_b'] = _conv_bwd(dyc, sv['agv'], w['cw'], f"l{l}_conv_bwd")
    dq_, dk_, dv_, dcs = _fox_bwd(sv['qkv'], sv['c_col'], sv['c_row'], sv['lse'], dattn, f"l{l}_fox_bwd")
    dcs16 = jnp.pad(dcs.reshape(H, T), ((0, 16 - H), (0, 0)))
    dflt, dwft, dbf = _fox_prep_bwd(dcs16, sv['sg'], sv['h2'], f"l{l}_fox_prep_bwd")
    small['b_f'] = dbf[:H].reshape(H)
    dqkv = jnp.concatenate([dq_, dk_, dv_], axis=1)
    dag = jnp.concatenate([dva, dga], axis=1)
    put('wqkv', sv['h2'], dqkv, 1.0, f"l{l}_dwqkv")
    put('wag', sv['h2'], dag, 1.0, f"l{l}_dwag")
    dx1, small['mix_norm_g'] = _bwd_h([(dqkv, w['wqkv'], 'nt'), (dag, w['wag'], 'nt'), (dflt, w['wft'], 'tn')],
                                      sv['x1'], w['gm'], dx2, f"l{l}_mix_bwd_h")
    dwf = jnp.pad(dwft[:H].T, ((0, 0), (0, LANES - H)))[None].astype(BF16)
    grads['w_in'] = _win_merge((grads.pop('wqkv'), dwf, grads.pop('wag')), cfg['pieces'], cfg['chunked_cols'], f"l{l}_w_in_merge")
    dep = reduce(l, 1, {n: grads.pop(n) for n in GROUPS[1]})
    dx0, small['ffn1_norm_g'], dep = _ffn_bwd(
        dx1, sv['x0'], sv['h1'], sv['G1'], sv['U1'], sv['A1'], w['wg1'], w['wu1'], w['wd1'], w['g1'], f"l{l}_ffn1", put, 'ffn1',
        dep, lambda: reduce(l, 0, {n: grads.pop(n) for n in GROUPS[0]}))
    small = {k: v.reshape(-1) for k, v in small.items()}
    return dx0, small, dcw[:CONV_WIDTH], dep


def _local_step(x, mem, tgt, a, conv_w_full, fetch, reduce, cfg):
    L = a['b_f'].shape[0]
    ws = [_layer_small(a, conv_w_full, l) for l in range(L)]
    saved = []
    for l in range(L):
        x, sv = _layer_fwd(x, mem, ws[l], fetch, cfg, l)
        saved.append(sv)
    loss, dx, dgf = _loss_head(x, a['final_norm_g'], tgt, "loss_head")
    smalls, dcws, dep = [None] * L, [None] * L, None
    for l in range(L - 1, -1, -1):
        dx, smalls[l], dcws[l], dep = _layer_bwd(dx, mem, ws[l], saved[l], reduce, cfg, l, dep)
    small = {n: jnp.stack([smalls[l][n] for l in range(L)]) for n in SMALL if n != 'final_norm_g'}
    small['final_norm_g'] = dgf.reshape(-1)
    return loss, dx, small, jnp.stack(dcws)


def kernel(x, mem, ffn1_norm_g, ffn1_w_gate, ffn1_w_up, ffn1_w_down, mix_norm_g, w_in, b_f, conv_w, conv_b, conv_ln_g, conv_ln_b, attn_out_g, conv_out_g, w_out, xattn_norm_g, mem_norm_g, xattn_w_q, xattn_w_kv, xattn_w_o, ffn2_norm_g, ffn2_w_gate, ffn2_w_up, ffn2_w_down, final_norm_g, loss_target, m_ffn1_norm_g, m_ffn1_w_gate, m_ffn1_w_up, m_ffn1_w_down, m_mix_norm_g, m_w_in, m_b_f, m_conv_w, m_conv_b, m_conv_ln_g, m_conv_ln_b, m_attn_out_g, m_conv_out_g, m_w_out, m_xattn_norm_g, m_mem_norm_g, m_xattn_w_q, m_xattn_w_kv, m_xattn_w_o, m_ffn2_norm_g, m_ffn2_w_gate, m_ffn2_w_up, m_ffn2_w_down, m_final_norm_g, v_ffn1_norm_g, v_ffn1_w_gate, v_ffn1_w_up, v_ffn1_w_down, v_mix_norm_g, v_w_in, v_b_f, v_conv_w, v_conv_b, v_conv_ln_g, v_conv_ln_b, v_attn_out_g, v_conv_out_g, v_w_out, v_xattn_norm_g, v_mem_norm_g, v_xattn_w_q, v_xattn_w_kv, v_xattn_w_o, v_ffn2_norm_g, v_ffn2_w_gate, v_ffn2_w_up, v_ffn2_w_down, v_final_norm_g):
    args = (x, mem, ffn1_norm_g, ffn1_w_gate, ffn1_w_up, ffn1_w_down, mix_norm_g, w_in, b_f, conv_w, conv_b, conv_ln_g, conv_ln_b, attn_out_g, conv_out_g, w_out, xattn_norm_g, mem_norm_g, xattn_w_q, xattn_w_kv, xattn_w_o, ffn2_norm_g, ffn2_w_gate, ffn2_w_up, ffn2_w_down, final_norm_g)
    moments_m = (m_ffn1_norm_g, m_ffn1_w_gate, m_ffn1_w_up, m_ffn1_w_down, m_mix_norm_g, m_w_in, m_b_f, m_conv_w, m_conv_b, m_conv_ln_g, m_conv_ln_b, m_attn_out_g, m_conv_out_g, m_w_out, m_xattn_norm_g, m_mem_norm_g, m_xattn_w_q, m_xattn_w_kv, m_xattn_w_o, m_ffn2_norm_g, m_ffn2_w_gate, m_ffn2_w_up, m_ffn2_w_down, m_final_norm_g)
    moments_v = (v_ffn1_norm_g, v_ffn1_w_gate, v_ffn1_w_up, v_ffn1_w_down, v_mix_norm_g, v_w_in, v_b_f, v_conv_w, v_conv_b, v_conv_ln_g, v_conv_ln_b, v_attn_out_g, v_conv_out_g, v_w_out, v_xattn_norm_g, v_mem_norm_g, v_xattn_w_q, v_xattn_w_kv, v_xattn_w_o, v_ffn2_norm_g, v_ffn2_w_gate, v_ffn2_w_up, v_ffn2_w_down, v_final_norm_g)
    a = dict(zip(NAMES, args))
    am = dict(zip(WEIGHTS, moments_m))
    av = dict(zip(WEIGHTS, moments_v))
    L, taps, cshard = conv_w.shape
    dev = 4 * lax.axis_index("x") + 2 * lax.axis_index("y") + lax.axis_index("c")

    big_names = [n for n, _ in BIG]
    geometry = dict(zip(big_names, _shard_axes(a)))
    n_attn, n_heads, n_conv = attn_out_g.shape[1], b_f.shape[1], conv_out_g.shape[1]
    chunk = geometry['w_in'][1]
    cfg = dict(heads=n_heads, pieces=_win_pieces(n_attn, n_heads, n_conv, w_in.shape[2], chunk),
               widths=(3 * n_attn, LANES, 2 * n_conv), chunked_cols=N_DEV * chunk)

    cw_rows = _pad_rows(conv_w.reshape(-1), LANES, 8)
    cw_all = _exchange_small(cw_rows, False, "allgather_conv_w")
    conv_w_full = cw_all.reshape(N_DEV, -1)[:, :conv_w.size].reshape(N_DEV, L, taps, cshard).transpose(1, 2, 0, 3).reshape(
        L, taps, N_DEV * cshard)

    keys = [(l, n) for l in range(L) for names in GROUPS for n in names]
    members = [[keys.index((l, n)) for n in names] for l in range(L) for names in GROUPS]
    shards = []
    for l, n in keys:
        ax, size = geometry[n]
        shard = _as_handled(n, a[n][l:l + 1]).astype(BF16)
        pad = [(0, 0)] * 3
        pad[ax] = (0, size - shard.shape[ax])
        shards.append(jnp.pad(shard, pad))
    key_axes = [geometry[n][0] for _, n in keys]
    fulls, thru, ag_send, ag_recv = _ag_start(shards, key_axes, members, cw_all, "allgather_start")

    def fetch(l, gi, after):
        g = l * len(GROUPS) + gi
        axs = [key_axes[i] for i in members[g]]
        own, landed = _ag_wait([thru[i] for i in members[g]], [fulls[i] for i in members[g]], ag_send[g], ag_recv[g], axs, after,
                               f"allgather_wait_l{l}g{gi}")
        return dict(zip(GROUPS[gi], _ag_forward(landed, own, axs, f"allgather_forward_l{l}g{gi}")))

    pending, own_part, landed_part = [], {}, {}

    def reduce(l, gi, grads):
        names = GROUPS[gi]
        axs = [geometry[n][0] for n in names]
        recv = _to_sibling([grads[n] for n in names], axs, [geometry[n][1] for n in names], f"reduce_to_sibling_l{l}g{gi}")
        parts = []
        for n, r, ax in zip(names, recv, axs):
            part, own_part[(l, n)] = _pair_add(grads[n], r, ax, f"reduce_pair_add_l{l}_{n}")
            parts.append(part)
        landing, parts_thru, send, recv_sems, token = _rs_start(parts, f"reduce_start_l{l}g{gi}")
        pending.append((l, gi, parts_thru, landing, send, recv_sems))
        return token

    loss, grad_x, gsmall, dcw = _local_step(x[0], mem[0], loss_target[0], a, conv_w_full, fetch, reduce, cfg)

    def wait_group(entry, after):
        l, gi, parts_thru, landing, send, recv_sems = entry
        for n, arr in zip(GROUPS[gi], _rs_wait(parts_thru, landing, send, recv_sems, after, f"reduce_wait_l{l}g{gi}")):
            landed_part[(l, n)] = arr

    for entry in pending[:-1]:
        wait_group(entry, grad_x)

    grads, delta, new_m, new_v = {}, {}, {}, {}

    def update(n):
        outs = _adamw_sum(_as_handled(n, a[n]), _as_handled(n, am[n]), _as_handled(n, av[n]),
                          [own_part[(l, n)] for l in range(L)], [landed_part[(l, n)] for l in range(L)], "adamw_" + n)
        grads[n], delta[n], new_m[n], new_v[n] = (_as_handled(n, o) for o in outs)

    last_names = GROUPS[pending[-1][1]]
    early = [n for n in big_names if n not in last_names]
    for n in early:
        update(n)
    wait_group(pending[-1], delta[early[-1]])
    for n in last_names:
        update(n)

    small_rows = _pack_small(gsmall)
    n_small = small_rows.shape[0]
    dcw_rows = jnp.pad(dcw, ((0, 0), (0, CONV_PAD - taps), (0, 0))).reshape(-1, SMALL_COLS)
    summed = _exchange_small(jnp.concatenate([small_rows, dcw_rows], axis=0), True, "allreduce_small",
                             after=landed_part[(pending[-1][0], last_names[0])])
    g_small, _ = _unpack_small(summed[:n_small], a)
    dcw_sum = summed[n_small:].reshape(L, CONV_PAD, N_DEV * cshard)[:, :taps]
    grads.update(g_small)
    grads['conv_w'] = lax.dynamic_slice_in_dim(dcw_sum, dev * cshard, cshard, axis=2)
    delta['conv_w'], new_m['conv_w'], new_v['conv_w'] = _adamw(conv_w, am['conv_w'], av['conv_w'], grads['conv_w'], "adamw_conv_w")
    pw, pm, pv, pg = (_pack_small(d) for d in (a, am, av, g_small))
    for dst, packed in zip((delta, new_m, new_v), _adamw(pw, pm, pv, pg, "adamw_small")):
        dst.update(_unpack_small(packed, a)[0])

    total = lax.psum(loss.reshape(()), ("x", "y", "c"))
    return (total, grad_x[None], *[grads[n] for n in WEIGHTS], *[delta[n] for n in WEIGHTS], *[new_m[n] for n in WEIGHTS],
            *[new_v[n] for n in WEIGHTS])
```

```python
import math

import jax
import jax.numpy as jnp
from jax import lax
from jax.experimental import pallas as pl
from jax.experimental.pallas import tpu as pltpu

F32, BF16 = jnp.float32, jnp.bfloat16
S = jax.ShapeDtypeStruct
MESH = pl.DeviceIdType.MESH

EPS = 1e-6
NEG_INF = -1e30
HEAD_DIM = 64
N_XATTN_HEADS = 4
CONV_WIDTH = 31
CONV_PAD = 32
LANES = 128
ADAM_LR, ADAM_B1, ADAM_B2, ADAM_EPS, ADAM_WD, ADAM_STEP = 0.001, 0.9, 0.999, 1e-08, 0.01, 10
N_DEV = 8
VMEM_LIMIT_BYTES = 56 * 1024 * 1024
ROW_TILE = 512
SMALL_COLS = 512

NN = ((1,), (0,))
NT = ((1,), (1,))
TN = ((0,), (0,))

NAMES = ['x', 'mem', 'ffn1_norm_g', 'ffn1_w_gate', 'ffn1_w_up', 'ffn1_w_down', 'mix_norm_g', 'w_in', 'b_f', 'conv_w', 'conv_b',
         'conv_ln_g', 'conv_ln_b', 'attn_out_g', 'conv_out_g', 'w_out', 'xattn_norm_g', 'mem_norm_g', 'xattn_w_q', 'xattn_w_kv',
         'xattn_w_o', 'ffn2_norm_g', 'ffn2_w_gate', 'ffn2_w_up', 'ffn2_w_down', 'final_norm_g']
WEIGHTS = NAMES[2:]
BIG = [('ffn1_w_gate', 'colT'), ('ffn1_w_up', 'colT'), ('ffn1_w_down', 'row'), ('w_in', 'col'), ('w_out', 'row'),
       ('xattn_w_q', 'row'), ('xattn_w_kv', 'colT'), ('xattn_w_o', 'row'), ('ffn2_w_gate', 'colT'), ('ffn2_w_up', 'colT'),
       ('ffn2_w_down', 'row')]
TRANSPOSED = tuple(n for n, kind in BIG if kind == 'colT')


def _as_handled(n, v):
    return jnp.swapaxes(v, 1, 2) if n in TRANSPOSED else v
SMALL = ['ffn1_norm_g', 'mix_norm_g', 'xattn_norm_g', 'mem_norm_g', 'ffn2_norm_g', 'conv_b', 'conv_ln_g', 'conv_ln_b',
         'attn_out_g', 'conv_out_g', 'b_f', 'final_norm_g']


def _dot(a, b, dims):
    return lax.dot_general(a, b, (dims, ((), ())), preferred_element_type=F32)


def _full(shape):
    nd = len(shape)
    return pl.BlockSpec(shape, lambda *_: (0,) * nd)


def _tile(n, pref):
    for t in (pref, 512, 384, 256, 128, 64, 32, 16, 8):
        if t <= n and n % t == 0:
            return t
    return n


def _pcall(name, body, grid, in_specs, out_specs, out_shape, scratch=(), aliases=None, dep=None):
    n_in = len(in_specs)
    kernel_body = body
    if dep is not None:
        in_specs = list(in_specs) + [pl.BlockSpec(memory_space=pl.ANY)]

        def kernel_body(*refs):
            return body(*refs[:n_in], *refs[n_in + 1:])

    call = pl.pallas_call(
        kernel_body, grid=grid, in_specs=in_specs, out_specs=out_specs, out_shape=out_shape, scratch_shapes=list(scratch),
        name=name, input_output_aliases=aliases or {},
        compiler_params=pltpu.CompilerParams(dimension_semantics=("arbitrary",) * len(grid), vmem_limit_bytes=VMEM_LIMIT_BYTES))
    return call if dep is None else (lambda *args: call(*args, dep))


def _arr(w):
    return w[0] if isinstance(w, tuple) else w


def _wshape(w):
    return w[0].shape[1:] if isinstance(w, tuple) else w.shape


def _wspec(w, block, imap):
    if isinstance(w, tuple):
        layer = w[1]
        return pl.BlockSpec((None,) + block, lambda *g: (layer,) + imap(*g))
    return pl.BlockSpec(block, imap)


def _wfull(w):
    shape = _wshape(w)
    return _wspec(w, shape, lambda *_: (0,) * len(shape))


def _sigmoid(z):
    return jax.nn.sigmoid(z)


def _rstd(x):
    return lax.rsqrt(jnp.mean(x * x, axis=-1, keepdims=True) + EPS)


def _rms_bwd(dy, x, g):
    r = _rstd(x)
    xh = x * r
    u = dy * g
    dx = r * (u - xh * jnp.mean(u * xh, axis=-1, keepdims=True))
    return dx, dy * xh


def _colsum(v):
    return jnp.sum(v, axis=0, keepdims=True)


def _rms_fwd(x, g, name):
    T, D = x.shape
    tm = _tile(T, ROW_TILE)

    def body(x_ref, g_ref, h_ref):
        xv = x_ref[...]
        h_ref[...] = (xv * _rstd(xv) * g_ref[...]).astype(BF16)

    row = pl.BlockSpec((tm, D), lambda i: (i, 0))
    return _pcall(name, body, (T // tm,), [row, _full((1, D))], row, S((T, D), BF16))(x, g.reshape(1, D))


def _mm(a, w, out_dtype, name):
    M, K = a.shape
    N = _wshape(w)[1]
    tm = _tile(M, ROW_TILE)
    tn = N if N <= 1536 else N // 2

    def body(a_ref, w_ref, o_ref):
        o_ref[...] = _dot(a_ref[...].astype(BF16), w_ref[...], NN).astype(out_dtype)

    return _pcall(name, body, (N // tn, M // tm),
                  [pl.BlockSpec((tm, K), lambda j, i: (i, 0)), _wspec(w, (K, tn), lambda j, i: (0, j))],
                  pl.BlockSpec((tm, tn), lambda j, i: (i, j)), S((M, N), out_dtype))(a, _arr(w))


def _mm_res(a, w, res, scale, name):
    M, K = a.shape
    N = _wshape(w)[1]
    tm = _tile(M, ROW_TILE)

    def body(a_ref, w_ref, r_ref, o_ref):
        o_ref[...] = r_ref[...] + scale * _dot(a_ref[...], w_ref[...], NN)

    row = pl.BlockSpec((tm, N), lambda i: (i, 0))
    return _pcall(name, body, (M // tm,), [pl.BlockSpec((tm, K), lambda i: (i, 0)), _wfull(w), row], row,
                  S((M, N), F32))(a, _arr(w), res)


def _mm_nt(a, w, out_dtype, name, dep=None):
    M, K = a.shape
    N = _wshape(w)[0]
    tm = _tile(M, ROW_TILE)
    tn = N if N <= 1536 else N // 2

    def body(a_ref, w_ref, o_ref):
        o_ref[...] = _dot(a_ref[...].astype(BF16), w_ref[...], NT).astype(out_dtype)

    return _pcall(name, body, (N // tn, M // tm),
                  [pl.BlockSpec((tm, K), lambda j, i: (i, 0)), _wspec(w, (tn, K), lambda j, i: (j, 0))],
                  pl.BlockSpec((tm, tn), lambda j, i: (i, j)), S((M, N), out_dtype), dep=dep)(a, _arr(w))


def _wgrad(a, dy, scale, name, into):
    buf, layer, L = into
    T, M = a.shape
    N = dy.shape[1]
    tm = _tile(M, ROW_TILE)

    def body(a_ref, dy_ref, *rest):
        rest[-1][...] = (scale * _dot(a_ref[...].astype(BF16), dy_ref[...].astype(BF16), TN)).astype(BF16)

    in_specs = [pl.BlockSpec((T, tm), lambda i: (0, i)), _full((T, N))]
    args = [a, dy]
    if buf is not None:
        in_specs.append(pl.BlockSpec(memory_space=pl.ANY))
        args.append(buf)
    return _pcall(name, body, (M // tm,), in_specs, pl.BlockSpec((None, tm, N), lambda i: (layer, i, 0)), S((L, M, N), BF16),
                  aliases={2: 0} if buf is not None else None)(*args)


def _bwd_h(dots, x, g, dres, name, dep=None):
    T, D = x.shape
    tm = _tile(T, 256)
    n = len(dots)
    dims = [{'nt': NT, 'nn': NN, 'tn': TN}[m] for _, _, m in dots]

    def body(*refs):
        x_ref, g_ref, r_ref, dx_ref, dg_ref = refs[2 * n:]
        dh = None
        for k in range(n):
            part = _dot(refs[2 * k][...], refs[2 * k + 1][...], dims[k])
            dh = part if dh is None else dh + part
        dx, dgrow = _rms_bwd(dh, x_ref[...], g_ref[...])
        dx_ref[...] = r_ref[...] + dx

        @pl.when(pl.program_id(0) == 0)
        def _():
            dg_ref[...] = jnp.zeros_like(dg_ref)

        dg_ref[...] += _colsum(dgrow)

    in_specs, args = [], []
    for lhs, w, mode in dots:
        if mode == 'tn':
            in_specs.append(pl.BlockSpec((lhs.shape[0], tm), lambda i: (0, i)))
        else:
            in_specs.append(pl.BlockSpec((tm, lhs.shape[1]), lambda i: (i, 0)))
        in_specs.append(_wfull(w))
        args += [lhs, _arr(w)]
    row = pl.BlockSpec((tm, D), lambda i: (i, 0))
    in_specs += [row, _full((1, D)), row]
    return _pcall(name, body, (T // tm,), in_specs, [row, _full((1, D))], [S((T, D), F32), S((1, D), F32)], dep=dep)(
        *args, x, g.reshape(1, D), dres)


def _rms_gain_grad(dy, x, g, name):
    T, D = x.shape

    def body(dy_ref, x_ref, g_ref, dg_ref):
        _, dgrow = _rms_bwd(dy_ref[...], x_ref[...], g_ref[...])
        dg_ref[...] = _colsum(dgrow)

    return _pcall(name, body, (), [_full((T, D)), _full((T, D)), _full((1, D))], _full((1, D)), S((1, D), F32))(
        dy, x, g.reshape(1, D))


def _ffn_up(h, wg, wu, name):
    T, D = h.shape
    Fh = _wshape(wg)[0]
    tm = _tile(T, ROW_TILE)
    tn = Fh if Fh <= 1536 else Fh // 2

    def body(h_ref, wg_ref, wu_ref, g_ref, u_ref, a_ref):
        hv = h_ref[...]
        gv = _dot(hv, wg_ref[...], NT)
        uv = _dot(hv, wu_ref[...], NT)
        g_ref[...] = gv.astype(BF16)
        u_ref[...] = uv.astype(BF16)
        a_ref[...] = (gv * _sigmoid(gv) * uv).astype(BF16)

    tile = pl.BlockSpec((tm, tn), lambda j, i: (i, j))
    return _pcall(name, body, (Fh // tn, T // tm),
                  [pl.BlockSpec((tm, D), lambda j, i: (i, 0)), _wspec(wg, (tn, D), lambda j, i: (j, 0)),
                   _wspec(wu, (tn, D), lambda j, i: (j, 0))],
                  [tile, tile, tile], [S((T, Fh), BF16)] * 3)(h, _arr(wg), _arr(wu))


def _ffn_bwd_act(dout, wd, gate, up, scale, name, dep=None):
    T, D = dout.shape
    Fh = _wshape(wd)[0]
    tm = _tile(T, ROW_TILE)
    tn = Fh if Fh <= 1536 else Fh // 2

    def body(d_ref, w_ref, g_ref, u_ref, dg_ref, du_ref):
        da = scale * _dot(d_ref[...].astype(BF16), w_ref[...], NT)
        gv = g_ref[...].astype(F32)
        uv = u_ref[...].astype(F32)
        sg = _sigmoid(gv)
        dg_ref[...] = (da * uv * (sg * (1.0 + gv * (1.0 - sg)))).astype(BF16)
        du_ref[...] = (da * (gv * sg)).astype(BF16)

    tile = pl.BlockSpec((tm, tn), lambda j, i: (i, j))
    return _pcall(name, body, (Fh // tn, T // tm),
                  [pl.BlockSpec((tm, D), lambda j, i: (i, 0)), _wspec(wd, (tn, D), lambda j, i: (j, 0)), tile, tile],
                  [tile, tile], [S((T, Fh), BF16)] * 2, dep=dep)(dout, _arr(wd), gate, up)


def _loss_head(x, g, tgt, name):
    T, D = x.shape
    tm = _tile(T, ROW_TILE)

    def body(x_ref, g_ref, t_ref, loss_ref, dx_ref, dg_ref):
        xv = x_ref[...]
        gv = g_ref[...]
        r = _rstd(xv)
        xh = xv * r
        e = xh * gv - t_ref[...]
        dy = e * (1.0 / D)
        u = dy * gv
        dx_ref[...] = r * (u - xh * jnp.mean(u * xh, axis=-1, keepdims=True))

        @pl.when(pl.program_id(0) == 0)
        def _():
            dg_ref[...] = jnp.zeros_like(dg_ref)
            loss_ref[...] = jnp.zeros_like(loss_ref)

        dg_ref[...] += _colsum(dy * xh)
        loss_ref[...] += 0.5 * _colsum(jnp.mean(e * e, axis=-1, keepdims=True))

    row = pl.BlockSpec((tm, D), lambda i: (i, 0))
    return _pcall(name, body, (T // tm,), [row, _full((1, D)), row], [_full((1, 1)), row, _full((1, D))],
                  [S((1, 1), F32), S((T, D), F32), S((1, D), F32)])(x, g.reshape(1, D), tgt)


def _split3(xb):
    hi = xb.astype(BF16)
    r1 = xb - hi.astype(F32)
    mid = r1.astype(BF16)
    lo = (r1 - mid.astype(F32)).astype(BF16)
    return hi, mid, lo


def _fox_prep(h, wft, bft, name):
    T, D = h.shape
    blk = _tile(T, 256)

    def body(h_ref, w_ref, b_ref, ct_ref, sg_ref):
        z = _dot(w_ref[...], h_ref[...], NT) + b_ref[...]
        sg_ref[...] = 1.0 - _sigmoid(z)
        logf = jnp.minimum(z, 0.0) - jnp.log1p(jnp.exp(-jnp.abs(z)))
        upper = (lax.broadcasted_iota(jnp.int32, (blk, blk), 0) <= lax.broadcasted_iota(jnp.int32, (blk, blk), 1)).astype(BF16)
        carry = jnp.zeros((16, 1), F32)
        for b in range(T // blk):
            hi, mid, lo = _split3(logf[:, b * blk:(b + 1) * blk])
            cb = _dot(hi, upper, NN) + _dot(mid, upper, NN) + _dot(lo, upper, NN) + carry
            ct_ref[:, b * blk:(b + 1) * blk] = cb
            carry = cb[:, blk - 1:blk]

    return _pcall(name, body, (), [_full((T, D)), _full((16, D)), _full((16, 1))], [_full((16, T)), _full((16, T))],
                  [S((16, T), F32), S((16, T), F32)])(h, wft, bft)


def _fox_prep_bwd(dcs, sg, h, name):
    T, D = h.shape
    blk = _tile(T, 256)
    nb = T // blk

    def body(dcs_ref, sg_ref, h_ref, dfl_ref, dw_ref, db_ref):
        lower = (lax.broadcasted_iota(jnp.int32, (blk, blk), 0) >= lax.broadcasted_iota(jnp.int32, (blk, blk), 1)).astype(BF16)
        carry = jnp.zeros((16, 1), F32)
        db = jnp.zeros((16, 1), F32)
        for b in range(nb - 1, -1, -1):
            cols = slice(b * blk, (b + 1) * blk)
            hi, mid, lo = _split3(-dcs_ref[:, cols])
            dlogf = _dot(hi, lower, NN) + _dot(mid, lower, NN) + _dot(lo, lower, NN) + carry
            carry = dlogf[:, 0:1]
            dfl = dlogf * sg_ref[:, cols]
            db = db + jnp.sum(dfl, axis=-1, keepdims=True)
            dfl_ref[:, cols] = dfl.astype(BF16)
        db_ref[...] = db
        dw_ref[...] = _dot(dfl_ref[...], h_ref[...], NN)

    return _pcall(name, body, (), [_full((16, T)), _full((16, T)), _full((T, D))],
                  [_full((16, T)), _full((16, D)), _full((16, 1))],
                  [S((16, T), BF16), S((16, D), F32), S((16, 1), F32)])(dcs, sg, h)


def _fox_logits(q, k, c_col, c_row):
    tq, kp = q.shape[0], k.shape[0]
    s = _dot(q, k, NT) * (1.0 / math.sqrt(HEAD_DIM)) + (c_col - c_row)
    row = lax.broadcasted_iota(jnp.int32, (tq, tq), 0)
    col = lax.broadcasted_iota(jnp.int32, (tq, tq), 1)
    diag = jnp.where(row >= col, s[:, kp - tq:], NEG_INF)
    return diag if kp == tq else jnp.concatenate([s[:, :kp - tq], diag], axis=1)


def _fox_specs(T, n_pairs):
    qs = pl.BlockSpec((T, LANES), lambda p: (0, p))
    ks = pl.BlockSpec((T, LANES), lambda p: (0, n_pairs + p))
    vs = pl.BlockSpec((T, LANES), lambda p: (0, 2 * n_pairs + p))
    col = pl.BlockSpec((2, T, 1), lambda p: (p, 0, 0))
    rowv = pl.BlockSpec((None, 2, T), lambda p: (p, 0, 0))
    return qs, ks, vs, col, rowv


def _fox_fwd(qkv, c_col, c_row, name):
    T = qkv.shape[0]
    DA = qkv.shape[1] // 3
    n_pairs = DA // LANES
    tq = _tile(T, 256)

    def body(q_ref, k_ref, v_ref, c_ref, ct_ref, o_ref, lse_ref):
        for hh in range(2):
            sl = slice(hh * HEAD_DIM, (hh + 1) * HEAD_DIM)
            for i in range(T // tq):
                rows = slice(i * tq, (i + 1) * tq)
                kp = (i + 1) * tq
                s = _fox_logits(q_ref[rows, sl], k_ref[0:kp, sl], c_ref[hh, rows, :], ct_ref[hh:hh + 1, 0:kp])
                m = jnp.max(s, axis=-1, keepdims=True)
                p = jnp.exp(s - m)
                l = jnp.sum(p, axis=-1, keepdims=True)
                o_ref[rows, sl] = _dot(p.astype(BF16), v_ref[0:kp, sl], NN) / l
                lse_ref[hh, rows, :] = m + jnp.log(l)

    qs, ks, vs, col, rowv = _fox_specs(T, n_pairs)
    return _pcall(name, body, (n_pairs,), [qs, ks, vs, col, rowv], [qs, col],
                  [S((T, DA), F32), S((2 * n_pairs, T, 1), F32)])(qkv, qkv, qkv, c_col, c_row)


def _fox_bwd(qkv, c_col, c_row, lse, do, name):
    T = qkv.shape[0]
    DA = qkv.shape[1] // 3
    n_pairs = DA // LANES
    tq = _tile(T, 256)
    scale = 1.0 / math.sqrt(HEAD_DIM)

    def body(q_ref, k_ref, v_ref, c_ref, ct_ref, lse_ref, do_ref, dq_ref, dk_ref, dv_ref, dcs_ref, dk_acc, dv_acc):
        dk_acc[...] = jnp.zeros_like(dk_acc)
        dv_acc[...] = jnp.zeros_like(dv_acc)
        dcs_ref[...] = jnp.zeros_like(dcs_ref)
        for hh in range(2):
            sl = slice(hh * HEAD_DIM, (hh + 1) * HEAD_DIM)
            for i in range(T // tq):
                rows = slice(i * tq, (i + 1) * tq)
                kp = (i + 1) * tq
                q = q_ref[rows, sl]
                k = k_ref[0:kp, sl]
                dob = do_ref[rows, sl]
                s = _fox_logits(q, k, c_ref[hh, rows, :], ct_ref[hh:hh + 1, 0:kp])
                p = jnp.exp(s - lse_ref[hh, rows, :])
                dp = _dot(dob, v_ref[0:kp, sl], NT)
                ds = p * (dp - jnp.sum(p * dp, axis=-1, keepdims=True))
                dsb = ds.astype(BF16)
                dq_ref[rows, sl] = (_dot(dsb, k, NN) * scale).astype(BF16)
                dk_acc[0:kp, sl] += _dot(dsb, q, TN) * scale
                dv_acc[0:kp, sl] += _dot(p.astype(BF16), dob, TN)
                dcs_ref[hh:hh + 1, 0:kp] += _colsum(ds)
        dk_ref[...] = dk_acc[...].astype(BF16)
        dv_ref[...] = dv_acc[...].astype(BF16)

    qs, ks, vs, col, rowv = _fox_specs(T, n_pairs)
    return _pcall(name, body, (n_pairs,), [qs, ks, vs, col, rowv, col, qs], [qs, qs, qs, rowv],
                  [S((T, DA), BF16)] * 3 + [S((n_pairs, 2, T), F32)],
                  scratch=[pltpu.VMEM((T, LANES), F32), pltpu.VMEM((T, LANES), F32)])(qkv, qkv, qkv, c_col, c_row, lse, do)


def _conv_fwd(ag, w, b, name):
    T = ag.shape[0]
    DC = ag.shape[1] // 2
    nb = DC // LANES
    tr = _tile(T, 256)

    def body(a_ref, g_ref, w_ref, b_ref, y_ref, pad):
        pad[0:CONV_PAD, :] = jnp.zeros((CONV_PAD, LANES), F32)
        pad[CONV_PAD:CONV_PAD + T, :] = a_ref[...] * _sigmoid(g_ref[...])
        for r in range(T // tr):
            acc = jnp.zeros((tr, LANES), F32) + b_ref[...]
            for j in range(CONV_WIDTH):
                o = r * tr + CONV_PAD - (CONV_WIDTH - 1) + j
                acc = acc + w_ref[j:j + 1, :] * pad[o:o + tr, :]
            y_ref[r * tr:(r + 1) * tr, :] = acc

    blk = pl.BlockSpec((T, LANES), lambda c: (0, c))
    return _pcall(name, body, (nb,), [blk, pl.BlockSpec((T, LANES), lambda c: (0, nb + c)),
                                      pl.BlockSpec((CONV_PAD, LANES), lambda c: (0, c)), pl.BlockSpec((1, LANES), lambda c: (0, c))],
                  blk, S((T, DC), F32), scratch=[pltpu.VMEM((T + CONV_PAD, LANES), F32)])(ag, ag, w, b)


def _conv_bwd(dy, ag, w, name):
    T = ag.shape[0]
    DC = ag.shape[1] // 2
    nb = DC // LANES
    tr = _tile(T, 256)

    def body(dy_ref, a_ref, g_ref, w_ref, da_ref, dg_ref, dw_ref, db_ref, pad, dpad):
        av = a_ref[...]
        sg = _sigmoid(g_ref[...])
        pad[0:CONV_PAD, :] = jnp.zeros((CONV_PAD, LANES), F32)
        pad[CONV_PAD:CONV_PAD + T, :] = av * sg
        dpad[0:T, :] = dy_ref[...]
        dpad[T:T + CONV_PAD, :] = jnp.zeros((CONV_PAD, LANES), F32)
        db_ref[...] = _colsum(dy_ref[...])
        dw_ref[...] = jnp.zeros_like(dw_ref)
        for j in range(CONV_WIDTH):
            acc = jnp.zeros((tr, LANES), F32)
            for r in range(T // tr):
                o = r * tr + CONV_PAD - (CONV_WIDTH - 1) + j
                acc = acc + dpad[r * tr:(r + 1) * tr, :] * pad[o:o + tr, :]
            dw_ref[j:j + 1, :] = _colsum(acc)
        for r in range(T // tr):
            acc = jnp.zeros((tr, LANES), F32)
            for j in range(CONV_WIDTH):
                o = r * tr + (CONV_WIDTH - 1) - j
                acc = acc + w_ref[j:j + 1, :] * dpad[o:o + tr, :]
            rows = slice(r * tr, (r + 1) * tr)
            sgr = sg[rows, :]
            da_ref[rows, :] = (acc * sgr).astype(BF16)
            dg_ref[rows, :] = (acc * av[rows, :] * sgr * (1.0 - sgr)).astype(BF16)

    blk = pl.BlockSpec((T, LANES), lambda c: (0, c))
    wblk = pl.BlockSpec((CONV_PAD, LANES), lambda c: (0, c))
    return _pcall(name, body, (nb,), [blk, blk, pl.BlockSpec((T, LANES), lambda c: (0, nb + c)), wblk],
                  [blk, blk, wblk, pl.BlockSpec((1, LANES), lambda c: (0, c))],
                  [S((T, DC), BF16), S((T, DC), BF16), S((CONV_PAD, DC), F32), S((1, DC), F32)],
                  scratch=[pltpu.VMEM((T + CONV_PAD, LANES), F32), pltpu.VMEM((T + CONV_PAD, LANES), F32)])(dy, ag, ag, w)


def _conv_norms(yc, lg, lb):
    mu = jnp.mean(yc, axis=-1, keepdims=True)
    xc = yc - mu
    rs = lax.rsqrt(jnp.mean(xc * xc, axis=-1, keepdims=True) + EPS)
    xh = xc * rs
    z = xh * lg + lb
    sg = _sigmoid(z)
    return rs, xh, z, sg, z * sg


def _mix_post(attn, yc, ag, cg, lg, lb, name):
    T, DA = attn.shape
    DC = yc.shape[1]
    tm = _tile(T, ROW_TILE)

    def body(at_ref, yc_ref, ag_ref, cg_ref, lg_ref, lb_ref, y_ref):
        at = at_ref[...]
        y_ref[:, 0:DA] = (at * _rstd(at) * ag_ref[...]).astype(BF16)
        _, _, _, _, sv = _conv_norms(yc_ref[...], lg_ref[...], lb_ref[...])
        y_ref[:, DA:DA + DC] = (sv * _rstd(sv) * cg_ref[...]).astype(BF16)

    return _pcall(name, body, (T // tm,),
                  [pl.BlockSpec((tm, DA), lambda i: (i, 0)), pl.BlockSpec((tm, DC), lambda i: (i, 0)), _full((1, DA)),
                   _full((1, DC)), _full((1, DC)), _full((1, DC))],
                  pl.BlockSpec((tm, DA + DC), lambda i: (i, 0)), S((T, DA + DC), BF16))(attn, yc, ag, cg, lg, lb)


def _mix_post_bwd(dy, attn, yc, ag, cg, lg, lb, name):
    T, DA = attn.shape
    DC = yc.shape[1]
    tm = _tile(T, ROW_TILE)

    def body(dy_ref, at_ref, yc_ref, ag_ref, cg_ref, lg_ref, lb_ref, dat_ref, dyc_ref, dag_ref, dcg_ref, dlg_ref, dlb_ref):
        dat, dag_rows = _rms_bwd(dy_ref[:, 0:DA], at_ref[...], ag_ref[...])
        dat_ref[...] = dat.astype(BF16)
        lgv = lg_ref[...]
        rs, xh, z, sg, sv = _conv_norms(yc_ref[...], lgv, lb_ref[...])
        dsv, dcg_rows = _rms_bwd(dy_ref[:, DA:DA + DC], sv, cg_ref[...])
        dz = dsv * (sg * (1.0 + z * (1.0 - sg)))
        dxh = dz * lgv
        dyc_ref[...] = rs * (dxh - jnp.mean(dxh, axis=-1, keepdims=True) - xh * jnp.mean(dxh * xh, axis=-1, keepdims=True))

        @pl.when(pl.program_id(0) == 0)
        def _():
            for r in (dag_ref, dcg_ref, dlg_ref, dlb_ref):
                r[...] = jnp.zeros_like(r)

        dag_ref[...] += _colsum(dag_rows)
        dcg_ref[...] += _colsum(dcg_rows)
        dlg_ref[...] += _colsum(dz * xh)
        dlb_ref[...] += _colsum(dz)

    ra = pl.BlockSpec((tm, DA), lambda i: (i, 0))
    rc = pl.BlockSpec((tm, DC), lambda i: (i, 0))
    return _pcall(name, body, (T // tm,),
                  [pl.BlockSpec((tm, DA + DC), lambda i: (i, 0)), ra, rc, _full((1, DA)), _full((1, DC)), _full((1, DC)),
                   _full((1, DC))],
                  [ra, rc, _full((1, DA)), _full((1, DC)), _full((1, DC)), _full((1, DC))],
                  [S((T, DA), BF16), S((T, DC), F32), S((1, DA), F32), S((1, DC), F32), S((1, DC), F32), S((1, DC), F32)])(
        dy, attn, yc, ag, cg, lg, lb)


def _xattn_probs(q, k, xd):
    s = _dot(q, k, NT) * (1.0 / math.sqrt(xd))
    p = jnp.exp(s - jnp.max(s, axis=-1, keepdims=True))
    return p / jnp.sum(p, axis=-1, keepdims=True)


def _xattn_fwd(q, kv, name):
    T, D = q.shape
    M = kv.shape[0]
    xd = D // N_XATTN_HEADS
    tq = _tile(T, ROW_TILE)

    def body(q_ref, kv_ref, o_ref):
        for h in range(N_XATTN_HEADS):
            sl = slice(h * xd, (h + 1) * xd)
            p = _xattn_probs(q_ref[:, sl], kv_ref[:, sl], xd)
            o_ref[:, sl] = _dot(p.astype(BF16), kv_ref[:, D + h * xd:D + (h + 1) * xd], NN).astype(BF16)

    row = pl.BlockSpec((tq, D), lambda i: (i, 0))
    return _pcall(name, body, (T // tq,), [row, _full((M, 2 * D))], row, S((T, D), BF16))(q, kv)


def _xattn_bwd(q, kv, do, name):
    T, D = q.shape
    M = kv.shape[0]
    xd = D // N_XATTN_HEADS
    tq = _tile(T, ROW_TILE)
    scale = 1.0 / math.sqrt(xd)

    def body(q_ref, kv_ref, do_ref, dq_ref, dkv_ref):
        @pl.when(pl.program_id(0) == 0)
        def _():
            dkv_ref[...] = jnp.zeros_like(dkv_ref)

        for h in range(N_XATTN_HEADS):
            sl = slice(h * xd, (h + 1) * xd)
            vsl = slice(D + h * xd, D + (h + 1) * xd)
            qh = q_ref[:, sl]
            kh = kv_ref[:, sl]
            doh = do_ref[:, sl]
            p = _xattn_probs(qh, kh, xd)
            dp = _dot(doh, kv_ref[:, vsl], NT)
            ds = (p * (dp - jnp.sum(p * dp, axis=-1, keepdims=True)) * scale).astype(BF16)
            dq_ref[:, sl] = _dot(ds, kh, NN).astype(BF16)
            dkv_ref[:, sl] += _dot(ds, qh, TN)
            dkv_ref[:, vsl] += _dot(p.astype(BF16), doh, TN)

    row = pl.BlockSpec((tq, D), lambda i: (i, 0))
    return _pcall(name, body, (T // tq,), [row, _full((M, 2 * D)), row], [row, _full((M, 2 * D))],
                  [S((T, D), BF16), S((M, 2 * D), F32)])(q, kv, do)


def _adamw(w, m, v, g, name):
    shape = w.shape
    C = shape[-1]
    R = w.size // C
    tr = R if R <= 512 else _tile(R, 512)

    def body(w_ref, m_ref, v_ref, g_ref, d_ref, nm_ref, nv_ref):
        gv = g_ref[...]
        mv = ADAM_B1 * m_ref[...] + (1.0 - ADAM_B1) * gv
        vv = ADAM_B2 * v_ref[...] + (1.0 - ADAM_B2) * (gv * gv)
        m_hat = mv / (1.0 - ADAM_B1 ** ADAM_STEP)
        v_hat = vv / (1.0 - ADAM_B2 ** ADAM_STEP)
        d_ref[...] = -ADAM_LR * (m_hat / (jnp.sqrt(v_hat) + ADAM_EPS) + ADAM_WD * w_ref[...])
        nm_ref[...] = mv
        nv_ref[...] = vv

    blk = pl.BlockSpec((tr, C), lambda i: (i, 0))
    outs = _pcall(name, body, (R // tr,), [blk] * 4, [blk] * 3, [S((R, C), F32)] * 3)(
        w.reshape(R, C), m.reshape(R, C), v.reshape(R, C), g.reshape(R, C))
    return [o.reshape(shape) for o in outs]


def _place_scalars():
    return jnp.stack([lax.axis_index("c"), 2 * lax.axis_index("x") + lax.axis_index("y")]).astype(jnp.int32)


def _adamw_sum(w, m, v, owns, landed, name):
    L, p, q = w.shape
    qq = owns[0].shape[2]
    tr = next((t for t in range(min(p, ROW_TILE) // 16 * 16, 0, -16) if p % t == 0), p)

    def body(place_ref, w_ref, m_ref, v_ref, *rest):
        own_refs, land_refs = rest[:L], rest[L:2 * L]
        g_ref, d_ref, nm_ref, nv_ref = rest[2 * L:]
        chip = place_ref[1]

        def update(l):
            own = own_refs[l][...].astype(F32)
            gs = None
            for k in range(4):
                term = jnp.where(chip == k, own, land_refs[l][k].astype(F32))
                gs = term if gs is None else gs + term
            gv = gs[:, 0:q]
            mv = ADAM_B1 * m_ref[...] + (1.0 - ADAM_B1) * gv
            vv = ADAM_B2 * v_ref[...] + (1.0 - ADAM_B2) * (gv * gv)
            m_hat = mv / (1.0 - ADAM_B1 ** ADAM_STEP)
            v_hat = vv / (1.0 - ADAM_B2 ** ADAM_STEP)
            g_ref[...] = gv
            d_ref[...] = -ADAM_LR * (m_hat / (jnp.sqrt(v_hat) + ADAM_EPS) + ADAM_WD * w_ref[...])
            nm_ref[...] = mv
            nv_ref[...] = vv

        for l in range(L):
            pl.when(pl.program_id(0) == l)(lambda l=l: update(l))

    def rows_of(layer):
        return lambda l, i, place: jnp.where(l == layer, i, 0)

    blk = pl.BlockSpec((None, tr, q), lambda l, i, place: (l, i, 0))
    in_specs = [blk, blk, blk]
    in_specs += [pl.BlockSpec((None, tr, qq), lambda l, i, place, r=rows_of(k): (0, r(l, i, place), 0)) for k in range(L)]
    in_specs += [pl.BlockSpec((4, None, tr, qq), lambda l, i, place, r=rows_of(k): (0, 0, r(l, i, place), 0)) for k in range(L)]
    gs = pltpu.PrefetchScalarGridSpec(num_scalar_prefetch=1, grid=(L, p // tr), in_specs=in_specs, out_specs=[blk] * 4)
    return pl.pallas_call(body, grid_spec=gs, out_shape=[S((L, p, q), F32)] * 4, name=name,
                          compiler_params=pltpu.CompilerParams(dimension_semantics=("arbitrary", "arbitrary")))(
        _place_scalars(), w, m, v, *owns, *landed)


def _pair_add(g, recv, axis, name):
    _, _, p, q = recv.shape

    def body(place_ref, g_ref, r_ref, o_ref, own_ref):
        s = (g_ref[...].astype(F32) + r_ref[...].astype(F32)).astype(BF16)
        o_ref[...] = s

        @pl.when(pl.program_id(0) == place_ref[1])
        def _():
            own_ref[...] = s

    if axis == 1:
        gspec = pl.BlockSpec((None, p, q), lambda k, place: (0, 2 * k + place[0], 0))
    else:
        gspec = pl.BlockSpec((None, p, q), lambda k, place: (0, 0, 2 * k + place[0]))
    part = pl.BlockSpec((None, None, p, q), lambda k, place: (k, 0, 0, 0))
    own = pl.BlockSpec((None, p, q), lambda k, place: (0, 0, 0))
    gs = pltpu.PrefetchScalarGridSpec(num_scalar_prefetch=1, grid=(4,), in_specs=[gspec, part], out_specs=[part, own])
    return pl.pallas_call(body, grid_spec=gs, out_shape=[S((4, 1, p, q), BF16), S((1, p, q), BF16)], name=name,
                          compiler_params=pltpu.CompilerParams(dimension_semantics=("arbitrary",)))(_place_scalars(), g, recv)


def _win_pieces(n_attn, n_heads, n_conv, shard, chunk):
    bounds = [0, 3 * n_attn, 3 * n_attn + n_heads, 3 * n_attn + n_heads + 2 * n_conv]
    pieces = []
    for j in range(N_DEV):
        lo, hi = shard * j, shard * (j + 1)
        for r in range(3):
            a, b = max(lo, bounds[r]), min(hi, bounds[r + 1])
            if a < b:
                pieces.append((r, a - bounds[r], b - bounds[r], chunk * j + a - lo))
    return pieces


def _win_split(w_in, pieces, widths, name):
    L, D, C = w_in.shape
    tr = _tile(D, 256)

    def body(x_ref, *outs):
        outs[1][...] = jnp.zeros_like(outs[1])
        for r, d0, d1, s0 in pieces:
            outs[r][:, d0:d1] = x_ref[:, s0:s0 + d1 - d0]

    return _pcall(name, body, (L, D // tr), [pl.BlockSpec((None, tr, C), lambda l, i: (l, i, 0))],
                  [pl.BlockSpec((None, tr, wd), lambda l, i: (l, i, 0)) for wd in widths],
                  [S((L, D, wd), BF16) for wd in widths])(w_in)


def _win_merge(parts, pieces, chunked_cols, name):
    L, D, _ = parts[0].shape
    tr = _tile(D, 256)

    def body(a_ref, b_ref, c_ref, o_ref):
        ins = (a_ref, b_ref, c_ref)
        o_ref[...] = jnp.zeros_like(o_ref)
        for r, d0, d1, s0 in pieces:
            o_ref[:, s0:s0 + d1 - d0] = ins[r][:, d0:d1]

    return _pcall(name, body, (L, D // tr), [pl.BlockSpec((None, tr, x.shape[2]), lambda l, i: (l, i, 0)) for x in parts],
                  pl.BlockSpec((None, tr, chunked_cols), lambda l, i: (l, i, 0)), S((L, D, chunked_cols), BF16))(*parts)


def _place():
    return lax.axis_index("x"), lax.axis_index("y"), lax.axis_index("c")


def _flip(v, f):
    return 1 - v if f else v


def _window(ref, axis, size, dev):
    start = dev * size if isinstance(dev, int) else pl.multiple_of(dev * size, LANES if axis == 2 else 16)
    return ref.at[:, pl.ds(start, size), :] if axis == 1 else ref.at[:, :, pl.ds(start, size)]


HBM_SPEC = pl.BlockSpec(memory_space=pltpu.HBM)
SEM_SPEC = pl.BlockSpec(memory_space=pltpu.SEMAPHORE)
SPLIT_COPY_PARAMS = dict(has_side_effects=pltpu.SideEffectType.DATAFLOW_SIDE_EFFECTING)


def _hbm(v):
    return pltpu.with_memory_space_constraint(v, pltpu.HBM)


def _full_shape(shard, axis):
    return tuple(N_DEV * d if i == axis else d for i, d in enumerate(shard.shape))


def _ag_peers(x, y, c):
    return [(x, y, 1 - c), (1 - x, y, c), (x, 1 - y, c), (1 - x, 1 - y, c)]


SIBLING_COLLECTIVE_ID = 0


def _sibling_handshake(x, y, c):
    barrier = pltpu.get_barrier_semaphore()
    pl.semaphore_signal(barrier, inc=1, device_id=(x, y, 1 - c), device_id_type=MESH)
    pl.semaphore_wait(barrier, 1)


def _ag_start(shards, axes, groups, after, name):
    n, ng = len(shards), len(groups)
    sizes = [s.shape[ax] for s, ax in zip(shards, axes)]
    where = {w: (g, i) for g, members in enumerate(groups) for i, w in enumerate(members)}

    def body(*refs):
        xs, fulls = refs[:n], refs[n + 1:2 * n + 1]
        send, recv = refs[3 * n + 1:3 * n + 1 + ng], refs[3 * n + 1 + ng:]
        x, y, c = _place()
        for members in groups:
            for w in members:
                g, i = where[w]
                for k, to in enumerate(_ag_peers(x, y, c)):
                    pltpu.make_async_remote_copy(
                        src_ref=xs[w], dst_ref=_window(fulls[w], axes[w], sizes[w], 4 * x + 2 * y + c),
                        send_sem=send[g].at[4 * i + k], recv_sem=recv[g].at[4 * i + k], device_id=to, device_id_type=MESH).start()

    sems = [pltpu.SemaphoreType.DMA((4 * len(m),)) for m in groups]
    outs = pl.pallas_call(
        body, name=name,
        out_shape=[pltpu.HBM(_full_shape(s, ax), s.dtype) for s, ax in zip(shards, axes)] + [pltpu.HBM(s.shape, s.dtype) for s in shards]
        + sems + sems,
        in_specs=[HBM_SPEC] * n + [pl.BlockSpec(memory_space=pl.ANY)], out_specs=[HBM_SPEC] * (2 * n) + [SEM_SPEC] * (2 * ng),
        input_output_aliases={w: n + w for w in range(n)},
        compiler_params=pltpu.CompilerParams(**SPLIT_COPY_PARAMS))(*[_hbm(s) for s in shards], after)
    return outs[:n], outs[n:2 * n], outs[2 * n:2 * n + ng], outs[2 * n + ng:]


def _ag_wait(shards, fulls, send_sems, recv_sems, axes, after, name):
    n = len(shards)
    sizes = [s.shape[ax] for s, ax in zip(shards, axes)]

    def body(*refs):
        xs = refs[:n]
        send, recv = refs[2 * n], refs[2 * n + 1]
        landed = refs[3 * n + 3:]
        x, y, c = _place()
        for w in range(n):
            for k, frm in enumerate(_ag_peers(x, y, c)):
                copy = pltpu.make_async_remote_copy(
                    src_ref=xs[w], dst_ref=_window(landed[w], axes[w], sizes[w], 4 * frm[0] + 2 * frm[1] + frm[2]),
                    send_sem=send.at[4 * w + k], recv_sem=recv.at[4 * w + k], device_id=frm, device_id_type=MESH)
                copy.wait_send()
                copy.wait_recv()

    outs = pl.pallas_call(
        body, name=name, out_shape=[pltpu.HBM(v.shape, v.dtype) for v in list(shards) + list(fulls)],
        in_specs=[HBM_SPEC] * (2 * n) + [SEM_SPEC, SEM_SPEC, pl.BlockSpec(memory_space=pl.ANY)], out_specs=[HBM_SPEC] * (2 * n),
        input_output_aliases={i: i for i in range(2 * n)},
        compiler_params=pltpu.CompilerParams(**SPLIT_COPY_PARAMS))(*shards, *fulls, send_sems, recv_sems, after)
    return outs[:n], outs[n:]


def _ag_forward(fulls, shards, axes, name):
    n = len(fulls)
    sizes = [s.shape[ax] for s, ax in zip(shards, axes)]

    def body(*refs):
        xs, full_refs = refs[:n], refs[2 * n:3 * n]
        send_sems, recv_sems, local_sems = refs[3 * n:3 * n + 3]
        staged = refs[3 * n + 3:]
        x, y, c = _place()
        _sibling_handshake(x, y, c)
        chips = [(1 - x, y), (x, 1 - y), (1 - x, 1 - y)]
        loads = [pltpu.make_async_copy(xs[w], staged[w], local_sems.at[w]) for w in range(n)]
        for cp in loads:
            cp.start()
        copies = []
        for w in range(n):
            for j, (px, py) in enumerate(chips):
                sent = _window(full_refs[w], axes[w], sizes[w], 4 * px + 2 * py + c)
                got = _window(full_refs[w], axes[w], sizes[w], 4 * px + 2 * py + 1 - c)
                out = pltpu.make_async_remote_copy(src_ref=sent, dst_ref=sent, send_sem=send_sems.at[3 * w + j],
                                                   recv_sem=recv_sems.at[3 * w + j], device_id=(x, y, 1 - c), device_id_type=MESH)
                out.start()
                back = pltpu.make_async_remote_copy(src_ref=got, dst_ref=got, send_sem=send_sems.at[3 * w + j],
                                                    recv_sem=recv_sems.at[3 * w + j], device_id=(x, y, 1 - c), device_id_type=MESH)
                copies.append((out, back))
        stores = []
        for w in range(n):
            loads[w].wait()
            store = pltpu.make_async_copy(staged[w], _window(full_refs[w], axes[w], sizes[w], 4 * x + 2 * y + c), local_sems.at[w])
            store.start()
            stores.append(store)
        for out, back in copies:
            out.wait_send()
            back.wait_recv()
        for cp in stores:
            cp.wait()

    any_spec = pl.BlockSpec(memory_space=pl.ANY)
    outs = pl.pallas_call(
        body, name=name, out_shape=[S(f.shape, f.dtype) for f in fulls], in_specs=[any_spec] * (2 * n), out_specs=[any_spec] * n,
        input_output_aliases={n + w: w for w in range(n)},
        scratch_shapes=[pltpu.SemaphoreType.DMA((3 * n,)), pltpu.SemaphoreType.DMA((3 * n,)), pltpu.SemaphoreType.DMA((n,))]
        + [pltpu.VMEM(s.shape, s.dtype) for s in shards],
        compiler_params=pltpu.CompilerParams(collective_id=SIBLING_COLLECTIVE_ID))(*shards, *fulls)
    return outs


def _to_sibling(grads, axes, sizes, name):
    n = len(grads)
    outs = []
    for g, ax, sz in zip(grads, axes, sizes):
        L, K, N = g.shape
        outs.append(S((4, L, sz, N) if ax == 1 else (4, L, K, sz), g.dtype))

    def body(*refs):
        g_refs, out_refs = refs[:n], refs[n:2 * n]
        send_sems, recv_sems = refs[2 * n:]
        x, y, c = _place()
        _sibling_handshake(x, y, c)
        copies = []
        for w in range(n):
            for k in range(4):
                copies.append(pltpu.make_async_remote_copy(
                    src_ref=_window(g_refs[w], axes[w], sizes[w], 2 * k + 1 - c), dst_ref=out_refs[w].at[k],
                    send_sem=send_sems.at[4 * w + k], recv_sem=recv_sems.at[4 * w + k], device_id=(x, y, 1 - c),
                    device_id_type=MESH))
        for cp in copies:
            cp.start()
        for cp in copies:
            cp.wait()

    any_spec = pl.BlockSpec(memory_space=pl.ANY)
    return pl.pallas_call(
        body, out_shape=outs, in_specs=[any_spec] * n, out_specs=[any_spec] * n, name=name,
        scratch_shapes=[pltpu.SemaphoreType.DMA((4 * n,)), pltpu.SemaphoreType.DMA((4 * n,))],
        compiler_params=pltpu.CompilerParams(collective_id=SIBLING_COLLECTIVE_ID))(*grads)


def _rs_copy(p_ref, out_ref, send_sems, recv_sems, w, rel, x, y, c):
    tx, ty = _flip(x, rel & 2), _flip(y, rel & 1)
    return pltpu.make_async_remote_copy(
        src_ref=p_ref.at[2 * tx + ty], dst_ref=out_ref.at[2 * x + y], send_sem=send_sems.at[3 * w + rel - 1],
        recv_sem=recv_sems.at[3 * w + rel - 1], device_id=(tx, ty, c), device_id_type=MESH)


def _rs_start(parts, name):
    n = len(parts)

    def body(*refs):
        p_refs, out_refs = refs[:n], refs[n:2 * n]
        send_sems, recv_sems, token = refs[3 * n:]
        x, y, c = _place()
        for w in range(n):
            for rel in (1, 2, 3):
                _rs_copy(p_refs[w], out_refs[w], send_sems, recv_sems, w, rel, x, y, c).start()
        token[...] = jnp.zeros_like(token)

    sems = pltpu.SemaphoreType.DMA((3 * n,))
    outs = pl.pallas_call(
        body, name=name,
        out_shape=[pltpu.HBM(p.shape, p.dtype) for p in parts] * 2 + [sems, sems, S((8, LANES), F32)],
        in_specs=[HBM_SPEC] * n, out_specs=[HBM_SPEC] * (2 * n) + [SEM_SPEC, SEM_SPEC, pl.BlockSpec(memory_space=pltpu.VMEM)],
        input_output_aliases={w: n + w for w in range(n)},
        compiler_params=pltpu.CompilerParams(**SPLIT_COPY_PARAMS))(*[_hbm(p) for p in parts])
    return outs[:n], outs[n:2 * n], outs[2 * n], outs[2 * n + 1], outs[2 * n + 2]


def _rs_wait(parts, landing, send_sems, recv_sems, after, name):
    n = len(parts)

    def body(*refs):
        p_refs = refs[:n]
        send, recv = refs[2 * n], refs[2 * n + 1]
        landed = refs[3 * n + 3:]
        x, y, c = _place()
        for w in range(n):
            for rel in (1, 2, 3):
                copy = _rs_copy(p_refs[w], landed[w], send, recv, w, rel, x, y, c)
                copy.wait_send()
                copy.wait_recv()

    outs = pl.pallas_call(
        body, name=name, out_shape=[pltpu.HBM(v.shape, v.dtype) for v in list(parts) + list(landing)],
        in_specs=[HBM_SPEC] * (2 * n) + [SEM_SPEC, SEM_SPEC, pl.BlockSpec(memory_space=pl.ANY)], out_specs=[HBM_SPEC] * (2 * n),
        input_output_aliases={i: i for i in range(2 * n)},
        compiler_params=pltpu.CompilerParams(**SPLIT_COPY_PARAMS))(*parts, *landing, send_sems, recv_sems, after)
    return outs[n:]


def _exchange_small(v, reduce, name, after=None):
    R, C = v.shape

    def body(v_ref, *rest):
        out_ref, gath, send_sems, recv_sems = rest[-4:]
        x, y, c = _place()
        me = 4 * x + 2 * y + c
        buf = gath if reduce else out_ref
        buf[me] = v_ref[...]
        copies = []
        for rel in range(1, N_DEV):
            peer = (_flip(x, rel & 4), _flip(y, rel & 2), _flip(c, rel & 1))
            copies.append(pltpu.make_async_remote_copy(
                src_ref=v_ref, dst_ref=buf.at[me], send_sem=send_sems.at[rel - 1], recv_sem=recv_sems.at[rel - 1],
                device_id=peer, device_id_type=MESH))
        for cp in copies:
            cp.start()
        for cp in copies:
            cp.wait()
        if reduce:
            acc = gath[0]
            for d in range(1, N_DEV):
                acc = acc + gath[d]
            out_ref[...] = acc

    vm = pl.BlockSpec(memory_space=pltpu.VMEM)
    extra = [] if after is None else [after]
    return pl.pallas_call(
        body, out_shape=S((R, C) if reduce else (N_DEV, R, C), F32), in_specs=[vm] + [pl.BlockSpec(memory_space=pl.ANY)] * len(extra),
        out_specs=vm, name=name,
        scratch_shapes=[pltpu.VMEM((N_DEV, R, C) if reduce else (8, LANES), F32), pltpu.SemaphoreType.DMA((N_DEV - 1,)),
                        pltpu.SemaphoreType.DMA((N_DEV - 1,))])(v, *extra)


def _pad_rows(flat, cols, mult):
    n = flat.shape[-1]
    rows = -(-n // cols)
    rows = -(-rows // mult) * mult
    pad = [(0, 0)] * (flat.ndim - 1) + [(0, rows * cols - n)]
    return jnp.pad(flat, pad).reshape(flat.shape[:-1] + (rows, cols))


def _round_up(n, m):
    return -(-n // m) * m


def _shard_axes(a):
    return [(2, _round_up(a[n].shape[2], LANES)) if kind == 'col' else (1, _round_up(a[n].shape[1 if kind == 'row' else 2], LANES))
            for n, kind in BIG]


def _pack_small(vals):
    rows = [_pad_rows(vals[n].astype(F32).reshape(-1), SMALL_COLS, 1) for n in SMALL]
    m = jnp.concatenate(rows, axis=0)
    return jnp.pad(m, ((0, -m.shape[0] % 8), (0, 0)))


def _unpack_small(m, a):
    out, r = {}, 0
    for n in SMALL:
        nr = -(-a[n].size // SMALL_COLS)
        out[n] = m[r:r + nr].reshape(-1)[:a[n].size].reshape(a[n].shape)
        r += nr
    return out, r


GROUPS = (('ffn1_w_gate', 'ffn1_w_up', 'ffn1_w_down'), ('w_in', 'w_out', 'xattn_w_q', 'xattn_w_kv', 'xattn_w_o'),
          ('ffn2_w_gate', 'ffn2_w_up', 'ffn2_w_down'))


def _layer_small(a, conv_w_full, l):
    H = a['b_f'].shape[1]
    return dict(
        bft=jnp.pad(a['b_f'][l].reshape(H, 1), ((0, 16 - H), (0, 0))),
        cw=jnp.pad(conv_w_full[l], ((0, CONV_PAD - CONV_WIDTH), (0, 0))), cb=a['conv_b'][l].reshape(1, -1),
        lg=a['conv_ln_g'][l].reshape(1, -1), lb=a['conv_ln_b'][l].reshape(1, -1),
        ag=a['attn_out_g'][l].reshape(1, -1), cg=a['conv_out_g'][l].reshape(1, -1),
        g1=a['ffn1_norm_g'][l], gm=a['mix_norm_g'][l], gx=a['xattn_norm_g'][l], gmem=a['mem_norm_g'][l], g2=a['ffn2_norm_g'][l])


def _layer_fwd(x0, mem, w, fetch, cfg, l):
    T = x0.shape[0]
    H = cfg['heads']
    sv = {'x0': x0}
    m = fetch(l, 0, x0)
    w.update(wg1=(m['ffn1_w_gate'], 0), wu1=(m['ffn1_w_up'], 0), wd1=(m['ffn1_w_down'], 0))
    sv['h1'] = _rms_fwd(x0, w['g1'], f"l{l}_ffn1_norm")
    sv['G1'], sv['U1'], sv['A1'] = _ffn_up(sv['h1'], w['wg1'], w['wu1'], f"l{l}_ffn1_up")
    x1 = sv['x1'] = _mm_res(sv['A1'], w['wd1'], x0, 0.5, f"l{l}_ffn1_down")
    m = fetch(l, 1, x1)
    wqkv, wf, wag = _win_split(m['w_in'], cfg['pieces'], cfg['widths'], f"l{l}_w_in_split")
    w.update(wqkv=(wqkv, 0), wft=wf[0, :, :16].T, wag=(wag, 0), wout=(m['w_out'], 0), wq=(m['xattn_w_q'], 0),
             wkv=(m['xattn_w_kv'], 0), wo=(m['xattn_w_o'], 0))
    h2 = sv['h2'] = _rms_fwd(x1, w['gm'], f"l{l}_mix_norm")
    sv['qkv'] = _mm(h2, w['wqkv'], BF16, f"l{l}_qkv_proj")
    sv['agv'] = _mm(h2, w['wag'], F32, f"l{l}_glu_proj")
    ct, sv['sg'] = _fox_prep(h2, w['wft'], w['bft'], f"l{l}_fox_prep")
    sv['c_col'] = ct[:H].reshape(H, T, 1)
    sv['c_row'] = ct[:H].reshape(H // 2, 2, T)
    sv['attn'], sv['lse'] = _fox_fwd(sv['qkv'], sv['c_col'], sv['c_row'], f"l{l}_fox_fwd")
    sv['yc'] = _conv_fwd(sv['agv'], w['cw'], w['cb'], f"l{l}_conv_fwd")
    sv['ycat'] = _mix_post(sv['attn'], sv['yc'], w['ag'], w['cg'], w['lg'], w['lb'], f"l{l}_mix_post")
    x2 = sv['x2'] = _mm_res(sv['ycat'], w['wout'], x1, 1.0, f"l{l}_out_proj")
    sv['h3'] = _rms_fwd(x2, w['gx'], f"l{l}_xattn_norm")
    sv['memn'] = _rms_fwd(mem, w['gmem'], f"l{l}_mem_norm")
    sv['q'] = _mm(sv['h3'], w['wq'], BF16, f"l{l}_xattn_q")
    sv['kv'] = _mm_nt(sv['memn'], w['wkv'], BF16, f"l{l}_xattn_kv")
    sv['o'] = _xattn_fwd(sv['q'], sv['kv'], f"l{l}_xattn_fwd")
    x3 = sv['x3'] = _mm_res(sv['o'], w['wo'], x2, 1.0, f"l{l}_xattn_out")
    m = fetch(l, 2, x3)
    w.update(wg2=(m['ffn2_w_gate'], 0), wu2=(m['ffn2_w_up'], 0), wd2=(m['ffn2_w_down'], 0))
    sv['h4'] = _rms_fwd(x3, w['g2'], f"l{l}_ffn2_norm")
    sv['G2'], sv['U2'], sv['A2'] = _ffn_up(sv['h4'], w['wg2'], w['wu2'], f"l{l}_ffn2_up")
    return _mm_res(sv['A2'], w['wd2'], x3, 0.5, f"l{l}_ffn2_down"), sv


def _ffn_bwd(dout, x_in, h, G, U, A, wg, wu, wd, g, tag, put, which, dep, flush):
    dG, dU = _ffn_bwd_act(dout, wd, G, U, 0.5, tag + "_bwd_act", dep)
    put(which + '_w_down', A, dout, 0.5, tag + "_dwd")
    put(which + '_w_gate', dG, h, 1.0, tag + "_dwg")
    put(which + '_w_up', dU, h, 1.0, tag + "_dwu")
    dep = flush()
    dx, dg = _bwd_h([(dG, wg, 'nn'), (dU, wu, 'nn')], x_in, g, dout, tag + "_bwd_h", dep)
    return dx, dg, dep


def _layer_bwd(dx4, mem, w, sv, reduce, cfg, l, dep):
    small, grads = {}, {}
    T = dx4.shape[0]
    H = cfg['heads']

    def put(key, act, dy, scale, name):
        grads[key] = _wgrad(act, dy, scale, name, (None, 0, 1))

    dx3, small['ffn2_norm_g'], dep = _ffn_bwd(
        dx4, sv['x3'], sv['h4'], sv['G2'], sv['U2'], sv['A2'], w['wg2'], w['wu2'], w['wd2'], w['g2'], f"l{l}_ffn2", put, 'ffn2',
        dep, lambda: reduce(l, 2, {n: grads.pop(n) for n in GROUPS[2]}))
    do = _mm_nt(dx3, w['wo'], BF16, f"l{l}_xattn_do", dep)
    put('xattn_w_o', sv['o'], dx3, 1.0, f"l{l}_dwo")
    dq, dkv = _xattn_bwd(sv['q'], sv['kv'], do, f"l{l}_xattn_bwd")
    put('xattn_w_q', sv['h3'], dq, 1.0, f"l{l}_dwq")
    dx2, small['xattn_norm_g'] = _bwd_h([(dq, w['wq'], 'nt')], sv['x2'], w['gx'], dx3, f"l{l}_xattn_bwd_h")
    dmemn = _mm(dkv, w['wkv'], F32, f"l{l}_dmemn")
    put('xattn_w_kv', dkv, sv['memn'], 1.0, f"l{l}_dwkv")
    small['mem_norm_g'] = _rms_gain_grad(dmemn, mem, w['gmem'], f"l{l}_dgmem")
    dycat = _mm_nt(dx2, w['wout'], F32, f"l{l}_dycat")
    put('w_out', sv['ycat'], dx2, 1.0, f"l{l}_dwout")
    dattn, dyc, small['attn_out_g'], small['conv_out_g'], small['conv_ln_g'], small['conv_ln_b'] = _mix_post_bwd(
        dycat, sv['attn'], sv['yc'], w['ag'], w['cg'], w['lg'], w['lb'], f"l{l}_mix_post_bwd")
    dva, dga, dcw, small['conv_b'] = _conv_bwd(dyc, sv['agv'], w['cw'], f"l{l}_conv_bwd")
    dq_, dk_, dv_, dcs = _fox_bwd(sv['qkv'], sv['c_col'], sv['c_row'], sv['lse'], dattn, f"l{l}_fox_bwd")
    dcs16 = jnp.pad(dcs.reshape(H, T), ((0, 16 - H), (0, 0)))
    dflt, dwft, dbf = _fox_prep_bwd(dcs16, sv['sg'], sv['h2'], f"l{l}_fox_prep_bwd")
    small['b_f'] = dbf[:H].reshape(H)
    dqkv = jnp.concatenate([dq_, dk_, dv_], axis=1)
    dag = jnp.concatenate([dva, dga], axis=1)
    put('wqkv', sv['h2'], dqkv, 1.0, f"l{l}_dwqkv")
    put('wag', sv['h2'], dag, 1.0, f"l{l}_dwag")
    dx1, small['mix_norm_g'] = _bwd_h([(dqkv, w['wqkv'], 'nt'), (dag, w['wag'], 'nt'), (dflt, w['wft'], 'tn')],
                                      sv['x1'], w['gm'], dx2, f"l{l}_mix_bwd_h")
    dwf = jnp.pad(dwft[:H].T, ((0, 0), (0, LANES - H)))[None].astype(BF16)
    grads['w_in'] = _win_merge((grads.pop('wqkv'), dwf, grads.pop('wag')), cfg['pieces'], cfg['chunked_cols'], f"l{l}_w_in_merge")
    dep = reduce(l, 1, {n: grads.pop(n) for n in GROUPS[1]})
    dx0, small['ffn1_norm_g'], dep = _ffn_bwd(
        dx1, sv['x0'], sv['h1'], sv['G1'], sv['U1'], sv['A1'], w['wg1'], w['wu1'], w['wd1'], w['g1'], f"l{l}_ffn1", put, 'ffn1',
        dep, lambda: reduce(l, 0, {n: grads.pop(n) for n in GROUPS[0]}))
    small = {k: v.reshape(-1) for k, v in small.items()}
    return dx0, small, dcw[:CONV_WIDTH], dep


def _local_step(x, mem, tgt, a, conv_w_full, fetch, reduce, cfg):
    L = a['b_f'].shape[0]
    ws = [_layer_small(a, conv_w_full, l) for l in range(L)]
    saved = []
    for l in range(L):
        x, sv = _layer_fwd(x, mem, ws[l], fetch, cfg, l)
        saved.append(sv)
    loss, dx, dgf = _loss_head(x, a['final_norm_g'], tgt, "loss_head")
    smalls, dcws, dep = [None] * L, [None] * L, None
    for l in range(L - 1, -1, -1):
        dx, smalls[l], dcws[l], dep = _layer_bwd(dx, mem, ws[l], saved[l], reduce, cfg, l, dep)
    small = {n: jnp.stack([smalls[l][n] for l in range(L)]) for n in SMALL if n != 'final_norm_g'}
    small['final_norm_g'] = dgf.reshape(-1)
    return loss, dx, small, jnp.stack(dcws)


def kernel(x, mem, ffn1_norm_g, ffn1_w_gate, ffn1_w_up, ffn1_w_down, mix_norm_g, w_in, b_f, conv_w, conv_b, conv_ln_g, conv_ln_b, attn_out_g, conv_out_g, w_out, xattn_norm_g, mem_norm_g, xattn_w_q, xattn_w_kv, xattn_w_o, ffn2_norm_g, ffn2_w_gate, ffn2_w_up, ffn2_w_down, final_norm_g, loss_target, m_ffn1_norm_g, m_ffn1_w_gate, m_ffn1_w_up, m_ffn1_w_down, m_mix_norm_g, m_w_in, m_b_f, m_conv_w, m_conv_b, m_conv_ln_g, m_conv_ln_b, m_attn_out_g, m_conv_out_g, m_w_out, m_xattn_norm_g, m_mem_norm_g, m_xattn_w_q, m_xattn_w_kv, m_xattn_w_o, m_ffn2_norm_g, m_ffn2_w_gate, m_ffn2_w_up, m_ffn2_w_down, m_final_norm_g, v_ffn1_norm_g, v_ffn1_w_gate, v_ffn1_w_up, v_ffn1_w_down, v_mix_norm_g, v_w_in, v_b_f, v_conv_w, v_conv_b, v_conv_ln_g, v_conv_ln_b, v_attn_out_g, v_conv_out_g, v_w_out, v_xattn_norm_g, v_mem_norm_g, v_xattn_w_q, v_xattn_w_kv, v_xattn_w_o, v_ffn2_norm_g, v_ffn2_w_gate, v_ffn2_w_up, v_ffn2_w_down, v_final_norm_g):
    args = (x, mem, ffn1_norm_g, ffn1_w_gate, ffn1_w_up, ffn1_w_down, mix_norm_g, w_in, b_f, conv_w, conv_b, conv_ln_g, conv_ln_b, attn_out_g, conv_out_g, w_out, xattn_norm_g, mem_norm_g, xattn_w_q, xattn_w_kv, xattn_w_o, ffn2_norm_g, ffn2_w_gate, ffn2_w_up, ffn2_w_down, final_norm_g)
    moments_m = (m_ffn1_norm_g, m_ffn1_w_gate, m_ffn1_w_up, m_ffn1_w_down, m_mix_norm_g, m_w_in, m_b_f, m_conv_w, m_conv_b, m_conv_ln_g, m_conv_ln_b, m_attn_out_g, m_conv_out_g, m_w_out, m_xattn_norm_g, m_mem_norm_g, m_xattn_w_q, m_xattn_w_kv, m_xattn_w_o, m_ffn2_norm_g, m_ffn2_w_gate, m_ffn2_w_up, m_ffn2_w_down, m_final_norm_g)
    moments_v = (v_ffn1_norm_g, v_ffn1_w_gate, v_ffn1_w_up, v_ffn1_w_down, v_mix_norm_g, v_w_in, v_b_f, v_conv_w, v_conv_b, v_conv_ln_g, v_conv_ln_b, v_attn_out_g, v_conv_out_g, v_w_out, v_xattn_norm_g, v_mem_norm_g, v_xattn_w_q, v_xattn_w_kv, v_xattn_w_o, v_ffn2_norm_g, v_ffn2_w_gate, v_ffn2_w_up, v_ffn2_w_down, v_final_norm_g)
    a = dict(zip(NAMES, args))
    am = dict(zip(WEIGHTS, moments_m))
    av = dict(zip(WEIGHTS, moments_v))
    L, taps, cshard = conv_w.shape
    dev = 4 * lax.axis_index("x") + 2 * lax.axis_index("y") + lax.axis_index("c")

    big_names = [n for n, _ in BIG]
    geometry = dict(zip(big_names, _shard_axes(a)))
    n_attn, n_heads, n_conv = attn_out_g.shape[1], b_f.shape[1], conv_out_g.shape[1]
    chunk = geometry['w_in'][1]
    cfg = dict(heads=n_heads, pieces=_win_pieces(n_attn, n_heads, n_conv, w_in.shape[2], chunk),
               widths=(3 * n_attn, LANES, 2 * n_conv), chunked_cols=N_DEV * chunk)

    cw_rows = _pad_rows(conv_w.reshape(-1), LANES, 8)
    cw_all = _exchange_small(cw_rows, False, "allgather_conv_w")
    conv_w_full = cw_all.reshape(N_DEV, -1)[:, :conv_w.size].reshape(N_DEV, L, taps, cshard).transpose(1, 2, 0, 3).reshape(
        L, taps, N_DEV * cshard)

    keys = [(l, n) for l in range(L) for names in GROUPS for n in names]
    members = [[keys.index((l, n)) for n in names] for l in range(L) for names in GROUPS]
    shards = []
    for l, n in keys:
        ax, size = geometry[n]
        shard = _as_handled(n, a[n][l:l + 1]).astype(BF16)
        pad = [(0, 0)] * 3
        pad[ax] = (0, size - shard.shape[ax])
        shards.append(jnp.pad(shard, pad))
    key_axes = [geometry[n][0] for _, n in keys]
    fulls, thru, ag_send, ag_recv = _ag_start(shards, key_axes, members, cw_all, "allgather_start")

    def fetch(l, gi, after):
        g = l * len(GROUPS) + gi
        axs = [key_axes[i] for i in members[g]]
        own, landed = _ag_wait([thru[i] for i in members[g]], [fulls[i] for i in members[g]], ag_send[g], ag_recv[g], axs, after,
                               f"allgather_wait_l{l}g{gi}")
        return dict(zip(GROUPS[gi], _ag_forward(landed, own, axs, f"allgather_forward_l{l}g{gi}")))

    pending, own_part, landed_part = [], {}, {}

    def reduce(l, gi, grads):
        names = GROUPS[gi]
        axs = [geometry[n][0] for n in names]
        recv = _to_sibling([grads[n] for n in names], axs, [geometry[n][1] for n in names], f"reduce_to_sibling_l{l}g{gi}")
        parts = []
        for n, r, ax in zip(names, recv, axs):
            part, own_part[(l, n)] = _pair_add(grads[n], r, ax, f"reduce_pair_add_l{l}_{n}")
            parts.append(part)
        landing, parts_thru, send, recv_sems, token = _rs_start(parts, f"reduce_start_l{l}g{gi}")
        pending.append((l, gi, parts_thru, landing, send, recv_sems))
        return token

    loss, grad_x, gsmall, dcw = _local_step(x[0], mem[0], loss_target[0], a, conv_w_full, fetch, reduce, cfg)

    def wait_group(entry, after):
        l, gi, parts_thru, landing, send, recv_sems = entry
        for n, arr in zip(GROUPS[gi], _rs_wait(parts_thru, landing, send, recv_sems, after, f"reduce_wait_l{l}g{gi}")):
            landed_part[(l, n)] = arr

    for entry in pending[:-1]:
        wait_group(entry, grad_x)

    grads, delta, new_m, new_v = {}, {}, {}, {}

    def update(n):
        outs = _adamw_sum(_as_handled(n, a[n]), _as_handled(n, am[n]), _as_handled(n, av[n]),
                          [own_part[(l, n)] for l in range(L)], [landed_part[(l, n)] for l in range(L)], "adamw_" + n)
        grads[n], delta[n], new_m[n], new_v[n] = (_as_handled(n, o) for o in outs)

    last_names = GROUPS[pending[-1][1]]
    early = [n for n in big_names if n not in last_names]
    for n in early:
        update(n)
    wait_group(pending[-1], delta[early[-1]])
    for n in last_names:
        update(n)

    small_rows = _pack_small(gsmall)
    n_small = small_rows.shape[0]
    dcw_rows = jnp.pad(dcw, ((0, 0), (0, CONV_PAD - taps), (0, 0))).reshape(-1, SMALL_COLS)
    summed = _exchange_small(jnp.concatenate([small_rows, dcw_rows], axis=0), True, "allreduce_small",
                             after=landed_part[(pending[-1][0], last_names[0])])
    g_small, _ = _unpack_small(summed[:n_small], a)
    dcw_sum = summed[n_small:].reshape(L, CONV_PAD, N_DEV * cshard)[:, :taps]
    grads.update(g_small)
    grads['conv_w'] = lax.dynamic_slice_in_dim(dcw_sum, dev * cshard, cshard, axis=2)
    delta['conv_w'], new_m['conv_w'], new_v['conv_w'] = _adamw(conv_w, am['conv_w'], av['conv_w'], grads['conv_w'], "adamw_conv_w")
    pw, pm, pv, pg = (_pack_small(d) for d in (a, am, av, g_small))
    for dst, packed in zip((delta, new_m, new_v), _adamw(pw, pm, pv, pg, "adamw_small")):
        dst.update(_unpack_small(packed, a)[0])

    total = lax.psum(loss.reshape(()), ("x", "y", "c"))
    return (total, grad_x[None], *[grads[n] for n in WEIGHTS], *[delta[n] for n in WEIGHTS], *[new_m[n] for n in WEIGHTS],
            *[new_v[n] for n in WEIGHTS])
```

```python
import math

import jax
import jax.numpy as jnp
from jax import lax
from jax.experimental import pallas as pl
from jax.experimental.pallas import tpu as pltpu

F32, BF16 = jnp.float32, jnp.bfloat16
S = jax.ShapeDtypeStruct
MESH = pl.DeviceIdType.MESH

EPS = 1e-6
NEG_INF = -1e30
HEAD_DIM = 64
N_XATTN_HEADS = 4
CONV_WIDTH = 31
CONV_PAD = 32
LANES = 128
ADAM_LR, ADAM_B1, ADAM_B2, ADAM_EPS, ADAM_WD, ADAM_STEP = 0.001, 0.9, 0.999, 1e-08, 0.01, 10
N_DEV = 8
VMEM_LIMIT_BYTES = 56 * 1024 * 1024
ROW_TILE = 512
MM_ROW_TILE = 1024
EPILOGUE_COLS = 512
SMALL_COLS = 512

NN = ((1,), (0,))
NT = ((1,), (1,))
TN = ((0,), (0,))

NAMES = ['x', 'mem', 'ffn1_norm_g', 'ffn1_w_gate', 'ffn1_w_up', 'ffn1_w_down', 'mix_norm_g', 'w_in', 'b_f', 'conv_w', 'conv_b',
         'conv_ln_g', 'conv_ln_b', 'attn_out_g', 'conv_out_g', 'w_out', 'xattn_norm_g', 'mem_norm_g', 'xattn_w_q', 'xattn_w_kv',
         'xattn_w_o', 'ffn2_norm_g', 'ffn2_w_gate', 'ffn2_w_up', 'ffn2_w_down', 'final_norm_g']
WEIGHTS = NAMES[2:]
BIG = [('ffn1_w_gate', 'colT'), ('ffn1_w_up', 'colT'), ('ffn1_w_down', 'row'), ('w_in', 'col'), ('w_out', 'row'),
       ('xattn_w_q', 'row'), ('xattn_w_kv', 'colT'), ('xattn_w_o', 'row'), ('ffn2_w_gate', 'colT'), ('ffn2_w_up', 'colT'),
       ('ffn2_w_down', 'row')]
TRANSPOSED = tuple(n for n, kind in BIG if kind == 'colT')


def _as_handled(n, v):
    return jnp.swapaxes(v, 1, 2) if n in TRANSPOSED else v
SMALL = ['ffn1_norm_g', 'mix_norm_g', 'xattn_norm_g', 'mem_norm_g', 'ffn2_norm_g', 'conv_b', 'conv_ln_g', 'conv_ln_b',
         'attn_out_g', 'conv_out_g', 'b_f', 'final_norm_g']


def _dot(a, b, dims):
    return lax.dot_general(a, b, (dims, ((), ())), preferred_element_type=F32)


def _full(shape):
    nd = len(shape)
    return pl.BlockSpec(shape, lambda *_: (0,) * nd)


def _tile(n, pref):
    for t in (pref, 512, 384, 256, 128, 64, 32, 16, 8):
        if t <= n and n % t == 0:
            return t
    return n


def _pcall(name, body, grid, in_specs, out_specs, out_shape, scratch=(), aliases=None, dep=None):
    n_in = len(in_specs)
    kernel_body = body
    if dep is not None:
        in_specs = list(in_specs) + [pl.BlockSpec(memory_space=pl.ANY)]

        def kernel_body(*refs):
            return body(*refs[:n_in], *refs[n_in + 1:])

    call = pl.pallas_call(
        kernel_body, grid=grid, in_specs=in_specs, out_specs=out_specs, out_shape=out_shape, scratch_shapes=list(scratch),
        name=name, input_output_aliases=aliases or {},
        compiler_params=pltpu.CompilerParams(dimension_semantics=("arbitrary",) * len(grid), vmem_limit_bytes=VMEM_LIMIT_BYTES))
    return call if dep is None else (lambda *args: call(*args, dep))


def _arr(w):
    return w[0] if isinstance(w, tuple) else w


def _wshape(w):
    return w[0].shape[1:] if isinstance(w, tuple) else w.shape


def _wspec(w, block, imap, **kw):
    if isinstance(w, tuple):
        layer = w[1]
        return pl.BlockSpec((None,) + block, lambda *g: (layer,) + imap(*g), **kw)
    return pl.BlockSpec(block, imap, **kw)


def _wfull(w, single_buffer=False):
    shape = _wshape(w)
    kw = dict(pipeline_mode=pl.Buffered(1)) if single_buffer else {}
    return _wspec(w, shape, lambda *_: (0,) * len(shape), **kw)


def _sigmoid(z):
    return jax.nn.sigmoid(z)


def _rstd(x):
    return lax.rsqrt(jnp.mean(x * x, axis=-1, keepdims=True) + EPS)


def _rms_bwd(dy, x, g):
    r = _rstd(x)
    xh = x * r
    u = dy * g
    dx = r * (u - xh * jnp.mean(u * xh, axis=-1, keepdims=True))
    return dx, dy * xh


def _colsum(v):
    return jnp.sum(v, axis=0, keepdims=True)


def _rms_fwd(x, g, name):
    T, D = x.shape
    tm = _tile(T, ROW_TILE)

    def body(x_ref, g_ref, h_ref):
        xv = x_ref[...]
        h_ref[...] = (xv * _rstd(xv) * g_ref[...]).astype(BF16)

    row = pl.BlockSpec((tm, D), lambda i: (i, 0))
    return _pcall(name, body, (T // tm,), [row, _full((1, D))], row, S((T, D), BF16))(x, g.reshape(1, D))


def _mm(a, w, out_dtype, name):
    M, K = a.shape
    N = _wshape(w)[1]
    tm = _tile(M, MM_ROW_TILE)
    tn = N if N <= 1536 else N // 2

    def body(a_ref, w_ref, o_ref):
        o_ref[...] = _dot(a_ref[...].astype(BF16), w_ref[...], NN).astype(out_dtype)

    return _pcall(name, body, (N // tn, M // tm),
                  [pl.BlockSpec((tm, K), lambda j, i: (i, 0)), _wspec(w, (K, tn), lambda j, i: (0, j))],
                  pl.BlockSpec((tm, tn), lambda j, i: (i, j)), S((M, N), out_dtype))(a, _arr(w))


def _mm_res(a, w, res, scale, name):
    M, K = a.shape
    N = _wshape(w)[1]
    tm = _tile(M, MM_ROW_TILE)

    def body(a_ref, w_ref, r_ref, o_ref):
        o_ref[...] = r_ref[...] + scale * _dot(a_ref[...], w_ref[...], NN)

    row = pl.BlockSpec((tm, N), lambda i: (i, 0))
    return _pcall(name, body, (M // tm,), [pl.BlockSpec((tm, K), lambda i: (i, 0)), _wfull(w, single_buffer=True), row], row,
                  S((M, N), F32))(a, _arr(w), res)


def _mm_nt(a, w, out_dtype, name, dep=None):
    M, K = a.shape
    N = _wshape(w)[0]
    tm = _tile(M, MM_ROW_TILE)
    tn = N if N <= 1536 else N // 2

    def body(a_ref, w_ref, o_ref):
        o_ref[...] = _dot(a_ref[...].astype(BF16), w_ref[...], NT).astype(out_dtype)

    return _pcall(name, body, (N // tn, M // tm),
                  [pl.BlockSpec((tm, K), lambda j, i: (i, 0)), _wspec(w, (tn, K), lambda j, i: (j, 0))],
                  pl.BlockSpec((tm, tn), lambda j, i: (i, j)), S((M, N), out_dtype), dep=dep)(a, _arr(w))


def _wgrad(a, dy, scale, name, into):
    buf, layer, L = into
    T, M = a.shape
    N = dy.shape[1]
    tm = _tile(M, MM_ROW_TILE)

    def body(a_ref, dy_ref, *rest):
        rest[-1][...] = (scale * _dot(a_ref[...].astype(BF16), dy_ref[...].astype(BF16), TN)).astype(BF16)

    in_specs = [pl.BlockSpec((T, tm), lambda i: (0, i)), pl.BlockSpec((T, N), lambda i: (0, 0), pipeline_mode=pl.Buffered(1))]
    args = [a, dy]
    if buf is not None:
        in_specs.append(pl.BlockSpec(memory_space=pl.ANY))
        args.append(buf)
    return _pcall(name, body, (M // tm,), in_specs, pl.BlockSpec((None, tm, N), lambda i: (layer, i, 0)), S((L, M, N), BF16),
                  aliases={2: 0} if buf is not None else None)(*args)


def _bwd_h(dots, x, g, dres, name, dep=None):
    T, D = x.shape
    tm = _tile(T, ROW_TILE)
    n = len(dots)
    dims = [{'nt': NT, 'nn': NN, 'tn': TN}[m] for _, _, m in dots]

    def body(*refs):
        x_ref, g_ref, r_ref, dx_ref, dg_ref = refs[2 * n:]
        dh = None
        for k in range(n):
            part = _dot(refs[2 * k][...], refs[2 * k + 1][...], dims[k])
            dh = part if dh is None else dh + part
        dx, dgrow = _rms_bwd(dh, x_ref[...], g_ref[...])
        dx_ref[...] = r_ref[...] + dx

        @pl.when(pl.program_id(0) == 0)
        def _():
            dg_ref[...] = jnp.zeros_like(dg_ref)

        dg_ref[...] += _colsum(dgrow)

    in_specs, args = [], []
    for lhs, w, mode in dots:
        if mode == 'tn':
            in_specs.append(pl.BlockSpec((lhs.shape[0], tm), lambda i: (0, i)))
        else:
            in_specs.append(pl.BlockSpec((tm, lhs.shape[1]), lambda i: (i, 0)))
        in_specs.append(_wfull(w, single_buffer=True))
        args += [lhs, _arr(w)]
    row = pl.BlockSpec((tm, D), lambda i: (i, 0))
    in_specs += [row, _full((1, D)), row]
    return _pcall(name, body, (T // tm,), in_specs, [row, _full((1, D))], [S((T, D), F32), S((1, D), F32)], dep=dep)(
        *args, x, g.reshape(1, D), dres)


def _rms_gain_grad(dy, x, g, name):
    T, D = x.shape

    def body(dy_ref, x_ref, g_ref, dg_ref):
        _, dgrow = _rms_bwd(dy_ref[...], x_ref[...], g_ref[...])
        dg_ref[...] = _colsum(dgrow)

    return _pcall(name, body, (), [_full((T, D)), _full((T, D)), _full((1, D))], _full((1, D)), S((1, D), F32))(
        dy, x, g.reshape(1, D))


def _ffn_up(h, wg, wu, name):
    T, D = h.shape
    Fh = _wshape(wg)[0]
    tm = _tile(T, MM_ROW_TILE)
    tn = Fh if Fh <= 1536 else Fh // 2
    tc = _tile(tn, EPILOGUE_COLS)

    def body(h_ref, wg_ref, wu_ref, g_ref, u_ref, a_ref):
        hv = h_ref[...]
        for cb in range(tn // tc):
            cols = slice(cb * tc, (cb + 1) * tc)
            gv = _dot(hv, wg_ref[cols, :], NT)
            uv = _dot(hv, wu_ref[cols, :], NT)
            g_ref[:, cols] = gv.astype(BF16)
            u_ref[:, cols] = uv.astype(BF16)
            a_ref[:, cols] = (gv * _sigmoid(gv) * uv).astype(BF16)

    tile = pl.BlockSpec((tm, tn), lambda j, i: (i, j))
    return _pcall(name, body, (Fh // tn, T // tm),
                  [pl.BlockSpec((tm, D), lambda j, i: (i, 0)), _wspec(wg, (tn, D), lambda j, i: (j, 0)),
                   _wspec(wu, (tn, D), lambda j, i: (j, 0))],
                  [tile, tile, tile], [S((T, Fh), BF16)] * 3)(h, _arr(wg), _arr(wu))


def _ffn_bwd_act(dout, wd, gate, up, scale, name, dep=None):
    T, D = dout.shape
    Fh = _wshape(wd)[0]
    tm = _tile(T, MM_ROW_TILE)
    tn = Fh if Fh <= 1536 else Fh // 2
    tc = _tile(tn, EPILOGUE_COLS)

    def body(d_ref, w_ref, g_ref, u_ref, dg_ref, du_ref):
        dv = d_ref[...].astype(BF16)
        for cb in range(tn // tc):
            cols = slice(cb * tc, (cb + 1) * tc)
            da = scale * _dot(dv, w_ref[cols, :], NT)
            gv = g_ref[:, cols].astype(F32)
            uv = u_ref[:, cols].astype(F32)
            sg = _sigmoid(gv)
            dg_ref[:, cols] = (da * uv * (sg * (1.0 + gv * (1.0 - sg)))).astype(BF16)
            du_ref[:, cols] = (da * (gv * sg)).astype(BF16)

    tile = pl.BlockSpec((tm, tn), lambda j, i: (i, j))
    return _pcall(name, body, (Fh // tn, T // tm),
                  [pl.BlockSpec((tm, D), lambda j, i: (i, 0)), _wspec(wd, (tn, D), lambda j, i: (j, 0)), tile, tile],
                  [tile, tile], [S((T, Fh), BF16)] * 2, dep=dep)(dout, _arr(wd), gate, up)


def _loss_head(x, g, tgt, name):
    T, D = x.shape
    tm = _tile(T, ROW_TILE)

    def body(x_ref, g_ref, t_ref, loss_ref, dx_ref, dg_ref):
        xv = x_ref[...]
        gv = g_ref[...]
        r = _rstd(xv)
        xh = xv * r
        e = xh * gv - t_ref[...]
        dy = e * (1.0 / D)
        u = dy * gv
        dx_ref[...] = r * (u - xh * jnp.mean(u * xh, axis=-1, keepdims=True))

        @pl.when(pl.program_id(0) == 0)
        def _():
            dg_ref[...] = jnp.zeros_like(dg_ref)
            loss_ref[...] = jnp.zeros_like(loss_ref)

        dg_ref[...] += _colsum(dy * xh)
        loss_ref[...] += 0.5 * _colsum(jnp.mean(e * e, axis=-1, keepdims=True))

    row = pl.BlockSpec((tm, D), lambda i: (i, 0))
    return _pcall(name, body, (T // tm,), [row, _full((1, D)), row], [_full((1, 1)), row, _full((1, D))],
                  [S((1, 1), F32), S((T, D), F32), S((1, D), F32)])(x, g.reshape(1, D), tgt)


def _split3(xb):
    hi = xb.astype(BF16)
    r1 = xb - hi.astype(F32)
    mid = r1.astype(BF16)
    lo = (r1 - mid.astype(F32)).astype(BF16)
    return hi, mid, lo


def _fox_prep(h, wft, bft, name):
    T, D = h.shape
    blk = _tile(T, 256)

    def body(h_ref, w_ref, b_ref, ct_ref, sg_ref):
        z = _dot(w_ref[...], h_ref[...], NT) + b_ref[...]
        sg_ref[...] = 1.0 - _sigmoid(z)
        logf = jnp.minimum(z, 0.0) - jnp.log1p(jnp.exp(-jnp.abs(z)))
        upper = (lax.broadcasted_iota(jnp.int32, (blk, blk), 0) <= lax.broadcasted_iota(jnp.int32, (blk, blk), 1)).astype(BF16)
        carry = jnp.zeros((16, 1), F32)
        for b in range(T // blk):
            hi, mid, lo = _split3(logf[:, b * blk:(b + 1) * blk])
            cb = _dot(hi, upper, NN) + _dot(mid, upper, NN) + _dot(lo, upper, NN) + carry
            ct_ref[:, b * blk:(b + 1) * blk] = cb
            carry = cb[:, blk - 1:blk]

    return _pcall(name, body, (), [_full((T, D)), _full((16, D)), _full((16, 1))], [_full((16, T)), _full((16, T))],
                  [S((16, T), F32), S((16, T), F32)])(h, wft, bft)


def _fox_prep_bwd(dcs, sg, h, name):
    T, D = h.shape
    blk = _tile(T, 256)
    nb = T // blk

    def body(dcs_ref, sg_ref, h_ref, dfl_ref, dw_ref, db_ref):
        lower = (lax.broadcasted_iota(jnp.int32, (blk, blk), 0) >= lax.broadcasted_iota(jnp.int32, (blk, blk), 1)).astype(BF16)
        carry = jnp.zeros((16, 1), F32)
        db = jnp.zeros((16, 1), F32)
        for b in range(nb - 1, -1, -1):
            cols = slice(b * blk, (b + 1) * blk)
            hi, mid, lo = _split3(-dcs_ref[:, cols])
            dlogf = _dot(hi, lower, NN) + _dot(mid, lower, NN) + _dot(lo, lower, NN) + carry
            carry = dlogf[:, 0:1]
            dfl = dlogf * sg_ref[:, cols]
            db = db + jnp.sum(dfl, axis=-1, keepdims=True)
            dfl_ref[:, cols] = dfl.astype(BF16)
        db_ref[...] = db
        dw_ref[...] = _dot(dfl_ref[...], h_ref[...], NN)

    return _pcall(name, body, (), [_full((16, T)), _full((16, T)), _full((T, D))],
                  [_full((16, T)), _full((16, D)), _full((16, 1))],
                  [S((16, T), BF16), S((16, D), F32), S((16, 1), F32)])(dcs, sg, h)


def _fox_logits(q, k, c_col, c_row):
    tq, kp = q.shape[0], k.shape[0]
    s = _dot(q, k, NT) * (1.0 / math.sqrt(HEAD_DIM)) + (c_col - c_row)
    row = lax.broadcasted_iota(jnp.int32, (tq, tq), 0)
    col = lax.broadcasted_iota(jnp.int32, (tq, tq), 1)
    diag = jnp.where(row >= col, s[:, kp - tq:], NEG_INF)
    return diag if kp == tq else jnp.concatenate([s[:, :kp - tq], diag], axis=1)


def _fox_specs(T, n_pairs):
    qs = pl.BlockSpec((T, LANES), lambda p: (0, p))
    ks = pl.BlockSpec((T, LANES), lambda p: (0, n_pairs + p))
    vs = pl.BlockSpec((T, LANES), lambda p: (0, 2 * n_pairs + p))
    col = pl.BlockSpec((2, T, 1), lambda p: (p, 0, 0))
    rowv = pl.BlockSpec((None, 2, T), lambda p: (p, 0, 0))
    return qs, ks, vs, col, rowv


def _fox_fwd(qkv, c_col, c_row, name):
    T = qkv.shape[0]
    DA = qkv.shape[1] // 3
    n_pairs = DA // LANES
    tq = _tile(T, 256)

    def body(q_ref, k_ref, v_ref, c_ref, ct_ref, o_ref, lse_ref):
        for hh in range(2):
            sl = slice(hh * HEAD_DIM, (hh + 1) * HEAD_DIM)
            for i in range(T // tq):
                rows = slice(i * tq, (i + 1) * tq)
                kp = (i + 1) * tq
                s = _fox_logits(q_ref[rows, sl], k_ref[0:kp, sl], c_ref[hh, rows, :], ct_ref[hh:hh + 1, 0:kp])
                m = jnp.max(s, axis=-1, keepdims=True)
                p = jnp.exp(s - m)
                l = jnp.sum(p, axis=-1, keepdims=True)
                o_ref[rows, sl] = _dot(p.astype(BF16), v_ref[0:kp, sl], NN) / l
                lse_ref[hh, rows, :] = m + jnp.log(l)

    qs, ks, vs, col, rowv = _fox_specs(T, n_pairs)
    return _pcall(name, body, (n_pairs,), [qs, ks, vs, col, rowv], [qs, col],
                  [S((T, DA), F32), S((2 * n_pairs, T, 1), F32)])(qkv, qkv, qkv, c_col, c_row)


def _fox_bwd(qkv, c_col, c_row, lse, do, name):
    T = qkv.shape[0]
    DA = qkv.shape[1] // 3
    n_pairs = DA // LANES
    tq = _tile(T, 256)
    scale = 1.0 / math.sqrt(HEAD_DIM)

    def body(q_ref, k_ref, v_ref, c_ref, ct_ref, lse_ref, do_ref, dq_ref, dk_ref, dv_ref, dcs_ref, dk_acc, dv_acc):
        dk_acc[...] = jnp.zeros_like(dk_acc)
        dv_acc[...] = jnp.zeros_like(dv_acc)
        dcs_ref[...] = jnp.zeros_like(dcs_ref)
        for hh in range(2):
            sl = slice(hh * HEAD_DIM, (hh + 1) * HEAD_DIM)
            for i in range(T // tq):
                rows = slice(i * tq, (i + 1) * tq)
                kp = (i + 1) * tq
                q = q_ref[rows, sl]
                k = k_ref[0:kp, sl]
                dob = do_ref[rows, sl]
                s = _fox_logits(q, k, c_ref[hh, rows, :], ct_ref[hh:hh + 1, 0:kp])
                p = jnp.exp(s - lse_ref[hh, rows, :])
                dp = _dot(dob, v_ref[0:kp, sl], NT)
                ds = p * (dp - jnp.sum(p * dp, axis=-1, keepdims=True))
                dsb = ds.astype(BF16)
                dq_ref[rows, sl] = (_dot(dsb, k, NN) * scale).astype(BF16)
                dk_acc[0:kp, sl] += _dot(dsb, q, TN) * scale
                dv_acc[0:kp, sl] += _dot(p.astype(BF16), dob, TN)
                dcs_ref[hh:hh + 1, 0:kp] += _colsum(ds)
        dk_ref[...] = dk_acc[...].astype(BF16)
        dv_ref[...] = dv_acc[...].astype(BF16)

    qs, ks, vs, col, rowv = _fox_specs(T, n_pairs)
    return _pcall(name, body, (n_pairs,), [qs, ks, vs, col, rowv, col, qs], [qs, qs, qs, rowv],
                  [S((T, DA), BF16)] * 3 + [S((n_pairs, 2, T), F32)],
                  scratch=[pltpu.VMEM((T, LANES), F32), pltpu.VMEM((T, LANES), F32)])(qkv, qkv, qkv, c_col, c_row, lse, do)


def _conv_fwd(ag, w, b, name):
    T = ag.shape[0]
    DC = ag.shape[1] // 2
    nb = DC // LANES
    tr = _tile(T, 256)

    def body(a_ref, g_ref, w_ref, b_ref, y_ref, pad):
        pad[0:CONV_PAD, :] = jnp.zeros((CONV_PAD, LANES), F32)
        pad[CONV_PAD:CONV_PAD + T, :] = a_ref[...] * _sigmoid(g_ref[...])
        for r in range(T // tr):
            acc = jnp.zeros((tr, LANES), F32) + b_ref[...]
            for j in range(CONV_WIDTH):
                o = r * tr + CONV_PAD - (CONV_WIDTH - 1) + j
                acc = acc + w_ref[j:j + 1, :] * pad[o:o + tr, :]
            y_ref[r * tr:(r + 1) * tr, :] = acc

    blk = pl.BlockSpec((T, LANES), lambda c: (0, c))
    return _pcall(name, body, (nb,), [blk, pl.BlockSpec((T, LANES), lambda c: (0, nb + c)),
                                      pl.BlockSpec((CONV_PAD, LANES), lambda c: (0, c)), pl.BlockSpec((1, LANES), lambda c: (0, c))],
                  blk, S((T, DC), F32), scratch=[pltpu.VMEM((T + CONV_PAD, LANES), F32)])(ag, ag, w, b)


def _conv_bwd(dy, ag, w, name):
    T = ag.shape[0]
    DC = ag.shape[1] // 2
    nb = DC // LANES
    tr = _tile(T, 256)

    def body(dy_ref, a_ref, g_ref, w_ref, da_ref, dg_ref, dw_ref, db_ref, pad, dpad):
        av = a_ref[...]
        sg = _sigmoid(g_ref[...])
        pad[0:CONV_PAD, :] = jnp.zeros((CONV_PAD, LANES), F32)
        pad[CONV_PAD:CONV_PAD + T, :] = av * sg
        dpad[0:T, :] = dy_ref[...]
        dpad[T:T + CONV_PAD, :] = jnp.zeros((CONV_PAD, LANES), F32)
        db_ref[...] = _colsum(dy_ref[...])
        dw_ref[...] = jnp.zeros_like(dw_ref)
        for j in range(CONV_WIDTH):
            acc = jnp.zeros((tr, LANES), F32)
            for r in range(T // tr):
                o = r * tr + CONV_PAD - (CONV_WIDTH - 1) + j
                acc = acc + dpad[r * tr:(r + 1) * tr, :] * pad[o:o + tr, :]
            dw_ref[j:j + 1, :] = _colsum(acc)
        for r in range(T // tr):
            acc = jnp.zeros((tr, LANES), F32)
            for j in range(CONV_WIDTH):
                o = r * tr + (CONV_WIDTH - 1) - j
                acc = acc + w_ref[j:j + 1, :] * dpad[o:o + tr, :]
            rows = slice(r * tr, (r + 1) * tr)
            sgr = sg[rows, :]
            da_ref[rows, :] = (acc * sgr).astype(BF16)
            dg_ref[rows, :] = (acc * av[rows, :] * sgr * (1.0 - sgr)).astype(BF16)

    blk = pl.BlockSpec((T, LANES), lambda c: (0, c))
    wblk = pl.BlockSpec((CONV_PAD, LANES), lambda c: (0, c))
    return _pcall(name, body, (nb,), [blk, blk, pl.BlockSpec((T, LANES), lambda c: (0, nb + c)), wblk],
                  [blk, blk, wblk, pl.BlockSpec((1, LANES), lambda c: (0, c))],
                  [S((T, DC), BF16), S((T, DC), BF16), S((CONV_PAD, DC), F32), S((1, DC), F32)],
                  scratch=[pltpu.VMEM((T + CONV_PAD, LANES), F32), pltpu.VMEM((T + CONV_PAD, LANES), F32)])(dy, ag, ag, w)


def _conv_norms(yc, lg, lb):
    mu = jnp.mean(yc, axis=-1, keepdims=True)
    xc = yc - mu
    rs = lax.rsqrt(jnp.mean(xc * xc, axis=-1, keepdims=True) + EPS)
    xh = xc * rs
    z = xh * lg + lb
    sg = _sigmoid(z)
    return rs, xh, z, sg, z * sg


def _mix_post(attn, yc, ag, cg, lg, lb, name):
    T, DA = attn.shape
    DC = yc.shape[1]
    tm = _tile(T, ROW_TILE)

    def body(at_ref, yc_ref, ag_ref, cg_ref, lg_ref, lb_ref, y_ref):
        at = at_ref[...]
        y_ref[:, 0:DA] = (at * _rstd(at) * ag_ref[...]).astype(BF16)
        _, _, _, _, sv = _conv_norms(yc_ref[...], lg_ref[...], lb_ref[...])
        y_ref[:, DA:DA + DC] = (sv * _rstd(sv) * cg_ref[...]).astype(BF16)

    return _pcall(name, body, (T // tm,),
                  [pl.BlockSpec((tm, DA), lambda i: (i, 0)), pl.BlockSpec((tm, DC), lambda i: (i, 0)), _full((1, DA)),
                   _full((1, DC)), _full((1, DC)), _full((1, DC))],
                  pl.BlockSpec((tm, DA + DC), lambda i: (i, 0)), S((T, DA + DC), BF16))(attn, yc, ag, cg, lg, lb)


def _mix_post_bwd(dy, attn, yc, ag, cg, lg, lb, name):
    T, DA = attn.shape
    DC = yc.shape[1]
    tm = _tile(T, ROW_TILE)

    def body(dy_ref, at_ref, yc_ref, ag_ref, cg_ref, lg_ref, lb_ref, dat_ref, dyc_ref, dag_ref, dcg_ref, dlg_ref, dlb_ref):
        dat, dag_rows = _rms_bwd(dy_ref[:, 0:DA], at_ref[...], ag_ref[...])
        dat_ref[...] = dat.astype(BF16)
        lgv = lg_ref[...]
        rs, xh, z, sg, sv = _conv_norms(yc_ref[...], lgv, lb_ref[...])
        dsv, dcg_rows = _rms_bwd(dy_ref[:, DA:DA + DC], sv, cg_ref[...])
        dz = dsv * (sg * (1.0 + z * (1.0 - sg)))
        dxh = dz * lgv
        dyc_ref[...] = rs * (dxh - jnp.mean(dxh, axis=-1, keepdims=True) - xh * jnp.mean(dxh * xh, axis=-1, keepdims=True))

        @pl.when(pl.program_id(0) == 0)
        def _():
            for r in (dag_ref, dcg_ref, dlg_ref, dlb_ref):
                r[...] = jnp.zeros_like(r)

        dag_ref[...] += _colsum(dag_rows)
        dcg_ref[...] += _colsum(dcg_rows)
        dlg_ref[...] += _colsum(dz * xh)
        dlb_ref[...] += _colsum(dz)

    ra = pl.BlockSpec((tm, DA), lambda i: (i, 0))
    rc = pl.BlockSpec((tm, DC), lambda i: (i, 0))
    return _pcall(name, body, (T // tm,),
                  [pl.BlockSpec((tm, DA + DC), lambda i: (i, 0)), ra, rc, _full((1, DA)), _full((1, DC)), _full((1, DC)),
                   _full((1, DC))],
                  [ra, rc, _full((1, DA)), _full((1, DC)), _full((1, DC)), _full((1, DC))],
                  [S((T, DA), BF16), S((T, DC), F32), S((1, DA), F32), S((1, DC), F32), S((1, DC), F32), S((1, DC), F32)])(
        dy, attn, yc, ag, cg, lg, lb)


def _xattn_probs(q, k, xd):
    s = _dot(q, k, NT) * (1.0 / math.sqrt(xd))
    p = jnp.exp(s - jnp.max(s, axis=-1, keepdims=True))
    return p / jnp.sum(p, axis=-1, keepdims=True)


def _xattn_fwd(q, kv, name):
    T, D = q.shape
    M = kv.shape[0]
    xd = D // N_XATTN_HEADS
    tq = _tile(T, ROW_TILE)

    def body(q_ref, kv_ref, o_ref):
        for h in range(N_XATTN_HEADS):
            sl = slice(h * xd, (h + 1) * xd)
            p = _xattn_probs(q_ref[:, sl], kv_ref[:, sl], xd)
            o_ref[:, sl] = _dot(p.astype(BF16), kv_ref[:, D + h * xd:D + (h + 1) * xd], NN).astype(BF16)

    row = pl.BlockSpec((tq, D), lambda i: (i, 0))
    return _pcall(name, body, (T // tq,), [row, _full((M, 2 * D))], row, S((T, D), BF16))(q, kv)


def _xattn_bwd(q, kv, do, name):
    T, D = q.shape
    M = kv.shape[0]
    xd = D // N_XATTN_HEADS
    tq = _tile(T, ROW_TILE)
    scale = 1.0 / math.sqrt(xd)

    def body(q_ref, kv_ref, do_ref, dq_ref, dkv_ref):
        @pl.when(pl.program_id(0) == 0)
        def _():
            dkv_ref[...] = jnp.zeros_like(dkv_ref)

        for h in range(N_XATTN_HEADS):
            sl = slice(h * xd, (h + 1) * xd)
            vsl = slice(D + h * xd, D + (h + 1) * xd)
            qh = q_ref[:, sl]
            kh = kv_ref[:, sl]
            doh = do_ref[:, sl]
            p = _xattn_probs(qh, kh, xd)
            dp = _dot(doh, kv_ref[:, vsl], NT)
            ds = (p * (dp - jnp.sum(p * dp, axis=-1, keepdims=True)) * scale).astype(BF16)
            dq_ref[:, sl] = _dot(ds, kh, NN).astype(BF16)
            dkv_ref[:, sl] += _dot(ds, qh, TN)
            dkv_ref[:, vsl] += _dot(p.astype(BF16), doh, TN)

    row = pl.BlockSpec((tq, D), lambda i: (i, 0))
    return _pcall(name, body, (T // tq,), [row, _full((M, 2 * D)), row], [row, _full((M, 2 * D))],
                  [S((T, D), BF16), S((M, 2 * D), F32)])(q, kv, do)


def _adamw(w, m, v, g, name):
    shape = w.shape
    C = shape[-1]
    R = w.size // C
    tr = R if R <= 512 else _tile(R, 512)

    def body(w_ref, m_ref, v_ref, g_ref, d_ref, nm_ref, nv_ref):
        gv = g_ref[...]
        mv = ADAM_B1 * m_ref[...] + (1.0 - ADAM_B1) * gv
        vv = ADAM_B2 * v_ref[...] + (1.0 - ADAM_B2) * (gv * gv)
        m_hat = mv / (1.0 - ADAM_B1 ** ADAM_STEP)
        v_hat = vv / (1.0 - ADAM_B2 ** ADAM_STEP)
        d_ref[...] = -ADAM_LR * (m_hat / (jnp.sqrt(v_hat) + ADAM_EPS) + ADAM_WD * w_ref[...])
        nm_ref[...] = mv
        nv_ref[...] = vv

    blk = pl.BlockSpec((tr, C), lambda i: (i, 0))
    outs = _pcall(name, body, (R // tr,), [blk] * 4, [blk] * 3, [S((R, C), F32)] * 3)(
        w.reshape(R, C), m.reshape(R, C), v.reshape(R, C), g.reshape(R, C))
    return [o.reshape(shape) for o in outs]


def _place_scalars():
    return jnp.stack([lax.axis_index("c"), 2 * lax.axis_index("x") + lax.axis_index("y")]).astype(jnp.int32)


def _adamw_sum(w, m, v, owns, landed, name):
    L, p, q = w.shape
    qq = owns[0].shape[2]
    tr = next((t for t in range(min(p, ROW_TILE) // 16 * 16, 0, -16) if p % t == 0), p)

    def body(place_ref, w_ref, m_ref, v_ref, *rest):
        own_refs, land_refs = rest[:L], rest[L:2 * L]
        g_ref, d_ref, nm_ref, nv_ref = rest[2 * L:]
        chip = place_ref[1]

        def update(l):
            own = own_refs[l][...].astype(F32)
            gs = None
            for k in range(4):
                term = jnp.where(chip == k, own, land_refs[l][k].astype(F32))
                gs = term if gs is None else gs + term
            gv = gs[:, 0:q]
            mv = ADAM_B1 * m_ref[...] + (1.0 - ADAM_B1) * gv
            vv = ADAM_B2 * v_ref[...] + (1.0 - ADAM_B2) * (gv * gv)
            m_hat = mv / (1.0 - ADAM_B1 ** ADAM_STEP)
            v_hat = vv / (1.0 - ADAM_B2 ** ADAM_STEP)
            g_ref[...] = gv
            d_ref[...] = -ADAM_LR * (m_hat / (jnp.sqrt(v_hat) + ADAM_EPS) + ADAM_WD * w_ref[...])
            nm_ref[...] = mv
            nv_ref[...] = vv

        for l in range(L):
            pl.when(pl.program_id(0) == l)(lambda l=l: update(l))

    def rows_of(layer):
        return lambda l, i, place: jnp.where(l == layer, i, 0)

    blk = pl.BlockSpec((None, tr, q), lambda l, i, place: (l, i, 0))
    in_specs = [blk, blk, blk]
    in_specs += [pl.BlockSpec((None, tr, qq), lambda l, i, place, r=rows_of(k): (0, r(l, i, place), 0)) for k in range(L)]
    in_specs += [pl.BlockSpec((4, None, tr, qq), lambda l, i, place, r=rows_of(k): (0, 0, r(l, i, place), 0)) for k in range(L)]
    gs = pltpu.PrefetchScalarGridSpec(num_scalar_prefetch=1, grid=(L, p // tr), in_specs=in_specs, out_specs=[blk] * 4)
    return pl.pallas_call(body, grid_spec=gs, out_shape=[S((L, p, q), F32)] * 4, name=name,
                          compiler_params=pltpu.CompilerParams(dimension_semantics=("arbitrary", "arbitrary")))(
        _place_scalars(), w, m, v, *owns, *landed)


def _pair_add(g, recv, axis, name):
    _, _, p, q = recv.shape

    def body(place_ref, g_ref, r_ref, o_ref, own_ref):
        s = (g_ref[...].astype(F32) + r_ref[...].astype(F32)).astype(BF16)
        o_ref[...] = s

        @pl.when(pl.program_id(0) == place_ref[1])
        def _():
            own_ref[...] = s

    if axis == 1:
        gspec = pl.BlockSpec((None, p, q), lambda k, place: (0, 2 * k + place[0], 0))
    else:
        gspec = pl.BlockSpec((None, p, q), lambda k, place: (0, 0, 2 * k + place[0]))
    part = pl.BlockSpec((None, None, p, q), lambda k, place: (k, 0, 0, 0))
    own = pl.BlockSpec((None, p, q), lambda k, place: (0, 0, 0))
    gs = pltpu.PrefetchScalarGridSpec(num_scalar_prefetch=1, grid=(4,), in_specs=[gspec, part], out_specs=[part, own])
    return pl.pallas_call(body, grid_spec=gs, out_shape=[S((4, 1, p, q), BF16), S((1, p, q), BF16)], name=name,
                          compiler_params=pltpu.CompilerParams(dimension_semantics=("arbitrary",)))(_place_scalars(), g, recv)


def _win_pieces(n_attn, n_heads, n_conv, shard, chunk):
    bounds = [0, 3 * n_attn, 3 * n_attn + n_heads, 3 * n_attn + n_heads + 2 * n_conv]
    pieces = []
    for j in range(N_DEV):
        lo, hi = shard * j, shard * (j + 1)
        for r in range(3):
            a, b = max(lo, bounds[r]), min(hi, bounds[r + 1])
            if a < b:
                pieces.append((r, a - bounds[r], b - bounds[r], chunk * j + a - lo))
    return pieces


def _win_split(w_in, pieces, widths, name):
    L, D, C = w_in.shape
    tr = _tile(D, 256)

    def body(x_ref, *outs):
        outs[1][...] = jnp.zeros_like(outs[1])
        for r, d0, d1, s0 in pieces:
            outs[r][:, d0:d1] = x_ref[:, s0:s0 + d1 - d0]

    return _pcall(name, body, (L, D // tr), [pl.BlockSpec((None, tr, C), lambda l, i: (l, i, 0))],
                  [pl.BlockSpec((None, tr, wd), lambda l, i: (l, i, 0)) for wd in widths],
                  [S((L, D, wd), BF16) for wd in widths])(w_in)


def _win_merge(parts, pieces, chunked_cols, name):
    L, D, _ = parts[0].shape
    tr = _tile(D, 256)

    def body(a_ref, b_ref, c_ref, o_ref):
        ins = (a_ref, b_ref, c_ref)
        o_ref[...] = jnp.zeros_like(o_ref)
        for r, d0, d1, s0 in pieces:
            o_ref[:, s0:s0 + d1 - d0] = ins[r][:, d0:d1]

    return _pcall(name, body, (L, D // tr), [pl.BlockSpec((None, tr, x.shape[2]), lambda l, i: (l, i, 0)) for x in parts],
                  pl.BlockSpec((None, tr, chunked_cols), lambda l, i: (l, i, 0)), S((L, D, chunked_cols), BF16))(*parts)


def _place():
    return lax.axis_index("x"), lax.axis_index("y"), lax.axis_index("c")


def _flip(v, f):
    return 1 - v if f else v


def _window(ref, axis, size, dev):
    start = dev * size if isinstance(dev, int) else pl.multiple_of(dev * size, LANES if axis == 2 else 16)
    return ref.at[:, pl.ds(start, size), :] if axis == 1 else ref.at[:, :, pl.ds(start, size)]


HBM_SPEC = pl.BlockSpec(memory_space=pltpu.HBM)
SEM_SPEC = pl.BlockSpec(memory_space=pltpu.SEMAPHORE)
SPLIT_COPY_PARAMS = dict(has_side_effects=pltpu.SideEffectType.DATAFLOW_SIDE_EFFECTING)


def _hbm(v):
    return pltpu.with_memory_space_constraint(v, pltpu.HBM)


def _full_shape(shard, axis):
    return tuple(N_DEV * d if i == axis else d for i, d in enumerate(shard.shape))


def _ag_peers(x, y, c):
    return [(x, y, 1 - c), (1 - x, y, c), (x, 1 - y, c), (1 - x, 1 - y, c)]


SIBLING_COLLECTIVE_ID = 0


def _sibling_handshake(x, y, c):
    barrier = pltpu.get_barrier_semaphore()
    pl.semaphore_signal(barrier, inc=1, device_id=(x, y, 1 - c), device_id_type=MESH)
    pl.semaphore_wait(barrier, 1)


def _ag_start(shards, axes, groups, after, name):
    n, ng = len(shards), len(groups)
    sizes = [s.shape[ax] for s, ax in zip(shards, axes)]
    where = {w: (g, i) for g, members in enumerate(groups) for i, w in enumerate(members)}

    def body(*refs):
        xs, fulls = refs[:n], refs[n + 1:2 * n + 1]
        send, recv = refs[3 * n + 1:3 * n + 1 + ng], refs[3 * n + 1 + ng:]
        x, y, c = _place()
        for members in groups:
            for w in members:
                g, i = where[w]
                for k, to in enumerate(_ag_peers(x, y, c)):
                    pltpu.make_async_remote_copy(
                        src_ref=xs[w], dst_ref=_window(fulls[w], axes[w], sizes[w], 4 * x + 2 * y + c),
                        send_sem=send[g].at[4 * i + k], recv_sem=recv[g].at[4 * i + k], device_id=to, device_id_type=MESH).start()

    sems = [pltpu.SemaphoreType.DMA((4 * len(m),)) for m in groups]
    outs = pl.pallas_call(
        body, name=name,
        out_shape=[pltpu.HBM(_full_shape(s, ax), s.dtype) for s, ax in zip(shards, axes)] + [pltpu.HBM(s.shape, s.dtype) for s in shards]
        + sems + sems,
        in_specs=[HBM_SPEC] * n + [pl.BlockSpec(memory_space=pl.ANY)], out_specs=[HBM_SPEC] * (2 * n) + [SEM_SPEC] * (2 * ng),
        input_output_aliases={w: n + w for w in range(n)},
        compiler_params=pltpu.CompilerParams(**SPLIT_COPY_PARAMS))(*[_hbm(s) for s in shards], after)
    return outs[:n], outs[n:2 * n], outs[2 * n:2 * n + ng], outs[2 * n + ng:]


def _ag_wait(shards, fulls, send_sems, recv_sems, axes, after, name):
    n = len(shards)
    sizes = [s.shape[ax] for s, ax in zip(shards, axes)]

    def body(*refs):
        xs = refs[:n]
        send, recv = refs[2 * n], refs[2 * n + 1]
        landed = refs[3 * n + 3:]
        x, y, c = _place()
        for w in range(n):
            for k, frm in enumerate(_ag_peers(x, y, c)):
                copy = pltpu.make_async_remote_copy(
                    src_ref=xs[w], dst_ref=_window(landed[w], axes[w], sizes[w], 4 * frm[0] + 2 * frm[1] + frm[2]),
                    send_sem=send.at[4 * w + k], recv_sem=recv.at[4 * w + k], device_id=frm, device_id_type=MESH)
                copy.wait_send()
                copy.wait_recv()

    outs = pl.pallas_call(
        body, name=name, out_shape=[pltpu.HBM(v.shape, v.dtype) for v in list(shards) + list(fulls)],
        in_specs=[HBM_SPEC] * (2 * n) + [SEM_SPEC, SEM_SPEC, pl.BlockSpec(memory_space=pl.ANY)], out_specs=[HBM_SPEC] * (2 * n),
        input_output_aliases={i: i for i in range(2 * n)},
        compiler_params=pltpu.CompilerParams(**SPLIT_COPY_PARAMS))(*shards, *fulls, send_sems, recv_sems, after)
    return outs[:n], outs[n:]


def _ag_forward(fulls, shards, axes, name):
    n = len(fulls)
    sizes = [s.shape[ax] for s, ax in zip(shards, axes)]

    def body(*refs):
        xs, full_refs = refs[:n], refs[2 * n:3 * n]
        send_sems, recv_sems, local_sems = refs[3 * n:3 * n + 3]
        staged = refs[3 * n + 3:]
        x, y, c = _place()
        _sibling_handshake(x, y, c)
        chips = [(1 - x, y), (x, 1 - y), (1 - x, 1 - y)]
        loads = [pltpu.make_async_copy(xs[w], staged[w], local_sems.at[w]) for w in range(n)]
        for cp in loads:
            cp.start()
        copies = []
        for w in range(n):
            for j, (px, py) in enumerate(chips):
                sent = _window(full_refs[w], axes[w], sizes[w], 4 * px + 2 * py + c)
                got = _window(full_refs[w], axes[w], sizes[w], 4 * px + 2 * py + 1 - c)
                out = pltpu.make_async_remote_copy(src_ref=sent, dst_ref=sent, send_sem=send_sems.at[3 * w + j],
                                                   recv_sem=recv_sems.at[3 * w + j], device_id=(x, y, 1 - c), device_id_type=MESH)
                out.start()
                back = pltpu.make_async_remote_copy(src_ref=got, dst_ref=got, send_sem=send_sems.at[3 * w + j],
                                                    recv_sem=recv_sems.at[3 * w + j], device_id=(x, y, 1 - c), device_id_type=MESH)
                copies.append((out, back))
        stores = []
        for w in range(n):
            loads[w].wait()
            store = pltpu.make_async_copy(staged[w], _window(full_refs[w], axes[w], sizes[w], 4 * x + 2 * y + c), local_sems.at[w])
            store.start()
            stores.append(store)
        for out, back in copies:
            out.wait_send()
            back.wait_recv()
        for cp in stores:
            cp.wait()

    any_spec = pl.BlockSpec(memory_space=pl.ANY)
    outs = pl.pallas_call(
        body, name=name, out_shape=[S(f.shape, f.dtype) for f in fulls], in_specs=[any_spec] * (2 * n), out_specs=[any_spec] * n,
        input_output_aliases={n + w: w for w in range(n)},
        scratch_shapes=[pltpu.SemaphoreType.DMA((3 * n,)), pltpu.SemaphoreType.DMA((3 * n,)), pltpu.SemaphoreType.DMA((n,))]
        + [pltpu.VMEM(s.shape, s.dtype) for s in shards],
        compiler_params=pltpu.CompilerParams(collective_id=SIBLING_COLLECTIVE_ID))(*shards, *fulls)
    return outs


def _sibling_copy(g_ref, land_ref, send_sems, recv_sems, axis, size, w, k, x, y, c):
    return pltpu.make_async_remote_copy(
        src_ref=_window(g_ref, axis, size, 2 * k + 1 - c), dst_ref=land_ref.at[k], send_sem=send_sems.at[4 * w + k],
        recv_sem=recv_sems.at[4 * w + k], device_id=(x, y, 1 - c), device_id_type=MESH)


def _to_sibling_start(grads, axes, sizes, after, name):
    n = len(grads)
    landing = []
    for g, ax, sz in zip(grads, axes, sizes):
        L, K, N = g.shape
        landing.append(pltpu.HBM((4, L, sz, N) if ax == 1 else (4, L, K, sz), g.dtype))
    extra = [] if after is None else [after]

    def body(*refs):
        g_refs = refs[:n]
        land_refs = refs[n + len(extra):2 * n + len(extra)]
        send_sems, recv_sems, token = refs[3 * n + len(extra):]
        x, y, c = _place()
        _sibling_handshake(x, y, c)
        for w in range(n):
            for k in range(4):
                _sibling_copy(g_refs[w], land_refs[w], send_sems, recv_sems, axes[w], sizes[w], w, k, x, y, c).start()
        token[...] = jnp.zeros_like(token)

    sems = pltpu.SemaphoreType.DMA((4 * n,))
    outs = pl.pallas_call(
        body, name=name, out_shape=landing + [pltpu.HBM(g.shape, g.dtype) for g in grads] + [sems, sems, S((8, LANES), F32)],
        in_specs=[HBM_SPEC] * n + [pl.BlockSpec(memory_space=pl.ANY)] * len(extra),
        out_specs=[HBM_SPEC] * (2 * n) + [SEM_SPEC, SEM_SPEC, pl.BlockSpec(memory_space=pltpu.VMEM)],
        input_output_aliases={w: n + w for w in range(n)},
        compiler_params=pltpu.CompilerParams(collective_id=SIBLING_COLLECTIVE_ID, **SPLIT_COPY_PARAMS))(
        *[_hbm(g) for g in grads], *extra)
    return outs[:n], outs[n:2 * n], outs[2 * n], outs[2 * n + 1], outs[2 * n + 2]


def _to_sibling_wait(grads, landing, send_sems, recv_sems, axes, sizes, after, name):
    n = len(grads)

    def body(*refs):
        g_refs = refs[:n]
        send, recv = refs[2 * n], refs[2 * n + 1]
        landed = refs[3 * n + 3:]
        x, y, c = _place()
        for w in range(n):
            for k in range(4):
                copy = _sibling_copy(g_refs[w], landed[w], send, recv, axes[w], sizes[w], w, k, x, y, c)
                copy.wait_send()
                copy.wait_recv()

    outs = pl.pallas_call(
        body, name=name, out_shape=[pltpu.HBM(v.shape, v.dtype) for v in list(grads) + list(landing)],
        in_specs=[HBM_SPEC] * (2 * n) + [SEM_SPEC, SEM_SPEC, pl.BlockSpec(memory_space=pl.ANY)], out_specs=[HBM_SPEC] * (2 * n),
        input_output_aliases={i: i for i in range(2 * n)},
        compiler_params=pltpu.CompilerParams(**SPLIT_COPY_PARAMS))(*grads, *landing, send_sems, recv_sems, after)
    return outs[:n], outs[n:]


def _rs_copy(p_ref, out_ref, send_sems, recv_sems, w, rel, x, y, c):
    tx, ty = _flip(x, rel & 2), _flip(y, rel & 1)
    return pltpu.make_async_remote_copy(
        src_ref=p_ref.at[2 * tx + ty], dst_ref=out_ref.at[2 * x + y], send_sem=send_sems.at[3 * w + rel - 1],
        recv_sem=recv_sems.at[3 * w + rel - 1], device_id=(tx, ty, c), device_id_type=MESH)


def _rs_start(parts, name):
    n = len(parts)

    def body(*refs):
        p_refs, out_refs = refs[:n], refs[n:2 * n]
        send_sems, recv_sems, token = refs[3 * n:]
        x, y, c = _place()
        for w in range(n):
            for rel in (1, 2, 3):
                _rs_copy(p_refs[w], out_refs[w], send_sems, recv_sems, w, rel, x, y, c).start()
        token[...] = jnp.zeros_like(token)

    sems = pltpu.SemaphoreType.DMA((3 * n,))
    outs = pl.pallas_call(
        body, name=name,
        out_shape=[pltpu.HBM(p.shape, p.dtype) for p in parts] * 2 + [sems, sems, S((8, LANES), F32)],
        in_specs=[HBM_SPEC] * n, out_specs=[HBM_SPEC] * (2 * n) + [SEM_SPEC, SEM_SPEC, pl.BlockSpec(memory_space=pltpu.VMEM)],
        input_output_aliases={w: n + w for w in range(n)},
        compiler_params=pltpu.CompilerParams(**SPLIT_COPY_PARAMS))(*[_hbm(p) for p in parts])
    return outs[:n], outs[n:2 * n], outs[2 * n], outs[2 * n + 1], outs[2 * n + 2]


def _rs_wait(parts, landing, send_sems, recv_sems, after, name):
    n = len(parts)

    def body(*refs):
        p_refs = refs[:n]
        send, recv = refs[2 * n], refs[2 * n + 1]
        landed = refs[3 * n + 3:]
        x, y, c = _place()
        for w in range(n):
            for rel in (1, 2, 3):
                copy = _rs_copy(p_refs[w], landed[w], send, recv, w, rel, x, y, c)
                copy.wait_send()
                copy.wait_recv()

    outs = pl.pallas_call(
        body, name=name, out_shape=[pltpu.HBM(v.shape, v.dtype) for v in list(parts) + list(landing)],
        in_specs=[HBM_SPEC] * (2 * n) + [SEM_SPEC, SEM_SPEC, pl.BlockSpec(memory_space=pl.ANY)], out_specs=[HBM_SPEC] * (2 * n),
        input_output_aliases={i: i for i in range(2 * n)},
        compiler_params=pltpu.CompilerParams(**SPLIT_COPY_PARAMS))(*parts, *landing, send_sems, recv_sems, after)
    return outs[n:]


def _exchange_small(v, reduce, name, after=None):
    R, C = v.shape

    def body(v_ref, *rest):
        out_ref, gath, send_sems, recv_sems = rest[-4:]
        x, y, c = _place()
        me = 4 * x + 2 * y + c
        buf = gath if reduce else out_ref
        buf[me] = v_ref[...]
        copies = []
        for rel in range(1, N_DEV):
            peer = (_flip(x, rel & 4), _flip(y, rel & 2), _flip(c, rel & 1))
            copies.append(pltpu.make_async_remote_copy(
                src_ref=v_ref, dst_ref=buf.at[me], send_sem=send_sems.at[rel - 1], recv_sem=recv_sems.at[rel - 1],
                device_id=peer, device_id_type=MESH))
        for cp in copies:
            cp.start()
        for cp in copies:
            cp.wait()
        if reduce:
            acc = gath[0]
            for d in range(1, N_DEV):
                acc = acc + gath[d]
            out_ref[...] = acc

    vm = pl.BlockSpec(memory_space=pltpu.VMEM)
    extra = [] if after is None else [after]
    return pl.pallas_call(
        body, out_shape=S((R, C) if reduce else (N_DEV, R, C), F32), in_specs=[vm] + [pl.BlockSpec(memory_space=pl.ANY)] * len(extra),
        out_specs=vm, name=name,
        scratch_shapes=[pltpu.VMEM((N_DEV, R, C) if reduce else (8, LANES), F32), pltpu.SemaphoreType.DMA((N_DEV - 1,)),
                        pltpu.SemaphoreType.DMA((N_DEV - 1,))])(v, *extra)


def _pad_rows(flat, cols, mult):
    n = flat.shape[-1]
    rows = -(-n // cols)
    rows = -(-rows // mult) * mult
    pad = [(0, 0)] * (flat.ndim - 1) + [(0, rows * cols - n)]
    return jnp.pad(flat, pad).reshape(flat.shape[:-1] + (rows, cols))


def _round_up(n, m):
    return -(-n // m) * m


def _shard_axes(a):
    return [(2, _round_up(a[n].shape[2], LANES)) if kind == 'col' else (1, _round_up(a[n].shape[1 if kind == 'row' else 2], LANES))
            for n, kind in BIG]


def _pack_small(vals):
    rows = [_pad_rows(vals[n].astype(F32).reshape(-1), SMALL_COLS, 1) for n in SMALL]
    m = jnp.concatenate(rows, axis=0)
    return jnp.pad(m, ((0, -m.shape[0] % 8), (0, 0)))


def _unpack_small(m, a):
    out, r = {}, 0
    for n in SMALL:
        nr = -(-a[n].size // SMALL_COLS)
        out[n] = m[r:r + nr].reshape(-1)[:a[n].size].reshape(a[n].shape)
        r += nr
    return out, r


GROUPS = (('ffn1_w_gate', 'ffn1_w_up', 'ffn1_w_down'), ('w_in', 'w_out', 'xattn_w_q', 'xattn_w_kv', 'xattn_w_o'),
          ('ffn2_w_gate', 'ffn2_w_up', 'ffn2_w_down'))


def _layer_small(a, conv_w_full, l):
    H = a['b_f'].shape[1]
    return dict(
        bft=jnp.pad(a['b_f'][l].reshape(H, 1), ((0, 16 - H), (0, 0))),
        cw=jnp.pad(conv_w_full[l], ((0, CONV_PAD - CONV_WIDTH), (0, 0))), cb=a['conv_b'][l].reshape(1, -1),
        lg=a['conv_ln_g'][l].reshape(1, -1), lb=a['conv_ln_b'][l].reshape(1, -1),
        ag=a['attn_out_g'][l].reshape(1, -1), cg=a['conv_out_g'][l].reshape(1, -1),
        g1=a['ffn1_norm_g'][l], gm=a['mix_norm_g'][l], gx=a['xattn_norm_g'][l], gmem=a['mem_norm_g'][l], g2=a['ffn2_norm_g'][l])


def _layer_fwd(x0, mem, w, fetch, cfg, l):
    T = x0.shape[0]
    H = cfg['heads']
    sv = {'x0': x0}
    m = fetch(l, 0, x0)
    w.update(wg1=(m['ffn1_w_gate'], 0), wu1=(m['ffn1_w_up'], 0), wd1=(m['ffn1_w_down'], 0))
    sv['h1'] = _rms_fwd(x0, w['g1'], f"l{l}_ffn1_norm")
    sv['G1'], sv['U1'], sv['A1'] = _ffn_up(sv['h1'], w['wg1'], w['wu1'], f"l{l}_ffn1_up")
    x1 = sv['x1'] = _mm_res(sv['A1'], w['wd1'], x0, 0.5, f"l{l}_ffn1_down")
    m = fetch(l, 1, x1)
    wqkv, wf, wag = _win_split(m['w_in'], cfg['pieces'], cfg['widths'], f"l{l}_w_in_split")
    w.update(wqkv=(wqkv, 0), wft=wf[0, :, :16].T, wag=(wag, 0), wout=(m['w_out'], 0), wq=(m['xattn_w_q'], 0),
             wkv=(m['xattn_w_kv'], 0), wo=(m['xattn_w_o'], 0))
    h2 = sv['h2'] = _rms_fwd(x1, w['gm'], f"l{l}_mix_norm")
    sv['qkv'] = _mm(h2, w['wqkv'], BF16, f"l{l}_qkv_proj")
    sv['agv'] = _mm(h2, w['wag'], F32, f"l{l}_glu_proj")
    ct, sv['sg'] = _fox_prep(h2, w['wft'], w['bft'], f"l{l}_fox_prep")
    sv['c_col'] = ct[:H].reshape(H, T, 1)
    sv['c_row'] = ct[:H].reshape(H // 2, 2, T)
    sv['attn'], sv['lse'] = _fox_fwd(sv['qkv'], sv['c_col'], sv['c_row'], f"l{l}_fox_fwd")
    sv['yc'] = _conv_fwd(sv['agv'], w['cw'], w['cb'], f"l{l}_conv_fwd")
    sv['ycat'] = _mix_post(sv['attn'], sv['yc'], w['ag'], w['cg'], w['lg'], w['lb'], f"l{l}_mix_post")
    x2 = sv['x2'] = _mm_res(sv['ycat'], w['wout'], x1, 1.0, f"l{l}_out_proj")
    sv['h3'] = _rms_fwd(x2, w['gx'], f"l{l}_xattn_norm")
    sv['memn'] = _rms_fwd(mem, w['gmem'], f"l{l}_mem_norm")
    sv['q'] = _mm(sv['h3'], w['wq'], BF16, f"l{l}_xattn_q")
    sv['kv'] = _mm_nt(sv['memn'], w['wkv'], BF16, f"l{l}_xattn_kv")
    sv['o'] = _xattn_fwd(sv['q'], sv['kv'], f"l{l}_xattn_fwd")
    x3 = sv['x3'] = _mm_res(sv['o'], w['wo'], x2, 1.0, f"l{l}_xattn_out")
    m = fetch(l, 2, x3)
    w.update(wg2=(m['ffn2_w_gate'], 0), wu2=(m['ffn2_w_up'], 0), wd2=(m['ffn2_w_down'], 0))
    sv['h4'] = _rms_fwd(x3, w['g2'], f"l{l}_ffn2_norm")
    sv['G2'], sv['U2'], sv['A2'] = _ffn_up(sv['h4'], w['wg2'], w['wu2'], f"l{l}_ffn2_up")
    return _mm_res(sv['A2'], w['wd2'], x3, 0.5, f"l{l}_ffn2_down"), sv


def _ffn_bwd(dout, x_in, h, G, U, A, wg, wu, wd, g, tag, put, which, dep, flush):
    dG, dU = _ffn_bwd_act(dout, wd, G, U, 0.5, tag + "_bwd_act", dep)
    put(which + '_w_down', A, dout, 0.5, tag + "_dwd")
    put(which + '_w_gate', dG, h, 1.0, tag + "_dwg")
    put(which + '_w_up', dU, h, 1.0, tag + "_dwu")
    dep = flush()
    dx, dg = _bwd_h([(dG, wg, 'nn'), (dU, wu, 'nn')], x_in, g, dout, tag + "_bwd_h", dep)
    return dx, dg, dep


def _layer_bwd(dx4, mem, w, sv, reduce, cfg, l, dep):
    small, grads = {}, {}
    T = dx4.shape[0]
    H = cfg['heads']

    def put(key, act, dy, scale, name):
        grads[key] = _wgrad(act, dy, scale, name, (None, 0, 1))

    dx3, small['ffn2_norm_g'], dep = _ffn_bwd(
        dx4, sv['x3'], sv['h4'], sv['G2'], sv['U2'], sv['A2'], w['wg2'], w['wu2'], w['wd2'], w['g2'], f"l{l}_ffn2", put, 'ffn2',
        dep, lambda: reduce(l, 2, {n: grads.pop(n) for n in GROUPS[2]}))
    do = _mm_nt(dx3, w['wo'], BF16, f"l{l}_xattn_do", dep)
    put('xattn_w_o', sv['o'], dx3, 1.0, f"l{l}_dwo")
    dq, dkv = _xattn_bwd(sv['q'], sv['kv'], do, f"l{l}_xattn_bwd")
    put('xattn_w_q', sv['h3'], dq, 1.0, f"l{l}_dwq")
    dx2, small['xattn_norm_g'] = _bwd_h([(dq, w['wq'], 'nt')], sv['x2'], w['gx'], dx3, f"l{l}_xattn_bwd_h")
    dmemn = _mm(dkv, w['wkv'], F32, f"l{l}_dmemn")
    put('xattn_w_kv', dkv, sv['memn'], 1.0, f"l{l}_dwkv")
    small['mem_norm_g'] = _rms_gain_grad(dmemn, mem, w['gmem'], f"l{l}_dgmem")
    dycat = _mm_nt(dx2, w['wout'], F32, f"l{l}_dycat")
    put('w_out', sv['ycat'], dx2, 1.0, f"l{l}_dwout")
    dattn, dyc, small['attn_out_g'], small['conv_out_g'], small['conv_ln_g'], small['conv_ln_b'] = _mix_post_bwd(
        dycat, sv['attn'], sv['yc'], w['ag'], w['cg'], w['lg'], w['lb'], f"l{l}_mix_post_bwd")
    dva, dga, dcw, small['conv_b'] = _conv_bwd(dyc, sv['agv'], w['cw'], f"l{l}_conv_bwd")
    dq_, dk_, dv_, dcs = _fox_bwd(sv['qkv'], sv['c_col'], sv['c_row'], sv['lse'], dattn, f"l{l}_fox_bwd")
    dcs16 = jnp.pad(dcs.reshape(H, T), ((0, 16 - H), (0, 0)))
    dflt, dwft, dbf = _fox_prep_bwd(dcs16, sv['sg'], sv['h2'], f"l{l}_fox_prep_bwd")
    small['b_f'] = dbf[:H].reshape(H)
    dqkv = jnp.concatenate([dq_, dk_, dv_], axis=1)
    dag = jnp.concatenate([dva, dga], axis=1)
    put('wqkv', sv['h2'], dqkv, 1.0, f"l{l}_dwqkv")
    put('wag', sv['h2'], dag, 1.0, f"l{l}_dwag")
    dx1, small['mix_norm_g'] = _bwd_h([(dqkv, w['wqkv'], 'nt'), (dag, w['wag'], 'nt'), (dflt, w['wft'], 'tn')],
                                      sv['x1'], w['gm'], dx2, f"l{l}_mix_bwd_h")
    dwf = jnp.pad(dwft[:H].T, ((0, 0), (0, LANES - H)))[None].astype(BF16)
    grads['w_in'] = _win_merge((grads.pop('wqkv'), dwf, grads.pop('wag')), cfg['pieces'], cfg['chunked_cols'], f"l{l}_w_in_merge")
    dep = reduce(l, 1, {n: grads.pop(n) for n in GROUPS[1]})
    dx0, small['ffn1_norm_g'], dep = _ffn_bwd(
        dx1, sv['x0'], sv['h1'], sv['G1'], sv['U1'], sv['A1'], w['wg1'], w['wu1'], w['wd1'], w['g1'], f"l{l}_ffn1", put, 'ffn1',
        dep, lambda: reduce(l, 0, {n: grads.pop(n) for n in GROUPS[0]}))
    small = {k: v.reshape(-1) for k, v in small.items()}
    return dx0, small, dcw[:CONV_WIDTH], dep


def _local_step(x, mem, tgt, a, conv_w_full, fetch, reduce, cfg):
    L = a['b_f'].shape[0]
    ws = [_layer_small(a, conv_w_full, l) for l in range(L)]
    saved = []
    for l in range(L):
        x, sv = _layer_fwd(x, mem, ws[l], fetch, cfg, l)
        saved.append(sv)
    loss, dx, dgf = _loss_head(x, a['final_norm_g'], tgt, "loss_head")
    smalls, dcws, dep = [None] * L, [None] * L, None
    for l in range(L - 1, -1, -1):
        dx, smalls[l], dcws[l], dep = _layer_bwd(dx, mem, ws[l], saved[l], reduce, cfg, l, dep)
    small = {n: jnp.stack([smalls[l][n] for l in range(L)]) for n in SMALL if n != 'final_norm_g'}
    small['final_norm_g'] = dgf.reshape(-1)
    return loss, dx, small, jnp.stack(dcws)


def kernel(x, mem, ffn1_norm_g, ffn1_w_gate, ffn1_w_up, ffn1_w_down, mix_norm_g, w_in, b_f, conv_w, conv_b, conv_ln_g, conv_ln_b, attn_out_g, conv_out_g, w_out, xattn_norm_g, mem_norm_g, xattn_w_q, xattn_w_kv, xattn_w_o, ffn2_norm_g, ffn2_w_gate, ffn2_w_up, ffn2_w_down, final_norm_g, loss_target, m_ffn1_norm_g, m_ffn1_w_gate, m_ffn1_w_up, m_ffn1_w_down, m_mix_norm_g, m_w_in, m_b_f, m_conv_w, m_conv_b, m_conv_ln_g, m_conv_ln_b, m_attn_out_g, m_conv_out_g, m_w_out, m_xattn_norm_g, m_mem_norm_g, m_xattn_w_q, m_xattn_w_kv, m_xattn_w_o, m_ffn2_norm_g, m_ffn2_w_gate, m_ffn2_w_up, m_ffn2_w_down, m_final_norm_g, v_ffn1_norm_g, v_ffn1_w_gate, v_ffn1_w_up, v_ffn1_w_down, v_mix_norm_g, v_w_in, v_b_f, v_conv_w, v_conv_b, v_conv_ln_g, v_conv_ln_b, v_attn_out_g, v_conv_out_g, v_w_out, v_xattn_norm_g, v_mem_norm_g, v_xattn_w_q, v_xattn_w_kv, v_xattn_w_o, v_ffn2_norm_g, v_ffn2_w_gate, v_ffn2_w_up, v_ffn2_w_down, v_final_norm_g):
    args = (x, mem, ffn1_norm_g, ffn1_w_gate, ffn1_w_up, ffn1_w_down, mix_norm_g, w_in, b_f, conv_w, conv_b, conv_ln_g, conv_ln_b, attn_out_g, conv_out_g, w_out, xattn_norm_g, mem_norm_g, xattn_w_q, xattn_w_kv, xattn_w_o, ffn2_norm_g, ffn2_w_gate, ffn2_w_up, ffn2_w_down, final_norm_g)
    moments_m = (m_ffn1_norm_g, m_ffn1_w_gate, m_ffn1_w_up, m_ffn1_w_down, m_mix_norm_g, m_w_in, m_b_f, m_conv_w, m_conv_b, m_conv_ln_g, m_conv_ln_b, m_attn_out_g, m_conv_out_g, m_w_out, m_xattn_norm_g, m_mem_norm_g, m_xattn_w_q, m_xattn_w_kv, m_xattn_w_o, m_ffn2_norm_g, m_ffn2_w_gate, m_ffn2_w_up, m_ffn2_w_down, m_final_norm_g)
    moments_v = (v_ffn1_norm_g, v_ffn1_w_gate, v_ffn1_w_up, v_ffn1_w_down, v_mix_norm_g, v_w_in, v_b_f, v_conv_w, v_conv_b, v_conv_ln_g, v_conv_ln_b, v_attn_out_g, v_conv_out_g, v_w_out, v_xattn_norm_g, v_mem_norm_g, v_xattn_w_q, v_xattn_w_kv, v_xattn_w_o, v_ffn2_norm_g, v_ffn2_w_gate, v_ffn2_w_up, v_ffn2_w_down, v_final_norm_g)
    a = dict(zip(NAMES, args))
    am = dict(zip(WEIGHTS, moments_m))
    av = dict(zip(WEIGHTS, moments_v))
    L, taps, cshard = conv_w.shape
    dev = 4 * lax.axis_index("x") + 2 * lax.axis_index("y") + lax.axis_index("c")

    big_names = [n for n, _ in BIG]
    geometry = dict(zip(big_names, _shard_axes(a)))
    n_attn, n_heads, n_conv = attn_out_g.shape[1], b_f.shape[1], conv_out_g.shape[1]
    chunk = geometry['w_in'][1]
    cfg = dict(heads=n_heads, pieces=_win_pieces(n_attn, n_heads, n_conv, w_in.shape[2], chunk),
               widths=(3 * n_attn, LANES, 2 * n_conv), chunked_cols=N_DEV * chunk)

    cw_rows = _pad_rows(conv_w.reshape(-1), LANES, 8)
    cw_all = _exchange_small(cw_rows, False, "allgather_conv_w")
    conv_w_full = cw_all.reshape(N_DEV, -1)[:, :conv_w.size].reshape(N_DEV, L, taps, cshard).transpose(1, 2, 0, 3).reshape(
        L, taps, N_DEV * cshard)

    keys = [(l, n) for l in range(L) for names in GROUPS for n in names]
    members = [[keys.index((l, n)) for n in names] for l in range(L) for names in GROUPS]
    shards = []
    for l, n in keys:
        ax, size = geometry[n]
        shard = _as_handled(n, a[n][l:l + 1]).astype(BF16)
        pad = [(0, 0)] * 3
        pad[ax] = (0, size - shard.shape[ax])
        shards.append(jnp.pad(shard, pad))
    key_axes = [geometry[n][0] for _, n in keys]
    fulls, thru, ag_send, ag_recv = _ag_start(shards, key_axes, members, cw_all, "allgather_start")

    def fetch(l, gi, after):
        g = l * len(GROUPS) + gi
        axs = [key_axes[i] for i in members[g]]
        own, landed = _ag_wait([thru[i] for i in members[g]], [fulls[i] for i in members[g]], ag_send[g], ag_recv[g], axs, after,
                               f"allgather_wait_l{l}g{gi}")
        return dict(zip(GROUPS[gi], _ag_forward(landed, own, axs, f"allgather_forward_l{l}g{gi}")))

    pending, own_part, landed_part, in_flight = [], {}, {}, []

    def finish_exchange(after):
        if not in_flight:
            return None
        l, gi, g_thru, landing, send, recv_sems = in_flight.pop()
        names = GROUPS[gi]
        axs = [geometry[n][0] for n in names]
        g_done, recv = _to_sibling_wait(g_thru, landing, send, recv_sems, axs, [geometry[n][1] for n in names], after,
                                       f"reduce_sibling_wait_l{l}g{gi}")
        parts = []
        for n, g, r, ax in zip(names, g_done, recv, axs):
            part, own_part[(l, n)] = _pair_add(g, r, ax, f"reduce_pair_add_l{l}_{n}")
            parts.append(part)
        landing, parts_thru, send, recv_sems, token = _rs_start(parts, f"reduce_start_l{l}g{gi}")
        pending.append((l, gi, parts_thru, landing, send, recv_sems))
        return token

    def reduce(l, gi, grads):
        names = GROUPS[gi]
        gl = [grads[n] for n in names]
        token = finish_exchange(gl[0])
        landing, g_thru, send, recv_sems, token = _to_sibling_start(
            gl, [geometry[n][0] for n in names], [geometry[n][1] for n in names], token, f"reduce_sibling_start_l{l}g{gi}")
        in_flight.append((l, gi, g_thru, landing, send, recv_sems))
        return token

    loss, grad_x, gsmall, dcw = _local_step(x[0], mem[0], loss_target[0], a, conv_w_full, fetch, reduce, cfg)
    finish_exchange(grad_x)

    def wait_group(entry, after):
        l, gi, parts_thru, landing, send, recv_sems = entry
        for n, arr in zip(GROUPS[gi], _rs_wait(parts_thru, landing, send, recv_sems, after, f"reduce_wait_l{l}g{gi}")):
            landed_part[(l, n)] = arr

    for entry in pending[:-1]:
        wait_group(entry, grad_x)

    grads, delta, new_m, new_v = {}, {}, {}, {}

    def update(n):
        outs = _adamw_sum(_as_handled(n, a[n]), _as_handled(n, am[n]), _as_handled(n, av[n]),
                          [own_part[(l, n)] for l in range(L)], [landed_part[(l, n)] for l in range(L)], "adamw_" + n)
        grads[n], delta[n], new_m[n], new_v[n] = (_as_handled(n, o) for o in outs)

    last_names = GROUPS[pending[-1][1]]
    early = [n for n in big_names if n not in last_names]
    for n in early:
        update(n)
    wait_group(pending[-1], delta[early[-1]])
    for n in last_names:
        update(n)

    small_rows = _pack_small(gsmall)
    n_small = small_rows.shape[0]
    dcw_rows = jnp.pad(dcw, ((0, 0), (0, CONV_PAD - taps), (0, 0))).reshape(-1, SMALL_COLS)
    summed = _exchange_small(jnp.concatenate([small_rows, dcw_rows], axis=0), True, "allreduce_small",
                             after=landed_part[(pending[-1][0], last_names[0])])
    g_small, _ = _unpack_small(summed[:n_small], a)
    dcw_sum = summed[n_small:].reshape(L, CONV_PAD, N_DEV * cshard)[:, :taps]
    grads.update(g_small)
    grads['conv_w'] = lax.dynamic_slice_in_dim(dcw_sum, dev * cshard, cshard, axis=2)
    delta['conv_w'], new_m['conv_w'], new_v['conv_w'] = _adamw(conv_w, am['conv_w'], av['conv_w'], grads['conv_w'], "adamw_conv_w")
    pw, pm, pv, pg = (_pack_small(d) for d in (a, am, av, g_small))
    for dst, packed in zip((delta, new_m, new_v), _adamw(pw, pm, pv, pg, "adamw_small")):
        dst.update(_unpack_small(packed, a)[0])

    total = lax.psum(loss.reshape(()), ("x", "y", "c"))
    return (total, grad_x[None], *[grads[n] for n in WEIGHTS], *[delta[n] for n in WEIGHTS], *[new_m[n] for n in WEIGHTS],
            *[new_v[n] for n in WEIGHTS])
```

```python
import math

import jax
import jax.numpy as jnp
from jax import lax
from jax.experimental import pallas as pl
from jax.experimental.pallas import tpu as pltpu

F32, BF16 = jnp.float32, jnp.bfloat16
S = jax.ShapeDtypeStruct
MESH = pl.DeviceIdType.MESH

EPS = 1e-6
NEG_INF = -1e30
HEAD_DIM = 64
N_XATTN_HEADS = 4
CONV_WIDTH = 31
CONV_PAD = 32
LANES = 128
ADAM_LR, ADAM_B1, ADAM_B2, ADAM_EPS, ADAM_WD, ADAM_STEP = 0.001, 0.9, 0.999, 1e-08, 0.01, 10
N_DEV = 8
VMEM_LIMIT_BYTES = 56 * 1024 * 1024
ROW_TILE = 512
MM_ROW_TILE = 1024
EPILOGUE_COLS = 512
SMALL_COLS = 512

NN = ((1,), (0,))
NT = ((1,), (1,))
TN = ((0,), (0,))

NAMES = ['x', 'mem', 'ffn1_norm_g', 'ffn1_w_gate', 'ffn1_w_up', 'ffn1_w_down', 'mix_norm_g', 'w_in', 'b_f', 'conv_w', 'conv_b',
         'conv_ln_g', 'conv_ln_b', 'attn_out_g', 'conv_out_g', 'w_out', 'xattn_norm_g', 'mem_norm_g', 'xattn_w_q', 'xattn_w_kv',
         'xattn_w_o', 'ffn2_norm_g', 'ffn2_w_gate', 'ffn2_w_up', 'ffn2_w_down', 'final_norm_g']
WEIGHTS = NAMES[2:]
BIG = [('ffn1_w_gate', 'colT'), ('ffn1_w_up', 'colT'), ('ffn1_w_down', 'row'), ('w_in', 'col'), ('w_out', 'row'),
       ('xattn_w_q', 'row'), ('xattn_w_kv', 'colT'), ('xattn_w_o', 'row'), ('ffn2_w_gate', 'colT'), ('ffn2_w_up', 'colT'),
       ('ffn2_w_down', 'row')]
TRANSPOSED = tuple(n for n, kind in BIG if kind == 'colT')


def _as_handled(n, v):
    return jnp.swapaxes(v, 1, 2) if n in TRANSPOSED else v
SMALL = ['ffn1_norm_g', 'mix_norm_g', 'xattn_norm_g', 'mem_norm_g', 'ffn2_norm_g', 'conv_b', 'conv_ln_g', 'conv_ln_b',
         'attn_out_g', 'conv_out_g', 'b_f', 'final_norm_g']


def _dot(a, b, dims):
    return lax.dot_general(a, b, (dims, ((), ())), preferred_element_type=F32)


def _full(shape):
    nd = len(shape)
    return pl.BlockSpec(shape, lambda *_: (0,) * nd)


def _tile(n, pref):
    for t in (pref, 512, 384, 256, 128, 64, 32, 16, 8):
        if t <= n and n % t == 0:
            return t
    return n


def _pcall(name, body, grid, in_specs, out_specs, out_shape, scratch=(), aliases=None, dep=None):
    n_in = len(in_specs)
    kernel_body = body
    if dep is not None:
        in_specs = list(in_specs) + [pl.BlockSpec(memory_space=pl.ANY)]

        def kernel_body(*refs):
            return body(*refs[:n_in], *refs[n_in + 1:])

    call = pl.pallas_call(
        kernel_body, grid=grid, in_specs=in_specs, out_specs=out_specs, out_shape=out_shape, scratch_shapes=list(scratch),
        name=name, input_output_aliases=aliases or {},
        compiler_params=pltpu.CompilerParams(dimension_semantics=("arbitrary",) * len(grid), vmem_limit_bytes=VMEM_LIMIT_BYTES))
    return call if dep is None else (lambda *args: call(*args, dep))


def _arr(w):
    return w[0] if isinstance(w, tuple) else w


def _wshape(w):
    return w[0].shape[1:] if isinstance(w, tuple) else w.shape


def _wspec(w, block, imap):
    if isinstance(w, tuple):
        layer = w[1]
        return pl.BlockSpec((None,) + block, lambda *g: (layer,) + imap(*g))
    return pl.BlockSpec(block, imap)


def _wfull(w):
    shape = _wshape(w)
    return _wspec(w, shape, lambda *_: (0,) * len(shape))


def _sigmoid(z):
    return jax.nn.sigmoid(z)


def _rstd(x):
    return lax.rsqrt(jnp.mean(x * x, axis=-1, keepdims=True) + EPS)


def _rms_bwd(dy, x, g):
    r = _rstd(x)
    xh = x * r
    u = dy * g
    dx = r * (u - xh * jnp.mean(u * xh, axis=-1, keepdims=True))
    return dx, dy * xh


def _colsum(v):
    return jnp.sum(v, axis=0, keepdims=True)


def _rms_fwd(x, g, name):
    T, D = x.shape
    tm = _tile(T, ROW_TILE)

    def body(x_ref, g_ref, h_ref):
        xv = x_ref[...]
        h_ref[...] = (xv * _rstd(xv) * g_ref[...]).astype(BF16)

    row = pl.BlockSpec((tm, D), lambda i: (i, 0))
    return _pcall(name, body, (T // tm,), [row, _full((1, D))], row, S((T, D), BF16))(x, g.reshape(1, D))


def _mm(a, w, out_dtype, name):
    M, K = a.shape
    N = _wshape(w)[1]
    tm = _tile(M, ROW_TILE)
    tn = N if N <= 1536 else N // 2

    def body(a_ref, w_ref, o_ref):
        o_ref[...] = _dot(a_ref[...].astype(BF16), w_ref[...], NN).astype(out_dtype)

    return _pcall(name, body, (N // tn, M // tm),
                  [pl.BlockSpec((tm, K), lambda j, i: (i, 0)), _wspec(w, (K, tn), lambda j, i: (0, j))],
                  pl.BlockSpec((tm, tn), lambda j, i: (i, j)), S((M, N), out_dtype))(a, _arr(w))


def _mm_res(a, w, res, scale, name):
    M, K = a.shape
    N = _wshape(w)[1]
    tm = _tile(M, ROW_TILE)

    def body(a_ref, w_ref, r_ref, o_ref):
        o_ref[...] = r_ref[...] + scale * _dot(a_ref[...], w_ref[...], NN)

    row = pl.BlockSpec((tm, N), lambda i: (i, 0))
    return _pcall(name, body, (M // tm,), [pl.BlockSpec((tm, K), lambda i: (i, 0)), _wfull(w), row], row,
                  S((M, N), F32))(a, _arr(w), res)


def _mm_nt(a, w, out_dtype, name, dep=None):
    M, K = a.shape
    N = _wshape(w)[0]
    tm = _tile(M, ROW_TILE)
    tn = N if N <= 1536 else N // 2

    def body(a_ref, w_ref, o_ref):
        o_ref[...] = _dot(a_ref[...].astype(BF16), w_ref[...], NT).astype(out_dtype)

    return _pcall(name, body, (N // tn, M // tm),
                  [pl.BlockSpec((tm, K), lambda j, i: (i, 0)), _wspec(w, (tn, K), lambda j, i: (j, 0))],
                  pl.BlockSpec((tm, tn), lambda j, i: (i, j)), S((M, N), out_dtype), dep=dep)(a, _arr(w))


def _wgrad(a, dy, scale, name, into):
    buf, layer, L = into
    T, M = a.shape
    N = dy.shape[1]
    tm = _tile(M, ROW_TILE)

    def body(a_ref, dy_ref, *rest):
        rest[-1][...] = (scale * _dot(a_ref[...].astype(BF16), dy_ref[...].astype(BF16), TN)).astype(BF16)

    in_specs = [pl.BlockSpec((T, tm), lambda i: (0, i)), _full((T, N))]
    args = [a, dy]
    if buf is not None:
        in_specs.append(pl.BlockSpec(memory_space=pl.ANY))
        args.append(buf)
    return _pcall(name, body, (M // tm,), in_specs, pl.BlockSpec((None, tm, N), lambda i: (layer, i, 0)), S((L, M, N), BF16),
                  aliases={2: 0} if buf is not None else None)(*args)


def _bwd_h(dots, x, g, dres, name, dep=None):
    T, D = x.shape
    tm = _tile(T, 256)
    n = len(dots)
    dims = [{'nt': NT, 'nn': NN, 'tn': TN}[m] for _, _, m in dots]

    def body(*refs):
        x_ref, g_ref, r_ref, dx_ref, dg_ref = refs[2 * n:]
        dh = None
        for k in range(n):
            part = _dot(refs[2 * k][...], refs[2 * k + 1][...], dims[k])
            dh = part if dh is None else dh + part
        dx, dgrow = _rms_bwd(dh, x_ref[...], g_ref[...])
        dx_ref[...] = r_ref[...] + dx

        @pl.when(pl.program_id(0) == 0)
        def _():
            dg_ref[...] = jnp.zeros_like(dg_ref)

        dg_ref[...] += _colsum(dgrow)

    in_specs, args = [], []
    for lhs, w, mode in dots:
        if mode == 'tn':
            in_specs.append(pl.BlockSpec((lhs.shape[0], tm), lambda i: (0, i)))
        else:
            in_specs.append(pl.BlockSpec((tm, lhs.shape[1]), lambda i: (i, 0)))
        in_specs.append(_wfull(w))
        args += [lhs, _arr(w)]
    row = pl.BlockSpec((tm, D), lambda i: (i, 0))
    in_specs += [row, _full((1, D)), row]
    return _pcall(name, body, (T // tm,), in_specs, [row, _full((1, D))], [S((T, D), F32), S((1, D), F32)], dep=dep)(
        *args, x, g.reshape(1, D), dres)


def _rms_gain_grad(dy, x, g, name):
    T, D = x.shape

    def body(dy_ref, x_ref, g_ref, dg_ref):
        _, dgrow = _rms_bwd(dy_ref[...], x_ref[...], g_ref[...])
        dg_ref[...] = _colsum(dgrow)

    return _pcall(name, body, (), [_full((T, D)), _full((T, D)), _full((1, D))], _full((1, D)), S((1, D), F32))(
        dy, x, g.reshape(1, D))


def _ffn_up(h, wg, wu, name):
    T, D = h.shape
    Fh = _wshape(wg)[0]
    tm = _tile(T, MM_ROW_TILE)
    tn = Fh if Fh <= 1536 else Fh // 2
    tc = _tile(tn, EPILOGUE_COLS)

    def body(h_ref, wg_ref, wu_ref, g_ref, u_ref, a_ref):
        hv = h_ref[...]
        for cb in range(tn // tc):
            cols = slice(cb * tc, (cb + 1) * tc)
            gv = _dot(hv, wg_ref[cols, :], NT)
            uv = _dot(hv, wu_ref[cols, :], NT)
            g_ref[:, cols] = gv.astype(BF16)
            u_ref[:, cols] = uv.astype(BF16)
            a_ref[:, cols] = (gv * _sigmoid(gv) * uv).astype(BF16)

    tile = pl.BlockSpec((tm, tn), lambda j, i: (i, j))
    return _pcall(name, body, (Fh // tn, T // tm),
                  [pl.BlockSpec((tm, D), lambda j, i: (i, 0)), _wspec(wg, (tn, D), lambda j, i: (j, 0)),
                   _wspec(wu, (tn, D), lambda j, i: (j, 0))],
                  [tile, tile, tile], [S((T, Fh), BF16)] * 3)(h, _arr(wg), _arr(wu))


def _ffn_bwd_act(dout, wd, gate, up, scale, name, dep=None):
    T, D = dout.shape
    Fh = _wshape(wd)[0]
    tm = _tile(T, MM_ROW_TILE)
    tn = Fh if Fh <= 1536 else Fh // 2
    tc = _tile(tn, EPILOGUE_COLS)

    def body(d_ref, w_ref, g_ref, u_ref, dg_ref, du_ref):
        dv = d_ref[...].astype(BF16)
        for cb in range(tn // tc):
            cols = slice(cb * tc, (cb + 1) * tc)
            da = scale * _dot(dv, w_ref[cols, :], NT)
            gv = g_ref[:, cols].astype(F32)
            uv = u_ref[:, cols].astype(F32)
            sg = _sigmoid(gv)
            dg_ref[:, cols] = (da * uv * (sg * (1.0 + gv * (1.0 - sg)))).astype(BF16)
            du_ref[:, cols] = (da * (gv * sg)).astype(BF16)

    tile = pl.BlockSpec((tm, tn), lambda j, i: (i, j))
    return _pcall(name, body, (Fh // tn, T // tm),
                  [pl.BlockSpec((tm, D), lambda j, i: (i, 0)), _wspec(wd, (tn, D), lambda j, i: (j, 0)), tile, tile],
                  [tile, tile], [S((T, Fh), BF16)] * 2, dep=dep)(dout, _arr(wd), gate, up)


def _loss_head(x, g, tgt, name):
    T, D = x.shape
    tm = _tile(T, ROW_TILE)

    def body(x_ref, g_ref, t_ref, loss_ref, dx_ref, dg_ref):
        xv = x_ref[...]
        gv = g_ref[...]
        r = _rstd(xv)
        xh = xv * r
        e = xh * gv - t_ref[...]
        dy = e * (1.0 / D)
        u = dy * gv
        dx_ref[...] = r * (u - xh * jnp.mean(u * xh, axis=-1, keepdims=True))

        @pl.when(pl.program_id(0) == 0)
        def _():
            dg_ref[...] = jnp.zeros_like(dg_ref)
            loss_ref[...] = jnp.zeros_like(loss_ref)

        dg_ref[...] += _colsum(dy * xh)
        loss_ref[...] += 0.5 * _colsum(jnp.mean(e * e, axis=-1, keepdims=True))

    row = pl.BlockSpec((tm, D), lambda i: (i, 0))
    return _pcall(name, body, (T // tm,), [row, _full((1, D)), row], [_full((1, 1)), row, _full((1, D))],
                  [S((1, 1), F32), S((T, D), F32), S((1, D), F32)])(x, g.reshape(1, D), tgt)


def _split3(xb):
    hi = xb.astype(BF16)
    r1 = xb - hi.astype(F32)
    mid = r1.astype(BF16)
    lo = (r1 - mid.astype(F32)).astype(BF16)
    return hi, mid, lo


def _fox_prep(h, wft, bft, name):
    T, D = h.shape
    blk = _tile(T, 256)

    def body(h_ref, w_ref, b_ref, ct_ref, sg_ref):
        z = _dot(w_ref[...], h_ref[...], NT) + b_ref[...]
        sg_ref[...] = 1.0 - _sigmoid(z)
        logf = jnp.minimum(z, 0.0) - jnp.log1p(jnp.exp(-jnp.abs(z)))
        upper = (lax.broadcasted_iota(jnp.int32, (blk, blk), 0) <= lax.broadcasted_iota(jnp.int32, (blk, blk), 1)).astype(BF16)
        carry = jnp.zeros((16, 1), F32)
        for b in range(T // blk):
            hi, mid, lo = _split3(logf[:, b * blk:(b + 1) * blk])
            cb = _dot(hi, upper, NN) + _dot(mid, upper, NN) + _dot(lo, upper, NN) + carry
            ct_ref[:, b * blk:(b + 1) * blk] = cb
            carry = cb[:, blk - 1:blk]

    return _pcall(name, body, (), [_full((T, D)), _full((16, D)), _full((16, 1))], [_full((16, T)), _full((16, T))],
                  [S((16, T), F32), S((16, T), F32)])(h, wft, bft)


def _fox_prep_bwd(dcs, sg, h, name):
    T, D = h.shape
    blk = _tile(T, 256)
    nb = T // blk

    def body(dcs_ref, sg_ref, h_ref, dfl_ref, dw_ref, db_ref):
        lower = (lax.broadcasted_iota(jnp.int32, (blk, blk), 0) >= lax.broadcasted_iota(jnp.int32, (blk, blk), 1)).astype(BF16)
        carry = jnp.zeros((16, 1), F32)
        db = jnp.zeros((16, 1), F32)
        for b in range(nb - 1, -1, -1):
            cols = slice(b * blk, (b + 1) * blk)
            hi, mid, lo = _split3(-dcs_ref[:, cols])
            dlogf = _dot(hi, lower, NN) + _dot(mid, lower, NN) + _dot(lo, lower, NN) + carry
            carry = dlogf[:, 0:1]
            dfl = dlogf * sg_ref[:, cols]
            db = db + jnp.sum(dfl, axis=-1, keepdims=True)
            dfl_ref[:, cols] = dfl.astype(BF16)
        db_ref[...] = db
        dw_ref[...] = _dot(dfl_ref[...], h_ref[...], NN)

    return _pcall(name, body, (), [_full((16, T)), _full((16, T)), _full((T, D))],
                  [_full((16, T)), _full((16, D)), _full((16, 1))],
                  [S((16, T), BF16), S((16, D), F32), S((16, 1), F32)])(dcs, sg, h)


def _fox_logits(q, k, c_col, c_row):
    tq, kp = q.shape[0], k.shape[0]
    s = _dot(q, k, NT) * (1.0 / math.sqrt(HEAD_DIM)) + (c_col - c_row)
    row = lax.broadcasted_iota(jnp.int32, (tq, tq), 0)
    col = lax.broadcasted_iota(jnp.int32, (tq, tq), 1)
    diag = jnp.where(row >= col, s[:, kp - tq:], NEG_INF)
    return diag if kp == tq else jnp.concatenate([s[:, :kp - tq], diag], axis=1)


def _fox_specs(T, n_pairs):
    qs = pl.BlockSpec((T, LANES), lambda p: (0, p))
    ks = pl.BlockSpec((T, LANES), lambda p: (0, n_pairs + p))
    vs = pl.BlockSpec((T, LANES), lambda p: (0, 2 * n_pairs + p))
    col = pl.BlockSpec((2, T, 1), lambda p: (p, 0, 0))
    rowv = pl.BlockSpec((None, 2, T), lambda p: (p, 0, 0))
    return qs, ks, vs, col, rowv


def _fox_fwd(qkv, c_col, c_row, name):
    T = qkv.shape[0]
    DA = qkv.shape[1] // 3
    n_pairs = DA // LANES
    tq = _tile(T, 256)

    def body(q_ref, k_ref, v_ref, c_ref, ct_ref, o_ref, lse_ref):
        for hh in range(2):
            sl = slice(hh * HEAD_DIM, (hh + 1) * HEAD_DIM)
            for i in range(T // tq):
                rows = slice(i * tq, (i + 1) * tq)
                kp = (i + 1) * tq
                s = _fox_logits(q_ref[rows, sl], k_ref[0:kp, sl], c_ref[hh, rows, :], ct_ref[hh:hh + 1, 0:kp])
                m = jnp.max(s, axis=-1, keepdims=True)
                p = jnp.exp(s - m)
                l = jnp.sum(p, axis=-1, keepdims=True)
                o_ref[rows, sl] = _dot(p.astype(BF16), v_ref[0:kp, sl], NN) / l
                lse_ref[hh, rows, :] = m + jnp.log(l)

    qs, ks, vs, col, rowv = _fox_specs(T, n_pairs)
    return _pcall(name, body, (n_pairs,), [qs, ks, vs, col, rowv], [qs, col],
                  [S((T, DA), F32), S((2 * n_pairs, T, 1), F32)])(qkv, qkv, qkv, c_col, c_row)


def _fox_bwd(qkv, c_col, c_row, lse, do, name):
    T = qkv.shape[0]
    DA = qkv.shape[1] // 3
    n_pairs = DA // LANES
    tq = _tile(T, 256)
    scale = 1.0 / math.sqrt(HEAD_DIM)

    def body(q_ref, k_ref, v_ref, c_ref, ct_ref, lse_ref, do_ref, dq_ref, dk_ref, dv_ref, dcs_ref, dk_acc, dv_acc):
        dk_acc[...] = jnp.zeros_like(dk_acc)
        dv_acc[...] = jnp.zeros_like(dv_acc)
        dcs_ref[...] = jnp.zeros_like(dcs_ref)
        for hh in range(2):
            sl = slice(hh * HEAD_DIM, (hh + 1) * HEAD_DIM)
            for i in range(T // tq):
                rows = slice(i * tq, (i + 1) * tq)
                kp = (i + 1) * tq
                q = q_ref[rows, sl]
                k = k_ref[0:kp, sl]
                dob = do_ref[rows, sl]
                s = _fox_logits(q, k, c_ref[hh, rows, :], ct_ref[hh:hh + 1, 0:kp])
                p = jnp.exp(s - lse_ref[hh, rows, :])
                dp = _dot(dob, v_ref[0:kp, sl], NT)
                ds = p * (dp - jnp.sum(p * dp, axis=-1, keepdims=True))
                dsb = ds.astype(BF16)
                dq_ref[rows, sl] = (_dot(dsb, k, NN) * scale).astype(BF16)
                dk_acc[0:kp, sl] += _dot(dsb, q, TN) * scale
                dv_acc[0:kp, sl] += _dot(p.astype(BF16), dob, TN)
                dcs_ref[hh:hh + 1, 0:kp] += _colsum(ds)
        dk_ref[...] = dk_acc[...].astype(BF16)
        dv_ref[...] = dv_acc[...].astype(BF16)

    qs, ks, vs, col, rowv = _fox_specs(T, n_pairs)
    return _pcall(name, body, (n_pairs,), [qs, ks, vs, col, rowv, col, qs], [qs, qs, qs, rowv],
                  [S((T, DA), BF16)] * 3 + [S((n_pairs, 2, T), F32)],
                  scratch=[pltpu.VMEM((T, LANES), F32), pltpu.VMEM((T, LANES), F32)])(qkv, qkv, qkv, c_col, c_row, lse, do)


def _conv_fwd(ag, w, b, name):
    T = ag.shape[0]
    DC = ag.shape[1] // 2
    nb = DC // LANES
    tr = _tile(T, 256)

    def body(a_ref, g_ref, w_ref, b_ref, y_ref, pad):
        pad[0:CONV_PAD, :] = jnp.zeros((CONV_PAD, LANES), F32)
        pad[CONV_PAD:CONV_PAD + T, :] = a_ref[...] * _sigmoid(g_ref[...])
        for r in range(T // tr):
            acc = jnp.zeros((tr, LANES), F32) + b_ref[...]
            for j in range(CONV_WIDTH):
                o = r * tr + CONV_PAD - (CONV_WIDTH - 1) + j
                acc = acc + w_ref[j:j + 1, :] * pad[o:o + tr, :]
            y_ref[r * tr:(r + 1) * tr, :] = acc

    blk = pl.BlockSpec((T, LANES), lambda c: (0, c))
    return _pcall(name, body, (nb,), [blk, pl.BlockSpec((T, LANES), lambda c: (0, nb + c)),
                                      pl.BlockSpec((CONV_PAD, LANES), lambda c: (0, c)), pl.BlockSpec((1, LANES), lambda c: (0, c))],
                  blk, S((T, DC), F32), scratch=[pltpu.VMEM((T + CONV_PAD, LANES), F32)])(ag, ag, w, b)


def _conv_bwd(dy, ag, w, name):
    T = ag.shape[0]
    DC = ag.shape[1] // 2
    nb = DC // LANES
    tr = _tile(T, 256)

    def body(dy_ref, a_ref, g_ref, w_ref, da_ref, dg_ref, dw_ref, db_ref, pad, dpad):
        av = a_ref[...]
        sg = _sigmoid(g_ref[...])
        pad[0:CONV_PAD, :] = jnp.zeros((CONV_PAD, LANES), F32)
        pad[CONV_PAD:CONV_PAD + T, :] = av * sg
        dpad[0:T, :] = dy_ref[...]
        dpad[T:T + CONV_PAD, :] = jnp.zeros((CONV_PAD, LANES), F32)
        db_ref[...] = _colsum(dy_ref[...])
        dw_ref[...] = jnp.zeros_like(dw_ref)
        for j in range(CONV_WIDTH):
            acc = jnp.zeros((tr, LANES), F32)
            for r in range(T // tr):
                o = r * tr + CONV_PAD - (CONV_WIDTH - 1) + j
                acc = acc + dpad[r * tr:(r + 1) * tr, :] * pad[o:o + tr, :]
            dw_ref[j:j + 1, :] = _colsum(acc)
        for r in range(T // tr):
            acc = jnp.zeros((tr, LANES), F32)
            for j in range(CONV_WIDTH):
                o = r * tr + (CONV_WIDTH - 1) - j
                acc = acc + w_ref[j:j + 1, :] * dpad[o:o + tr, :]
            rows = slice(r * tr, (r + 1) * tr)
            sgr = sg[rows, :]
            da_ref[rows, :] = (acc * sgr).astype(BF16)
            dg_ref[rows, :] = (acc * av[rows, :] * sgr * (1.0 - sgr)).astype(BF16)

    blk = pl.BlockSpec((T, LANES), lambda c: (0, c))
    wblk = pl.BlockSpec((CONV_PAD, LANES), lambda c: (0, c))
    return _pcall(name, body, (nb,), [blk, blk, pl.BlockSpec((T, LANES), lambda c: (0, nb + c)), wblk],
                  [blk, blk, wblk, pl.BlockSpec((1, LANES), lambda c: (0, c))],
                  [S((T, DC), BF16), S((T, DC), BF16), S((CONV_PAD, DC), F32), S((1, DC), F32)],
                  scratch=[pltpu.VMEM((T + CONV_PAD, LANES), F32), pltpu.VMEM((T + CONV_PAD, LANES), F32)])(dy, ag, ag, w)


def _conv_norms(yc, lg, lb):
    mu = jnp.mean(yc, axis=-1, keepdims=True)
    xc = yc - mu
    rs = lax.rsqrt(jnp.mean(xc * xc, axis=-1, keepdims=True) + EPS)
    xh = xc * rs
    z = xh * lg + lb
    sg = _sigmoid(z)
    return rs, xh, z, sg, z * sg


def _mix_post(attn, yc, ag, cg, lg, lb, name):
    T, DA = attn.shape
    DC = yc.shape[1]
    tm = _tile(T, ROW_TILE)

    def body(at_ref, yc_ref, ag_ref, cg_ref, lg_ref, lb_ref, y_ref):
        at = at_ref[...]
        y_ref[:, 0:DA] = (at * _rstd(at) * ag_ref[...]).astype(BF16)
        _, _, _, _, sv = _conv_norms(yc_ref[...], lg_ref[...], lb_ref[...])
        y_ref[:, DA:DA + DC] = (sv * _rstd(sv) * cg_ref[...]).astype(BF16)

    return _pcall(name, body, (T // tm,),
                  [pl.BlockSpec((tm, DA), lambda i: (i, 0)), pl.BlockSpec((tm, DC), lambda i: (i, 0)), _full((1, DA)),
                   _full((1, DC)), _full((1, DC)), _full((1, DC))],
                  pl.BlockSpec((tm, DA + DC), lambda i: (i, 0)), S((T, DA + DC), BF16))(attn, yc, ag, cg, lg, lb)


def _mix_post_bwd(dy, attn, yc, ag, cg, lg, lb, name):
    T, DA = attn.shape
    DC = yc.shape[1]
    tm = _tile(T, ROW_TILE)

    def body(dy_ref, at_ref, yc_ref, ag_ref, cg_ref, lg_ref, lb_ref, dat_ref, dyc_ref, dag_ref, dcg_ref, dlg_ref, dlb_ref):
        dat, dag_rows = _rms_bwd(dy_ref[:, 0:DA], at_ref[...], ag_ref[...])
        dat_ref[...] = dat.astype(BF16)
        lgv = lg_ref[...]
        rs, xh, z, sg, sv = _conv_norms(yc_ref[...], lgv, lb_ref[...])
        dsv, dcg_rows = _rms_bwd(dy_ref[:, DA:DA + DC], sv, cg_ref[...])
        dz = dsv * (sg * (1.0 + z * (1.0 - sg)))
        dxh = dz * lgv
        dyc_ref[...] = rs * (dxh - jnp.mean(dxh, axis=-1, keepdims=True) - xh * jnp.mean(dxh * xh, axis=-1, keepdims=True))

        @pl.when(pl.program_id(0) == 0)
        def _():
            for r in (dag_ref, dcg_ref, dlg_ref, dlb_ref):
                r[...] = jnp.zeros_like(r)

        dag_ref[...] += _colsum(dag_rows)
        dcg_ref[...] += _colsum(dcg_rows)
        dlg_ref[...] += _colsum(dz * xh)
        dlb_ref[...] += _colsum(dz)

    ra = pl.BlockSpec((tm, DA), lambda i: (i, 0))
    rc = pl.BlockSpec((tm, DC), lambda i: (i, 0))
    return _pcall(name, body, (T // tm,),
                  [pl.BlockSpec((tm, DA + DC), lambda i: (i, 0)), ra, rc, _full((1, DA)), _full((1, DC)), _full((1, DC)),
                   _full((1, DC))],
                  [ra, rc, _full((1, DA)), _full((1, DC)), _full((1, DC)), _full((1, DC))],
                  [S((T, DA), BF16), S((T, DC), F32), S((1, DA), F32), S((1, DC), F32), S((1, DC), F32), S((1, DC), F32)])(
        dy, attn, yc, ag, cg, lg, lb)


def _xattn_probs(q, k, xd):
    s = _dot(q, k, NT) * (1.0 / math.sqrt(xd))
    p = jnp.exp(s - jnp.max(s, axis=-1, keepdims=True))
    return p / jnp.sum(p, axis=-1, keepdims=True)


def _xattn_fwd(q, kv, name):
    T, D = q.shape
    M = kv.shape[0]
    xd = D // N_XATTN_HEADS
    tq = _tile(T, ROW_TILE)

    def body(q_ref, kv_ref, o_ref):
        for h in range(N_XATTN_HEADS):
            sl = slice(h * xd, (h + 1) * xd)
            p = _xattn_probs(q_ref[:, sl], kv_ref[:, sl], xd)
            o_ref[:, sl] = _dot(p.astype(BF16), kv_ref[:, D + h * xd:D + (h + 1) * xd], NN).astype(BF16)

    row = pl.BlockSpec((tq, D), lambda i: (i, 0))
    return _pcall(name, body, (T // tq,), [row, _full((M, 2 * D))], row, S((T, D), BF16))(q, kv)


def _xattn_bwd(q, kv, do, name):
    T, D = q.shape
    M = kv.shape[0]
    xd = D // N_XATTN_HEADS
    tq = _tile(T, ROW_TILE)
    scale = 1.0 / math.sqrt(xd)

    def body(q_ref, kv_ref, do_ref, dq_ref, dkv_ref):
        @pl.when(pl.program_id(0) == 0)
        def _():
            dkv_ref[...] = jnp.zeros_like(dkv_ref)

        for h in range(N_XATTN_HEADS):
            sl = slice(h * xd, (h + 1) * xd)
            vsl = slice(D + h * xd, D + (h + 1) * xd)
            qh = q_ref[:, sl]
            kh = kv_ref[:, sl]
            doh = do_ref[:, sl]
            p = _xattn_probs(qh, kh, xd)
            dp = _dot(doh, kv_ref[:, vsl], NT)
            ds = (p * (dp - jnp.sum(p * dp, axis=-1, keepdims=True)) * scale).astype(BF16)
            dq_ref[:, sl] = _dot(ds, kh, NN).astype(BF16)
            dkv_ref[:, sl] += _dot(ds, qh, TN)
            dkv_ref[:, vsl] += _dot(p.astype(BF16), doh, TN)

    row = pl.BlockSpec((tq, D), lambda i: (i, 0))
    return _pcall(name, body, (T // tq,), [row, _full((M, 2 * D)), row], [row, _full((M, 2 * D))],
                  [S((T, D), BF16), S((M, 2 * D), F32)])(q, kv, do)


def _adamw(w, m, v, g, name):
    shape = w.shape
    C = shape[-1]
    R = w.size // C
    tr = R if R <= 512 else _tile(R, 512)

    def body(w_ref, m_ref, v_ref, g_ref, d_ref, nm_ref, nv_ref):
        gv = g_ref[...]
        mv = ADAM_B1 * m_ref[...] + (1.0 - ADAM_B1) * gv
        vv = ADAM_B2 * v_ref[...] + (1.0 - ADAM_B2) * (gv * gv)
        m_hat = mv / (1.0 - ADAM_B1 ** ADAM_STEP)
        v_hat = vv / (1.0 - ADAM_B2 ** ADAM_STEP)
        d_ref[...] = -ADAM_LR * (m_hat / (jnp.sqrt(v_hat) + ADAM_EPS) + ADAM_WD * w_ref[...])
        nm_ref[...] = mv
        nv_ref[...] = vv

    blk = pl.BlockSpec((tr, C), lambda i: (i, 0))
    outs = _pcall(name, body, (R // tr,), [blk] * 4, [blk] * 3, [S((R, C), F32)] * 3)(
        w.reshape(R, C), m.reshape(R, C), v.reshape(R, C), g.reshape(R, C))
    return [o.reshape(shape) for o in outs]


def _place_scalars():
    return jnp.stack([lax.axis_index("c"), 2 * lax.axis_index("x") + lax.axis_index("y")]).astype(jnp.int32)


def _adamw_sum(w, m, v, owns, landed, name):
    L, p, q = w.shape
    qq = owns[0].shape[2]
    tr = next((t for t in range(min(p, ROW_TILE) // 16 * 16, 0, -16) if p % t == 0), p)

    def body(place_ref, w_ref, m_ref, v_ref, *rest):
        own_refs, land_refs = rest[:L], rest[L:2 * L]
        g_ref, d_ref, nm_ref, nv_ref = rest[2 * L:]
        chip = place_ref[1]

        def update(l):
            own = own_refs[l][...].astype(F32)
            gs = None
            for k in range(4):
                term = jnp.where(chip == k, own, land_refs[l][k].astype(F32))
                gs = term if gs is None else gs + term
            gv = gs[:, 0:q]
            mv = ADAM_B1 * m_ref[...] + (1.0 - ADAM_B1) * gv
            vv = ADAM_B2 * v_ref[...] + (1.0 - ADAM_B2) * (gv * gv)
            m_hat = mv / (1.0 - ADAM_B1 ** ADAM_STEP)
            v_hat = vv / (1.0 - ADAM_B2 ** ADAM_STEP)
            g_ref[...] = gv
            d_ref[...] = -ADAM_LR * (m_hat / (jnp.sqrt(v_hat) + ADAM_EPS) + ADAM_WD * w_ref[...])
            nm_ref[...] = mv
            nv_ref[...] = vv

        for l in range(L):
            pl.when(pl.program_id(0) == l)(lambda l=l: update(l))

    def rows_of(layer):
        return lambda l, i, place: jnp.where(l == layer, i, 0)

    blk = pl.BlockSpec((None, tr, q), lambda l, i, place: (l, i, 0))
    in_specs = [blk, blk, blk]
    in_specs += [pl.BlockSpec((None, tr, qq), lambda l, i, place, r=rows_of(k): (0, r(l, i, place), 0)) for k in range(L)]
    in_specs += [pl.BlockSpec((4, None, tr, qq), lambda l, i, place, r=rows_of(k): (0, 0, r(l, i, place), 0)) for k in range(L)]
    gs = pltpu.PrefetchScalarGridSpec(num_scalar_prefetch=1, grid=(L, p // tr), in_specs=in_specs, out_specs=[blk] * 4)
    return pl.pallas_call(body, grid_spec=gs, out_shape=[S((L, p, q), F32)] * 4, name=name,
                          compiler_params=pltpu.CompilerParams(dimension_semantics=("arbitrary", "arbitrary")))(
        _place_scalars(), w, m, v, *owns, *landed)


def _pair_add(g, recv, axis, name):
    _, _, p, q = recv.shape

    def body(place_ref, g_ref, r_ref, o_ref, own_ref):
        s = (g_ref[...].astype(F32) + r_ref[...].astype(F32)).astype(BF16)
        o_ref[...] = s

        @pl.when(pl.program_id(0) == place_ref[1])
        def _():
            own_ref[...] = s

    if axis == 1:
        gspec = pl.BlockSpec((None, p, q), lambda k, place: (0, 2 * k + place[0], 0))
    else:
        gspec = pl.BlockSpec((None, p, q), lambda k, place: (0, 0, 2 * k + place[0]))
    part = pl.BlockSpec((None, None, p, q), lambda k, place: (k, 0, 0, 0))
    own = pl.BlockSpec((None, p, q), lambda k, place: (0, 0, 0))
    gs = pltpu.PrefetchScalarGridSpec(num_scalar_prefetch=1, grid=(4,), in_specs=[gspec, part], out_specs=[part, own])
    return pl.pallas_call(body, grid_spec=gs, out_shape=[S((4, 1, p, q), BF16), S((1, p, q), BF16)], name=name,
                          compiler_params=pltpu.CompilerParams(dimension_semantics=("arbitrary",)))(_place_scalars(), g, recv)


def _win_pieces(n_attn, n_heads, n_conv, shard, chunk):
    bounds = [0, 3 * n_attn, 3 * n_attn + n_heads, 3 * n_attn + n_heads + 2 * n_conv]
    pieces = []
    for j in range(N_DEV):
        lo, hi = shard * j, shard * (j + 1)
        for r in range(3):
            a, b = max(lo, bounds[r]), min(hi, bounds[r + 1])
            if a < b:
                pieces.append((r, a - bounds[r], b - bounds[r], chunk * j + a - lo))
    return pieces


def _win_split(w_in, pieces, widths, name):
    L, D, C = w_in.shape
    tr = _tile(D, 256)

    def body(x_ref, *outs):
        outs[1][...] = jnp.zeros_like(outs[1])
        for r, d0, d1, s0 in pieces:
            outs[r][:, d0:d1] = x_ref[:, s0:s0 + d1 - d0]

    return _pcall(name, body, (L, D // tr), [pl.BlockSpec((None, tr, C), lambda l, i: (l, i, 0))],
                  [pl.BlockSpec((None, tr, wd), lambda l, i: (l, i, 0)) for wd in widths],
                  [S((L, D, wd), BF16) for wd in widths])(w_in)


def _win_merge(parts, pieces, chunked_cols, name):
    L, D, _ = parts[0].shape
    tr = _tile(D, 256)

    def body(a_ref, b_ref, c_ref, o_ref):
        ins = (a_ref, b_ref, c_ref)
        o_ref[...] = jnp.zeros_like(o_ref)
        for r, d0, d1, s0 in pieces:
            o_ref[:, s0:s0 + d1 - d0] = ins[r][:, d0:d1]

    return _pcall(name, body, (L, D // tr), [pl.BlockSpec((None, tr, x.shape[2]), lambda l, i: (l, i, 0)) for x in parts],
                  pl.BlockSpec((None, tr, chunked_cols), lambda l, i: (l, i, 0)), S((L, D, chunked_cols), BF16))(*parts)


def _place():
    return lax.axis_index("x"), lax.axis_index("y"), lax.axis_index("c")


def _flip(v, f):
    return 1 - v if f else v


def _window(ref, axis, size, dev):
    start = dev * size if isinstance(dev, int) else pl.multiple_of(dev * size, LANES if axis == 2 else 16)
    return ref.at[:, pl.ds(start, size), :] if axis == 1 else ref.at[:, :, pl.ds(start, size)]


HBM_SPEC = pl.BlockSpec(memory_space=pltpu.HBM)
SEM_SPEC = pl.BlockSpec(memory_space=pltpu.SEMAPHORE)
SPLIT_COPY_PARAMS = dict(has_side_effects=pltpu.SideEffectType.DATAFLOW_SIDE_EFFECTING)


def _hbm(v):
    return pltpu.with_memory_space_constraint(v, pltpu.HBM)


def _full_shape(shard, axis):
    return tuple(N_DEV * d if i == axis else d for i, d in enumerate(shard.shape))


def _ag_peers(x, y, c):
    return [(x, y, 1 - c), (1 - x, y, c), (x, 1 - y, c), (1 - x, 1 - y, c)]


SIBLING_COLLECTIVE_ID = 0


def _sibling_handshake(x, y, c):
    barrier = pltpu.get_barrier_semaphore()
    pl.semaphore_signal(barrier, inc=1, device_id=(x, y, 1 - c), device_id_type=MESH)
    pl.semaphore_wait(barrier, 1)


def _ag_start(shards, axes, groups, after, name):
    n, ng = len(shards), len(groups)
    sizes = [s.shape[ax] for s, ax in zip(shards, axes)]
    where = {w: (g, i) for g, members in enumerate(groups) for i, w in enumerate(members)}

    def body(*refs):
        xs, fulls = refs[:n], refs[n + 1:2 * n + 1]
        send, recv = refs[3 * n + 1:3 * n + 1 + ng], refs[3 * n + 1 + ng:]
        x, y, c = _place()
        for members in groups:
            for w in members:
                g, i = where[w]
                for k, to in enumerate(_ag_peers(x, y, c)):
                    pltpu.make_async_remote_copy(
                        src_ref=xs[w], dst_ref=_window(fulls[w], axes[w], sizes[w], 4 * x + 2 * y + c),
                        send_sem=send[g].at[4 * i + k], recv_sem=recv[g].at[4 * i + k], device_id=to, device_id_type=MESH).start()

    sems = [pltpu.SemaphoreType.DMA((4 * len(m),)) for m in groups]
    outs = pl.pallas_call(
        body, name=name,
        out_shape=[pltpu.HBM(_full_shape(s, ax), s.dtype) for s, ax in zip(shards, axes)] + [pltpu.HBM(s.shape, s.dtype) for s in shards]
        + sems + sems,
        in_specs=[HBM_SPEC] * n + [pl.BlockSpec(memory_space=pl.ANY)], out_specs=[HBM_SPEC] * (2 * n) + [SEM_SPEC] * (2 * ng),
        input_output_aliases={w: n + w for w in range(n)},
        compiler_params=pltpu.CompilerParams(**SPLIT_COPY_PARAMS))(*[_hbm(s) for s in shards], after)
    return outs[:n], outs[n:2 * n], outs[2 * n:2 * n + ng], outs[2 * n + ng:]


def _ag_wait(shards, fulls, send_sems, recv_sems, axes, after, name):
    n = len(shards)
    sizes = [s.shape[ax] for s, ax in zip(shards, axes)]

    def body(*refs):
        xs = refs[:n]
        send, recv = refs[2 * n], refs[2 * n + 1]
        landed = refs[3 * n + 3:]
        x, y, c = _place()
        for w in range(n):
            for k, frm in enumerate(_ag_peers(x, y, c)):
                copy = pltpu.make_async_remote_copy(
                    src_ref=xs[w], dst_ref=_window(landed[w], axes[w], sizes[w], 4 * frm[0] + 2 * frm[1] + frm[2]),
                    send_sem=send.at[4 * w + k], recv_sem=recv.at[4 * w + k], device_id=frm, device_id_type=MESH)
                copy.wait_send()
                copy.wait_recv()

    outs = pl.pallas_call(
        body, name=name, out_shape=[pltpu.HBM(v.shape, v.dtype) for v in list(shards) + list(fulls)],
        in_specs=[HBM_SPEC] * (2 * n) + [SEM_SPEC, SEM_SPEC, pl.BlockSpec(memory_space=pl.ANY)], out_specs=[HBM_SPEC] * (2 * n),
        input_output_aliases={i: i for i in range(2 * n)},
        compiler_params=pltpu.CompilerParams(**SPLIT_COPY_PARAMS))(*shards, *fulls, send_sems, recv_sems, after)
    return outs[:n], outs[n:]


def _ag_forward(fulls, shards, axes, name):
    n = len(fulls)
    sizes = [s.shape[ax] for s, ax in zip(shards, axes)]

    def body(*refs):
        xs, full_refs = refs[:n], refs[2 * n:3 * n]
        send_sems, recv_sems, local_sems = refs[3 * n:3 * n + 3]
        staged = refs[3 * n + 3:]
        x, y, c = _place()
        _sibling_handshake(x, y, c)
        chips = [(1 - x, y), (x, 1 - y), (1 - x, 1 - y)]
        loads = [pltpu.make_async_copy(xs[w], staged[w], local_sems.at[w]) for w in range(n)]
        for cp in loads:
            cp.start()
        copies = []
        for w in range(n):
            for j, (px, py) in enumerate(chips):
                sent = _window(full_refs[w], axes[w], sizes[w], 4 * px + 2 * py + c)
                got = _window(full_refs[w], axes[w], sizes[w], 4 * px + 2 * py + 1 - c)
                out = pltpu.make_async_remote_copy(src_ref=sent, dst_ref=sent, send_sem=send_sems.at[3 * w + j],
                                                   recv_sem=recv_sems.at[3 * w + j], device_id=(x, y, 1 - c), device_id_type=MESH)
                out.start()
                back = pltpu.make_async_remote_copy(src_ref=got, dst_ref=got, send_sem=send_sems.at[3 * w + j],
                                                    recv_sem=recv_sems.at[3 * w + j], device_id=(x, y, 1 - c), device_id_type=MESH)
                copies.append((out, back))
        stores = []
        for w in range(n):
            loads[w].wait()
            store = pltpu.make_async_copy(staged[w], _window(full_refs[w], axes[w], sizes[w], 4 * x + 2 * y + c), local_sems.at[w])
            store.start()
            stores.append(store)
        for out, back in copies:
            out.wait_send()
            back.wait_recv()
        for cp in stores:
            cp.wait()

    any_spec = pl.BlockSpec(memory_space=pl.ANY)
    outs = pl.pallas_call(
        body, name=name, out_shape=[S(f.shape, f.dtype) for f in fulls], in_specs=[any_spec] * (2 * n), out_specs=[any_spec] * n,
        input_output_aliases={n + w: w for w in range(n)},
        scratch_shapes=[pltpu.SemaphoreType.DMA((3 * n,)), pltpu.SemaphoreType.DMA((3 * n,)), pltpu.SemaphoreType.DMA((n,))]
        + [pltpu.VMEM(s.shape, s.dtype) for s in shards],
        compiler_params=pltpu.CompilerParams(collective_id=SIBLING_COLLECTIVE_ID))(*shards, *fulls)
    return outs


def _sibling_copy(g_ref, land_ref, send_sems, recv_sems, axis, size, w, k, x, y, c):
    return pltpu.make_async_remote_copy(
        src_ref=_window(g_ref, axis, size, 2 * k + 1 - c), dst_ref=land_ref.at[k], send_sem=send_sems.at[4 * w + k],
        recv_sem=recv_sems.at[4 * w + k], device_id=(x, y, 1 - c), device_id_type=MESH)


def _to_sibling_start(grads, axes, sizes, after, name):
    n = len(grads)
    landing = []
    for g, ax, sz in zip(grads, axes, sizes):
        L, K, N = g.shape
        landing.append(pltpu.HBM((4, L, sz, N) if ax == 1 else (4, L, K, sz), g.dtype))
    extra = [] if after is None else [after]

    def body(*refs):
        g_refs = refs[:n]
        land_refs = refs[n + len(extra):2 * n + len(extra)]
        send_sems, recv_sems, token = refs[3 * n + len(extra):]
        x, y, c = _place()
        _sibling_handshake(x, y, c)
        for w in range(n):
            for k in range(4):
                _sibling_copy(g_refs[w], land_refs[w], send_sems, recv_sems, axes[w], sizes[w], w, k, x, y, c).start()
        token[...] = jnp.zeros_like(token)

    sems = pltpu.SemaphoreType.DMA((4 * n,))
    outs = pl.pallas_call(
        body, name=name, out_shape=landing + [pltpu.HBM(g.shape, g.dtype) for g in grads] + [sems, sems, S((8, LANES), F32)],
        in_specs=[HBM_SPEC] * n + [pl.BlockSpec(memory_space=pl.ANY)] * len(extra),
        out_specs=[HBM_SPEC] * (2 * n) + [SEM_SPEC, SEM_SPEC, pl.BlockSpec(memory_space=pltpu.VMEM)],
        input_output_aliases={w: n + w for w in range(n)},
        compiler_params=pltpu.CompilerParams(collective_id=SIBLING_COLLECTIVE_ID, **SPLIT_COPY_PARAMS))(
        *[_hbm(g) for g in grads], *extra)
    return outs[:n], outs[n:2 * n], outs[2 * n], outs[2 * n + 1], outs[2 * n + 2]


def _to_sibling_wait(grads, landing, send_sems, recv_sems, axes, sizes, after, name):
    n = len(grads)

    def body(*refs):
        g_refs = refs[:n]
        send, recv = refs[2 * n], refs[2 * n + 1]
        landed = refs[3 * n + 3:]
        x, y, c = _place()
        for w in range(n):
            for k in range(4):
                copy = _sibling_copy(g_refs[w], landed[w], send, recv, axes[w], sizes[w], w, k, x, y, c)
                copy.wait_send()
                copy.wait_recv()

    outs = pl.pallas_call(
        body, name=name, out_shape=[pltpu.HBM(v.shape, v.dtype) for v in list(grads) + list(landing)],
        in_specs=[HBM_SPEC] * (2 * n) + [SEM_SPEC, SEM_SPEC, pl.BlockSpec(memory_space=pl.ANY)], out_specs=[HBM_SPEC] * (2 * n),
        input_output_aliases={i: i for i in range(2 * n)},
        compiler_params=pltpu.CompilerParams(**SPLIT_COPY_PARAMS))(*grads, *landing, send_sems, recv_sems, after)
    return outs[:n], outs[n:]


def _rs_copy(p_ref, out_ref, send_sems, recv_sems, w, rel, x, y, c):
    tx, ty = _flip(x, rel & 2), _flip(y, rel & 1)
    return pltpu.make_async_remote_copy(
        src_ref=p_ref.at[2 * tx + ty], dst_ref=out_ref.at[2 * x + y], send_sem=send_sems.at[3 * w + rel - 1],
        recv_sem=recv_sems.at[3 * w + rel - 1], device_id=(tx, ty, c), device_id_type=MESH)


def _rs_start(parts, name):
    n = len(parts)

    def body(*refs):
        p_refs, out_refs = refs[:n], refs[n:2 * n]
        send_sems, recv_sems, token = refs[3 * n:]
        x, y, c = _place()
        for w in range(n):
            for rel in (1, 2, 3):
                _rs_copy(p_refs[w], out_refs[w], send_sems, recv_sems, w, rel, x, y, c).start()
        token[...] = jnp.zeros_like(token)

    sems = pltpu.SemaphoreType.DMA((3 * n,))
    outs = pl.pallas_call(
        body, name=name,
        out_shape=[pltpu.HBM(p.shape, p.dtype) for p in parts] * 2 + [sems, sems, S((8, LANES), F32)],
        in_specs=[HBM_SPEC] * n, out_specs=[HBM_SPEC] * (2 * n) + [SEM_SPEC, SEM_SPEC, pl.BlockSpec(memory_space=pltpu.VMEM)],
        input_output_aliases={w: n + w for w in range(n)},
        compiler_params=pltpu.CompilerParams(**SPLIT_COPY_PARAMS))(*[_hbm(p) for p in parts])
    return outs[:n], outs[n:2 * n], outs[2 * n], outs[2 * n + 1], outs[2 * n + 2]


def _rs_wait(parts, landing, send_sems, recv_sems, after, name):
    n = len(parts)

    def body(*refs):
        p_refs = refs[:n]
        send, recv = refs[2 * n], refs[2 * n + 1]
        landed = refs[3 * n + 3:]
        x, y, c = _place()
        for w in range(n):
            for rel in (1, 2, 3):
                copy = _rs_copy(p_refs[w], landed[w], send, recv, w, rel, x, y, c)
                copy.wait_send()
                copy.wait_recv()

    outs = pl.pallas_call(
        body, name=name, out_shape=[pltpu.HBM(v.shape, v.dtype) for v in list(parts) + list(landing)],
        in_specs=[HBM_SPEC] * (2 * n) + [SEM_SPEC, SEM_SPEC, pl.BlockSpec(memory_space=pl.ANY)], out_specs=[HBM_SPEC] * (2 * n),
        input_output_aliases={i: i for i in range(2 * n)},
        compiler_params=pltpu.CompilerParams(**SPLIT_COPY_PARAMS))(*parts, *landing, send_sems, recv_sems, after)
    return outs[n:]


def _exchange_small(v, reduce, name, after=None):
    R, C = v.shape

    def body(v_ref, *rest):
        out_ref, gath, send_sems, recv_sems = rest[-4:]
        x, y, c = _place()
        me = 4 * x + 2 * y + c
        buf = gath if reduce else out_ref
        buf[me] = v_ref[...]
        copies = []
        for rel in range(1, N_DEV):
            peer = (_flip(x, rel & 4), _flip(y, rel & 2), _flip(c, rel & 1))
            copies.append(pltpu.make_async_remote_copy(
                src_ref=v_ref, dst_ref=buf.at[me], send_sem=send_sems.at[rel - 1], recv_sem=recv_sems.at[rel - 1],
                device_id=peer, device_id_type=MESH))
        for cp in copies:
            cp.start()
        for cp in copies:
            cp.wait()
        if reduce:
            acc = gath[0]
            for d in range(1, N_DEV):
                acc = acc + gath[d]
            out_ref[...] = acc

    vm = pl.BlockSpec(memory_space=pltpu.VMEM)
    extra = [] if after is None else [after]
    return pl.pallas_call(
        body, out_shape=S((R, C) if reduce else (N_DEV, R, C), F32), in_specs=[vm] + [pl.BlockSpec(memory_space=pl.ANY)] * len(extra),
        out_specs=vm, name=name,
        scratch_shapes=[pltpu.VMEM((N_DEV, R, C) if reduce else (8, LANES), F32), pltpu.SemaphoreType.DMA((N_DEV - 1,)),
                        pltpu.SemaphoreType.DMA((N_DEV - 1,))])(v, *extra)


def _pad_rows(flat, cols, mult):
    n = flat.shape[-1]
    rows = -(-n // cols)
    rows = -(-rows // mult) * mult
    pad = [(0, 0)] * (flat.ndim - 1) + [(0, rows * cols - n)]
    return jnp.pad(flat, pad).reshape(flat.shape[:-1] + (rows, cols))


def _round_up(n, m):
    return -(-n // m) * m


def _shard_axes(a):
    return [(2, _round_up(a[n].shape[2], LANES)) if kind == 'col' else (1, _round_up(a[n].shape[1 if kind == 'row' else 2], LANES))
            for n, kind in BIG]


def _pack_small(vals):
    rows = [_pad_rows(vals[n].astype(F32).reshape(-1), SMALL_COLS, 1) for n in SMALL]
    m = jnp.concatenate(rows, axis=0)
    return jnp.pad(m, ((0, -m.shape[0] % 8), (0, 0)))


def _unpack_small(m, a):
    out, r = {}, 0
    for n in SMALL:
        nr = -(-a[n].size // SMALL_COLS)
        out[n] = m[r:r + nr].reshape(-1)[:a[n].size].reshape(a[n].shape)
        r += nr
    return out, r


GROUPS = (('ffn1_w_gate', 'ffn1_w_up', 'ffn1_w_down'), ('w_in', 'w_out', 'xattn_w_q', 'xattn_w_kv', 'xattn_w_o'),
          ('ffn2_w_gate', 'ffn2_w_up', 'ffn2_w_down'))


def _layer_small(a, conv_w_full, l):
    H = a['b_f'].shape[1]
    return dict(
        bft=jnp.pad(a['b_f'][l].reshape(H, 1), ((0, 16 - H), (0, 0))),
        cw=jnp.pad(conv_w_full[l], ((0, CONV_PAD - CONV_WIDTH), (0, 0))), cb=a['conv_b'][l].reshape(1, -1),
        lg=a['conv_ln_g'][l].reshape(1, -1), lb=a['conv_ln_b'][l].reshape(1, -1),
        ag=a['attn_out_g'][l].reshape(1, -1), cg=a['conv_out_g'][l].reshape(1, -1),
        g1=a['ffn1_norm_g'][l], gm=a['mix_norm_g'][l], gx=a['xattn_norm_g'][l], gmem=a['mem_norm_g'][l], g2=a['ffn2_norm_g'][l])


def _layer_fwd(x0, mem, w, fetch, cfg, l):
    T = x0.shape[0]
    H = cfg['heads']
    sv = {'x0': x0}
    m = fetch(l, 0, x0)
    w.update(wg1=(m['ffn1_w_gate'], 0), wu1=(m['ffn1_w_up'], 0), wd1=(m['ffn1_w_down'], 0))
    sv['h1'] = _rms_fwd(x0, w['g1'], f"l{l}_ffn1_norm")
    sv['G1'], sv['U1'], sv['A1'] = _ffn_up(sv['h1'], w['wg1'], w['wu1'], f"l{l}_ffn1_up")
    x1 = sv['x1'] = _mm_res(sv['A1'], w['wd1'], x0, 0.5, f"l{l}_ffn1_down")
    m = fetch(l, 1, x1)
    wqkv, wf, wag = _win_split(m['w_in'], cfg['pieces'], cfg['widths'], f"l{l}_w_in_split")
    w.update(wqkv=(wqkv, 0), wft=wf[0, :, :16].T, wag=(wag, 0), wout=(m['w_out'], 0), wq=(m['xattn_w_q'], 0),
             wkv=(m['xattn_w_kv'], 0), wo=(m['xattn_w_o'], 0))
    h2 = sv['h2'] = _rms_fwd(x1, w['gm'], f"l{l}_mix_norm")
    sv['qkv'] = _mm(h2, w['wqkv'], BF16, f"l{l}_qkv_proj")
    sv['agv'] = _mm(h2, w['wag'], F32, f"l{l}_glu_proj")
    ct, sv['sg'] = _fox_prep(h2, w['wft'], w['bft'], f"l{l}_fox_prep")
    sv['c_col'] = ct[:H].reshape(H, T, 1)
    sv['c_row'] = ct[:H].reshape(H // 2, 2, T)
    sv['attn'], sv['lse'] = _fox_fwd(sv['qkv'], sv['c_col'], sv['c_row'], f"l{l}_fox_fwd")
    sv['yc'] = _conv_fwd(sv['agv'], w['cw'], w['cb'], f"l{l}_conv_fwd")
    sv['ycat'] = _mix_post(sv['attn'], sv['yc'], w['ag'], w['cg'], w['lg'], w['lb'], f"l{l}_mix_post")
    x2 = sv['x2'] = _mm_res(sv['ycat'], w['wout'], x1, 1.0, f"l{l}_out_proj")
    sv['h3'] = _rms_fwd(x2, w['gx'], f"l{l}_xattn_norm")
    sv['memn'] = _rms_fwd(mem, w['gmem'], f"l{l}_mem_norm")
    sv['q'] = _mm(sv['h3'], w['wq'], BF16, f"l{l}_xattn_q")
    sv['kv'] = _mm_nt(sv['memn'], w['wkv'], BF16, f"l{l}_xattn_kv")
    sv['o'] = _xattn_fwd(sv['q'], sv['kv'], f"l{l}_xattn_fwd")
    x3 = sv['x3'] = _mm_res(sv['o'], w['wo'], x2, 1.0, f"l{l}_xattn_out")
    m = fetch(l, 2, x3)
    w.update(wg2=(m['ffn2_w_gate'], 0), wu2=(m['ffn2_w_up'], 0), wd2=(m['ffn2_w_down'], 0))
    sv['h4'] = _rms_fwd(x3, w['g2'], f"l{l}_ffn2_norm")
    sv['G2'], sv['U2'], sv['A2'] = _ffn_up(sv['h4'], w['wg2'], w['wu2'], f"l{l}_ffn2_up")
    return _mm_res(sv['A2'], w['wd2'], x3, 0.5, f"l{l}_ffn2_down"), sv


def _ffn_bwd(dout, x_in, h, G, U, A, wg, wu, wd, g, tag, put, which, dep, flush):
    dG, dU = _ffn_bwd_act(dout, wd, G, U, 0.5, tag + "_bwd_act", dep)
    put(which + '_w_down', A, dout, 0.5, tag + "_dwd")
    put(which + '_w_gate', dG, h, 1.0, tag + "_dwg")
    put(which + '_w_up', dU, h, 1.0, tag + "_dwu")
    dep = flush()
    dx, dg = _bwd_h([(dG, wg, 'nn'), (dU, wu, 'nn')], x_in, g, dout, tag + "_bwd_h", dep)
    return dx, dg, dep


def _layer_bwd(dx4, mem, w, sv, reduce, cfg, l, dep):
    small, grads = {}, {}
    T = dx4.shape[0]
    H = cfg['heads']
    tokens = []

    def put(key, act, dy, scale, name):
        grads[key] = _wgrad(act, dy, scale, name, (None, 0, 1))

    def put_and_reduce(key, act, dy, scale, name):
        put(key, act, dy, scale, name)
        tokens.append(reduce(l, (key,), {key: grads.pop(key)}))

    dx3, small['ffn2_norm_g'], dep = _ffn_bwd(
        dx4, sv['x3'], sv['h4'], sv['G2'], sv['U2'], sv['A2'], w['wg2'], w['wu2'], w['wd2'], w['g2'], f"l{l}_ffn2", put, 'ffn2',
        dep, lambda: reduce(l, GROUPS[2], {n: grads.pop(n) for n in GROUPS[2]}))
    do = _mm_nt(dx3, w['wo'], BF16, f"l{l}_xattn_do", dep)
    put('xattn_w_o', sv['o'], dx3, 1.0, f"l{l}_dwo")
    dq, dkv = _xattn_bwd(sv['q'], sv['kv'], do, f"l{l}_xattn_bwd")
    put('xattn_w_q', sv['h3'], dq, 1.0, f"l{l}_dwq")
    dx2, small['xattn_norm_g'] = _bwd_h([(dq, w['wq'], 'nt')], sv['x2'], w['gx'], dx3, f"l{l}_xattn_bwd_h")
    dmemn = _mm(dkv, w['wkv'], F32, f"l{l}_dmemn")
    put('xattn_w_kv', dkv, sv['memn'], 1.0, f"l{l}_dwkv")
    small['mem_norm_g'] = _rms_gain_grad(dmemn, mem, w['gmem'], f"l{l}_dgmem")
    dycat = _mm_nt(dx2, w['wout'], F32, f"l{l}_dycat")
    put('w_out', sv['ycat'], dx2, 1.0, f"l{l}_dwout")
    dattn, dyc, small['attn_out_g'], small['conv_out_g'], small['conv_ln_g'], small['conv_ln_b'] = _mix_post_bwd(
        dycat, sv['attn'], sv['yc'], w['ag'], w['cg'], w['lg'], w['lb'], f"l{l}_mix_post_bwd")
    dva, dga, dcw, small['conv_b'] = _conv_bwd(dyc, sv['agv'], w['cw'], f"l{l}_conv_bwd")
    dq_, dk_, dv_, dcs = _fox_bwd(sv['qkv'], sv['c_col'], sv['c_row'], sv['lse'], dattn, f"l{l}_fox_bwd")
    dcs16 = jnp.pad(dcs.reshape(H, T), ((0, 16 - H), (0, 0)))
    dflt, dwft, dbf = _fox_prep_bwd(dcs16, sv['sg'], sv['h2'], f"l{l}_fox_prep_bwd")
    small['b_f'] = dbf[:H].reshape(H)
    dqkv = jnp.concatenate([dq_, dk_, dv_], axis=1)
    dag = jnp.concatenate([dva, dga], axis=1)
    put('wqkv', sv['h2'], dqkv, 1.0, f"l{l}_dwqkv")
    put('wag', sv['h2'], dag, 1.0, f"l{l}_dwag")
    dx1, small['mix_norm_g'] = _bwd_h([(dqkv, w['wqkv'], 'nt'), (dag, w['wag'], 'nt'), (dflt, w['wft'], 'tn')],
                                      sv['x1'], w['gm'], dx2, f"l{l}_mix_bwd_h")
    dwf = jnp.pad(dwft[:H].T, ((0, 0), (0, LANES - H)))[None].astype(BF16)
    grads['w_in'] = _win_merge((grads.pop('wqkv'), dwf, grads.pop('wag')), cfg['pieces'], cfg['chunked_cols'], f"l{l}_w_in_merge")
    dep = reduce(l, GROUPS[1], {n: grads.pop(n) for n in GROUPS[1]})
    if l == 0:
        dx0, small['ffn1_norm_g'], dep = _ffn_bwd(
            dx1, sv['x0'], sv['h1'], sv['G1'], sv['U1'], sv['A1'], w['wg1'], w['wu1'], w['wd1'], w['g1'], f"l{l}_ffn1",
            put_and_reduce, 'ffn1', dep, lambda: tokens[-1])
    else:
        dx0, small['ffn1_norm_g'], dep = _ffn_bwd(
            dx1, sv['x0'], sv['h1'], sv['G1'], sv['U1'], sv['A1'], w['wg1'], w['wu1'], w['wd1'], w['g1'], f"l{l}_ffn1", put, 'ffn1',
            dep, lambda: reduce(l, GROUPS[0], {n: grads.pop(n) for n in GROUPS[0]}))
    small = {k: v.reshape(-1) for k, v in small.items()}
    return dx0, small, dcw[:CONV_WIDTH], dep


def _local_step(x, mem, tgt, a, conv_w_full, fetch, reduce, cfg):
    L = a['b_f'].shape[0]
    ws = [_layer_small(a, conv_w_full, l) for l in range(L)]
    saved = []
    for l in range(L):
        x, sv = _layer_fwd(x, mem, ws[l], fetch, cfg, l)
        saved.append(sv)
    loss, dx, dgf = _loss_head(x, a['final_norm_g'], tgt, "loss_head")
    smalls, dcws, dep = [None] * L, [None] * L, None
    for l in range(L - 1, -1, -1):
        dx, smalls[l], dcws[l], dep = _layer_bwd(dx, mem, ws[l], saved[l], reduce, cfg, l, dep)
    small = {n: jnp.stack([smalls[l][n] for l in range(L)]) for n in SMALL if n != 'final_norm_g'}
    small['final_norm_g'] = dgf.reshape(-1)
    return loss, dx, small, jnp.stack(dcws)


def kernel(x, mem, ffn1_norm_g, ffn1_w_gate, ffn1_w_up, ffn1_w_down, mix_norm_g, w_in, b_f, conv_w, conv_b, conv_ln_g, conv_ln_b, attn_out_g, conv_out_g, w_out, xattn_norm_g, mem_norm_g, xattn_w_q, xattn_w_kv, xattn_w_o, ffn2_norm_g, ffn2_w_gate, ffn2_w_up, ffn2_w_down, final_norm_g, loss_target, m_ffn1_norm_g, m_ffn1_w_gate, m_ffn1_w_up, m_ffn1_w_down, m_mix_norm_g, m_w_in, m_b_f, m_conv_w, m_conv_b, m_conv_ln_g, m_conv_ln_b, m_attn_out_g, m_conv_out_g, m_w_out, m_xattn_norm_g, m_mem_norm_g, m_xattn_w_q, m_xattn_w_kv, m_xattn_w_o, m_ffn2_norm_g, m_ffn2_w_gate, m_ffn2_w_up, m_ffn2_w_down, m_final_norm_g, v_ffn1_norm_g, v_ffn1_w_gate, v_ffn1_w_up, v_ffn1_w_down, v_mix_norm_g, v_w_in, v_b_f, v_conv_w, v_conv_b, v_conv_ln_g, v_conv_ln_b, v_attn_out_g, v_conv_out_g, v_w_out, v_xattn_norm_g, v_mem_norm_g, v_xattn_w_q, v_xattn_w_kv, v_xattn_w_o, v_ffn2_norm_g, v_ffn2_w_gate, v_ffn2_w_up, v_ffn2_w_down, v_final_norm_g):
    args = (x, mem, ffn1_norm_g, ffn1_w_gate, ffn1_w_up, ffn1_w_down, mix_norm_g, w_in, b_f, conv_w, conv_b, conv_ln_g, conv_ln_b, attn_out_g, conv_out_g, w_out, xattn_norm_g, mem_norm_g, xattn_w_q, xattn_w_kv, xattn_w_o, ffn2_norm_g, ffn2_w_gate, ffn2_w_up, ffn2_w_down, final_norm_g)
    moments_m = (m_ffn1_norm_g, m_ffn1_w_gate, m_ffn1_w_up, m_ffn1_w_down, m_mix_norm_g, m_w_in, m_b_f, m_conv_w, m_conv_b, m_conv_ln_g, m_conv_ln_b, m_attn_out_g, m_conv_out_g, m_w_out, m_xattn_norm_g, m_mem_norm_g, m_xattn_w_q, m_xattn_w_kv, m_xattn_w_o, m_ffn2_norm_g, m_ffn2_w_gate, m_ffn2_w_up, m_ffn2_w_down, m_final_norm_g)
    moments_v = (v_ffn1_norm_g, v_ffn1_w_gate, v_ffn1_w_up, v_ffn1_w_down, v_mix_norm_g, v_w_in, v_b_f, v_conv_w, v_conv_b, v_conv_ln_g, v_conv_ln_b, v_attn_out_g, v_conv_out_g, v_w_out, v_xattn_norm_g, v_mem_norm_g, v_xattn_w_q, v_xattn_w_kv, v_xattn_w_o, v_ffn2_norm_g, v_ffn2_w_gate, v_ffn2_w_up, v_ffn2_w_down, v_final_norm_g)
    a = dict(zip(NAMES, args))
    am = dict(zip(WEIGHTS, moments_m))
    av = dict(zip(WEIGHTS, moments_v))
    L, taps, cshard = conv_w.shape
    dev = 4 * lax.axis_index("x") + 2 * lax.axis_index("y") + lax.axis_index("c")

    big_names = [n for n, _ in BIG]
    geometry = dict(zip(big_names, _shard_axes(a)))
    n_attn, n_heads, n_conv = attn_out_g.shape[1], b_f.shape[1], conv_out_g.shape[1]
    chunk = geometry['w_in'][1]
    cfg = dict(heads=n_heads, pieces=_win_pieces(n_attn, n_heads, n_conv, w_in.shape[2], chunk),
               widths=(3 * n_attn, LANES, 2 * n_conv), chunked_cols=N_DEV * chunk)

    cw_rows = _pad_rows(conv_w.reshape(-1), LANES, 8)
    cw_all = _exchange_small(cw_rows, False, "allgather_conv_w")
    conv_w_full = cw_all.reshape(N_DEV, -1)[:, :conv_w.size].reshape(N_DEV, L, taps, cshard).transpose(1, 2, 0, 3).reshape(
        L, taps, N_DEV * cshard)

    keys = [(l, n) for l in range(L) for names in GROUPS for n in names]
    members = [[keys.index((l, n)) for n in names] for l in range(L) for names in GROUPS]
    shards = []
    for l, n in keys:
        ax, size = geometry[n]
        shard = _as_handled(n, a[n][l:l + 1]).astype(BF16)
        pad = [(0, 0)] * 3
        pad[ax] = (0, size - shard.shape[ax])
        shards.append(jnp.pad(shard, pad))
    key_axes = [geometry[n][0] for _, n in keys]
    fulls, thru, ag_send, ag_recv = _ag_start(shards, key_axes, members, cw_all, "allgather_start")

    def fetch(l, gi, after):
        g = l * len(GROUPS) + gi
        axs = [key_axes[i] for i in members[g]]
        own, landed = _ag_wait([thru[i] for i in members[g]], [fulls[i] for i in members[g]], ag_send[g], ag_recv[g], axs, after,
                               f"allgather_wait_l{l}g{gi}")
        return dict(zip(GROUPS[gi], _ag_forward(landed, own, axs, f"allgather_forward_l{l}g{gi}")))

    pending, own_part, landed_part, in_flight = [], {}, {}, []

    def finish_exchange(after):
        if not in_flight:
            return None
        l, names, g_thru, landing, send, recv_sems = in_flight.pop()
        tag = f"l{l}_{names[0]}_{len(names)}"
        axs = [geometry[n][0] for n in names]
        g_done, recv = _to_sibling_wait(g_thru, landing, send, recv_sems, axs, [geometry[n][1] for n in names], after,
                                       "reduce_sibling_wait_" + tag)
        parts = []
        for n, g, r, ax in zip(names, g_done, recv, axs):
            part, own_part[(l, n)] = _pair_add(g, r, ax, f"reduce_pair_add_l{l}_{n}")
            parts.append(part)
        landing, parts_thru, send, recv_sems, token = _rs_start(parts, "reduce_start_" + tag)
        pending.append((l, names, parts_thru, landing, send, recv_sems))
        return token

    def reduce(l, names, grads):
        gl = [grads[n] for n in names]
        token = finish_exchange(gl[0])
        landing, g_thru, send, recv_sems, token = _to_sibling_start(
            gl, [geometry[n][0] for n in names], [geometry[n][1] for n in names], token,
            f"reduce_sibling_start_l{l}_{names[0]}_{len(names)}")
        in_flight.append((l, names, g_thru, landing, send, recv_sems))
        return token

    loss, grad_x, gsmall, dcw = _local_step(x[0], mem[0], loss_target[0], a, conv_w_full, fetch, reduce, cfg)
    finish_exchange(grad_x)

    def wait_group(entry, after):
        l, names, parts_thru, landing, send, recv_sems = entry
        landed = _rs_wait(parts_thru, landing, send, recv_sems, after, f"reduce_wait_l{l}_{names[0]}_{len(names)}")
        for n, arr in zip(names, landed):
            landed_part[(l, n)] = arr

    for entry in pending[:-1]:
        wait_group(entry, grad_x)

    grads, delta, new_m, new_v = {}, {}, {}, {}

    def update(n):
        outs = _adamw_sum(_as_handled(n, a[n]), _as_handled(n, am[n]), _as_handled(n, av[n]),
                          [own_part[(l, n)] for l in range(L)], [landed_part[(l, n)] for l in range(L)], "adamw_" + n)
        grads[n], delta[n], new_m[n], new_v[n] = (_as_handled(n, o) for o in outs)

    last_names = pending[-1][1]
    early = [n for n in big_names if n not in last_names]
    for n in early:
        update(n)
    wait_group(pending[-1], delta[early[-1]])
    for n in last_names:
        update(n)

    small_rows = _pack_small(gsmall)
    n_small = small_rows.shape[0]
    dcw_rows = jnp.pad(dcw, ((0, 0), (0, CONV_PAD - taps), (0, 0))).reshape(-1, SMALL_COLS)
    summed = _exchange_small(jnp.concatenate([small_rows, dcw_rows], axis=0), True, "allreduce_small",
                             after=landed_part[(pending[-1][0], last_names[0])])
    g_small, _ = _unpack_small(summed[:n_small], a)
    dcw_sum = summed[n_small:].reshape(L, CONV_PAD, N_DEV * cshard)[:, :taps]
    grads.update(g_small)
    grads['conv_w'] = lax.dynamic_slice_in_dim(dcw_sum, dev * cshard, cshard, axis=2)
    delta['conv_w'], new_m['conv_w'], new_v['conv_w'] = _adamw(conv_w, am['conv_w'], av['conv_w'], grads['conv_w'], "adamw_conv_w")
    pw, pm, pv, pg = (_pack_small(d) for d in (a, am, av, g_small))
    for dst, packed in zip((delta, new_m, new_v), _adamw(pw, pm, pv, pg, "adamw_small")):
        dst.update(_unpack_small(packed, a)[0])

    total = lax.psum(loss.reshape(()), ("x", "y", "c"))
    return (total, grad_x[None], *[grads[n] for n in WEIGHTS], *[delta[n] for n in WEIGHTS], *[new_m[n] for n in WEIGHTS],
            *[new_v[n] for n in WEIGHTS])
```

```python
import math

import jax
import jax.numpy as jnp
from jax import lax
from jax.experimental import pallas as pl
from jax.experimental.pallas import tpu as pltpu

F32, BF16 = jnp.float32, jnp.bfloat16
S = jax.ShapeDtypeStruct
MESH = pl.DeviceIdType.MESH

EPS = 1e-6
NEG_INF = -1e30
HEAD_DIM = 64
N_XATTN_HEADS = 4
CONV_WIDTH = 31
CONV_PAD = 32
LANES = 128
ADAM_LR, ADAM_B1, ADAM_B2, ADAM_EPS, ADAM_WD, ADAM_STEP = 0.001, 0.9, 0.999, 1e-08, 0.01, 10
N_DEV = 8
VMEM_LIMIT_BYTES = 56 * 1024 * 1024
ROW_TILE = 512
MM_ROW_TILE = 1024
EPILOGUE_COLS = 512
SMALL_COLS = 512

NN = ((1,), (0,))
NT = ((1,), (1,))
TN = ((0,), (0,))

NAMES = ['x', 'mem', 'ffn1_norm_g', 'ffn1_w_gate', 'ffn1_w_up', 'ffn1_w_down', 'mix_norm_g', 'w_in', 'b_f', 'conv_w', 'conv_b',
         'conv_ln_g', 'conv_ln_b', 'attn_out_g', 'conv_out_g', 'w_out', 'xattn_norm_g', 'mem_norm_g', 'xattn_w_q', 'xattn_w_kv',
         'xattn_w_o', 'ffn2_norm_g', 'ffn2_w_gate', 'ffn2_w_up', 'ffn2_w_down', 'final_norm_g']
WEIGHTS = NAMES[2:]
BIG = [('ffn1_w_gate', 'colT'), ('ffn1_w_up', 'colT'), ('ffn1_w_down', 'row'), ('w_in', 'col'), ('w_out', 'row'),
       ('xattn_w_q', 'row'), ('xattn_w_kv', 'colT'), ('xattn_w_o', 'row'), ('ffn2_w_gate', 'colT'), ('ffn2_w_up', 'colT'),
       ('ffn2_w_down', 'row')]
TRANSPOSED = tuple(n for n, kind in BIG if kind == 'colT')


def _as_handled(n, v):
    return jnp.swapaxes(v, 1, 2) if n in TRANSPOSED else v
SMALL = ['ffn1_norm_g', 'mix_norm_g', 'xattn_norm_g', 'mem_norm_g', 'ffn2_norm_g', 'conv_b', 'conv_ln_g', 'conv_ln_b',
         'attn_out_g', 'conv_out_g', 'b_f', 'final_norm_g']


def _dot(a, b, dims):
    return lax.dot_general(a, b, (dims, ((), ())), preferred_element_type=F32)


def _full(shape):
    nd = len(shape)
    return pl.BlockSpec(shape, lambda *_: (0,) * nd)


def _tile(n, pref):
    for t in (pref, 512, 384, 256, 128, 64, 32, 16, 8):
        if t <= n and n % t == 0:
            return t
    return n


def _pcall(name, body, grid, in_specs, out_specs, out_shape, scratch=(), aliases=None, dep=None):
    n_in = len(in_specs)
    kernel_body = body
    if dep is not None:
        in_specs = list(in_specs) + [pl.BlockSpec(memory_space=pl.ANY)]

        def kernel_body(*refs):
            return body(*refs[:n_in], *refs[n_in + 1:])

    call = pl.pallas_call(
        kernel_body, grid=grid, in_specs=in_specs, out_specs=out_specs, out_shape=out_shape, scratch_shapes=list(scratch),
        name=name, input_output_aliases=aliases or {},
        compiler_params=pltpu.CompilerParams(dimension_semantics=("arbitrary",) * len(grid), vmem_limit_bytes=VMEM_LIMIT_BYTES))
    return call if dep is None else (lambda *args: call(*args, dep))


def _arr(w):
    return w[0] if isinstance(w, tuple) else w


def _wshape(w):
    return w[0].shape[1:] if isinstance(w, tuple) else w.shape


def _wspec(w, block, imap):
    if isinstance(w, tuple):
        layer = w[1]
        return pl.BlockSpec((None,) + block, lambda *g: (layer,) + imap(*g))
    return pl.BlockSpec(block, imap)


def _wfull(w):
    shape = _wshape(w)
    return _wspec(w, shape, lambda *_: (0,) * len(shape))


def _sigmoid(z):
    return jax.nn.sigmoid(z)


def _rstd(x):
    return lax.rsqrt(jnp.mean(x * x, axis=-1, keepdims=True) + EPS)


def _rms_bwd(dy, x, g):
    r = _rstd(x)
    xh = x * r
    u = dy * g
    dx = r * (u - xh * jnp.mean(u * xh, axis=-1, keepdims=True))
    return dx, dy * xh


def _colsum(v):
    return jnp.sum(v, axis=0, keepdims=True)


def _rms_fwd(x, g, name):
    T, D = x.shape
    tm = _tile(T, ROW_TILE)

    def body(x_ref, g_ref, h_ref):
        xv = x_ref[...]
        h_ref[...] = (xv * _rstd(xv) * g_ref[...]).astype(BF16)

    row = pl.BlockSpec((tm, D), lambda i: (i, 0))
    return _pcall(name, body, (T // tm,), [row, _full((1, D))], row, S((T, D), BF16))(x, g.reshape(1, D))


def _mm(a, w, out_dtype, name):
    M, K = a.shape
    N = _wshape(w)[1]
    tm = _tile(M, ROW_TILE)
    tn = N if N <= 1536 else N // 2

    def body(a_ref, w_ref, o_ref):
        o_ref[...] = _dot(a_ref[...].astype(BF16), w_ref[...], NN).astype(out_dtype)

    return _pcall(name, body, (N // tn, M // tm),
                  [pl.BlockSpec((tm, K), lambda j, i: (i, 0)), _wspec(w, (K, tn), lambda j, i: (0, j))],
                  pl.BlockSpec((tm, tn), lambda j, i: (i, j)), S((M, N), out_dtype))(a, _arr(w))


def _mm_res(a, w, res, scale, name):
    M, K = a.shape
    N = _wshape(w)[1]
    tm = _tile(M, ROW_TILE)

    def body(a_ref, w_ref, r_ref, o_ref):
        o_ref[...] = r_ref[...] + scale * _dot(a_ref[...], w_ref[...], NN)

    row = pl.BlockSpec((tm, N), lambda i: (i, 0))
    return _pcall(name, body, (M // tm,), [pl.BlockSpec((tm, K), lambda i: (i, 0)), _wfull(w), row], row,
                  S((M, N), F32))(a, _arr(w), res)


def _mm_nt(a, w, out_dtype, name, dep=None):
    M, K = a.shape
    N = _wshape(w)[0]
    tm = _tile(M, ROW_TILE)
    tn = N if N <= 1536 else N // 2

    def body(a_ref, w_ref, o_ref):
        o_ref[...] = _dot(a_ref[...].astype(BF16), w_ref[...], NT).astype(out_dtype)

    return _pcall(name, body, (N // tn, M // tm),
                  [pl.BlockSpec((tm, K), lambda j, i: (i, 0)), _wspec(w, (tn, K), lambda j, i: (j, 0))],
                  pl.BlockSpec((tm, tn), lambda j, i: (i, j)), S((M, N), out_dtype), dep=dep)(a, _arr(w))


def _wgrad(a, dy, scale, name, into):
    buf, layer, L = into
    T, M = a.shape
    N = dy.shape[1]
    tm = _tile(M, ROW_TILE)

    def body(a_ref, dy_ref, *rest):
        rest[-1][...] = (scale * _dot(a_ref[...].astype(BF16), dy_ref[...].astype(BF16), TN)).astype(BF16)

    in_specs = [pl.BlockSpec((T, tm), lambda i: (0, i)), _full((T, N))]
    args = [a, dy]
    if buf is not None:
        in_specs.append(pl.BlockSpec(memory_space=pl.ANY))
        args.append(buf)
    return _pcall(name, body, (M // tm,), in_specs, pl.BlockSpec((None, tm, N), lambda i: (layer, i, 0)), S((L, M, N), BF16),
                  aliases={2: 0} if buf is not None else None)(*args)


def _bwd_h(dots, x, g, dres, name, dep=None):
    T, D = x.shape
    tm = _tile(T, 256)
    n = len(dots)
    dims = [{'nt': NT, 'nn': NN, 'tn': TN}[m] for _, _, m in dots]

    def body(*refs):
        x_ref, g_ref, r_ref, dx_ref, dg_ref = refs[2 * n:]
        dh = None
        for k in range(n):
            part = _dot(refs[2 * k][...], refs[2 * k + 1][...], dims[k])
            dh = part if dh is None else dh + part
        dx, dgrow = _rms_bwd(dh, x_ref[...], g_ref[...])
        dx_ref[...] = r_ref[...] + dx

        @pl.when(pl.program_id(0) == 0)
        def _():
            dg_ref[...] = jnp.zeros_like(dg_ref)

        dg_ref[...] += _colsum(dgrow)

    in_specs, args = [], []
    for lhs, w, mode in dots:
        if mode == 'tn':
            in_specs.append(pl.BlockSpec((lhs.shape[0], tm), lambda i: (0, i)))
        else:
            in_specs.append(pl.BlockSpec((tm, lhs.shape[1]), lambda i: (i, 0)))
        in_specs.append(_wfull(w))
        args += [lhs, _arr(w)]
    row = pl.BlockSpec((tm, D), lambda i: (i, 0))
    in_specs += [row, _full((1, D)), row]
    return _pcall(name, body, (T // tm,), in_specs, [row, _full((1, D))], [S((T, D), F32), S((1, D), F32)], dep=dep)(
        *args, x, g.reshape(1, D), dres)


def _rms_gain_grad(dy, x, g, name):
    T, D = x.shape

    def body(dy_ref, x_ref, g_ref, dg_ref):
        _, dgrow = _rms_bwd(dy_ref[...], x_ref[...], g_ref[...])
        dg_ref[...] = _colsum(dgrow)

    return _pcall(name, body, (), [_full((T, D)), _full((T, D)), _full((1, D))], _full((1, D)), S((1, D), F32))(
        dy, x, g.reshape(1, D))


def _ffn_up(h, wg, wu, name):
    T, D = h.shape
    Fh = _wshape(wg)[0]
    tm = _tile(T, MM_ROW_TILE)
    tn = Fh if Fh <= 1536 else Fh // 2
    tc = _tile(tn, EPILOGUE_COLS)

    def body(h_ref, wg_ref, wu_ref, g_ref, u_ref, a_ref):
        hv = h_ref[...]
        for cb in range(tn // tc):
            cols = slice(cb * tc, (cb + 1) * tc)
            gv = _dot(hv, wg_ref[cols, :], NT)
            uv = _dot(hv, wu_ref[cols, :], NT)
            g_ref[:, cols] = gv.astype(BF16)
            u_ref[:, cols] = uv.astype(BF16)
            a_ref[:, cols] = (gv * _sigmoid(gv) * uv).astype(BF16)

    tile = pl.BlockSpec((tm, tn), lambda j, i: (i, j))
    return _pcall(name, body, (Fh // tn, T // tm),
                  [pl.BlockSpec((tm, D), lambda j, i: (i, 0)), _wspec(wg, (tn, D), lambda j, i: (j, 0)),
                   _wspec(wu, (tn, D), lambda j, i: (j, 0))],
                  [tile, tile, tile], [S((T, Fh), BF16)] * 3)(h, _arr(wg), _arr(wu))


def _ffn_bwd_act(dout, wd, gate, up, scale, name, dep=None):
    T, D = dout.shape
    Fh = _wshape(wd)[0]
    tm = _tile(T, MM_ROW_TILE)
    tn = Fh if Fh <= 1536 else Fh // 2
    tc = _tile(tn, EPILOGUE_COLS)

    def body(d_ref, w_ref, g_ref, u_ref, dg_ref, du_ref):
        dv = d_ref[...].astype(BF16)
        for cb in range(tn // tc):
            cols = slice(cb * tc, (cb + 1) * tc)
            da = scale * _dot(dv, w_ref[cols, :], NT)
            gv = g_ref[:, cols].astype(F32)
            uv = u_ref[:, cols].astype(F32)
            sg = _sigmoid(gv)
            dg_ref[:, cols] = (da * uv * (sg * (1.0 + gv * (1.0 - sg)))).astype(BF16)
            du_ref[:, cols] = (da * (gv * sg)).astype(BF16)

    tile = pl.BlockSpec((tm, tn), lambda j, i: (i, j))
    return _pcall(name, body, (Fh // tn, T // tm),
                  [pl.BlockSpec((tm, D), lambda j, i: (i, 0)), _wspec(wd, (tn, D), lambda j, i: (j, 0)), tile, tile],
                  [tile, tile], [S((T, Fh), BF16)] * 2, dep=dep)(dout, _arr(wd), gate, up)


def _loss_head(x, g, tgt, name):
    T, D = x.shape
    tm = _tile(T, ROW_TILE)

    def body(x_ref, g_ref, t_ref, loss_ref, dx_ref, dg_ref):
        xv = x_ref[...]
        gv = g_ref[...]
        r = _rstd(xv)
        xh = xv * r
        e = xh * gv - t_ref[...]
        dy = e * (1.0 / D)
        u = dy * gv
        dx_ref[...] = r * (u - xh * jnp.mean(u * xh, axis=-1, keepdims=True))

        @pl.when(pl.program_id(0) == 0)
        def _():
            dg_ref[...] = jnp.zeros_like(dg_ref)
            loss_ref[...] = jnp.zeros_like(loss_ref)

        dg_ref[...] += _colsum(dy * xh)
        loss_ref[...] += 0.5 * _colsum(jnp.mean(e * e, axis=-1, keepdims=True))

    row = pl.BlockSpec((tm, D), lambda i: (i, 0))
    return _pcall(name, body, (T // tm,), [row, _full((1, D)), row], [_full((1, 1)), row, _full((1, D))],
                  [S((1, 1), F32), S((T, D), F32), S((1, D), F32)])(x, g.reshape(1, D), tgt)


def _split3(xb):
    hi = xb.astype(BF16)
    r1 = xb - hi.astype(F32)
    mid = r1.astype(BF16)
    lo = (r1 - mid.astype(F32)).astype(BF16)
    return hi, mid, lo


def _fox_prep(h, wft, bft, name):
    T, D = h.shape
    blk = _tile(T, 256)

    def body(h_ref, w_ref, b_ref, ct_ref, sg_ref):
        z = _dot(w_ref[...], h_ref[...], NT) + b_ref[...]
        sg_ref[...] = 1.0 - _sigmoid(z)
        logf = jnp.minimum(z, 0.0) - jnp.log1p(jnp.exp(-jnp.abs(z)))
        upper = (lax.broadcasted_iota(jnp.int32, (blk, blk), 0) <= lax.broadcasted_iota(jnp.int32, (blk, blk), 1)).astype(BF16)
        carry = jnp.zeros((16, 1), F32)
        for b in range(T // blk):
            hi, mid, lo = _split3(logf[:, b * blk:(b + 1) * blk])
            cb = _dot(hi, upper, NN) + _dot(mid, upper, NN) + _dot(lo, upper, NN) + carry
            ct_ref[:, b * blk:(b + 1) * blk] = cb
            carry = cb[:, blk - 1:blk]

    return _pcall(name, body, (), [_full((T, D)), _full((16, D)), _full((16, 1))], [_full((16, T)), _full((16, T))],
                  [S((16, T), F32), S((16, T), F32)])(h, wft, bft)


def _fox_prep_bwd(dcs, sg, h, name):
    T, D = h.shape
    blk = _tile(T, 256)
    nb = T // blk

    def body(dcs_ref, sg_ref, h_ref, dfl_ref, dw_ref, db_ref):
        lower = (lax.broadcasted_iota(jnp.int32, (blk, blk), 0) >= lax.broadcasted_iota(jnp.int32, (blk, blk), 1)).astype(BF16)
        carry = jnp.zeros((16, 1), F32)
        db = jnp.zeros((16, 1), F32)
        for b in range(nb - 1, -1, -1):
            cols = slice(b * blk, (b + 1) * blk)
            hi, mid, lo = _split3(-dcs_ref[:, cols])
            dlogf = _dot(hi, lower, NN) + _dot(mid, lower, NN) + _dot(lo, lower, NN) + carry
            carry = dlogf[:, 0:1]
            dfl = dlogf * sg_ref[:, cols]
            db = db + jnp.sum(dfl, axis=-1, keepdims=True)
            dfl_ref[:, cols] = dfl.astype(BF16)
        db_ref[...] = db
        dw_ref[...] = _dot(dfl_ref[...], h_ref[...], NN)

    return _pcall(name, body, (), [_full((16, T)), _full((16, T)), _full((T, D))],
                  [_full((16, T)), _full((16, D)), _full((16, 1))],
                  [S((16, T), BF16), S((16, D), F32), S((16, 1), F32)])(dcs, sg, h)


def _fox_logits(q, k, c_col, c_row):
    tq, kp = q.shape[0], k.shape[0]
    s = _dot(q, k, NT) * (1.0 / math.sqrt(HEAD_DIM)) + (c_col - c_row)
    row = lax.broadcasted_iota(jnp.int32, (tq, tq), 0)
    col = lax.broadcasted_iota(jnp.int32, (tq, tq), 1)
    diag = jnp.where(row >= col, s[:, kp - tq:], NEG_INF)
    return diag if kp == tq else jnp.concatenate([s[:, :kp - tq], diag], axis=1)


def _fox_specs(T, n_pairs):
    qs = pl.BlockSpec((T, LANES), lambda p: (0, p))
    ks = pl.BlockSpec((T, LANES), lambda p: (0, n_pairs + p))
    vs = pl.BlockSpec((T, LANES), lambda p: (0, 2 * n_pairs + p))
    col = pl.BlockSpec((2, T, 1), lambda p: (p, 0, 0))
    rowv = pl.BlockSpec((None, 2, T), lambda p: (p, 0, 0))
    return qs, ks, vs, col, rowv


def _fox_fwd(qkv, c_col, c_row, name):
    T = qkv.shape[0]
    DA = qkv.shape[1] // 3
    n_pairs = DA // LANES
    tq = _tile(T, 256)

    def body(q_ref, k_ref, v_ref, c_ref, ct_ref, o_ref, lse_ref):
        for hh in range(2):
            sl = slice(hh * HEAD_DIM, (hh + 1) * HEAD_DIM)
            for i in range(T // tq):
                rows = slice(i * tq, (i + 1) * tq)
                kp = (i + 1) * tq
                s = _fox_logits(q_ref[rows, sl], k_ref[0:kp, sl], c_ref[hh, rows, :], ct_ref[hh:hh + 1, 0:kp])
                m = jnp.max(s, axis=-1, keepdims=True)
                p = jnp.exp(s - m)
                l = jnp.sum(p, axis=-1, keepdims=True)
                o_ref[rows, sl] = _dot(p.astype(BF16), v_ref[0:kp, sl], NN) / l
                lse_ref[hh, rows, :] = m + jnp.log(l)

    qs, ks, vs, col, rowv = _fox_specs(T, n_pairs)
    return _pcall(name, body, (n_pairs,), [qs, ks, vs, col, rowv], [qs, col],
                  [S((T, DA), F32), S((2 * n_pairs, T, 1), F32)])(qkv, qkv, qkv, c_col, c_row)


def _fox_bwd(qkv, c_col, c_row, lse, do, name):
    T = qkv.shape[0]
    DA = qkv.shape[1] // 3
    n_pairs = DA // LANES
    tq = _tile(T, 256)
    scale = 1.0 / math.sqrt(HEAD_DIM)

    def body(q_ref, k_ref, v_ref, c_ref, ct_ref, lse_ref, do_ref, dq_ref, dk_ref, dv_ref, dcs_ref, dk_acc, dv_acc):
        dk_acc[...] = jnp.zeros_like(dk_acc)
        dv_acc[...] = jnp.zeros_like(dv_acc)
        dcs_ref[...] = jnp.zeros_like(dcs_ref)
        for hh in range(2):
            sl = slice(hh * HEAD_DIM, (hh + 1) * HEAD_DIM)
            for i in range(T // tq):
                rows = slice(i * tq, (i + 1) * tq)
                kp = (i + 1) * tq
                q = q_ref[rows, sl]
                k = k_ref[0:kp, sl]
                dob = do_ref[rows, sl]
                s = _fox_logits(q, k, c_ref[hh, rows, :], ct_ref[hh:hh + 1, 0:kp])
                p = jnp.exp(s - lse_ref[hh, rows, :])
                dp = _dot(dob, v_ref[0:kp, sl], NT)
                ds = p * (dp - jnp.sum(p * dp, axis=-1, keepdims=True))
                dsb = ds.astype(BF16)
                dq_ref[rows, sl] = (_dot(dsb, k, NN) * scale).astype(BF16)
                dk_acc[0:kp, sl] += _dot(dsb, q, TN) * scale
                dv_acc[0:kp, sl] += _dot(p.astype(BF16), dob, TN)
                dcs_ref[hh:hh + 1, 0:kp] += _colsum(ds)
        dk_ref[...] = dk_acc[...].astype(BF16)
        dv_ref[...] = dv_acc[...].astype(BF16)

    qs, ks, vs, col, rowv = _fox_specs(T, n_pairs)
    return _pcall(name, body, (n_pairs,), [qs, ks, vs, col, rowv, col, qs], [qs, qs, qs, rowv],
                  [S((T, DA), BF16)] * 3 + [S((n_pairs, 2, T), F32)],
                  scratch=[pltpu.VMEM((T, LANES), F32), pltpu.VMEM((T, LANES), F32)])(qkv, qkv, qkv, c_col, c_row, lse, do)


def _conv_fwd(ag, w, b, name):
    T = ag.shape[0]
    DC = ag.shape[1] // 2
    nb = DC // LANES
    tr = _tile(T, 256)

    def body(a_ref, g_ref, w_ref, b_ref, y_ref, pad):
        pad[0:CONV_PAD, :] = jnp.zeros((CONV_PAD, LANES), F32)
        pad[CONV_PAD:CONV_PAD + T, :] = a_ref[...] * _sigmoid(g_ref[...])
        for r in range(T // tr):
            acc = jnp.zeros((tr, LANES), F32) + b_ref[...]
            for j in range(CONV_WIDTH):
                o = r * tr + CONV_PAD - (CONV_WIDTH - 1) + j
                acc = acc + w_ref[j:j + 1, :] * pad[o:o + tr, :]
            y_ref[r * tr:(r + 1) * tr, :] = acc

    blk = pl.BlockSpec((T, LANES), lambda c: (0, c))
    return _pcall(name, body, (nb,), [blk, pl.BlockSpec((T, LANES), lambda c: (0, nb + c)),
                                      pl.BlockSpec((CONV_PAD, LANES), lambda c: (0, c)), pl.BlockSpec((1, LANES), lambda c: (0, c))],
                  blk, S((T, DC), F32), scratch=[pltpu.VMEM((T + CONV_PAD, LANES), F32)])(ag, ag, w, b)


def _conv_bwd(dy, ag, w, name):
    T = ag.shape[0]
    DC = ag.shape[1] // 2
    nb = DC // LANES
    tr = _tile(T, 256)

    def body(dy_ref, a_ref, g_ref, w_ref, da_ref, dg_ref, dw_ref, db_ref, pad, dpad):
        av = a_ref[...]
        sg = _sigmoid(g_ref[...])
        pad[0:CONV_PAD, :] = jnp.zeros((CONV_PAD, LANES), F32)
        pad[CONV_PAD:CONV_PAD + T, :] = av * sg
        dpad[0:T, :] = dy_ref[...]
        dpad[T:T + CONV_PAD, :] = jnp.zeros((CONV_PAD, LANES), F32)
        db_ref[...] = _colsum(dy_ref[...])
        dw_ref[...] = jnp.zeros_like(dw_ref)
        for j in range(CONV_WIDTH):
            acc = jnp.zeros((tr, LANES), F32)
            for r in range(T // tr):
                o = r * tr + CONV_PAD - (CONV_WIDTH - 1) + j
                acc = acc + dpad[r * tr:(r + 1) * tr, :] * pad[o:o + tr, :]
            dw_ref[j:j + 1, :] = _colsum(acc)
        for r in range(T // tr):
            acc = jnp.zeros((tr, LANES), F32)
            for j in range(CONV_WIDTH):
                o = r * tr + (CONV_WIDTH - 1) - j
                acc = acc + w_ref[j:j + 1, :] * dpad[o:o + tr, :]
            rows = slice(r * tr, (r + 1) * tr)
            sgr = sg[rows, :]
            da_ref[rows, :] = (acc * sgr).astype(BF16)
            dg_ref[rows, :] = (acc * av[rows, :] * sgr * (1.0 - sgr)).astype(BF16)

    blk = pl.BlockSpec((T, LANES), lambda c: (0, c))
    wblk = pl.BlockSpec((CONV_PAD, LANES), lambda c: (0, c))
    return _pcall(name, body, (nb,), [blk, blk, pl.BlockSpec((T, LANES), lambda c: (0, nb + c)), wblk],
                  [blk, blk, wblk, pl.BlockSpec((1, LANES), lambda c: (0, c))],
                  [S((T, DC), BF16), S((T, DC), BF16), S((CONV_PAD, DC), F32), S((1, DC), F32)],
                  scratch=[pltpu.VMEM((T + CONV_PAD, LANES), F32), pltpu.VMEM((T + CONV_PAD, LANES), F32)])(dy, ag, ag, w)


def _conv_norms(yc, lg, lb):
    mu = jnp.mean(yc, axis=-1, keepdims=True)
    xc = yc - mu
    rs = lax.rsqrt(jnp.mean(xc * xc, axis=-1, keepdims=True) + EPS)
    xh = xc * rs
    z = xh * lg + lb
    sg = _sigmoid(z)
    return rs, xh, z, sg, z * sg


def _mix_post(attn, yc, ag, cg, lg, lb, name):
    T, DA = attn.shape
    DC = yc.shape[1]
    tm = _tile(T, ROW_TILE)

    def body(at_ref, yc_ref, ag_ref, cg_ref, lg_ref, lb_ref, y_ref):
        at = at_ref[...]
        y_ref[:, 0:DA] = (at * _rstd(at) * ag_ref[...]).astype(BF16)
        _, _, _, _, sv = _conv_norms(yc_ref[...], lg_ref[...], lb_ref[...])
        y_ref[:, DA:DA + DC] = (sv * _rstd(sv) * cg_ref[...]).astype(BF16)

    return _pcall(name, body, (T // tm,),
                  [pl.BlockSpec((tm, DA), lambda i: (i, 0)), pl.BlockSpec((tm, DC), lambda i: (i, 0)), _full((1, DA)),
                   _full((1, DC)), _full((1, DC)), _full((1, DC))],
                  pl.BlockSpec((tm, DA + DC), lambda i: (i, 0)), S((T, DA + DC), BF16))(attn, yc, ag, cg, lg, lb)


def _mix_post_bwd(dy, attn, yc, ag, cg, lg, lb, name):
    T, DA = attn.shape
    DC = yc.shape[1]
    tm = _tile(T, ROW_TILE)

    def body(dy_ref, at_ref, yc_ref, ag_ref, cg_ref, lg_ref, lb_ref, dat_ref, dyc_ref, dag_ref, dcg_ref, dlg_ref, dlb_ref):
        dat, dag_rows = _rms_bwd(dy_ref[:, 0:DA], at_ref[...], ag_ref[...])
        dat_ref[...] = dat.astype(BF16)
        lgv = lg_ref[...]
        rs, xh, z, sg, sv = _conv_norms(yc_ref[...], lgv, lb_ref[...])
        dsv, dcg_rows = _rms_bwd(dy_ref[:, DA:DA + DC], sv, cg_ref[...])
        dz = dsv * (sg * (1.0 + z * (1.0 - sg)))
        dxh = dz * lgv
        dyc_ref[...] = rs * (dxh - jnp.mean(dxh, axis=-1, keepdims=True) - xh * jnp.mean(dxh * xh, axis=-1, keepdims=True))

        @pl.when(pl.program_id(0) == 0)
        def _():
            for r in (dag_ref, dcg_ref, dlg_ref, dlb_ref):
                r[...] = jnp.zeros_like(r)

        dag_ref[...] += _colsum(dag_rows)
        dcg_ref[...] += _colsum(dcg_rows)
        dlg_ref[...] += _colsum(dz * xh)
        dlb_ref[...] += _colsum(dz)

    ra = pl.BlockSpec((tm, DA), lambda i: (i, 0))
    rc = pl.BlockSpec((tm, DC), lambda i: (i, 0))
    return _pcall(name, body, (T // tm,),
                  [pl.BlockSpec((tm, DA + DC), lambda i: (i, 0)), ra, rc, _full((1, DA)), _full((1, DC)), _full((1, DC)),
                   _full((1, DC))],
                  [ra, rc, _full((1, DA)), _full((1, DC)), _full((1, DC)), _full((1, DC))],
                  [S((T, DA), BF16), S((T, DC), F32), S((1, DA), F32), S((1, DC), F32), S((1, DC), F32), S((1, DC), F32)])(
        dy, attn, yc, ag, cg, lg, lb)


def _xattn_probs(q, k, xd):
    s = _dot(q, k, NT) * (1.0 / math.sqrt(xd))
    p = jnp.exp(s - jnp.max(s, axis=-1, keepdims=True))
    return p / jnp.sum(p, axis=-1, keepdims=True)


def _xattn_fwd(q, kv, name):
    T, D = q.shape
    M = kv.shape[0]
    xd = D // N_XATTN_HEADS
    tq = _tile(T, ROW_TILE)

    def body(q_ref, kv_ref, o_ref):
        for h in range(N_XATTN_HEADS):
            sl = slice(h * xd, (h + 1) * xd)
            p = _xattn_probs(q_ref[:, sl], kv_ref[:, sl], xd)
            o_ref[:, sl] = _dot(p.astype(BF16), kv_ref[:, D + h * xd:D + (h + 1) * xd], NN).astype(BF16)

    row = pl.BlockSpec((tq, D), lambda i: (i, 0))
    return _pcall(name, body, (T // tq,), [row, _full((M, 2 * D))], row, S((T, D), BF16))(q, kv)


def _xattn_bwd(q, kv, do, name):
    T, D = q.shape
    M = kv.shape[0]
    xd = D // N_XATTN_HEADS
    tq = _tile(T, ROW_TILE)
    scale = 1.0 / math.sqrt(xd)

    def body(q_ref, kv_ref, do_ref, dq_ref, dkv_ref):
        @pl.when(pl.program_id(0) == 0)
        def _():
            dkv_ref[...] = jnp.zeros_like(dkv_ref)

        for h in range(N_XATTN_HEADS):
            sl = slice(h * xd, (h + 1) * xd)
            vsl = slice(D + h * xd, D + (h + 1) * xd)
            qh = q_ref[:, sl]
            kh = kv_ref[:, sl]
            doh = do_ref[:, sl]
            p = _xattn_probs(qh, kh, xd)
            dp = _dot(doh, kv_ref[:, vsl], NT)
            ds = (p * (dp - jnp.sum(p * dp, axis=-1, keepdims=True)) * scale).astype(BF16)
            dq_ref[:, sl] = _dot(ds, kh, NN).astype(BF16)
            dkv_ref[:, sl] += _dot(ds, qh, TN)
            dkv_ref[:, vsl] += _dot(p.astype(BF16), doh, TN)

    row = pl.BlockSpec((tq, D), lambda i: (i, 0))
    return _pcall(name, body, (T // tq,), [row, _full((M, 2 * D)), row], [row, _full((M, 2 * D))],
                  [S((T, D), BF16), S((M, 2 * D), F32)])(q, kv, do)


def _adamw(w, m, v, g, name):
    shape = w.shape
    C = shape[-1]
    R = w.size // C
    tr = R if R <= 512 else _tile(R, 512)

    def body(w_ref, m_ref, v_ref, g_ref, d_ref, nm_ref, nv_ref):
        gv = g_ref[...]
        mv = ADAM_B1 * m_ref[...] + (1.0 - ADAM_B1) * gv
        vv = ADAM_B2 * v_ref[...] + (1.0 - ADAM_B2) * (gv * gv)
        m_hat = mv / (1.0 - ADAM_B1 ** ADAM_STEP)
        v_hat = vv / (1.0 - ADAM_B2 ** ADAM_STEP)
        d_ref[...] = -ADAM_LR * (m_hat / (jnp.sqrt(v_hat) + ADAM_EPS) + ADAM_WD * w_ref[...])
        nm_ref[...] = mv
        nv_ref[...] = vv

    blk = pl.BlockSpec((tr, C), lambda i: (i, 0))
    outs = _pcall(name, body, (R // tr,), [blk] * 4, [blk] * 3, [S((R, C), F32)] * 3)(
        w.reshape(R, C), m.reshape(R, C), v.reshape(R, C), g.reshape(R, C))
    return [o.reshape(shape) for o in outs]


def _place_scalars():
    return jnp.stack([lax.axis_index("c"), 2 * lax.axis_index("x") + lax.axis_index("y")]).astype(jnp.int32)


def _adamw_sum(w, m, v, owns, landed, name):
    L, p, q = w.shape
    qq = owns[0].shape[2]
    tr = next((t for t in range(min(p, ROW_TILE) // 16 * 16, 0, -16) if p % t == 0), p)

    def body(place_ref, w_ref, m_ref, v_ref, *rest):
        own_refs, land_refs = rest[:L], rest[L:2 * L]
        g_ref, d_ref, nm_ref, nv_ref = rest[2 * L:]
        chip = place_ref[1]

        def update(l):
            own = own_refs[l][...].astype(F32)
            gs = None
            for k in range(4):
                term = jnp.where(chip == k, own, land_refs[l][k].astype(F32))
                gs = term if gs is None else gs + term
            gv = gs[:, 0:q]
            mv = ADAM_B1 * m_ref[...] + (1.0 - ADAM_B1) * gv
            vv = ADAM_B2 * v_ref[...] + (1.0 - ADAM_B2) * (gv * gv)
            m_hat = mv / (1.0 - ADAM_B1 ** ADAM_STEP)
            v_hat = vv / (1.0 - ADAM_B2 ** ADAM_STEP)
            g_ref[...] = gv
            d_ref[...] = -ADAM_LR * (m_hat / (jnp.sqrt(v_hat) + ADAM_EPS) + ADAM_WD * w_ref[...])
            nm_ref[...] = mv
            nv_ref[...] = vv

        for l in range(L):
            pl.when(pl.program_id(0) == l)(lambda l=l: update(l))

    def rows_of(layer):
        return lambda l, i, place: jnp.where(l == layer, i, 0)

    blk = pl.BlockSpec((None, tr, q), lambda l, i, place: (l, i, 0))
    in_specs = [blk, blk, blk]
    in_specs += [pl.BlockSpec((None, tr, qq), lambda l, i, place, r=rows_of(k): (0, r(l, i, place), 0)) for k in range(L)]
    in_specs += [pl.BlockSpec((4, None, tr, qq), lambda l, i, place, r=rows_of(k): (0, 0, r(l, i, place), 0)) for k in range(L)]
    gs = pltpu.PrefetchScalarGridSpec(num_scalar_prefetch=1, grid=(L, p // tr), in_specs=in_specs, out_specs=[blk] * 4)
    return pl.pallas_call(body, grid_spec=gs, out_shape=[S((L, p, q), F32)] * 4, name=name,
                          compiler_params=pltpu.CompilerParams(dimension_semantics=("arbitrary", "arbitrary")))(
        _place_scalars(), w, m, v, *owns, *landed)


def _pair_add(g, recv, axis, name):
    _, _, p, q = recv.shape

    def body(place_ref, g_ref, r_ref, o_ref, own_ref):
        s = (g_ref[...].astype(F32) + r_ref[...].astype(F32)).astype(BF16)
        o_ref[...] = s

        @pl.when(pl.program_id(0) == place_ref[1])
        def _():
            own_ref[...] = s

    if axis == 1:
        gspec = pl.BlockSpec((None, p, q), lambda k, place: (0, 2 * k + place[0], 0))
    else:
        gspec = pl.BlockSpec((None, p, q), lambda k, place: (0, 0, 2 * k + place[0]))
    part = pl.BlockSpec((None, None, p, q), lambda k, place: (k, 0, 0, 0))
    own = pl.BlockSpec((None, p, q), lambda k, place: (0, 0, 0))
    gs = pltpu.PrefetchScalarGridSpec(num_scalar_prefetch=1, grid=(4,), in_specs=[gspec, part], out_specs=[part, own])
    return pl.pallas_call(body, grid_spec=gs, out_shape=[S((4, 1, p, q), BF16), S((1, p, q), BF16)], name=name,
                          compiler_params=pltpu.CompilerParams(dimension_semantics=("arbitrary",)))(_place_scalars(), g, recv)


def _win_pieces(n_attn, n_heads, n_conv, shard, chunk):
    bounds = [0, 3 * n_attn, 3 * n_attn + n_heads, 3 * n_attn + n_heads + 2 * n_conv]
    pieces = []
    for j in range(N_DEV):
        lo, hi = shard * j, shard * (j + 1)
        for r in range(3):
            a, b = max(lo, bounds[r]), min(hi, bounds[r + 1])
            if a < b:
                pieces.append((r, a - bounds[r], b - bounds[r], chunk * j + a - lo))
    return pieces


def _win_split(w_in, pieces, widths, name):
    L, D, C = w_in.shape
    tr = _tile(D, 256)

    def body(x_ref, *outs):
        outs[1][...] = jnp.zeros_like(outs[1])
        for r, d0, d1, s0 in pieces:
            outs[r][:, d0:d1] = x_ref[:, s0:s0 + d1 - d0]

    return _pcall(name, body, (L, D // tr), [pl.BlockSpec((None, tr, C), lambda l, i: (l, i, 0))],
                  [pl.BlockSpec((None, tr, wd), lambda l, i: (l, i, 0)) for wd in widths],
                  [S((L, D, wd), BF16) for wd in widths])(w_in)


def _win_merge(parts, pieces, chunked_cols, name):
    L, D, _ = parts[0].shape
    tr = _tile(D, 256)

    def body(a_ref, b_ref, c_ref, o_ref):
        ins = (a_ref, b_ref, c_ref)
        o_ref[...] = jnp.zeros_like(o_ref)
        for r, d0, d1, s0 in pieces:
            o_ref[:, s0:s0 + d1 - d0] = ins[r][:, d0:d1]

    return _pcall(name, body, (L, D // tr), [pl.BlockSpec((None, tr, x.shape[2]), lambda l, i: (l, i, 0)) for x in parts],
                  pl.BlockSpec((None, tr, chunked_cols), lambda l, i: (l, i, 0)), S((L, D, chunked_cols), BF16))(*parts)


def _place():
    return lax.axis_index("x"), lax.axis_index("y"), lax.axis_index("c")


def _flip(v, f):
    return 1 - v if f else v


def _window(ref, axis, size, dev):
    start = dev * size if isinstance(dev, int) else pl.multiple_of(dev * size, LANES if axis == 2 else 16)
    return ref.at[:, pl.ds(start, size), :] if axis == 1 else ref.at[:, :, pl.ds(start, size)]


HBM_SPEC = pl.BlockSpec(memory_space=pltpu.HBM)
SEM_SPEC = pl.BlockSpec(memory_space=pltpu.SEMAPHORE)
SPLIT_COPY_PARAMS = dict(has_side_effects=pltpu.SideEffectType.DATAFLOW_SIDE_EFFECTING)


def _hbm(v):
    return pltpu.with_memory_space_constraint(v, pltpu.HBM)


def _full_shape(shard, axis):
    return tuple(N_DEV * d if i == axis else d for i, d in enumerate(shard.shape))


def _ag_peers(x, y, c):
    return [(x, y, 1 - c), (1 - x, y, c), (x, 1 - y, c), (1 - x, 1 - y, c)]


SIBLING_COLLECTIVE_ID = 0


def _sibling_handshake(x, y, c):
    barrier = pltpu.get_barrier_semaphore()
    pl.semaphore_signal(barrier, inc=1, device_id=(x, y, 1 - c), device_id_type=MESH)
    pl.semaphore_wait(barrier, 1)


def _ag_start(shards, axes, groups, after, name):
    n, ng = len(shards), len(groups)
    sizes = [s.shape[ax] for s, ax in zip(shards, axes)]
    where = {w: (g, i) for g, members in enumerate(groups) for i, w in enumerate(members)}

    def body(*refs):
        xs, fulls = refs[:n], refs[n + 1:2 * n + 1]
        send, recv = refs[3 * n + 1:3 * n + 1 + ng], refs[3 * n + 1 + ng:]
        x, y, c = _place()
        for members in groups:
            for w in members:
                g, i = where[w]
                for k, to in enumerate(_ag_peers(x, y, c)):
                    pltpu.make_async_remote_copy(
                        src_ref=xs[w], dst_ref=_window(fulls[w], axes[w], sizes[w], 4 * x + 2 * y + c),
                        send_sem=send[g].at[4 * i + k], recv_sem=recv[g].at[4 * i + k], device_id=to, device_id_type=MESH).start()

    sems = [pltpu.SemaphoreType.DMA((4 * len(m),)) for m in groups]
    outs = pl.pallas_call(
        body, name=name,
        out_shape=[pltpu.HBM(_full_shape(s, ax), s.dtype) for s, ax in zip(shards, axes)] + [pltpu.HBM(s.shape, s.dtype) for s in shards]
        + sems + sems,
        in_specs=[HBM_SPEC] * n + [pl.BlockSpec(memory_space=pl.ANY)], out_specs=[HBM_SPEC] * (2 * n) + [SEM_SPEC] * (2 * ng),
        input_output_aliases={w: n + w for w in range(n)},
        compiler_params=pltpu.CompilerParams(**SPLIT_COPY_PARAMS))(*[_hbm(s) for s in shards], after)
    return outs[:n], outs[n:2 * n], outs[2 * n:2 * n + ng], outs[2 * n + ng:]


def _ag_wait(shards, fulls, send_sems, recv_sems, axes, after, name):
    n = len(shards)
    sizes = [s.shape[ax] for s, ax in zip(shards, axes)]

    def body(*refs):
        xs = refs[:n]
        send, recv = refs[2 * n], refs[2 * n + 1]
        landed = refs[3 * n + 3:]
        x, y, c = _place()
        for w in range(n):
            for k, frm in enumerate(_ag_peers(x, y, c)):
                copy = pltpu.make_async_remote_copy(
                    src_ref=xs[w], dst_ref=_window(landed[w], axes[w], sizes[w], 4 * frm[0] + 2 * frm[1] + frm[2]),
                    send_sem=send.at[4 * w + k], recv_sem=recv.at[4 * w + k], device_id=frm, device_id_type=MESH)
                copy.wait_send()
                copy.wait_recv()

    outs = pl.pallas_call(
        body, name=name, out_shape=[pltpu.HBM(v.shape, v.dtype) for v in list(shards) + list(fulls)],
        in_specs=[HBM_SPEC] * (2 * n) + [SEM_SPEC, SEM_SPEC, pl.BlockSpec(memory_space=pl.ANY)], out_specs=[HBM_SPEC] * (2 * n),
        input_output_aliases={i: i for i in range(2 * n)},
        compiler_params=pltpu.CompilerParams(**SPLIT_COPY_PARAMS))(*shards, *fulls, send_sems, recv_sems, after)
    return outs[:n], outs[n:]


def _ag_forward(fulls, shards, axes, name):
    n = len(fulls)
    sizes = [s.shape[ax] for s, ax in zip(shards, axes)]

    def body(*refs):
        xs, full_refs = refs[:n], refs[2 * n:3 * n]
        send_sems, recv_sems, local_sems = refs[3 * n:3 * n + 3]
        staged = refs[3 * n + 3:]
        x, y, c = _place()
        _sibling_handshake(x, y, c)
        chips = [(1 - x, y), (x, 1 - y), (1 - x, 1 - y)]
        loads = [pltpu.make_async_copy(xs[w], staged[w], local_sems.at[w]) for w in range(n)]
        for cp in loads:
            cp.start()
        copies = []
        for w in range(n):
            for j, (px, py) in enumerate(chips):
                sent = _window(full_refs[w], axes[w], sizes[w], 4 * px + 2 * py + c)
                got = _window(full_refs[w], axes[w], sizes[w], 4 * px + 2 * py + 1 - c)
                out = pltpu.make_async_remote_copy(src_ref=sent, dst_ref=sent, send_sem=send_sems.at[3 * w + j],
                                                   recv_sem=recv_sems.at[3 * w + j], device_id=(x, y, 1 - c), device_id_type=MESH)
                out.start()
                back = pltpu.make_async_remote_copy(src_ref=got, dst_ref=got, send_sem=send_sems.at[3 * w + j],
                                                    recv_sem=recv_sems.at[3 * w + j], device_id=(x, y, 1 - c), device_id_type=MESH)
                copies.append((out, back))
        stores = []
        for w in range(n):
            loads[w].wait()
            store = pltpu.make_async_copy(staged[w], _window(full_refs[w], axes[w], sizes[w], 4 * x + 2 * y + c), local_sems.at[w])
            store.start()
            stores.append(store)
        for out, back in copies:
            out.wait_send()
            back.wait_recv()
        for cp in stores:
            cp.wait()

    any_spec = pl.BlockSpec(memory_space=pl.ANY)
    outs = pl.pallas_call(
        body, name=name, out_shape=[S(f.shape, f.dtype) for f in fulls], in_specs=[any_spec] * (2 * n), out_specs=[any_spec] * n,
        input_output_aliases={n + w: w for w in range(n)},
        scratch_shapes=[pltpu.SemaphoreType.DMA((3 * n,)), pltpu.SemaphoreType.DMA((3 * n,)), pltpu.SemaphoreType.DMA((n,))]
        + [pltpu.VMEM(s.shape, s.dtype) for s in shards],
        compiler_params=pltpu.CompilerParams(collective_id=SIBLING_COLLECTIVE_ID))(*shards, *fulls)
    return outs


def _sibling_copy(g_ref, land_ref, send_sems, recv_sems, axis, size, w, k, x, y, c):
    return pltpu.make_async_remote_copy(
        src_ref=_window(g_ref, axis, size, 2 * k + 1 - c), dst_ref=land_ref.at[k], send_sem=send_sems.at[4 * w + k],
        recv_sem=recv_sems.at[4 * w + k], device_id=(x, y, 1 - c), device_id_type=MESH)


def _to_sibling_start(grads, axes, sizes, after, name):
    n = len(grads)
    landing = []
    for g, ax, sz in zip(grads, axes, sizes):
        L, K, N = g.shape
        landing.append(pltpu.HBM((4, L, sz, N) if ax == 1 else (4, L, K, sz), g.dtype))
    extra = [] if after is None else [after]

    def body(*refs):
        g_refs = refs[:n]
        land_refs = refs[n + len(extra):2 * n + len(extra)]
        send_sems, recv_sems, token = refs[3 * n + len(extra):]
        x, y, c = _place()
        _sibling_handshake(x, y, c)
        for w in range(n):
            for k in range(4):
                _sibling_copy(g_refs[w], land_refs[w], send_sems, recv_sems, axes[w], sizes[w], w, k, x, y, c).start()
        token[...] = jnp.zeros_like(token)

    sems = pltpu.SemaphoreType.DMA((4 * n,))
    outs = pl.pallas_call(
        body, name=name, out_shape=landing + [pltpu.HBM(g.shape, g.dtype) for g in grads] + [sems, sems, S((8, LANES), F32)],
        in_specs=[HBM_SPEC] * n + [pl.BlockSpec(memory_space=pl.ANY)] * len(extra),
        out_specs=[HBM_SPEC] * (2 * n) + [SEM_SPEC, SEM_SPEC, pl.BlockSpec(memory_space=pltpu.VMEM)],
        input_output_aliases={w: n + w for w in range(n)},
        compiler_params=pltpu.CompilerParams(collective_id=SIBLING_COLLECTIVE_ID, **SPLIT_COPY_PARAMS))(
        *[_hbm(g) for g in grads], *extra)
    return outs[:n], outs[n:2 * n], outs[2 * n], outs[2 * n + 1], outs[2 * n + 2]


def _to_sibling_wait(grads, landing, send_sems, recv_sems, axes, sizes, after, name):
    n = len(grads)

    def body(*refs):
        g_refs = refs[:n]
        send, recv = refs[2 * n], refs[2 * n + 1]
        landed = refs[3 * n + 3:]
        x, y, c = _place()
        for w in range(n):
            for k in range(4):
                copy = _sibling_copy(g_refs[w], landed[w], send, recv, axes[w], sizes[w], w, k, x, y, c)
                copy.wait_send()
                copy.wait_recv()

    outs = pl.pallas_call(
        body, name=name, out_shape=[pltpu.HBM(v.shape, v.dtype) for v in list(grads) + list(landing)],
        in_specs=[HBM_SPEC] * (2 * n) + [SEM_SPEC, SEM_SPEC, pl.BlockSpec(memory_space=pl.ANY)], out_specs=[HBM_SPEC] * (2 * n),
        input_output_aliases={i: i for i in range(2 * n)},
        compiler_params=pltpu.CompilerParams(**SPLIT_COPY_PARAMS))(*grads, *landing, send_sems, recv_sems, after)
    return outs[:n], outs[n:]


def _rs_copy(p_ref, out_ref, send_sems, recv_sems, w, rel, x, y, c):
    tx, ty = _flip(x, rel & 2), _flip(y, rel & 1)
    return pltpu.make_async_remote_copy(
        src_ref=p_ref.at[2 * tx + ty], dst_ref=out_ref.at[2 * x + y], send_sem=send_sems.at[3 * w + rel - 1],
        recv_sem=recv_sems.at[3 * w + rel - 1], device_id=(tx, ty, c), device_id_type=MESH)


def _rs_start(parts, name):
    n = len(parts)

    def body(*refs):
        p_refs, out_refs = refs[:n], refs[n:2 * n]
        send_sems, recv_sems, token = refs[3 * n:]
        x, y, c = _place()
        for w in range(n):
            for rel in (1, 2, 3):
                _rs_copy(p_refs[w], out_refs[w], send_sems, recv_sems, w, rel, x, y, c).start()
        token[...] = jnp.zeros_like(token)

    sems = pltpu.SemaphoreType.DMA((3 * n,))
    outs = pl.pallas_call(
        body, name=name,
        out_shape=[pltpu.HBM(p.shape, p.dtype) for p in parts] * 2 + [sems, sems, S((8, LANES), F32)],
        in_specs=[HBM_SPEC] * n, out_specs=[HBM_SPEC] * (2 * n) + [SEM_SPEC, SEM_SPEC, pl.BlockSpec(memory_space=pltpu.VMEM)],
        input_output_aliases={w: n + w for w in range(n)},
        compiler_params=pltpu.CompilerParams(**SPLIT_COPY_PARAMS))(*[_hbm(p) for p in parts])
    return outs[:n], outs[n:2 * n], outs[2 * n], outs[2 * n + 1], outs[2 * n + 2]


def _rs_wait(parts, landing, send_sems, recv_sems, after, name):
    n = len(parts)

    def body(*refs):
        p_refs = refs[:n]
        send, recv = refs[2 * n], refs[2 * n + 1]
        landed = refs[3 * n + 3:]
        x, y, c = _place()
        for w in range(n):
            for rel in (1, 2, 3):
                copy = _rs_copy(p_refs[w], landed[w], send, recv, w, rel, x, y, c)
                copy.wait_send()
                copy.wait_recv()

    outs = pl.pallas_call(
        body, name=name, out_shape=[pltpu.HBM(v.shape, v.dtype) for v in list(parts) + list(landing)],
        in_specs=[HBM_SPEC] * (2 * n) + [SEM_SPEC, SEM_SPEC, pl.BlockSpec(memory_space=pl.ANY)], out_specs=[HBM_SPEC] * (2 * n),
        input_output_aliases={i: i for i in range(2 * n)},
        compiler_params=pltpu.CompilerParams(**SPLIT_COPY_PARAMS))(*parts, *landing, send_sems, recv_sems, after)
    return outs[n:]


def _exchange_small(v, reduce, name, after=None):
    R, C = v.shape

    def body(v_ref, *rest):
        out_ref, gath, send_sems, recv_sems = rest[-4:]
        x, y, c = _place()
        me = 4 * x + 2 * y + c
        buf = gath if reduce else out_ref
        buf[me] = v_ref[...]
        copies = []
        for rel in range(1, N_DEV):
            peer = (_flip(x, rel & 4), _flip(y, rel & 2), _flip(c, rel & 1))
            copies.append(pltpu.make_async_remote_copy(
                src_ref=v_ref, dst_ref=buf.at[me], send_sem=send_sems.at[rel - 1], recv_sem=recv_sems.at[rel - 1],
                device_id=peer, device_id_type=MESH))
        for cp in copies:
            cp.start()
        for cp in copies:
            cp.wait()
        if reduce:
            acc = gath[0]
            for d in range(1, N_DEV):
                acc = acc + gath[d]
            out_ref[...] = acc

    vm = pl.BlockSpec(memory_space=pltpu.VMEM)
    extra = [] if after is None else [after]
    return pl.pallas_call(
        body, out_shape=S((R, C) if reduce else (N_DEV, R, C), F32), in_specs=[vm] + [pl.BlockSpec(memory_space=pl.ANY)] * len(extra),
        out_specs=vm, name=name,
        scratch_shapes=[pltpu.VMEM((N_DEV, R, C) if reduce else (8, LANES), F32), pltpu.SemaphoreType.DMA((N_DEV - 1,)),
                        pltpu.SemaphoreType.DMA((N_DEV - 1,))])(v, *extra)


def _pad_rows(flat, cols, mult):
    n = flat.shape[-1]
    rows = -(-n // cols)
    rows = -(-rows // mult) * mult
    pad = [(0, 0)] * (flat.ndim - 1) + [(0, rows * cols - n)]
    return jnp.pad(flat, pad).reshape(flat.shape[:-1] + (rows, cols))


def _round_up(n, m):
    return -(-n // m) * m


def _shard_axes(a):
    return [(2, _round_up(a[n].shape[2], LANES)) if kind == 'col' else (1, _round_up(a[n].shape[1 if kind == 'row' else 2], LANES))
            for n, kind in BIG]


def _pack_small(vals):
    rows = [_pad_rows(vals[n].astype(F32).reshape(-1), SMALL_COLS, 1) for n in SMALL]
    m = jnp.concatenate(rows, axis=0)
    return jnp.pad(m, ((0, -m.shape[0] % 8), (0, 0)))


def _unpack_small(m, a):
    out, r = {}, 0
    for n in SMALL:
        nr = -(-a[n].size // SMALL_COLS)
        out[n] = m[r:r + nr].reshape(-1)[:a[n].size].reshape(a[n].shape)
        r += nr
    return out, r


GROUPS = (('ffn1_w_gate', 'ffn1_w_up', 'ffn1_w_down'), ('w_in', 'w_out', 'xattn_w_q', 'xattn_w_kv', 'xattn_w_o'),
          ('ffn2_w_gate', 'ffn2_w_up', 'ffn2_w_down'))
GATHER_GROUPS = (('ffn1_w_gate', 'ffn1_w_up'), ('ffn1_w_down',)) + GROUPS[1:]


def _layer_small(a, conv_w_full, l):
    H = a['b_f'].shape[1]
    return dict(
        bft=jnp.pad(a['b_f'][l].reshape(H, 1), ((0, 16 - H), (0, 0))),
        cw=jnp.pad(conv_w_full[l], ((0, CONV_PAD - CONV_WIDTH), (0, 0))), cb=a['conv_b'][l].reshape(1, -1),
        lg=a['conv_ln_g'][l].reshape(1, -1), lb=a['conv_ln_b'][l].reshape(1, -1),
        ag=a['attn_out_g'][l].reshape(1, -1), cg=a['conv_out_g'][l].reshape(1, -1),
        g1=a['ffn1_norm_g'][l], gm=a['mix_norm_g'][l], gx=a['xattn_norm_g'][l], gmem=a['mem_norm_g'][l], g2=a['ffn2_norm_g'][l])


def _layer_fwd(x0, mem, w, fetch, cfg, l):
    T = x0.shape[0]
    H = cfg['heads']
    sv = {'x0': x0}
    m = fetch(l, 0, x0)
    w.update(wg1=(m['ffn1_w_gate'], 0), wu1=(m['ffn1_w_up'], 0))
    sv['h1'] = _rms_fwd(x0, w['g1'], f"l{l}_ffn1_norm")
    sv['G1'], sv['U1'], sv['A1'] = _ffn_up(sv['h1'], w['wg1'], w['wu1'], f"l{l}_ffn1_up")
    w.update(wd1=(fetch(l, 1, sv['A1'])['ffn1_w_down'], 0))
    x1 = sv['x1'] = _mm_res(sv['A1'], w['wd1'], x0, 0.5, f"l{l}_ffn1_down")
    m = fetch(l, 2, x1)
    wqkv, wf, wag = _win_split(m['w_in'], cfg['pieces'], cfg['widths'], f"l{l}_w_in_split")
    w.update(wqkv=(wqkv, 0), wft=wf[0, :, :16].T, wag=(wag, 0), wout=(m['w_out'], 0), wq=(m['xattn_w_q'], 0),
             wkv=(m['xattn_w_kv'], 0), wo=(m['xattn_w_o'], 0))
    h2 = sv['h2'] = _rms_fwd(x1, w['gm'], f"l{l}_mix_norm")
    sv['qkv'] = _mm(h2, w['wqkv'], BF16, f"l{l}_qkv_proj")
    sv['agv'] = _mm(h2, w['wag'], F32, f"l{l}_glu_proj")
    ct, sv['sg'] = _fox_prep(h2, w['wft'], w['bft'], f"l{l}_fox_prep")
    sv['c_col'] = ct[:H].reshape(H, T, 1)
    sv['c_row'] = ct[:H].reshape(H // 2, 2, T)
    sv['attn'], sv['lse'] = _fox_fwd(sv['qkv'], sv['c_col'], sv['c_row'], f"l{l}_fox_fwd")
    sv['yc'] = _conv_fwd(sv['agv'], w['cw'], w['cb'], f"l{l}_conv_fwd")
    sv['ycat'] = _mix_post(sv['attn'], sv['yc'], w['ag'], w['cg'], w['lg'], w['lb'], f"l{l}_mix_post")
    x2 = sv['x2'] = _mm_res(sv['ycat'], w['wout'], x1, 1.0, f"l{l}_out_proj")
    sv['h3'] = _rms_fwd(x2, w['gx'], f"l{l}_xattn_norm")
    sv['memn'] = _rms_fwd(mem, w['gmem'], f"l{l}_mem_norm")
    sv['q'] = _mm(sv['h3'], w['wq'], BF16, f"l{l}_xattn_q")
    sv['kv'] = _mm_nt(sv['memn'], w['wkv'], BF16, f"l{l}_xattn_kv")
    sv['o'] = _xattn_fwd(sv['q'], sv['kv'], f"l{l}_xattn_fwd")
    x3 = sv['x3'] = _mm_res(sv['o'], w['wo'], x2, 1.0, f"l{l}_xattn_out")
    m = fetch(l, 3, x3)
    w.update(wg2=(m['ffn2_w_gate'], 0), wu2=(m['ffn2_w_up'], 0), wd2=(m['ffn2_w_down'], 0))
    sv['h4'] = _rms_fwd(x3, w['g2'], f"l{l}_ffn2_norm")
    sv['G2'], sv['U2'], sv['A2'] = _ffn_up(sv['h4'], w['wg2'], w['wu2'], f"l{l}_ffn2_up")
    return _mm_res(sv['A2'], w['wd2'], x3, 0.5, f"l{l}_ffn2_down"), sv


def _ffn_bwd(dout, x_in, h, G, U, A, wg, wu, wd, g, tag, put, which, dep, flush, settle):
    dG, dU = _ffn_bwd_act(dout, wd, G, U, 0.5, tag + "_bwd_act", dep)
    settle(dG)
    put(which + '_w_down', A, dout, 0.5, tag + "_dwd")
    put(which + '_w_gate', dG, h, 1.0, tag + "_dwg")
    put(which + '_w_up', dU, h, 1.0, tag + "_dwu")
    dep = flush()
    dx, dg = _bwd_h([(dG, wg, 'nn'), (dU, wu, 'nn')], x_in, g, dout, tag + "_bwd_h", dep)
    token = settle(dx)
    return dx, dg, dep if token is None else token


def _layer_bwd(dx4, mem, w, sv, reduce, settle, cfg, l, dep):
    small, grads = {}, {}
    T = dx4.shape[0]
    H = cfg['heads']
    tokens = []

    def put(key, act, dy, scale, name):
        grads[key] = _wgrad(act, dy, scale, name, (None, 0, 1))

    def put_and_reduce(key, act, dy, scale, name):
        put(key, act, dy, scale, name)
        tokens.append(reduce(l, (key,), {key: grads.pop(key)}))

    dx3, small['ffn2_norm_g'], dep = _ffn_bwd(
        dx4, sv['x3'], sv['h4'], sv['G2'], sv['U2'], sv['A2'], w['wg2'], w['wu2'], w['wd2'], w['g2'], f"l{l}_ffn2", put, 'ffn2',
        dep, lambda: reduce(l, GROUPS[2], {n: grads.pop(n) for n in GROUPS[2]}), settle)
    do = _mm_nt(dx3, w['wo'], BF16, f"l{l}_xattn_do", dep)
    put('xattn_w_o', sv['o'], dx3, 1.0, f"l{l}_dwo")
    dq, dkv = _xattn_bwd(sv['q'], sv['kv'], do, f"l{l}_xattn_bwd")
    put('xattn_w_q', sv['h3'], dq, 1.0, f"l{l}_dwq")
    dx2, small['xattn_norm_g'] = _bwd_h([(dq, w['wq'], 'nt')], sv['x2'], w['gx'], dx3, f"l{l}_xattn_bwd_h")
    dmemn = _mm(dkv, w['wkv'], F32, f"l{l}_dmemn")
    put('xattn_w_kv', dkv, sv['memn'], 1.0, f"l{l}_dwkv")
    small['mem_norm_g'] = _rms_gain_grad(dmemn, mem, w['gmem'], f"l{l}_dgmem")
    dycat = _mm_nt(dx2, w['wout'], F32, f"l{l}_dycat")
    put('w_out', sv['ycat'], dx2, 1.0, f"l{l}_dwout")
    dattn, dyc, small['attn_out_g'], small['conv_out_g'], small['conv_ln_g'], small['conv_ln_b'] = _mix_post_bwd(
        dycat, sv['attn'], sv['yc'], w['ag'], w['cg'], w['lg'], w['lb'], f"l{l}_mix_post_bwd")
    dva, dga, dcw, small['conv_b'] = _conv_bwd(dyc, sv['agv'], w['cw'], f"l{l}_conv_bwd")
    dq_, dk_, dv_, dcs = _fox_bwd(sv['qkv'], sv['c_col'], sv['c_row'], sv['lse'], dattn, f"l{l}_fox_bwd")
    dcs16 = jnp.pad(dcs.reshape(H, T), ((0, 16 - H), (0, 0)))
    dflt, dwft, dbf = _fox_prep_bwd(dcs16, sv['sg'], sv['h2'], f"l{l}_fox_prep_bwd")
    small['b_f'] = dbf[:H].reshape(H)
    dqkv = jnp.concatenate([dq_, dk_, dv_], axis=1)
    dag = jnp.concatenate([dva, dga], axis=1)
    put('wqkv', sv['h2'], dqkv, 1.0, f"l{l}_dwqkv")
    put('wag', sv['h2'], dag, 1.0, f"l{l}_dwag")
    dx1, small['mix_norm_g'] = _bwd_h([(dqkv, w['wqkv'], 'nt'), (dag, w['wag'], 'nt'), (dflt, w['wft'], 'tn')],
                                      sv['x1'], w['gm'], dx2, f"l{l}_mix_bwd_h")
    dwf = jnp.pad(dwft[:H].T, ((0, 0), (0, LANES - H)))[None].astype(BF16)
    grads['w_in'] = _win_merge((grads.pop('wqkv'), dwf, grads.pop('wag')), cfg['pieces'], cfg['chunked_cols'], f"l{l}_w_in_merge")
    dep = reduce(l, GROUPS[1], {n: grads.pop(n) for n in GROUPS[1]})
    if l == 0:
        dx0, small['ffn1_norm_g'], dep = _ffn_bwd(
            dx1, sv['x0'], sv['h1'], sv['G1'], sv['U1'], sv['A1'], w['wg1'], w['wu1'], w['wd1'], w['g1'], f"l{l}_ffn1",
            put_and_reduce, 'ffn1', dep, lambda: tokens[-1], settle)
    else:
        dx0, small['ffn1_norm_g'], dep = _ffn_bwd(
            dx1, sv['x0'], sv['h1'], sv['G1'], sv['U1'], sv['A1'], w['wg1'], w['wu1'], w['wd1'], w['g1'], f"l{l}_ffn1", put, 'ffn1',
            dep, lambda: reduce(l, GROUPS[0], {n: grads.pop(n) for n in GROUPS[0]}), settle)
    small = {k: v.reshape(-1) for k, v in small.items()}
    return dx0, small, dcw[:CONV_WIDTH], dep


def _local_step(x, mem, tgt, a, conv_w_full, fetch, reduce, settle, cfg):
    L = a['b_f'].shape[0]
    ws = [_layer_small(a, conv_w_full, l) for l in range(L)]
    saved = []
    for l in range(L):
        x, sv = _layer_fwd(x, mem, ws[l], fetch, cfg, l)
        saved.append(sv)
    loss, dx, dgf = _loss_head(x, a['final_norm_g'], tgt, "loss_head")
    smalls, dcws, dep = [None] * L, [None] * L, None
    for l in range(L - 1, -1, -1):
        dx, smalls[l], dcws[l], dep = _layer_bwd(dx, mem, ws[l], saved[l], reduce, settle, cfg, l, dep)
    small = {n: jnp.stack([smalls[l][n] for l in range(L)]) for n in SMALL if n != 'final_norm_g'}
    small['final_norm_g'] = dgf.reshape(-1)
    return loss, dx, small, jnp.stack(dcws)


def kernel(x, mem, ffn1_norm_g, ffn1_w_gate, ffn1_w_up, ffn1_w_down, mix_norm_g, w_in, b_f, conv_w, conv_b, conv_ln_g, conv_ln_b, attn_out_g, conv_out_g, w_out, xattn_norm_g, mem_norm_g, xattn_w_q, xattn_w_kv, xattn_w_o, ffn2_norm_g, ffn2_w_gate, ffn2_w_up, ffn2_w_down, final_norm_g, loss_target, m_ffn1_norm_g, m_ffn1_w_gate, m_ffn1_w_up, m_ffn1_w_down, m_mix_norm_g, m_w_in, m_b_f, m_conv_w, m_conv_b, m_conv_ln_g, m_conv_ln_b, m_attn_out_g, m_conv_out_g, m_w_out, m_xattn_norm_g, m_mem_norm_g, m_xattn_w_q, m_xattn_w_kv, m_xattn_w_o, m_ffn2_norm_g, m_ffn2_w_gate, m_ffn2_w_up, m_ffn2_w_down, m_final_norm_g, v_ffn1_norm_g, v_ffn1_w_gate, v_ffn1_w_up, v_ffn1_w_down, v_mix_norm_g, v_w_in, v_b_f, v_conv_w, v_conv_b, v_conv_ln_g, v_conv_ln_b, v_attn_out_g, v_conv_out_g, v_w_out, v_xattn_norm_g, v_mem_norm_g, v_xattn_w_q, v_xattn_w_kv, v_xattn_w_o, v_ffn2_norm_g, v_ffn2_w_gate, v_ffn2_w_up, v_ffn2_w_down, v_final_norm_g):
    args = (x, mem, ffn1_norm_g, ffn1_w_gate, ffn1_w_up, ffn1_w_down, mix_norm_g, w_in, b_f, conv_w, conv_b, conv_ln_g, conv_ln_b, attn_out_g, conv_out_g, w_out, xattn_norm_g, mem_norm_g, xattn_w_q, xattn_w_kv, xattn_w_o, ffn2_norm_g, ffn2_w_gate, ffn2_w_up, ffn2_w_down, final_norm_g)
    moments_m = (m_ffn1_norm_g, m_ffn1_w_gate, m_ffn1_w_up, m_ffn1_w_down, m_mix_norm_g, m_w_in, m_b_f, m_conv_w, m_conv_b, m_conv_ln_g, m_conv_ln_b, m_attn_out_g, m_conv_out_g, m_w_out, m_xattn_norm_g, m_mem_norm_g, m_xattn_w_q, m_xattn_w_kv, m_xattn_w_o, m_ffn2_norm_g, m_ffn2_w_gate, m_ffn2_w_up, m_ffn2_w_down, m_final_norm_g)
    moments_v = (v_ffn1_norm_g, v_ffn1_w_gate, v_ffn1_w_up, v_ffn1_w_down, v_mix_norm_g, v_w_in, v_b_f, v_conv_w, v_conv_b, v_conv_ln_g, v_conv_ln_b, v_attn_out_g, v_conv_out_g, v_w_out, v_xattn_norm_g, v_mem_norm_g, v_xattn_w_q, v_xattn_w_kv, v_xattn_w_o, v_ffn2_norm_g, v_ffn2_w_gate, v_ffn2_w_up, v_ffn2_w_down, v_final_norm_g)
    a = dict(zip(NAMES, args))
    am = dict(zip(WEIGHTS, moments_m))
    av = dict(zip(WEIGHTS, moments_v))
    L, taps, cshard = conv_w.shape
    dev = 4 * lax.axis_index("x") + 2 * lax.axis_index("y") + lax.axis_index("c")

    big_names = [n for n, _ in BIG]
    geometry = dict(zip(big_names, _shard_axes(a)))
    n_attn, n_heads, n_conv = attn_out_g.shape[1], b_f.shape[1], conv_out_g.shape[1]
    chunk = geometry['w_in'][1]
    cfg = dict(heads=n_heads, pieces=_win_pieces(n_attn, n_heads, n_conv, w_in.shape[2], chunk),
               widths=(3 * n_attn, LANES, 2 * n_conv), chunked_cols=N_DEV * chunk)

    cw_rows = _pad_rows(conv_w.reshape(-1), LANES, 8)
    cw_all = _exchange_small(cw_rows, False, "allgather_conv_w")
    conv_w_full = cw_all.reshape(N_DEV, -1)[:, :conv_w.size].reshape(N_DEV, L, taps, cshard).transpose(1, 2, 0, 3).reshape(
        L, taps, N_DEV * cshard)

    keys = [(l, n) for l in range(L) for names in GATHER_GROUPS for n in names]
    members = [[keys.index((l, n)) for n in names] for l in range(L) for names in GATHER_GROUPS]
    shards = []
    for l, n in keys:
        ax, size = geometry[n]
        shard = _as_handled(n, a[n][l:l + 1]).astype(BF16)
        pad = [(0, 0)] * 3
        pad[ax] = (0, size - shard.shape[ax])
        shards.append(jnp.pad(shard, pad))
    key_axes = [geometry[n][0] for _, n in keys]
    fulls, thru, ag_send, ag_recv = _ag_start(shards, key_axes, members, cw_all, "allgather_start")

    def fetch(l, gi, after):
        g = l * len(GATHER_GROUPS) + gi
        axs = [key_axes[i] for i in members[g]]
        own, landed = _ag_wait([thru[i] for i in members[g]], [fulls[i] for i in members[g]], ag_send[g], ag_recv[g], axs, after,
                               f"allgather_wait_l{l}g{gi}")
        return dict(zip(GATHER_GROUPS[gi], _ag_forward(landed, own, axs, f"allgather_forward_l{l}g{gi}")))

    pending, own_part, landed_part, in_flight = [], {}, {}, []

    def finish_exchange(after):
        if not in_flight:
            return None
        l, names, g_thru, landing, send, recv_sems = in_flight.pop()
        tag = f"l{l}_{names[0]}_{len(names)}"
        axs = [geometry[n][0] for n in names]
        g_done, recv = _to_sibling_wait(g_thru, landing, send, recv_sems, axs, [geometry[n][1] for n in names], after,
                                       "reduce_sibling_wait_" + tag)
        parts = []
        for n, g, r, ax in zip(names, g_done, recv, axs):
            part, own_part[(l, n)] = _pair_add(g, r, ax, f"reduce_pair_add_l{l}_{n}")
            parts.append(part)
        landing, parts_thru, send, recv_sems, token = _rs_start(parts, "reduce_start_" + tag)
        pending.append((l, names, parts_thru, landing, send, recv_sems))
        return token

    def reduce(l, names, grads):
        gl = [grads[n] for n in names]
        token = finish_exchange(gl[0])
        landing, g_thru, send, recv_sems, token = _to_sibling_start(
            gl, [geometry[n][0] for n in names], [geometry[n][1] for n in names], token,
            f"reduce_sibling_start_l{l}_{names[0]}_{len(names)}")
        in_flight.append((l, names, g_thru, landing, send, recv_sems))
        return token

    loss, grad_x, gsmall, dcw = _local_step(x[0], mem[0], loss_target[0], a, conv_w_full, fetch, reduce, finish_exchange, cfg)
    finish_exchange(grad_x)

    def wait_group(entry, after):
        l, names, parts_thru, landing, send, recv_sems = entry
        landed = _rs_wait(parts_thru, landing, send, recv_sems, after, f"reduce_wait_l{l}_{names[0]}_{len(names)}")
        for n, arr in zip(names, landed):
            landed_part[(l, n)] = arr

    for entry in pending[:-1]:
        wait_group(entry, grad_x)

    grads, delta, new_m, new_v = {}, {}, {}, {}

    def update(n):
        outs = _adamw_sum(_as_handled(n, a[n]), _as_handled(n, am[n]), _as_handled(n, av[n]),
                          [own_part[(l, n)] for l in range(L)], [landed_part[(l, n)] for l in range(L)], "adamw_" + n)
        grads[n], delta[n], new_m[n], new_v[n] = (_as_handled(n, o) for o in outs)

    last_names = pending[-1][1]
    early = [n for n in big_names if n not in last_names]
    for n in early:
        update(n)
    wait_group(pending[-1], delta[early[-1]])
    for n in last_names:
        update(n)

    small_rows = _pack_small(gsmall)
    n_small = small_rows.shape[0]
    dcw_rows = jnp.pad(dcw, ((0, 0), (0, CONV_PAD - taps), (0, 0))).reshape(-1, SMALL_COLS)
    summed = _exchange_small(jnp.concatenate([small_rows, dcw_rows], axis=0), True, "allreduce_small",
                             after=landed_part[(pending[-1][0], last_names[0])])
    g_small, _ = _unpack_small(summed[:n_small], a)
    dcw_sum = summed[n_small:].reshape(L, CONV_PAD, N_DEV * cshard)[:, :taps]
    grads.update(g_small)
    grads['conv_w'] = lax.dynamic_slice_in_dim(dcw_sum, dev * cshard, cshard, axis=2)
    delta['conv_w'], new_m['conv_w'], new_v['conv_w'] = _adamw(conv_w, am['conv_w'], av['conv_w'], grads['conv_w'], "adamw_conv_w")
    pw, pm, pv, pg = (_pack_small(d) for d in (a, am, av, g_small))
    for dst, packed in zip((delta, new_m, new_v), _adamw(pw, pm, pv, pg, "adamw_small")):
        dst.update(_unpack_small(packed, a)[0])

    total = lax.psum(loss.reshape(()), ("x", "y", "c"))
    return (total, grad_x[None], *[grads[n] for n in WEIGHTS], *[delta[n] for n in WEIGHTS], *[new_m[n] for n in WEIGHTS],
            *[new_v[n] for n in WEIGHTS])
```

```python
import math

import jax
import jax.numpy as jnp
from jax import lax
from jax.experimental import pallas as pl
from jax.experimental.pallas import tpu as pltpu

F32, BF16 = jnp.float32, jnp.bfloat16
S = jax.ShapeDtypeStruct
MESH = pl.DeviceIdType.MESH

EPS = 1e-6
NEG_INF = -1e30
HEAD_DIM = 64
N_XATTN_HEADS = 4
CONV_WIDTH = 31
CONV_PAD = 32
LANES = 128
ADAM_LR, ADAM_B1, ADAM_B2, ADAM_EPS, ADAM_WD, ADAM_STEP = 0.001, 0.9, 0.999, 1e-08, 0.01, 10
N_DEV = 8
VMEM_LIMIT_BYTES = 56 * 1024 * 1024
ROW_TILE = 512
MM_ROW_TILE = 1024
EPILOGUE_COLS = 512
SMALL_COLS = 512

NN = ((1,), (0,))
NT = ((1,), (1,))
TN = ((0,), (0,))

NAMES = ['x', 'mem', 'ffn1_norm_g', 'ffn1_w_gate', 'ffn1_w_up', 'ffn1_w_down', 'mix_norm_g', 'w_in', 'b_f', 'conv_w', 'conv_b',
         'conv_ln_g', 'conv_ln_b', 'attn_out_g', 'conv_out_g', 'w_out', 'xattn_norm_g', 'mem_norm_g', 'xattn_w_q', 'xattn_w_kv',
         'xattn_w_o', 'ffn2_norm_g', 'ffn2_w_gate', 'ffn2_w_up', 'ffn2_w_down', 'final_norm_g']
WEIGHTS = NAMES[2:]
BIG = [('ffn1_w_gate', 'colT'), ('ffn1_w_up', 'colT'), ('ffn1_w_down', 'row'), ('w_in', 'col'), ('w_out', 'row'),
       ('xattn_w_q', 'row'), ('xattn_w_kv', 'colT'), ('xattn_w_o', 'row'), ('ffn2_w_gate', 'colT'), ('ffn2_w_up', 'colT'),
       ('ffn2_w_down', 'row')]
TRANSPOSED = tuple(n for n, kind in BIG if kind == 'colT')


def _as_handled(n, v):
    return jnp.swapaxes(v, 1, 2) if n in TRANSPOSED else v
SMALL = ['ffn1_norm_g', 'mix_norm_g', 'xattn_norm_g', 'mem_norm_g', 'ffn2_norm_g', 'conv_b', 'conv_ln_g', 'conv_ln_b',
         'attn_out_g', 'conv_out_g', 'b_f', 'final_norm_g']


def _dot(a, b, dims):
    return lax.dot_general(a, b, (dims, ((), ())), preferred_element_type=F32)


def _full(shape):
    nd = len(shape)
    return pl.BlockSpec(shape, lambda *_: (0,) * nd)


def _tile(n, pref):
    for t in (pref, 512, 384, 256, 128, 64, 32, 16, 8):
        if t <= n and n % t == 0:
            return t
    return n


def _pcall(name, body, grid, in_specs, out_specs, out_shape, scratch=(), aliases=None, dep=None):
    n_in = len(in_specs)
    kernel_body = body
    if dep is not None:
        in_specs = list(in_specs) + [pl.BlockSpec(memory_space=pl.ANY)]

        def kernel_body(*refs):
            return body(*refs[:n_in], *refs[n_in + 1:])

    call = pl.pallas_call(
        kernel_body, grid=grid, in_specs=in_specs, out_specs=out_specs, out_shape=out_shape, scratch_shapes=list(scratch),
        name=name, input_output_aliases=aliases or {},
        compiler_params=pltpu.CompilerParams(dimension_semantics=("arbitrary",) * len(grid), vmem_limit_bytes=VMEM_LIMIT_BYTES))
    return call if dep is None else (lambda *args: call(*args, dep))


def _arr(w):
    return w[0] if isinstance(w, tuple) else w


def _wshape(w):
    return w[0].shape[1:] if isinstance(w, tuple) else w.shape


def _wspec(w, block, imap):
    if isinstance(w, tuple):
        layer = w[1]
        return pl.BlockSpec((None,) + block, lambda *g: (layer,) + imap(*g))
    return pl.BlockSpec(block, imap)


def _wfull(w):
    shape = _wshape(w)
    return _wspec(w, shape, lambda *_: (0,) * len(shape))


def _sigmoid(z):
    return jax.nn.sigmoid(z)


def _rstd(x):
    return lax.rsqrt(jnp.mean(x * x, axis=-1, keepdims=True) + EPS)


def _rms_bwd(dy, x, g):
    r = _rstd(x)
    xh = x * r
    u = dy * g
    dx = r * (u - xh * jnp.mean(u * xh, axis=-1, keepdims=True))
    return dx, dy * xh


def _colsum(v):
    return jnp.sum(v, axis=0, keepdims=True)


def _rms_fwd(x, g, name):
    T, D = x.shape
    tm = _tile(T, ROW_TILE)

    def body(x_ref, g_ref, h_ref):
        xv = x_ref[...]
        h_ref[...] = (xv * _rstd(xv) * g_ref[...]).astype(BF16)

    row = pl.BlockSpec((tm, D), lambda i: (i, 0))
    return _pcall(name, body, (T // tm,), [row, _full((1, D))], row, S((T, D), BF16))(x, g.reshape(1, D))


def _mm(a, w, out_dtype, name):
    M, K = a.shape
    N = _wshape(w)[1]
    tm = _tile(M, ROW_TILE)
    tn = N if N <= 1536 else N // 2

    def body(a_ref, w_ref, o_ref):
        o_ref[...] = _dot(a_ref[...].astype(BF16), w_ref[...], NN).astype(out_dtype)

    return _pcall(name, body, (N // tn, M // tm),
                  [pl.BlockSpec((tm, K), lambda j, i: (i, 0)), _wspec(w, (K, tn), lambda j, i: (0, j))],
                  pl.BlockSpec((tm, tn), lambda j, i: (i, j)), S((M, N), out_dtype))(a, _arr(w))


def _mm_res(a, w, res, scale, name):
    M, K = a.shape
    N = _wshape(w)[1]
    tm = _tile(M, ROW_TILE)

    def body(a_ref, w_ref, r_ref, o_ref):
        o_ref[...] = r_ref[...] + scale * _dot(a_ref[...], w_ref[...], NN)

    row = pl.BlockSpec((tm, N), lambda i: (i, 0))
    return _pcall(name, body, (M // tm,), [pl.BlockSpec((tm, K), lambda i: (i, 0)), _wfull(w), row], row,
                  S((M, N), F32))(a, _arr(w), res)


def _mm_nt(a, w, out_dtype, name, dep=None):
    M, K = a.shape
    N = _wshape(w)[0]
    tm = _tile(M, ROW_TILE)
    tn = N if N <= 1536 else N // 2

    def body(a_ref, w_ref, o_ref):
        o_ref[...] = _dot(a_ref[...].astype(BF16), w_ref[...], NT).astype(out_dtype)

    return _pcall(name, body, (N // tn, M // tm),
                  [pl.BlockSpec((tm, K), lambda j, i: (i, 0)), _wspec(w, (tn, K), lambda j, i: (j, 0))],
                  pl.BlockSpec((tm, tn), lambda j, i: (i, j)), S((M, N), out_dtype), dep=dep)(a, _arr(w))


def _wgrad(a, dy, scale, name, into):
    buf, layer, L = into
    T, M = a.shape
    N = dy.shape[1]
    tm = _tile(M, ROW_TILE)

    def body(a_ref, dy_ref, *rest):
        rest[-1][...] = (scale * _dot(a_ref[...].astype(BF16), dy_ref[...].astype(BF16), TN)).astype(BF16)

    in_specs = [pl.BlockSpec((T, tm), lambda i: (0, i)), _full((T, N))]
    args = [a, dy]
    if buf is not None:
        in_specs.append(pl.BlockSpec(memory_space=pl.ANY))
        args.append(buf)
    return _pcall(name, body, (M // tm,), in_specs, pl.BlockSpec((None, tm, N), lambda i: (layer, i, 0)), S((L, M, N), BF16),
                  aliases={2: 0} if buf is not None else None)(*args)


def _bwd_h(dots, x, g, dres, name, dep=None):
    T, D = x.shape
    tm = _tile(T, 256)
    n = len(dots)
    dims = [{'nt': NT, 'nn': NN, 'tn': TN}[m] for _, _, m in dots]

    def body(*refs):
        x_ref, g_ref, r_ref, dx_ref, dg_ref = refs[2 * n:]
        dh = None
        for k in range(n):
            part = _dot(refs[2 * k][...], refs[2 * k + 1][...], dims[k])
            dh = part if dh is None else dh + part
        dx, dgrow = _rms_bwd(dh, x_ref[...], g_ref[...])
        dx_ref[...] = r_ref[...] + dx

        @pl.when(pl.program_id(0) == 0)
        def _():
            dg_ref[...] = jnp.zeros_like(dg_ref)

        dg_ref[...] += _colsum(dgrow)

    in_specs, args = [], []
    for lhs, w, mode in dots:
        if mode == 'tn':
            in_specs.append(pl.BlockSpec((lhs.shape[0], tm), lambda i: (0, i)))
        else:
            in_specs.append(pl.BlockSpec((tm, lhs.shape[1]), lambda i: (i, 0)))
        in_specs.append(_wfull(w))
        args += [lhs, _arr(w)]
    row = pl.BlockSpec((tm, D), lambda i: (i, 0))
    in_specs += [row, _full((1, D)), row]
    return _pcall(name, body, (T // tm,), in_specs, [row, _full((1, D))], [S((T, D), F32), S((1, D), F32)], dep=dep)(
        *args, x, g.reshape(1, D), dres)


def _rms_gain_grad(dy, x, g, name):
    T, D = x.shape

    def body(dy_ref, x_ref, g_ref, dg_ref):
        _, dgrow = _rms_bwd(dy_ref[...], x_ref[...], g_ref[...])
        dg_ref[...] = _colsum(dgrow)

    return _pcall(name, body, (), [_full((T, D)), _full((T, D)), _full((1, D))], _full((1, D)), S((1, D), F32))(
        dy, x, g.reshape(1, D))


def _ffn_up(h, wg, wu, name):
    T, D = h.shape
    Fh = _wshape(wg)[0]
    tm = _tile(T, MM_ROW_TILE)
    tn = Fh if Fh <= 1536 else Fh // 2
    tc = _tile(tn, EPILOGUE_COLS)

    def body(h_ref, wg_ref, wu_ref, g_ref, u_ref, a_ref):
        hv = h_ref[...]
        for cb in range(tn // tc):
            cols = slice(cb * tc, (cb + 1) * tc)
            gv = _dot(hv, wg_ref[cols, :], NT)
            uv = _dot(hv, wu_ref[cols, :], NT)
            g_ref[:, cols] = gv.astype(BF16)
            u_ref[:, cols] = uv.astype(BF16)
            a_ref[:, cols] = (gv * _sigmoid(gv) * uv).astype(BF16)

    tile = pl.BlockSpec((tm, tn), lambda j, i: (i, j))
    return _pcall(name, body, (Fh // tn, T // tm),
                  [pl.BlockSpec((tm, D), lambda j, i: (i, 0)), _wspec(wg, (tn, D), lambda j, i: (j, 0)),
                   _wspec(wu, (tn, D), lambda j, i: (j, 0))],
                  [tile, tile, tile], [S((T, Fh), BF16)] * 3)(h, _arr(wg), _arr(wu))


def _ffn_bwd_act(dout, wd, gate, up, scale, name, dep=None):
    T, D = dout.shape
    Fh = _wshape(wd)[0]
    tm = _tile(T, MM_ROW_TILE)
    tn = Fh if Fh <= 1536 else Fh // 2
    tc = _tile(tn, EPILOGUE_COLS)

    def body(d_ref, w_ref, g_ref, u_ref, dg_ref, du_ref):
        dv = d_ref[...].astype(BF16)
        for cb in range(tn // tc):
            cols = slice(cb * tc, (cb + 1) * tc)
            da = scale * _dot(dv, w_ref[cols, :], NT)
            gv = g_ref[:, cols].astype(F32)
            uv = u_ref[:, cols].astype(F32)
            sg = _sigmoid(gv)
            dg_ref[:, cols] = (da * uv * (sg * (1.0 + gv * (1.0 - sg)))).astype(BF16)
            du_ref[:, cols] = (da * (gv * sg)).astype(BF16)

    tile = pl.BlockSpec((tm, tn), lambda j, i: (i, j))
    return _pcall(name, body, (Fh // tn, T // tm),
                  [pl.BlockSpec((tm, D), lambda j, i: (i, 0)), _wspec(wd, (tn, D), lambda j, i: (j, 0)), tile, tile],
                  [tile, tile], [S((T, Fh), BF16)] * 2, dep=dep)(dout, _arr(wd), gate, up)


def _loss_head(x, g, tgt, name):
    T, D = x.shape
    tm = _tile(T, ROW_TILE)

    def body(x_ref, g_ref, t_ref, loss_ref, dx_ref, dg_ref):
        xv = x_ref[...]
        gv = g_ref[...]
        r = _rstd(xv)
        xh = xv * r
        e = xh * gv - t_ref[...]
        dy = e * (1.0 / D)
        u = dy * gv
        dx_ref[...] = r * (u - xh * jnp.mean(u * xh, axis=-1, keepdims=True))

        @pl.when(pl.program_id(0) == 0)
        def _():
            dg_ref[...] = jnp.zeros_like(dg_ref)
            loss_ref[...] = jnp.zeros_like(loss_ref)

        dg_ref[...] += _colsum(dy * xh)
        loss_ref[...] += 0.5 * _colsum(jnp.mean(e * e, axis=-1, keepdims=True))

    row = pl.BlockSpec((tm, D), lambda i: (i, 0))
    return _pcall(name, body, (T // tm,), [row, _full((1, D)), row], [_full((1, 1)), row, _full((1, D))],
                  [S((1, 1), F32), S((T, D), F32), S((1, D), F32)])(x, g.reshape(1, D), tgt)


def _split3(xb):
    hi = xb.astype(BF16)
    r1 = xb - hi.astype(F32)
    mid = r1.astype(BF16)
    lo = (r1 - mid.astype(F32)).astype(BF16)
    return hi, mid, lo


def _fox_prep(h, wft, bft, name):
    T, D = h.shape
    blk = _tile(T, 256)

    def body(h_ref, w_ref, b_ref, ct_ref, sg_ref):
        z = _dot(w_ref[...], h_ref[...], NT) + b_ref[...]
        sg_ref[...] = 1.0 - _sigmoid(z)
        logf = jnp.minimum(z, 0.0) - jnp.log1p(jnp.exp(-jnp.abs(z)))
        upper = (lax.broadcasted_iota(jnp.int32, (blk, blk), 0) <= lax.broadcasted_iota(jnp.int32, (blk, blk), 1)).astype(BF16)
        carry = jnp.zeros((16, 1), F32)
        for b in range(T // blk):
            hi, mid, lo = _split3(logf[:, b * blk:(b + 1) * blk])
            cb = _dot(hi, upper, NN) + _dot(mid, upper, NN) + _dot(lo, upper, NN) + carry
            ct_ref[:, b * blk:(b + 1) * blk] = cb
            carry = cb[:, blk - 1:blk]

    return _pcall(name, body, (), [_full((T, D)), _full((16, D)), _full((16, 1))], [_full((16, T)), _full((16, T))],
                  [S((16, T), F32), S((16, T), F32)])(h, wft, bft)


def _fox_prep_bwd(dcs, sg, h, name):
    T, D = h.shape
    blk = _tile(T, 256)
    nb = T // blk

    def body(dcs_ref, sg_ref, h_ref, dfl_ref, dw_ref, db_ref):
        lower = (lax.broadcasted_iota(jnp.int32, (blk, blk), 0) >= lax.broadcasted_iota(jnp.int32, (blk, blk), 1)).astype(BF16)
        carry = jnp.zeros((16, 1), F32)
        db = jnp.zeros((16, 1), F32)
        for b in range(nb - 1, -1, -1):
            cols = slice(b * blk, (b + 1) * blk)
            hi, mid, lo = _split3(-dcs_ref[:, cols])
            dlogf = _dot(hi, lower, NN) + _dot(mid, lower, NN) + _dot(lo, lower, NN) + carry
            carry = dlogf[:, 0:1]
            dfl = dlogf * sg_ref[:, cols]
            db = db + jnp.sum(dfl, axis=-1, keepdims=True)
            dfl_ref[:, cols] = dfl.astype(BF16)
        db_ref[...] = db
        dw_ref[...] = _dot(dfl_ref[...], h_ref[...], NN)

    return _pcall(name, body, (), [_full((16, T)), _full((16, T)), _full((T, D))],
                  [_full((16, T)), _full((16, D)), _full((16, 1))],
                  [S((16, T), BF16), S((16, D), F32), S((16, 1), F32)])(dcs, sg, h)


def _fox_logits(q, k, c_col, c_row):
    tq, kp = q.shape[0], k.shape[0]
    s = _dot(q, k, NT) * (1.0 / math.sqrt(HEAD_DIM)) + (c_col - c_row)
    row = lax.broadcasted_iota(jnp.int32, (tq, tq), 0)
    col = lax.broadcasted_iota(jnp.int32, (tq, tq), 1)
    diag = jnp.where(row >= col, s[:, kp - tq:], NEG_INF)
    return diag if kp == tq else jnp.concatenate([s[:, :kp - tq], diag], axis=1)


def _fox_specs(T, n_pairs):
    qs = pl.BlockSpec((T, LANES), lambda p: (0, p))
    ks = pl.BlockSpec((T, LANES), lambda p: (0, n_pairs + p))
    vs = pl.BlockSpec((T, LANES), lambda p: (0, 2 * n_pairs + p))
    col = pl.BlockSpec((2, T, 1), lambda p: (p, 0, 0))
    rowv = pl.BlockSpec((None, 2, T), lambda p: (p, 0, 0))
    return qs, ks, vs, col, rowv


def _fox_fwd(qkv, c_col, c_row, name):
    T = qkv.shape[0]
    DA = qkv.shape[1] // 3
    n_pairs = DA // LANES
    tq = _tile(T, 256)

    def body(q_ref, k_ref, v_ref, c_ref, ct_ref, o_ref, lse_ref):
        for hh in range(2):
            sl = slice(hh * HEAD_DIM, (hh + 1) * HEAD_DIM)
            for i in range(T // tq):
                rows = slice(i * tq, (i + 1) * tq)
                kp = (i + 1) * tq
                s = _fox_logits(q_ref[rows, sl], k_ref[0:kp, sl], c_ref[hh, rows, :], ct_ref[hh:hh + 1, 0:kp])
                m = jnp.max(s, axis=-1, keepdims=True)
                p = jnp.exp(s - m)
                l = jnp.sum(p, axis=-1, keepdims=True)
                o_ref[rows, sl] = _dot(p.astype(BF16), v_ref[0:kp, sl], NN) / l
                lse_ref[hh, rows, :] = m + jnp.log(l)

    qs, ks, vs, col, rowv = _fox_specs(T, n_pairs)
    return _pcall(name, body, (n_pairs,), [qs, ks, vs, col, rowv], [qs, col],
                  [S((T, DA), F32), S((2 * n_pairs, T, 1), F32)])(qkv, qkv, qkv, c_col, c_row)


def _fox_bwd(qkv, c_col, c_row, lse, do, name):
    T = qkv.shape[0]
    DA = qkv.shape[1] // 3
    n_pairs = DA // LANES
    tq = _tile(T, 256)
    scale = 1.0 / math.sqrt(HEAD_DIM)

    def body(q_ref, k_ref, v_ref, c_ref, ct_ref, lse_ref, do_ref, dq_ref, dk_ref, dv_ref, dcs_ref, dk_acc, dv_acc):
        dk_acc[...] = jnp.zeros_like(dk_acc)
        dv_acc[...] = jnp.zeros_like(dv_acc)
        dcs_ref[...] = jnp.zeros_like(dcs_ref)
        for hh in range(2):
            sl = slice(hh * HEAD_DIM, (hh + 1) * HEAD_DIM)
            for i in range(T // tq):
                rows = slice(i * tq, (i + 1) * tq)
                kp = (i + 1) * tq
                q = q_ref[rows, sl]
                k = k_ref[0:kp, sl]
                dob = do_ref[rows, sl]
                s = _fox_logits(q, k, c_ref[hh, rows, :], ct_ref[hh:hh + 1, 0:kp])
                p = jnp.exp(s - lse_ref[hh, rows, :])
                dp = _dot(dob, v_ref[0:kp, sl], NT)
                ds = p * (dp - jnp.sum(p * dp, axis=-1, keepdims=True))
                dsb = ds.astype(BF16)
                dq_ref[rows, sl] = (_dot(dsb, k, NN) * scale).astype(BF16)
                dk_acc[0:kp, sl] += _dot(dsb, q, TN) * scale
                dv_acc[0:kp, sl] += _dot(p.astype(BF16), dob, TN)
                dcs_ref[hh:hh + 1, 0:kp] += _colsum(ds)
        dk_ref[...] = dk_acc[...].astype(BF16)
        dv_ref[...] = dv_acc[...].astype(BF16)

    qs, ks, vs, col, rowv = _fox_specs(T, n_pairs)
    return _pcall(name, body, (n_pairs,), [qs, ks, vs, col, rowv, col, qs], [qs, qs, qs, rowv],
                  [S((T, DA), BF16)] * 3 + [S((n_pairs, 2, T), F32)],
                  scratch=[pltpu.VMEM((T, LANES), F32), pltpu.VMEM((T, LANES), F32)])(qkv, qkv, qkv, c_col, c_row, lse, do)


def _conv_fwd(ag, w, b, name):
    T = ag.shape[0]
    DC = ag.shape[1] // 2
    nb = DC // LANES
    tr = _tile(T, 256)

    def body(a_ref, g_ref, w_ref, b_ref, y_ref, pad):
        pad[0:CONV_PAD, :] = jnp.zeros((CONV_PAD, LANES), F32)
        pad[CONV_PAD:CONV_PAD + T, :] = a_ref[...] * _sigmoid(g_ref[...])
        for r in range(T // tr):
            acc = jnp.zeros((tr, LANES), F32) + b_ref[...]
            for j in range(CONV_WIDTH):
                o = r * tr + CONV_PAD - (CONV_WIDTH - 1) + j
                acc = acc + w_ref[j:j + 1, :] * pad[o:o + tr, :]
            y_ref[r * tr:(r + 1) * tr, :] = acc

    blk = pl.BlockSpec((T, LANES), lambda c: (0, c))
    return _pcall(name, body, (nb,), [blk, pl.BlockSpec((T, LANES), lambda c: (0, nb + c)),
                                      pl.BlockSpec((CONV_PAD, LANES), lambda c: (0, c)), pl.BlockSpec((1, LANES), lambda c: (0, c))],
                  blk, S((T, DC), F32), scratch=[pltpu.VMEM((T + CONV_PAD, LANES), F32)])(ag, ag, w, b)


def _conv_bwd(dy, ag, w, name):
    T = ag.shape[0]
    DC = ag.shape[1] // 2
    nb = DC // LANES
    tr = _tile(T, 256)

    def body(dy_ref, a_ref, g_ref, w_ref, da_ref, dg_ref, dw_ref, db_ref, pad, dpad):
        av = a_ref[...]
        sg = _sigmoid(g_ref[...])
        pad[0:CONV_PAD, :] = jnp.zeros((CONV_PAD, LANES), F32)
        pad[CONV_PAD:CONV_PAD + T, :] = av * sg
        dpad[0:T, :] = dy_ref[...]
        dpad[T:T + CONV_PAD, :] = jnp.zeros((CONV_PAD, LANES), F32)
        db_ref[...] = _colsum(dy_ref[...])
        dw_ref[...] = jnp.zeros_like(dw_ref)
        for j in range(CONV_WIDTH):
            acc = jnp.zeros((tr, LANES), F32)
            for r in range(T // tr):
                o = r * tr + CONV_PAD - (CONV_WIDTH - 1) + j
                acc = acc + dpad[r * tr:(r + 1) * tr, :] * pad[o:o + tr, :]
            dw_ref[j:j + 1, :] = _colsum(acc)
        for r in range(T // tr):
            acc = jnp.zeros((tr, LANES), F32)
            for j in range(CONV_WIDTH):
                o = r * tr + (CONV_WIDTH - 1) - j
                acc = acc + w_ref[j:j + 1, :] * dpad[o:o + tr, :]
            rows = slice(r * tr, (r + 1) * tr)
            sgr = sg[rows, :]
            da_ref[rows, :] = (acc * sgr).astype(BF16)
            dg_ref[rows, :] = (acc * av[rows, :] * sgr * (1.0 - sgr)).astype(BF16)

    blk = pl.BlockSpec((T, LANES), lambda c: (0, c))
    wblk = pl.BlockSpec((CONV_PAD, LANES), lambda c: (0, c))
    return _pcall(name, body, (nb,), [blk, blk, pl.BlockSpec((T, LANES), lambda c: (0, nb + c)), wblk],
                  [blk, blk, wblk, pl.BlockSpec((1, LANES), lambda c: (0, c))],
                  [S((T, DC), BF16), S((T, DC), BF16), S((CONV_PAD, DC), F32), S((1, DC), F32)],
                  scratch=[pltpu.VMEM((T + CONV_PAD, LANES), F32), pltpu.VMEM((T + CONV_PAD, LANES), F32)])(dy, ag, ag, w)


def _conv_norms(yc, lg, lb):
    mu = jnp.mean(yc, axis=-1, keepdims=True)
    xc = yc - mu
    rs = lax.rsqrt(jnp.mean(xc * xc, axis=-1, keepdims=True) + EPS)
    xh = xc * rs
    z = xh * lg + lb
    sg = _sigmoid(z)
    return rs, xh, z, sg, z * sg


def _mix_post(attn, yc, ag, cg, lg, lb, name):
    T, DA = attn.shape
    DC = yc.shape[1]
    tm = _tile(T, ROW_TILE)

    def body(at_ref, yc_ref, ag_ref, cg_ref, lg_ref, lb_ref, y_ref):
        at = at_ref[...]
        y_ref[:, 0:DA] = (at * _rstd(at) * ag_ref[...]).astype(BF16)
        _, _, _, _, sv = _conv_norms(yc_ref[...], lg_ref[...], lb_ref[...])
        y_ref[:, DA:DA + DC] = (sv * _rstd(sv) * cg_ref[...]).astype(BF16)

    return _pcall(name, body, (T // tm,),
                  [pl.BlockSpec((tm, DA), lambda i: (i, 0)), pl.BlockSpec((tm, DC), lambda i: (i, 0)), _full((1, DA)),
                   _full((1, DC)), _full((1, DC)), _full((1, DC))],
                  pl.BlockSpec((tm, DA + DC), lambda i: (i, 0)), S((T, DA + DC), BF16))(attn, yc, ag, cg, lg, lb)


def _mix_post_bwd(dy, attn, yc, ag, cg, lg, lb, name):
    T, DA = attn.shape
    DC = yc.shape[1]
    tm = _tile(T, ROW_TILE)

    def body(dy_ref, at_ref, yc_ref, ag_ref, cg_ref, lg_ref, lb_ref, dat_ref, dyc_ref, dag_ref, dcg_ref, dlg_ref, dlb_ref):
        dat, dag_rows = _rms_bwd(dy_ref[:, 0:DA], at_ref[...], ag_ref[...])
        dat_ref[...] = dat.astype(BF16)
        lgv = lg_ref[...]
        rs, xh, z, sg, sv = _conv_norms(yc_ref[...], lgv, lb_ref[...])
        dsv, dcg_rows = _rms_bwd(dy_ref[:, DA:DA + DC], sv, cg_ref[...])
        dz = dsv * (sg * (1.0 + z * (1.0 - sg)))
        dxh = dz * lgv
        dyc_ref[...] = rs * (dxh - jnp.mean(dxh, axis=-1, keepdims=True) - xh * jnp.mean(dxh * xh, axis=-1, keepdims=True))

        @pl.when(pl.program_id(0) == 0)
        def _():
            for r in (dag_ref, dcg_ref, dlg_ref, dlb_ref):
                r[...] = jnp.zeros_like(r)

        dag_ref[...] += _colsum(dag_rows)
        dcg_ref[...] += _colsum(dcg_rows)
        dlg_ref[...] += _colsum(dz * xh)
        dlb_ref[...] += _colsum(dz)

    ra = pl.BlockSpec((tm, DA), lambda i: (i, 0))
    rc = pl.BlockSpec((tm, DC), lambda i: (i, 0))
    return _pcall(name, body, (T // tm,),
                  [pl.BlockSpec((tm, DA + DC), lambda i: (i, 0)), ra, rc, _full((1, DA)), _full((1, DC)), _full((1, DC)),
                   _full((1, DC))],
                  [ra, rc, _full((1, DA)), _full((1, DC)), _full((1, DC)), _full((1, DC))],
                  [S((T, DA), BF16), S((T, DC), F32), S((1, DA), F32), S((1, DC), F32), S((1, DC), F32), S((1, DC), F32)])(
        dy, attn, yc, ag, cg, lg, lb)


def _xattn_probs(q, k, xd):
    s = _dot(q, k, NT) * (1.0 / math.sqrt(xd))
    p = jnp.exp(s - jnp.max(s, axis=-1, keepdims=True))
    return p / jnp.sum(p, axis=-1, keepdims=True)


def _xattn_fwd(q, kv, name):
    T, D = q.shape
    M = kv.shape[0]
    xd = D // N_XATTN_HEADS
    tq = _tile(T, ROW_TILE)

    def body(q_ref, kv_ref, o_ref):
        for h in range(N_XATTN_HEADS):
            sl = slice(h * xd, (h + 1) * xd)
            p = _xattn_probs(q_ref[:, sl], kv_ref[:, sl], xd)
            o_ref[:, sl] = _dot(p.astype(BF16), kv_ref[:, D + h * xd:D + (h + 1) * xd], NN).astype(BF16)

    row = pl.BlockSpec((tq, D), lambda i: (i, 0))
    return _pcall(name, body, (T // tq,), [row, _full((M, 2 * D))], row, S((T, D), BF16))(q, kv)


def _xattn_bwd(q, kv, do, name):
    T, D = q.shape
    M = kv.shape[0]
    xd = D // N_XATTN_HEADS
    tq = _tile(T, ROW_TILE)
    scale = 1.0 / math.sqrt(xd)

    def body(q_ref, kv_ref, do_ref, dq_ref, dkv_ref):
        @pl.when(pl.program_id(0) == 0)
        def _():
            dkv_ref[...] = jnp.zeros_like(dkv_ref)

        for h in range(N_XATTN_HEADS):
            sl = slice(h * xd, (h + 1) * xd)
            vsl = slice(D + h * xd, D + (h + 1) * xd)
            qh = q_ref[:, sl]
            kh = kv_ref[:, sl]
            doh = do_ref[:, sl]
            p = _xattn_probs(qh, kh, xd)
            dp = _dot(doh, kv_ref[:, vsl], NT)
            ds = (p * (dp - jnp.sum(p * dp, axis=-1, keepdims=True)) * scale).astype(BF16)
            dq_ref[:, sl] = _dot(ds, kh, NN).astype(BF16)
            dkv_ref[:, sl] += _dot(ds, qh, TN)
            dkv_ref[:, vsl] += _dot(p.astype(BF16), doh, TN)

    row = pl.BlockSpec((tq, D), lambda i: (i, 0))
    return _pcall(name, body, (T // tq,), [row, _full((M, 2 * D)), row], [row, _full((M, 2 * D))],
                  [S((T, D), BF16), S((M, 2 * D), F32)])(q, kv, do)


def _adamw(w, m, v, g, name):
    shape = w.shape
    C = shape[-1]
    R = w.size // C
    tr = R if R <= 512 else _tile(R, 512)

    def body(w_ref, m_ref, v_ref, g_ref, d_ref, nm_ref, nv_ref):
        gv = g_ref[...]
        mv = ADAM_B1 * m_ref[...] + (1.0 - ADAM_B1) * gv
        vv = ADAM_B2 * v_ref[...] + (1.0 - ADAM_B2) * (gv * gv)
        m_hat = mv / (1.0 - ADAM_B1 ** ADAM_STEP)
        v_hat = vv / (1.0 - ADAM_B2 ** ADAM_STEP)
        d_ref[...] = -ADAM_LR * (m_hat / (jnp.sqrt(v_hat) + ADAM_EPS) + ADAM_WD * w_ref[...])
        nm_ref[...] = mv
        nv_ref[...] = vv

    blk = pl.BlockSpec((tr, C), lambda i: (i, 0))
    outs = _pcall(name, body, (R // tr,), [blk] * 4, [blk] * 3, [S((R, C), F32)] * 3)(
        w.reshape(R, C), m.reshape(R, C), v.reshape(R, C), g.reshape(R, C))
    return [o.reshape(shape) for o in outs]


def _place_scalars():
    return jnp.stack([lax.axis_index("c"), 2 * lax.axis_index("x") + lax.axis_index("y")]).astype(jnp.int32)


def _adamw_sum(w, m, v, owns, landed, name):
    L, p, q = w.shape
    qq = owns[0].shape[2]
    tr = next((t for t in range(min(p, ROW_TILE) // 16 * 16, 0, -16) if p % t == 0), p)

    def body(place_ref, w_ref, m_ref, v_ref, *rest):
        own_refs, land_refs = rest[:L], rest[L:2 * L]
        g_ref, d_ref, nm_ref, nv_ref = rest[2 * L:]
        chip = place_ref[1]

        def update(l):
            own = own_refs[l][...].astype(F32)
            gs = None
            for k in range(4):
                term = jnp.where(chip == k, own, land_refs[l][k].astype(F32))
                gs = term if gs is None else gs + term
            gv = gs[:, 0:q]
            mv = ADAM_B1 * m_ref[...] + (1.0 - ADAM_B1) * gv
            vv = ADAM_B2 * v_ref[...] + (1.0 - ADAM_B2) * (gv * gv)
            m_hat = mv / (1.0 - ADAM_B1 ** ADAM_STEP)
            v_hat = vv / (1.0 - ADAM_B2 ** ADAM_STEP)
            g_ref[...] = gv
            d_ref[...] = -ADAM_LR * (m_hat / (jnp.sqrt(v_hat) + ADAM_EPS) + ADAM_WD * w_ref[...])
            nm_ref[...] = mv
            nv_ref[...] = vv

        for l in range(L):
            pl.when(pl.program_id(0) == l)(lambda l=l: update(l))

    def rows_of(layer):
        return lambda l, i, place: jnp.where(l == layer, i, 0)

    blk = pl.BlockSpec((None, tr, q), lambda l, i, place: (l, i, 0))
    in_specs = [blk, blk, blk]
    in_specs += [pl.BlockSpec((None, tr, qq), lambda l, i, place, r=rows_of(k): (0, r(l, i, place), 0)) for k in range(L)]
    in_specs += [pl.BlockSpec((4, None, tr, qq), lambda l, i, place, r=rows_of(k): (0, 0, r(l, i, place), 0)) for k in range(L)]
    gs = pltpu.PrefetchScalarGridSpec(num_scalar_prefetch=1, grid=(L, p // tr), in_specs=in_specs, out_specs=[blk] * 4)
    return pl.pallas_call(body, grid_spec=gs, out_shape=[S((L, p, q), F32)] * 4, name=name,
                          compiler_params=pltpu.CompilerParams(dimension_semantics=("arbitrary", "arbitrary")))(
        _place_scalars(), w, m, v, *owns, *landed)


def _pair_add(g, recv, axis, name):
    _, _, p, q = recv.shape

    def body(place_ref, g_ref, r_ref, o_ref, own_ref):
        s = (g_ref[...].astype(F32) + r_ref[...].astype(F32)).astype(BF16)
        o_ref[...] = s

        @pl.when(pl.program_id(0) == place_ref[1])
        def _():
            own_ref[...] = s

    if axis == 1:
        gspec = pl.BlockSpec((None, p, q), lambda k, place: (0, 2 * k + place[0], 0))
    else:
        gspec = pl.BlockSpec((None, p, q), lambda k, place: (0, 0, 2 * k + place[0]))
    part = pl.BlockSpec((None, None, p, q), lambda k, place: (k, 0, 0, 0))
    own = pl.BlockSpec((None, p, q), lambda k, place: (0, 0, 0))
    gs = pltpu.PrefetchScalarGridSpec(num_scalar_prefetch=1, grid=(4,), in_specs=[gspec, part], out_specs=[part, own])
    return pl.pallas_call(body, grid_spec=gs, out_shape=[S((4, 1, p, q), BF16), S((1, p, q), BF16)], name=name,
                          compiler_params=pltpu.CompilerParams(dimension_semantics=("arbitrary",)))(_place_scalars(), g, recv)


def _win_pieces(n_attn, n_heads, n_conv, shard, chunk):
    bounds = [0, 3 * n_attn, 3 * n_attn + n_heads, 3 * n_attn + n_heads + 2 * n_conv]
    pieces = []
    for j in range(N_DEV):
        lo, hi = shard * j, shard * (j + 1)
        for r in range(3):
            a, b = max(lo, bounds[r]), min(hi, bounds[r + 1])
            if a < b:
                pieces.append((r, a - bounds[r], b - bounds[r], chunk * j + a - lo))
    return pieces


def _win_split(w_in, pieces, widths, name):
    L, D, C = w_in.shape
    tr = _tile(D, 256)

    def body(x_ref, *outs):
        outs[1][...] = jnp.zeros_like(outs[1])
        for r, d0, d1, s0 in pieces:
            outs[r][:, d0:d1] = x_ref[:, s0:s0 + d1 - d0]

    return _pcall(name, body, (L, D // tr), [pl.BlockSpec((None, tr, C), lambda l, i: (l, i, 0))],
                  [pl.BlockSpec((None, tr, wd), lambda l, i: (l, i, 0)) for wd in widths],
                  [S((L, D, wd), BF16) for wd in widths])(w_in)


def _win_merge(parts, pieces, chunked_cols, name):
    L, D, _ = parts[0].shape
    tr = _tile(D, 256)

    def body(a_ref, b_ref, c_ref, o_ref):
        ins = (a_ref, b_ref, c_ref)
        o_ref[...] = jnp.zeros_like(o_ref)
        for r, d0, d1, s0 in pieces:
            o_ref[:, s0:s0 + d1 - d0] = ins[r][:, d0:d1]

    return _pcall(name, body, (L, D // tr), [pl.BlockSpec((None, tr, x.shape[2]), lambda l, i: (l, i, 0)) for x in parts],
                  pl.BlockSpec((None, tr, chunked_cols), lambda l, i: (l, i, 0)), S((L, D, chunked_cols), BF16))(*parts)


def _place():
    return lax.axis_index("x"), lax.axis_index("y"), lax.axis_index("c")


def _flip(v, f):
    return 1 - v if f else v


def _window(ref, axis, size, dev):
    start = dev * size if isinstance(dev, int) else pl.multiple_of(dev * size, LANES if axis == 2 else 16)
    return ref.at[:, pl.ds(start, size), :] if axis == 1 else ref.at[:, :, pl.ds(start, size)]


HBM_SPEC = pl.BlockSpec(memory_space=pltpu.HBM)
SEM_SPEC = pl.BlockSpec(memory_space=pltpu.SEMAPHORE)
SPLIT_COPY_PARAMS = dict(has_side_effects=pltpu.SideEffectType.DATAFLOW_SIDE_EFFECTING)


def _hbm(v):
    return pltpu.with_memory_space_constraint(v, pltpu.HBM)


def _full_shape(shard, axis):
    return tuple(N_DEV * d if i == axis else d for i, d in enumerate(shard.shape))


def _ag_peers(x, y, c):
    return [(x, y, 1 - c), (1 - x, y, c), (x, 1 - y, c), (1 - x, 1 - y, c)]


SIBLING_COLLECTIVE_ID = 0


def _sibling_handshake(x, y, c):
    barrier = pltpu.get_barrier_semaphore()
    pl.semaphore_signal(barrier, inc=1, device_id=(x, y, 1 - c), device_id_type=MESH)
    pl.semaphore_wait(barrier, 1)


def _ag_start(shards, axes, groups, after, name):
    n, ng = len(shards), len(groups)
    sizes = [s.shape[ax] for s, ax in zip(shards, axes)]
    where = {w: (g, i) for g, members in enumerate(groups) for i, w in enumerate(members)}

    def body(*refs):
        xs, fulls = refs[:n], refs[n + 1:2 * n + 1]
        send, recv = refs[3 * n + 1:3 * n + 1 + ng], refs[3 * n + 1 + ng:]
        x, y, c = _place()
        for members in groups:
            for w in members:
                g, i = where[w]
                for k, to in enumerate(_ag_peers(x, y, c)):
                    pltpu.make_async_remote_copy(
                        src_ref=xs[w], dst_ref=_window(fulls[w], axes[w], sizes[w], 4 * x + 2 * y + c),
                        send_sem=send[g].at[4 * i + k], recv_sem=recv[g].at[4 * i + k], device_id=to, device_id_type=MESH).start()

    sems = [pltpu.SemaphoreType.DMA((4 * len(m),)) for m in groups]
    outs = pl.pallas_call(
        body, name=name,
        out_shape=[pltpu.HBM(_full_shape(s, ax), s.dtype) for s, ax in zip(shards, axes)] + [pltpu.HBM(s.shape, s.dtype) for s in shards]
        + sems + sems,
        in_specs=[HBM_SPEC] * n + [pl.BlockSpec(memory_space=pl.ANY)], out_specs=[HBM_SPEC] * (2 * n) + [SEM_SPEC] * (2 * ng),
        input_output_aliases={w: n + w for w in range(n)},
        compiler_params=pltpu.CompilerParams(**SPLIT_COPY_PARAMS))(*[_hbm(s) for s in shards], after)
    return outs[:n], outs[n:2 * n], outs[2 * n:2 * n + ng], outs[2 * n + ng:]


def _ag_wait(shards, fulls, send_sems, recv_sems, axes, after, name):
    n = len(shards)
    sizes = [s.shape[ax] for s, ax in zip(shards, axes)]

    def body(*refs):
        xs = refs[:n]
        send, recv = refs[2 * n], refs[2 * n + 1]
        landed = refs[3 * n + 3:]
        x, y, c = _place()
        for w in range(n):
            for k, frm in enumerate(_ag_peers(x, y, c)):
                copy = pltpu.make_async_remote_copy(
                    src_ref=xs[w], dst_ref=_window(landed[w], axes[w], sizes[w], 4 * frm[0] + 2 * frm[1] + frm[2]),
                    send_sem=send.at[4 * w + k], recv_sem=recv.at[4 * w + k], device_id=frm, device_id_type=MESH)
                copy.wait_send()
                copy.wait_recv()

    outs = pl.pallas_call(
        body, name=name, out_shape=[pltpu.HBM(v.shape, v.dtype) for v in list(shards) + list(fulls)],
        in_specs=[HBM_SPEC] * (2 * n) + [SEM_SPEC, SEM_SPEC, pl.BlockSpec(memory_space=pl.ANY)], out_specs=[HBM_SPEC] * (2 * n),
        input_output_aliases={i: i for i in range(2 * n)},
        compiler_params=pltpu.CompilerParams(**SPLIT_COPY_PARAMS))(*shards, *fulls, send_sems, recv_sems, after)
    return outs[:n], outs[n:]


def _ag_forward(fulls, shards, axes, name):
    n = len(fulls)
    sizes = [s.shape[ax] for s, ax in zip(shards, axes)]

    def body(*refs):
        xs, full_refs = refs[:n], refs[2 * n:3 * n]
        send_sems, recv_sems, local_sems = refs[3 * n:3 * n + 3]
        staged = refs[3 * n + 3:]
        x, y, c = _place()
        _sibling_handshake(x, y, c)
        chips = [(1 - x, y), (x, 1 - y), (1 - x, 1 - y)]
        loads = [pltpu.make_async_copy(xs[w], staged[w], local_sems.at[w]) for w in range(n)]
        for cp in loads:
            cp.start()
        copies = []
        for w in range(n):
            for j, (px, py) in enumerate(chips):
                sent = _window(full_refs[w], axes[w], sizes[w], 4 * px + 2 * py + c)
                got = _window(full_refs[w], axes[w], sizes[w], 4 * px + 2 * py + 1 - c)
                out = pltpu.make_async_remote_copy(src_ref=sent, dst_ref=sent, send_sem=send_sems.at[3 * w + j],
                                                   recv_sem=recv_sems.at[3 * w + j], device_id=(x, y, 1 - c), device_id_type=MESH)
                out.start()
                back = pltpu.make_async_remote_copy(src_ref=got, dst_ref=got, send_sem=send_sems.at[3 * w + j],
                                                    recv_sem=recv_sems.at[3 * w + j], device_id=(x, y, 1 - c), device_id_type=MESH)
                copies.append((out, back))
        stores = []
        for w in range(n):
            loads[w].wait()
            store = pltpu.make_async_copy(staged[w], _window(full_refs[w], axes[w], sizes[w], 4 * x + 2 * y + c), local_sems.at[w])
            store.start()
            stores.append(store)
        for out, back in copies:
            out.wait_send()
            back.wait_recv()
        for cp in stores:
            cp.wait()

    any_spec = pl.BlockSpec(memory_space=pl.ANY)
    outs = pl.pallas_call(
        body, name=name, out_shape=[S(f.shape, f.dtype) for f in fulls], in_specs=[any_spec] * (2 * n), out_specs=[any_spec] * n,
        input_output_aliases={n + w: w for w in range(n)},
        scratch_shapes=[pltpu.SemaphoreType.DMA((3 * n,)), pltpu.SemaphoreType.DMA((3 * n,)), pltpu.SemaphoreType.DMA((n,))]
        + [pltpu.VMEM(s.shape, s.dtype) for s in shards],
        compiler_params=pltpu.CompilerParams(collective_id=SIBLING_COLLECTIVE_ID))(*shards, *fulls)
    return outs


def _sibling_copy(g_ref, land_ref, send_sems, recv_sems, axis, size, w, k, x, y, c):
    return pltpu.make_async_remote_copy(
        src_ref=_window(g_ref, axis, size, 2 * k + 1 - c), dst_ref=land_ref.at[k], send_sem=send_sems.at[4 * w + k],
        recv_sem=recv_sems.at[4 * w + k], device_id=(x, y, 1 - c), device_id_type=MESH)


def _to_sibling_start(grads, axes, sizes, after, name):
    n = len(grads)
    landing = []
    for g, ax, sz in zip(grads, axes, sizes):
        L, K, N = g.shape
        landing.append(pltpu.HBM((4, L, sz, N) if ax == 1 else (4, L, K, sz), g.dtype))
    extra = [] if after is None else [after]

    def body(*refs):
        g_refs = refs[:n]
        land_refs = refs[n + len(extra):2 * n + len(extra)]
        send_sems, recv_sems, token = refs[3 * n + len(extra):]
        x, y, c = _place()
        _sibling_handshake(x, y, c)
        for w in range(n):
            for k in range(4):
                _sibling_copy(g_refs[w], land_refs[w], send_sems, recv_sems, axes[w], sizes[w], w, k, x, y, c).start()
        token[...] = jnp.zeros_like(token)

    sems = pltpu.SemaphoreType.DMA((4 * n,))
    outs = pl.pallas_call(
        body, name=name, out_shape=landing + [pltpu.HBM(g.shape, g.dtype) for g in grads] + [sems, sems, S((8, LANES), F32)],
        in_specs=[HBM_SPEC] * n + [pl.BlockSpec(memory_space=pl.ANY)] * len(extra),
        out_specs=[HBM_SPEC] * (2 * n) + [SEM_SPEC, SEM_SPEC, pl.BlockSpec(memory_space=pltpu.VMEM)],
        input_output_aliases={w: n + w for w in range(n)},
        compiler_params=pltpu.CompilerParams(collective_id=SIBLING_COLLECTIVE_ID, **SPLIT_COPY_PARAMS))(
        *[_hbm(g) for g in grads], *extra)
    return outs[:n], outs[n:2 * n], outs[2 * n], outs[2 * n + 1], outs[2 * n + 2]


def _to_sibling_wait(grads, landing, send_sems, recv_sems, axes, sizes, after, name):
    n = len(grads)

    def body(*refs):
        g_refs = refs[:n]
        send, recv = refs[2 * n], refs[2 * n + 1]
        landed = refs[3 * n + 3:]
        x, y, c = _place()
        for w in range(n):
            for k in range(4):
                copy = _sibling_copy(g_refs[w], landed[w], send, recv, axes[w], sizes[w], w, k, x, y, c)
                copy.wait_send()
                copy.wait_recv()

    outs = pl.pallas_call(
        body, name=name, out_shape=[pltpu.HBM(v.shape, v.dtype) for v in list(grads) + list(landing)],
        in_specs=[HBM_SPEC] * (2 * n) + [SEM_SPEC, SEM_SPEC, pl.BlockSpec(memory_space=pl.ANY)], out_specs=[HBM_SPEC] * (2 * n),
        input_output_aliases={i: i for i in range(2 * n)},
        compiler_params=pltpu.CompilerParams(**SPLIT_COPY_PARAMS))(*grads, *landing, send_sems, recv_sems, after)
    return outs[:n], outs[n:]


def _rs_copy(p_ref, out_ref, send_sems, recv_sems, w, rel, x, y, c):
    tx, ty = _flip(x, rel & 2), _flip(y, rel & 1)
    return pltpu.make_async_remote_copy(
        src_ref=p_ref.at[2 * tx + ty], dst_ref=out_ref.at[2 * x + y], send_sem=send_sems.at[3 * w + rel - 1],
        recv_sem=recv_sems.at[3 * w + rel - 1], device_id=(tx, ty, c), device_id_type=MESH)


def _rs_start(parts, name):
    n = len(parts)

    def body(*refs):
        p_refs, out_refs = refs[:n], refs[n:2 * n]
        send_sems, recv_sems, token = refs[3 * n:]
        x, y, c = _place()
        for w in range(n):
            for rel in (1, 2, 3):
                _rs_copy(p_refs[w], out_refs[w], send_sems, recv_sems, w, rel, x, y, c).start()
        token[...] = jnp.zeros_like(token)

    sems = pltpu.SemaphoreType.DMA((3 * n,))
    outs = pl.pallas_call(
        body, name=name,
        out_shape=[pltpu.HBM(p.shape, p.dtype) for p in parts] * 2 + [sems, sems, S((8, LANES), F32)],
        in_specs=[HBM_SPEC] * n, out_specs=[HBM_SPEC] * (2 * n) + [SEM_SPEC, SEM_SPEC, pl.BlockSpec(memory_space=pltpu.VMEM)],
        input_output_aliases={w: n + w for w in range(n)},
        compiler_params=pltpu.CompilerParams(**SPLIT_COPY_PARAMS))(*[_hbm(p) for p in parts])
    return outs[:n], outs[n:2 * n], outs[2 * n], outs[2 * n + 1], outs[2 * n + 2]


def _rs_wait(parts, landing, send_sems, recv_sems, after, name):
    n = len(parts)

    def body(*refs):
        p_refs = refs[:n]
        send, recv = refs[2 * n], refs[2 * n + 1]
        landed = refs[3 * n + 3:]
        x, y, c = _place()
        for w in range(n):
            for rel in (1, 2, 3):
                copy = _rs_copy(p_refs[w], landed[w], send, recv, w, rel, x, y, c)
                copy.wait_send()
                copy.wait_recv()

    outs = pl.pallas_call(
        body, name=name, out_shape=[pltpu.HBM(v.shape, v.dtype) for v in list(parts) + list(landing)],
        in_specs=[HBM_SPEC] * (2 * n) + [SEM_SPEC, SEM_SPEC, pl.BlockSpec(memory_space=pl.ANY)], out_specs=[HBM_SPEC] * (2 * n),
        input_output_aliases={i: i for i in range(2 * n)},
        compiler_params=pltpu.CompilerParams(**SPLIT_COPY_PARAMS))(*parts, *landing, send_sems, recv_sems, after)
    return outs[n:]


def _exchange_small(v, reduce, name, after=None):
    R, C = v.shape

    def body(v_ref, *rest):
        out_ref, gath, send_sems, recv_sems = rest[-4:]
        x, y, c = _place()
        me = 4 * x + 2 * y + c
        buf = gath if reduce else out_ref
        buf[me] = v_ref[...]
        copies = []
        for rel in range(1, N_DEV):
            peer = (_flip(x, rel & 4), _flip(y, rel & 2), _flip(c, rel & 1))
            copies.append(pltpu.make_async_remote_copy(
                src_ref=v_ref, dst_ref=buf.at[me], send_sem=send_sems.at[rel - 1], recv_sem=recv_sems.at[rel - 1],
                device_id=peer, device_id_type=MESH))
        for cp in copies:
            cp.start()
        for cp in copies:
            cp.wait()
        if reduce:
            acc = gath[0]
            for d in range(1, N_DEV):
                acc = acc + gath[d]
            out_ref[...] = acc

    vm = pl.BlockSpec(memory_space=pltpu.VMEM)
    extra = [] if after is None else [after]
    return pl.pallas_call(
        body, out_shape=S((R, C) if reduce else (N_DEV, R, C), F32), in_specs=[vm] + [pl.BlockSpec(memory_space=pl.ANY)] * len(extra),
        out_specs=vm, name=name,
        scratch_shapes=[pltpu.VMEM((N_DEV, R, C) if reduce else (8, LANES), F32), pltpu.SemaphoreType.DMA((N_DEV - 1,)),
                        pltpu.SemaphoreType.DMA((N_DEV - 1,))])(v, *extra)


def _pad_rows(flat, cols, mult):
    n = flat.shape[-1]
    rows = -(-n // cols)
    rows = -(-rows // mult) * mult
    pad = [(0, 0)] * (flat.ndim - 1) + [(0, rows * cols - n)]
    return jnp.pad(flat, pad).reshape(flat.shape[:-1] + (rows, cols))


def _round_up(n, m):
    return -(-n // m) * m


def _shard_axes(a):
    return [(2, _round_up(a[n].shape[2], LANES)) if kind == 'col' else (1, _round_up(a[n].shape[1 if kind == 'row' else 2], LANES))
            for n, kind in BIG]


def _pack_small(vals):
    rows = [_pad_rows(vals[n].astype(F32).reshape(-1), SMALL_COLS, 1) for n in SMALL]
    m = jnp.concatenate(rows, axis=0)
    return jnp.pad(m, ((0, -m.shape[0] % 8), (0, 0)))


def _unpack_small(m, a):
    out, r = {}, 0
    for n in SMALL:
        nr = -(-a[n].size // SMALL_COLS)
        out[n] = m[r:r + nr].reshape(-1)[:a[n].size].reshape(a[n].shape)
        r += nr
    return out, r


GROUPS = (('ffn1_w_gate', 'ffn1_w_up', 'ffn1_w_down'), ('w_in', 'w_out', 'xattn_w_q', 'xattn_w_kv', 'xattn_w_o'),
          ('ffn2_w_gate', 'ffn2_w_up', 'ffn2_w_down'))
GATHER_GROUPS = GROUPS


def _layer_small(a, conv_w_full, l):
    H = a['b_f'].shape[1]
    return dict(
        bft=jnp.pad(a['b_f'][l].reshape(H, 1), ((0, 16 - H), (0, 0))),
        cw=jnp.pad(conv_w_full[l], ((0, CONV_PAD - CONV_WIDTH), (0, 0))), cb=a['conv_b'][l].reshape(1, -1),
        lg=a['conv_ln_g'][l].reshape(1, -1), lb=a['conv_ln_b'][l].reshape(1, -1),
        ag=a['attn_out_g'][l].reshape(1, -1), cg=a['conv_out_g'][l].reshape(1, -1),
        g1=a['ffn1_norm_g'][l], gm=a['mix_norm_g'][l], gx=a['xattn_norm_g'][l], gmem=a['mem_norm_g'][l], g2=a['ffn2_norm_g'][l])


def _layer_fwd(x0, mem, w, fetch, cfg, l):
    T = x0.shape[0]
    H = cfg['heads']
    sv = {'x0': x0}
    m = fetch(l, 0, x0)
    w.update(wg1=(m['ffn1_w_gate'], 0), wu1=(m['ffn1_w_up'], 0), wd1=(m['ffn1_w_down'], 0))
    sv['h1'] = _rms_fwd(x0, w['g1'], f"l{l}_ffn1_norm")
    sv['G1'], sv['U1'], sv['A1'] = _ffn_up(sv['h1'], w['wg1'], w['wu1'], f"l{l}_ffn1_up")
    x1 = sv['x1'] = _mm_res(sv['A1'], w['wd1'], x0, 0.5, f"l{l}_ffn1_down")
    m = fetch(l, 1, x1)
    wqkv, wf, wag = _win_split(m['w_in'], cfg['pieces'], cfg['widths'], f"l{l}_w_in_split")
    w.update(wqkv=(wqkv, 0), wft=wf[0, :, :16].T, wag=(wag, 0), wout=(m['w_out'], 0), wq=(m['xattn_w_q'], 0),
             wkv=(m['xattn_w_kv'], 0), wo=(m['xattn_w_o'], 0))
    h2 = sv['h2'] = _rms_fwd(x1, w['gm'], f"l{l}_mix_norm")
    sv['qkv'] = _mm(h2, w['wqkv'], BF16, f"l{l}_qkv_proj")
    sv['agv'] = _mm(h2, w['wag'], F32, f"l{l}_glu_proj")
    ct, sv['sg'] = _fox_prep(h2, w['wft'], w['bft'], f"l{l}_fox_prep")
    sv['c_col'] = ct[:H].reshape(H, T, 1)
    sv['c_row'] = ct[:H].reshape(H // 2, 2, T)
    sv['attn'], sv['lse'] = _fox_fwd(sv['qkv'], sv['c_col'], sv['c_row'], f"l{l}_fox_fwd")
    sv['yc'] = _conv_fwd(sv['agv'], w['cw'], w['cb'], f"l{l}_conv_fwd")
    sv['ycat'] = _mix_post(sv['attn'], sv['yc'], w['ag'], w['cg'], w['lg'], w['lb'], f"l{l}_mix_post")
    x2 = sv['x2'] = _mm_res(sv['ycat'], w['wout'], x1, 1.0, f"l{l}_out_proj")
    sv['h3'] = _rms_fwd(x2, w['gx'], f"l{l}_xattn_norm")
    sv['memn'] = _rms_fwd(mem, w['gmem'], f"l{l}_mem_norm")
    sv['q'] = _mm(sv['h3'], w['wq'], BF16, f"l{l}_xattn_q")
    sv['kv'] = _mm_nt(sv['memn'], w['wkv'], BF16, f"l{l}_xattn_kv")
    sv['o'] = _xattn_fwd(sv['q'], sv['kv'], f"l{l}_xattn_fwd")
    x3 = sv['x3'] = _mm_res(sv['o'], w['wo'], x2, 1.0, f"l{l}_xattn_out")
    m = fetch(l, 2, x3)
    w.update(wg2=(m['ffn2_w_gate'], 0), wu2=(m['ffn2_w_up'], 0), wd2=(m['ffn2_w_down'], 0))
    sv['h4'] = _rms_fwd(x3, w['g2'], f"l{l}_ffn2_norm")
    sv['G2'], sv['U2'], sv['A2'] = _ffn_up(sv['h4'], w['wg2'], w['wu2'], f"l{l}_ffn2_up")
    return _mm_res(sv['A2'], w['wd2'], x3, 0.5, f"l{l}_ffn2_down"), sv


def _ffn_bwd(dout, x_in, h, G, U, A, wg, wu, wd, g, tag, put, which, dep, flush, settle):
    dG, dU = _ffn_bwd_act(dout, wd, G, U, 0.5, tag + "_bwd_act", dep)
    settle(dG)
    put(which + '_w_down', A, dout, 0.5, tag + "_dwd")
    put(which + '_w_gate', dG, h, 1.0, tag + "_dwg")
    put(which + '_w_up', dU, h, 1.0, tag + "_dwu")
    dep = flush()
    dx, dg = _bwd_h([(dG, wg, 'nn'), (dU, wu, 'nn')], x_in, g, dout, tag + "_bwd_h", dep)
    token = settle(dx)
    return dx, dg, dep if token is None else token


def _layer_bwd(dx4, mem, w, sv, reduce, settle, cfg, l, dep):
    small, grads = {}, {}
    T = dx4.shape[0]
    H = cfg['heads']
    tokens = []

    def put(key, act, dy, scale, name):
        grads[key] = _wgrad(act, dy, scale, name, (None, 0, 1))

    def put_and_reduce(key, act, dy, scale, name):
        put(key, act, dy, scale, name)
        tokens.append(reduce(l, (key,), {key: grads.pop(key)}))

    dx3, small['ffn2_norm_g'], dep = _ffn_bwd(
        dx4, sv['x3'], sv['h4'], sv['G2'], sv['U2'], sv['A2'], w['wg2'], w['wu2'], w['wd2'], w['g2'], f"l{l}_ffn2", put, 'ffn2',
        dep, lambda: reduce(l, GROUPS[2], {n: grads.pop(n) for n in GROUPS[2]}), settle)
    do = _mm_nt(dx3, w['wo'], BF16, f"l{l}_xattn_do", dep)
    put('xattn_w_o', sv['o'], dx3, 1.0, f"l{l}_dwo")
    dq, dkv = _xattn_bwd(sv['q'], sv['kv'], do, f"l{l}_xattn_bwd")
    put('xattn_w_q', sv['h3'], dq, 1.0, f"l{l}_dwq")
    dx2, small['xattn_norm_g'] = _bwd_h([(dq, w['wq'], 'nt')], sv['x2'], w['gx'], dx3, f"l{l}_xattn_bwd_h")
    dmemn = _mm(dkv, w['wkv'], F32, f"l{l}_dmemn")
    put('xattn_w_kv', dkv, sv['memn'], 1.0, f"l{l}_dwkv")
    small['mem_norm_g'] = _rms_gain_grad(dmemn, mem, w['gmem'], f"l{l}_dgmem")
    dycat = _mm_nt(dx2, w['wout'], F32, f"l{l}_dycat")
    put('w_out', sv['ycat'], dx2, 1.0, f"l{l}_dwout")
    dattn, dyc, small['attn_out_g'], small['conv_out_g'], small['conv_ln_g'], small['conv_ln_b'] = _mix_post_bwd(
        dycat, sv['attn'], sv['yc'], w['ag'], w['cg'], w['lg'], w['lb'], f"l{l}_mix_post_bwd")
    dva, dga, dcw, small['conv_b'] = _conv_bwd(dyc, sv['agv'], w['cw'], f"l{l}_conv_bwd")
    dq_, dk_, dv_, dcs = _fox_bwd(sv['qkv'], sv['c_col'], sv['c_row'], sv['lse'], dattn, f"l{l}_fox_bwd")
    dcs16 = jnp.pad(dcs.reshape(H, T), ((0, 16 - H), (0, 0)))
    dflt, dwft, dbf = _fox_prep_bwd(dcs16, sv['sg'], sv['h2'], f"l{l}_fox_prep_bwd")
    small['b_f'] = dbf[:H].reshape(H)
    dqkv = jnp.concatenate([dq_, dk_, dv_], axis=1)
    dag = jnp.concatenate([dva, dga], axis=1)
    put('wqkv', sv['h2'], dqkv, 1.0, f"l{l}_dwqkv")
    put('wag', sv['h2'], dag, 1.0, f"l{l}_dwag")
    dx1, small['mix_norm_g'] = _bwd_h([(dqkv, w['wqkv'], 'nt'), (dag, w['wag'], 'nt'), (dflt, w['wft'], 'tn')],
                                      sv['x1'], w['gm'], dx2, f"l{l}_mix_bwd_h")
    dwf = jnp.pad(dwft[:H].T, ((0, 0), (0, LANES - H)))[None].astype(BF16)
    grads['w_in'] = _win_merge((grads.pop('wqkv'), dwf, grads.pop('wag')), cfg['pieces'], cfg['chunked_cols'], f"l{l}_w_in_merge")
    dep = reduce(l, GROUPS[1], {n: grads.pop(n) for n in GROUPS[1]})
    if l == 0:
        dx0, small['ffn1_norm_g'], dep = _ffn_bwd(
            dx1, sv['x0'], sv['h1'], sv['G1'], sv['U1'], sv['A1'], w['wg1'], w['wu1'], w['wd1'], w['g1'], f"l{l}_ffn1",
            put_and_reduce, 'ffn1', dep, lambda: tokens[-1], settle)
    else:
        dx0, small['ffn1_norm_g'], dep = _ffn_bwd(
            dx1, sv['x0'], sv['h1'], sv['G1'], sv['U1'], sv['A1'], w['wg1'], w['wu1'], w['wd1'], w['g1'], f"l{l}_ffn1", put, 'ffn1',
            dep, lambda: reduce(l, GROUPS[0], {n: grads.pop(n) for n in GROUPS[0]}), settle)
    small = {k: v.reshape(-1) for k, v in small.items()}
    return dx0, small, dcw[:CONV_WIDTH], dep


def _local_step(x, mem, tgt, a, conv_w_full, fetch, reduce, settle, cfg):
    L = a['b_f'].shape[0]
    ws = [_layer_small(a, conv_w_full, l) for l in range(L)]
    saved = []
    for l in range(L):
        x, sv = _layer_fwd(x, mem, ws[l], fetch, cfg, l)
        saved.append(sv)
    loss, dx, dgf = _loss_head(x, a['final_norm_g'], tgt, "loss_head")
    smalls, dcws, dep = [None] * L, [None] * L, None
    for l in range(L - 1, -1, -1):
        dx, smalls[l], dcws[l], dep = _layer_bwd(dx, mem, ws[l], saved[l], reduce, settle, cfg, l, dep)
    small = {n: jnp.stack([smalls[l][n] for l in range(L)]) for n in SMALL if n != 'final_norm_g'}
    small['final_norm_g'] = dgf.reshape(-1)
    return loss, dx, small, jnp.stack(dcws)


def kernel(x, mem, ffn1_norm_g, ffn1_w_gate, ffn1_w_up, ffn1_w_down, mix_norm_g, w_in, b_f, conv_w, conv_b, conv_ln_g, conv_ln_b, attn_out_g, conv_out_g, w_out, xattn_norm_g, mem_norm_g, xattn_w_q, xattn_w_kv, xattn_w_o, ffn2_norm_g, ffn2_w_gate, ffn2_w_up, ffn2_w_down, final_norm_g, loss_target, m_ffn1_norm_g, m_ffn1_w_gate, m_ffn1_w_up, m_ffn1_w_down, m_mix_norm_g, m_w_in, m_b_f, m_conv_w, m_conv_b, m_conv_ln_g, m_conv_ln_b, m_attn_out_g, m_conv_out_g, m_w_out, m_xattn_norm_g, m_mem_norm_g, m_xattn_w_q, m_xattn_w_kv, m_xattn_w_o, m_ffn2_norm_g, m_ffn2_w_gate, m_ffn2_w_up, m_ffn2_w_down, m_final_norm_g, v_ffn1_norm_g, v_ffn1_w_gate, v_ffn1_w_up, v_ffn1_w_down, v_mix_norm_g, v_w_in, v_b_f, v_conv_w, v_conv_b, v_conv_ln_g, v_conv_ln_b, v_attn_out_g, v_conv_out_g, v_w_out, v_xattn_norm_g, v_mem_norm_g, v_xattn_w_q, v_xattn_w_kv, v_xattn_w_o, v_ffn2_norm_g, v_ffn2_w_gate, v_ffn2_w_up, v_ffn2_w_down, v_final_norm_g):
    args = (x, mem, ffn1_norm_g, ffn1_w_gate, ffn1_w_up, ffn1_w_down, mix_norm_g, w_in, b_f, conv_w, conv_b, conv_ln_g, conv_ln_b, attn_out_g, conv_out_g, w_out, xattn_norm_g, mem_norm_g, xattn_w_q, xattn_w_kv, xattn_w_o, ffn2_norm_g, ffn2_w_gate, ffn2_w_up, ffn2_w_down, final_norm_g)
    moments_m = (m_ffn1_norm_g, m_ffn1_w_gate, m_ffn1_w_up, m_ffn1_w_down, m_mix_norm_g, m_w_in, m_b_f, m_conv_w, m_conv_b, m_conv_ln_g, m_conv_ln_b, m_attn_out_g, m_conv_out_g, m_w_out, m_xattn_norm_g, m_mem_norm_g, m_xattn_w_q, m_xattn_w_kv, m_xattn_w_o, m_ffn2_norm_g, m_ffn2_w_gate, m_ffn2_w_up, m_ffn2_w_down, m_final_norm_g)
    moments_v = (v_ffn1_norm_g, v_ffn1_w_gate, v_ffn1_w_up, v_ffn1_w_down, v_mix_norm_g, v_w_in, v_b_f, v_conv_w, v_conv_b, v_conv_ln_g, v_conv_ln_b, v_attn_out_g, v_conv_out_g, v_w_out, v_xattn_norm_g, v_mem_norm_g, v_xattn_w_q, v_xattn_w_kv, v_xattn_w_o, v_ffn2_norm_g, v_ffn2_w_gate, v_ffn2_w_up, v_ffn2_w_down, v_final_norm_g)
    a = dict(zip(NAMES, args))
    am = dict(zip(WEIGHTS, moments_m))
    av = dict(zip(WEIGHTS, moments_v))
    L, taps, cshard = conv_w.shape
    dev = 4 * lax.axis_index("x") + 2 * lax.axis_index("y") + lax.axis_index("c")

    big_names = [n for n, _ in BIG]
    geometry = dict(zip(big_names, _shard_axes(a)))
    n_attn, n_heads, n_conv = attn_out_g.shape[1], b_f.shape[1], conv_out_g.shape[1]
    chunk = geometry['w_in'][1]
    cfg = dict(heads=n_heads, pieces=_win_pieces(n_attn, n_heads, n_conv, w_in.shape[2], chunk),
               widths=(3 * n_attn, LANES, 2 * n_conv), chunked_cols=N_DEV * chunk)

    cw_rows = _pad_rows(conv_w.reshape(-1), LANES, 8)
    cw_all = _exchange_small(cw_rows, False, "allgather_conv_w")
    conv_w_full = cw_all.reshape(N_DEV, -1)[:, :conv_w.size].reshape(N_DEV, L, taps, cshard).transpose(1, 2, 0, 3).reshape(
        L, taps, N_DEV * cshard)

    keys = [(l, n) for l in range(L) for names in GATHER_GROUPS for n in names]
    members = [[keys.index((l, n)) for n in names] for l in range(L) for names in GATHER_GROUPS]
    shards = []
    for l, n in keys:
        ax, size = geometry[n]
        shard = _as_handled(n, a[n][l:l + 1]).astype(BF16)
        pad = [(0, 0)] * 3
        pad[ax] = (0, size - shard.shape[ax])
        shards.append(jnp.pad(shard, pad))
    key_axes = [geometry[n][0] for _, n in keys]
    fulls, thru, ag_send, ag_recv = _ag_start(shards, key_axes, members, cw_all, "allgather_start")

    def fetch(l, gi, after):
        g = l * len(GATHER_GROUPS) + gi
        axs = [key_axes[i] for i in members[g]]
        own, landed = _ag_wait([thru[i] for i in members[g]], [fulls[i] for i in members[g]], ag_send[g], ag_recv[g], axs, after,
                               f"allgather_wait_l{l}g{gi}")
        return dict(zip(GATHER_GROUPS[gi], _ag_forward(landed, own, axs, f"allgather_forward_l{l}g{gi}")))

    pending, own_part, landed_part, in_flight = [], {}, {}, []

    def finish_exchange(after):
        if not in_flight:
            return None
        l, names, g_thru, landing, send, recv_sems = in_flight.pop()
        tag = f"l{l}_{names[0]}_{len(names)}"
        axs = [geometry[n][0] for n in names]
        g_done, recv = _to_sibling_wait(g_thru, landing, send, recv_sems, axs, [geometry[n][1] for n in names], after,
                                       "reduce_sibling_wait_" + tag)
        parts = []
        for n, g, r, ax in zip(names, g_done, recv, axs):
            part, own_part[(l, n)] = _pair_add(g, r, ax, f"reduce_pair_add_l{l}_{n}")
            parts.append(part)
        landing, parts_thru, send, recv_sems, token = _rs_start(parts, "reduce_start_" + tag)
        pending.append((l, names, parts_thru, landing, send, recv_sems))
        last_token[0] = token
        return token

    last_token = [None]

    def reduce(l, names, grads):
        gl = [grads[n] for n in names]
        finish_exchange(gl[0])
        token = last_token[0]
        landing, g_thru, send, recv_sems, token = _to_sibling_start(
            gl, [geometry[n][0] for n in names], [geometry[n][1] for n in names], token,
            f"reduce_sibling_start_l{l}_{names[0]}_{len(names)}")
        in_flight.append((l, names, g_thru, landing, send, recv_sems))
        return token

    loss, grad_x, gsmall, dcw = _local_step(x[0], mem[0], loss_target[0], a, conv_w_full, fetch, reduce, finish_exchange, cfg)
    finish_exchange(grad_x)

    def wait_group(entry, after):
        l, names, parts_thru, landing, send, recv_sems = entry
        landed = _rs_wait(parts_thru, landing, send, recv_sems, after, f"reduce_wait_l{l}_{names[0]}_{len(names)}")
        for n, arr in zip(names, landed):
            landed_part[(l, n)] = arr

    for entry in pending[:-1]:
        wait_group(entry, grad_x)

    grads, delta, new_m, new_v = {}, {}, {}, {}

    def update(n):
        outs = _adamw_sum(_as_handled(n, a[n]), _as_handled(n, am[n]), _as_handled(n, av[n]),
                          [own_part[(l, n)] for l in range(L)], [landed_part[(l, n)] for l in range(L)], "adamw_" + n)
        grads[n], delta[n], new_m[n], new_v[n] = (_as_handled(n, o) for o in outs)

    last_names = pending[-1][1]
    early = [n for n in big_names if n not in last_names]
    for n in early:
        update(n)
    wait_group(pending[-1], delta[early[-1]])
    for n in last_names:
        update(n)

    small_rows = _pack_small(gsmall)
    n_small = small_rows.shape[0]
    dcw_rows = jnp.pad(dcw, ((0, 0), (0, CONV_PAD - taps), (0, 0))).reshape(-1, SMALL_COLS)
    summed = _exchange_small(jnp.concatenate([small_rows, dcw_rows], axis=0), True, "allreduce_small",
                             after=landed_part[(pending[-1][0], last_names[0])])
    g_small, _ = _unpack_small(summed[:n_small], a)
    dcw_sum = summed[n_small:].reshape(L, CONV_PAD, N_DEV * cshard)[:, :taps]
    grads.update(g_small)
    grads['conv_w'] = lax.dynamic_slice_in_dim(dcw_sum, dev * cshard, cshard, axis=2)
    delta['conv_w'], new_m['conv_w'], new_v['conv_w'] = _adamw(conv_w, am['conv_w'], av['conv_w'], grads['conv_w'], "adamw_conv_w")
    pw, pm, pv, pg = (_pack_small(d) for d in (a, am, av, g_small))
    for dst, packed in zip((delta, new_m, new_v), _adamw(pw, pm, pv, pg, "adamw_small")):
        dst.update(_unpack_small(packed, a)[0])

    total = lax.psum(loss.reshape(()), ("x", "y", "c"))
    return (total, grad_x[None], *[grads[n] for n in WEIGHTS], *[delta[n] for n in WEIGHTS], *[new_m[n] for n in WEIGHTS],
            *[new_v[n] for n in WEIGHTS])
```

```python
import math

import jax
import jax.numpy as jnp
from jax import lax
from jax.experimental import pallas as pl
from jax.experimental.pallas import tpu as pltpu

F32, BF16 = jnp.float32, jnp.bfloat16
S = jax.ShapeDtypeStruct
MESH = pl.DeviceIdType.MESH

EPS = 1e-6
NEG_INF = -1e30
HEAD_DIM = 64
N_XATTN_HEADS = 4
CONV_WIDTH = 31
CONV_PAD = 32
LANES = 128
ADAM_LR, ADAM_B1, ADAM_B2, ADAM_EPS, ADAM_WD, ADAM_STEP = 0.001, 0.9, 0.999, 1e-08, 0.01, 10
N_DEV = 8
VMEM_LIMIT_BYTES = 56 * 1024 * 1024
ROW_TILE = 512
MM_ROW_TILE = 1024
EPILOGUE_COLS = 512
SMALL_COLS = 512

NN = ((1,), (0,))
NT = ((1,), (1,))
TN = ((0,), (0,))

NAMES = ['x', 'mem', 'ffn1_norm_g', 'ffn1_w_gate', 'ffn1_w_up', 'ffn1_w_down', 'mix_norm_g', 'w_in', 'b_f', 'conv_w', 'conv_b',
         'conv_ln_g', 'conv_ln_b', 'attn_out_g', 'conv_out_g', 'w_out', 'xattn_norm_g', 'mem_norm_g', 'xattn_w_q', 'xattn_w_kv',
         'xattn_w_o', 'ffn2_norm_g', 'ffn2_w_gate', 'ffn2_w_up', 'ffn2_w_down', 'final_norm_g']
WEIGHTS = NAMES[2:]
BIG = [('ffn1_w_gate', 'colT'), ('ffn1_w_up', 'colT'), ('ffn1_w_down', 'row'), ('w_in', 'col'), ('w_out', 'row'),
       ('xattn_w_q', 'row'), ('xattn_w_kv', 'colT'), ('xattn_w_o', 'row'), ('ffn2_w_gate', 'colT'), ('ffn2_w_up', 'colT'),
       ('ffn2_w_down', 'row')]
TRANSPOSED = tuple(n for n, kind in BIG if kind == 'colT')


def _as_handled(n, v):
    return jnp.swapaxes(v, 1, 2) if n in TRANSPOSED else v
SMALL = ['ffn1_norm_g', 'mix_norm_g', 'xattn_norm_g', 'mem_norm_g', 'ffn2_norm_g', 'conv_b', 'conv_ln_g', 'conv_ln_b',
         'attn_out_g', 'conv_out_g', 'b_f', 'final_norm_g']


def _dot(a, b, dims):
    return lax.dot_general(a, b, (dims, ((), ())), preferred_element_type=F32)


def _full(shape):
    nd = len(shape)
    return pl.BlockSpec(shape, lambda *_: (0,) * nd)


def _tile(n, pref):
    for t in (pref, 512, 384, 256, 128, 64, 32, 16, 8):
        if t <= n and n % t == 0:
            return t
    return n


def _pcall(name, body, grid, in_specs, out_specs, out_shape, scratch=(), aliases=None, dep=None):
    n_in = len(in_specs)
    kernel_body = body
    if dep is not None:
        in_specs = list(in_specs) + [pl.BlockSpec(memory_space=pl.ANY)]

        def kernel_body(*refs):
            return body(*refs[:n_in], *refs[n_in + 1:])

    call = pl.pallas_call(
        kernel_body, grid=grid, in_specs=in_specs, out_specs=out_specs, out_shape=out_shape, scratch_shapes=list(scratch),
        name=name, input_output_aliases=aliases or {},
        compiler_params=pltpu.CompilerParams(dimension_semantics=("arbitrary",) * len(grid), vmem_limit_bytes=VMEM_LIMIT_BYTES))
    return call if dep is None else (lambda *args: call(*args, dep))


def _arr(w):
    return w[0] if isinstance(w, tuple) else w


def _wshape(w):
    return w[0].shape[1:] if isinstance(w, tuple) else w.shape


def _wspec(w, block, imap):
    if isinstance(w, tuple):
        layer = w[1]
        return pl.BlockSpec((None,) + block, lambda *g: (layer,) + imap(*g))
    return pl.BlockSpec(block, imap)


def _wfull(w):
    shape = _wshape(w)
    return _wspec(w, shape, lambda *_: (0,) * len(shape))


def _sigmoid(z):
    return jax.nn.sigmoid(z)


def _rstd(x):
    return lax.rsqrt(jnp.mean(x * x, axis=-1, keepdims=True) + EPS)


def _rms_bwd(dy, x, g):
    r = _rstd(x)
    xh = x * r
    u = dy * g
    dx = r * (u - xh * jnp.mean(u * xh, axis=-1, keepdims=True))
    return dx, dy * xh


def _colsum(v):
    return jnp.sum(v, axis=0, keepdims=True)


def _rms_fwd(x, g, name):
    T, D = x.shape
    tm = _tile(T, ROW_TILE)

    def body(x_ref, g_ref, h_ref):
        xv = x_ref[...]
        h_ref[...] = (xv * _rstd(xv) * g_ref[...]).astype(BF16)

    row = pl.BlockSpec((tm, D), lambda i: (i, 0))
    return _pcall(name, body, (T // tm,), [row, _full((1, D))], row, S((T, D), BF16))(x, g.reshape(1, D))


def _mm(a, w, out_dtype, name):
    M, K = a.shape
    N = _wshape(w)[1]
    tm = _tile(M, ROW_TILE)
    tn = N if N <= 1536 else N // 2

    def body(a_ref, w_ref, o_ref):
        o_ref[...] = _dot(a_ref[...].astype(BF16), w_ref[...], NN).astype(out_dtype)

    return _pcall(name, body, (N // tn, M // tm),
                  [pl.BlockSpec((tm, K), lambda j, i: (i, 0)), _wspec(w, (K, tn), lambda j, i: (0, j))],
                  pl.BlockSpec((tm, tn), lambda j, i: (i, j)), S((M, N), out_dtype))(a, _arr(w))


def _mm_res(a, w, res, scale, name):
    M, K = a.shape
    N = _wshape(w)[1]
    tm = _tile(M, ROW_TILE)

    def body(a_ref, w_ref, r_ref, o_ref):
        o_ref[...] = r_ref[...] + scale * _dot(a_ref[...], w_ref[...], NN)

    row = pl.BlockSpec((tm, N), lambda i: (i, 0))
    return _pcall(name, body, (M // tm,), [pl.BlockSpec((tm, K), lambda i: (i, 0)), _wfull(w), row], row,
                  S((M, N), F32))(a, _arr(w), res)


def _mm_nt(a, w, out_dtype, name, dep=None):
    M, K = a.shape
    N = _wshape(w)[0]
    tm = _tile(M, ROW_TILE)
    tn = N if N <= 1536 else N // 2

    def body(a_ref, w_ref, o_ref):
        o_ref[...] = _dot(a_ref[...].astype(BF16), w_ref[...], NT).astype(out_dtype)

    return _pcall(name, body, (N // tn, M // tm),
                  [pl.BlockSpec((tm, K), lambda j, i: (i, 0)), _wspec(w, (tn, K), lambda j, i: (j, 0))],
                  pl.BlockSpec((tm, tn), lambda j, i: (i, j)), S((M, N), out_dtype), dep=dep)(a, _arr(w))


def _wgrad(a, dy, scale, name, into):
    buf, layer, L = into
    T, M = a.shape
    N = dy.shape[1]
    tm = _tile(M, ROW_TILE)

    def body(a_ref, dy_ref, *rest):
        rest[-1][...] = (scale * _dot(a_ref[...].astype(BF16), dy_ref[...].astype(BF16), TN)).astype(BF16)

    in_specs = [pl.BlockSpec((T, tm), lambda i: (0, i)), _full((T, N))]
    args = [a, dy]
    if buf is not None:
        in_specs.append(pl.BlockSpec(memory_space=pl.ANY))
        args.append(buf)
    return _pcall(name, body, (M // tm,), in_specs, pl.BlockSpec((None, tm, N), lambda i: (layer, i, 0)), S((L, M, N), BF16),
                  aliases={2: 0} if buf is not None else None)(*args)


def _bwd_h(dots, x, g, dres, name, dep=None):
    T, D = x.shape
    tm = _tile(T, 256)
    n = len(dots)
    dims = [{'nt': NT, 'nn': NN, 'tn': TN}[m] for _, _, m in dots]

    def body(*refs):
        x_ref, g_ref, r_ref, dx_ref, dg_ref = refs[2 * n:]
        dh = None
        for k in range(n):
            part = _dot(refs[2 * k][...], refs[2 * k + 1][...], dims[k])
            dh = part if dh is None else dh + part
        dx, dgrow = _rms_bwd(dh, x_ref[...], g_ref[...])
        dx_ref[...] = r_ref[...] + dx

        @pl.when(pl.program_id(0) == 0)
        def _():
            dg_ref[...] = jnp.zeros_like(dg_ref)

        dg_ref[...] += _colsum(dgrow)

    in_specs, args = [], []
    for lhs, w, mode in dots:
        if mode == 'tn':
            in_specs.append(pl.BlockSpec((lhs.shape[0], tm), lambda i: (0, i)))
        else:
            in_specs.append(pl.BlockSpec((tm, lhs.shape[1]), lambda i: (i, 0)))
        in_specs.append(_wfull(w))
        args += [lhs, _arr(w)]
    row = pl.BlockSpec((tm, D), lambda i: (i, 0))
    in_specs += [row, _full((1, D)), row]
    return _pcall(name, body, (T // tm,), in_specs, [row, _full((1, D))], [S((T, D), F32), S((1, D), F32)], dep=dep)(
        *args, x, g.reshape(1, D), dres)


def _rms_gain_grad(dy, x, g, name):
    T, D = x.shape

    def body(dy_ref, x_ref, g_ref, dg_ref):
        _, dgrow = _rms_bwd(dy_ref[...], x_ref[...], g_ref[...])
        dg_ref[...] = _colsum(dgrow)

    return _pcall(name, body, (), [_full((T, D)), _full((T, D)), _full((1, D))], _full((1, D)), S((1, D), F32))(
        dy, x, g.reshape(1, D))


def _ffn_up(h, wg, wu, name):
    T, D = h.shape
    Fh = _wshape(wg)[0]
    tm = _tile(T, MM_ROW_TILE)
    tn = Fh if Fh <= 1536 else Fh // 2
    tc = _tile(tn, EPILOGUE_COLS)

    def body(h_ref, wg_ref, wu_ref, g_ref, u_ref, a_ref):
        hv = h_ref[...]
        for cb in range(tn // tc):
            cols = slice(cb * tc, (cb + 1) * tc)
            gv = _dot(hv, wg_ref[cols, :], NT)
            uv = _dot(hv, wu_ref[cols, :], NT)
            g_ref[:, cols] = gv.astype(BF16)
            u_ref[:, cols] = uv.astype(BF16)
            a_ref[:, cols] = (gv * _sigmoid(gv) * uv).astype(BF16)

    tile = pl.BlockSpec((tm, tn), lambda j, i: (i, j))
    return _pcall(name, body, (Fh // tn, T // tm),
                  [pl.BlockSpec((tm, D), lambda j, i: (i, 0)), _wspec(wg, (tn, D), lambda j, i: (j, 0)),
                   _wspec(wu, (tn, D), lambda j, i: (j, 0))],
                  [tile, tile, tile], [S((T, Fh), BF16)] * 3)(h, _arr(wg), _arr(wu))


def _ffn_bwd_act(dout, wd, gate, up, scale, name, dep=None):
    T, D = dout.shape
    Fh = _wshape(wd)[0]
    tm = _tile(T, MM_ROW_TILE)
    tn = Fh if Fh <= 1536 else Fh // 2
    tc = _tile(tn, EPILOGUE_COLS)

    def body(d_ref, w_ref, g_ref, u_ref, dg_ref, du_ref):
        dv = d_ref[...].astype(BF16)
        for cb in range(tn // tc):
            cols = slice(cb * tc, (cb + 1) * tc)
            da = scale * _dot(dv, w_ref[cols, :], NT)
            gv = g_ref[:, cols].astype(F32)
            uv = u_ref[:, cols].astype(F32)
            sg = _sigmoid(gv)
            dg_ref[:, cols] = (da * uv * (sg * (1.0 + gv * (1.0 - sg)))).astype(BF16)
            du_ref[:, cols] = (da * (gv * sg)).astype(BF16)

    tile = pl.BlockSpec((tm, tn), lambda j, i: (i, j))
    return _pcall(name, body, (Fh // tn, T // tm),
                  [pl.BlockSpec((tm, D), lambda j, i: (i, 0)), _wspec(wd, (tn, D), lambda j, i: (j, 0)), tile, tile],
                  [tile, tile], [S((T, Fh), BF16)] * 2, dep=dep)(dout, _arr(wd), gate, up)


def _loss_head(x, g, tgt, name):
    T, D = x.shape
    tm = _tile(T, ROW_TILE)

    def body(x_ref, g_ref, t_ref, loss_ref, dx_ref, dg_ref):
        xv = x_ref[...]
        gv = g_ref[...]
        r = _rstd(xv)
        xh = xv * r
        e = xh * gv - t_ref[...]
        dy = e * (1.0 / D)
        u = dy * gv
        dx_ref[...] = r * (u - xh * jnp.mean(u * xh, axis=-1, keepdims=True))

        @pl.when(pl.program_id(0) == 0)
        def _():
            dg_ref[...] = jnp.zeros_like(dg_ref)
            loss_ref[...] = jnp.zeros_like(loss_ref)

        dg_ref[...] += _colsum(dy * xh)
        loss_ref[...] += 0.5 * _colsum(jnp.mean(e * e, axis=-1, keepdims=True))

    row = pl.BlockSpec((tm, D), lambda i: (i, 0))
    return _pcall(name, body, (T // tm,), [row, _full((1, D)), row], [_full((1, 1)), row, _full((1, D))],
                  [S((1, 1), F32), S((T, D), F32), S((1, D), F32)])(x, g.reshape(1, D), tgt)


def _split3(xb):
    hi = xb.astype(BF16)
    r1 = xb - hi.astype(F32)
    mid = r1.astype(BF16)
    lo = (r1 - mid.astype(F32)).astype(BF16)
    return hi, mid, lo


def _fox_prep(h, wft, bft, name):
    T, D = h.shape
    blk = _tile(T, 256)

    def body(h_ref, w_ref, b_ref, ct_ref, sg_ref):
        z = _dot(w_ref[...], h_ref[...], NT) + b_ref[...]
        sg_ref[...] = 1.0 - _sigmoid(z)
        logf = jnp.minimum(z, 0.0) - jnp.log1p(jnp.exp(-jnp.abs(z)))
        upper = (lax.broadcasted_iota(jnp.int32, (blk, blk), 0) <= lax.broadcasted_iota(jnp.int32, (blk, blk), 1)).astype(BF16)
        carry = jnp.zeros((16, 1), F32)
        for b in range(T // blk):
            hi, mid, lo = _split3(logf[:, b * blk:(b + 1) * blk])
            cb = _dot(hi, upper, NN) + _dot(mid, upper, NN) + _dot(lo, upper, NN) + carry
            ct_ref[:, b * blk:(b + 1) * blk] = cb
            carry = cb[:, blk - 1:blk]

    return _pcall(name, body, (), [_full((T, D)), _full((16, D)), _full((16, 1))], [_full((16, T)), _full((16, T))],
                  [S((16, T), F32), S((16, T), F32)])(h, wft, bft)


def _fox_prep_bwd(dcs, sg, h, name):
    T, D = h.shape
    blk = _tile(T, 256)
    nb = T // blk

    def body(dcs_ref, sg_ref, h_ref, dfl_ref, dw_ref, db_ref):
        lower = (lax.broadcasted_iota(jnp.int32, (blk, blk), 0) >= lax.broadcasted_iota(jnp.int32, (blk, blk), 1)).astype(BF16)
        carry = jnp.zeros((16, 1), F32)
        db = jnp.zeros((16, 1), F32)
        for b in range(nb - 1, -1, -1):
            cols = slice(b * blk, (b + 1) * blk)
            hi, mid, lo = _split3(-dcs_ref[:, cols])
            dlogf = _dot(hi, lower, NN) + _dot(mid, lower, NN) + _dot(lo, lower, NN) + carry
            carry = dlogf[:, 0:1]
            dfl = dlogf * sg_ref[:, cols]
            db = db + jnp.sum(dfl, axis=-1, keepdims=True)
            dfl_ref[:, cols] = dfl.astype(BF16)
        db_ref[...] = db
        dw_ref[...] = _dot(dfl_ref[...], h_ref[...], NN)

    return _pcall(name, body, (), [_full((16, T)), _full((16, T)), _full((T, D))],
                  [_full((16, T)), _full((16, D)), _full((16, 1))],
                  [S((16, T), BF16), S((16, D), F32), S((16, 1), F32)])(dcs, sg, h)


def _fox_logits(q, k, c_col, c_row):
    tq, kp = q.shape[0], k.shape[0]
    s = _dot(q, k, NT) * (1.0 / math.sqrt(HEAD_DIM)) + (c_col - c_row)
    row = lax.broadcasted_iota(jnp.int32, (tq, tq), 0)
    col = lax.broadcasted_iota(jnp.int32, (tq, tq), 1)
    diag = jnp.where(row >= col, s[:, kp - tq:], NEG_INF)
    return diag if kp == tq else jnp.concatenate([s[:, :kp - tq], diag], axis=1)


def _fox_specs(T, n_pairs):
    qs = pl.BlockSpec((T, LANES), lambda p: (0, p))
    ks = pl.BlockSpec((T, LANES), lambda p: (0, n_pairs + p))
    vs = pl.BlockSpec((T, LANES), lambda p: (0, 2 * n_pairs + p))
    col = pl.BlockSpec((2, T, 1), lambda p: (p, 0, 0))
    rowv = pl.BlockSpec((None, 2, T), lambda p: (p, 0, 0))
    return qs, ks, vs, col, rowv


def _fox_fwd(qkv, c_col, c_row, name):
    T = qkv.shape[0]
    DA = qkv.shape[1] // 3
    n_pairs = DA // LANES
    tq = _tile(T, 256)

    def body(q_ref, k_ref, v_ref, c_ref, ct_ref, o_ref, lse_ref):
        for hh in range(2):
            sl = slice(hh * HEAD_DIM, (hh + 1) * HEAD_DIM)
            for i in range(T // tq):
                rows = slice(i * tq, (i + 1) * tq)
                kp = (i + 1) * tq
                s = _fox_logits(q_ref[rows, sl], k_ref[0:kp, sl], c_ref[hh, rows, :], ct_ref[hh:hh + 1, 0:kp])
                m = jnp.max(s, axis=-1, keepdims=True)
                p = jnp.exp(s - m)
                l = jnp.sum(p, axis=-1, keepdims=True)
                o_ref[rows, sl] = _dot(p.astype(BF16), v_ref[0:kp, sl], NN) / l
                lse_ref[hh, rows, :] = m + jnp.log(l)

    qs, ks, vs, col, rowv = _fox_specs(T, n_pairs)
    return _pcall(name, body, (n_pairs,), [qs, ks, vs, col, rowv], [qs, col],
                  [S((T, DA), F32), S((2 * n_pairs, T, 1), F32)])(qkv, qkv, qkv, c_col, c_row)


def _fox_bwd(qkv, c_col, c_row, lse, do, name):
    T = qkv.shape[0]
    DA = qkv.shape[1] // 3
    n_pairs = DA // LANES
    tq = _tile(T, 256)
    scale = 1.0 / math.sqrt(HEAD_DIM)

    def body(q_ref, k_ref, v_ref, c_ref, ct_ref, lse_ref, do_ref, dq_ref, dk_ref, dv_ref, dcs_ref, dk_acc, dv_acc):
        dk_acc[...] = jnp.zeros_like(dk_acc)
        dv_acc[...] = jnp.zeros_like(dv_acc)
        dcs_ref[...] = jnp.zeros_like(dcs_ref)
        for hh in range(2):
            sl = slice(hh * HEAD_DIM, (hh + 1) * HEAD_DIM)
            for i in range(T // tq):
                rows = slice(i * tq, (i + 1) * tq)
                kp = (i + 1) * tq
                q = q_ref[rows, sl]
                k = k_ref[0:kp, sl]
                dob = do_ref[rows, sl]
                s = _fox_logits(q, k, c_ref[hh, rows, :], ct_ref[hh:hh + 1, 0:kp])
                p = jnp.exp(s - lse_ref[hh, rows, :])
                dp = _dot(dob, v_ref[0:kp, sl], NT)
                ds = p * (dp - jnp.sum(p * dp, axis=-1, keepdims=True))
                dsb = ds.astype(BF16)
                dq_ref[rows, sl] = (_dot(dsb, k, NN) * scale).astype(BF16)
                dk_acc[0:kp, sl] += _dot(dsb, q, TN) * scale
                dv_acc[0:kp, sl] += _dot(p.astype(BF16), dob, TN)
                dcs_ref[hh:hh + 1, 0:kp] += _colsum(ds)
        dk_ref[...] = dk_acc[...].astype(BF16)
        dv_ref[...] = dv_acc[...].astype(BF16)

    qs, ks, vs, col, rowv = _fox_specs(T, n_pairs)
    return _pcall(name, body, (n_pairs,), [qs, ks, vs, col, rowv, col, qs], [qs, qs, qs, rowv],
                  [S((T, DA), BF16)] * 3 + [S((n_pairs, 2, T), F32)],
                  scratch=[pltpu.VMEM((T, LANES), F32), pltpu.VMEM((T, LANES), F32)])(qkv, qkv, qkv, c_col, c_row, lse, do)


def _conv_fwd(ag, w, b, name):
    T = ag.shape[0]
    DC = ag.shape[1] // 2
    nb = DC // LANES
    tr = _tile(T, 256)

    def body(a_ref, g_ref, w_ref, b_ref, y_ref, pad):
        pad[0:CONV_PAD, :] = jnp.zeros((CONV_PAD, LANES), F32)
        pad[CONV_PAD:CONV_PAD + T, :] = a_ref[...] * _sigmoid(g_ref[...])
        for r in range(T // tr):
            acc = jnp.zeros((tr, LANES), F32) + b_ref[...]
            for j in range(CONV_WIDTH):
                o = r * tr + CONV_PAD - (CONV_WIDTH - 1) + j
                acc = acc + w_ref[j:j + 1, :] * pad[o:o + tr, :]
            y_ref[r * tr:(r + 1) * tr, :] = acc

    blk = pl.BlockSpec((T, LANES), lambda c: (0, c))
    return _pcall(name, body, (nb,), [blk, pl.BlockSpec((T, LANES), lambda c: (0, nb + c)),
                                      pl.BlockSpec((CONV_PAD, LANES), lambda c: (0, c)), pl.BlockSpec((1, LANES), lambda c: (0, c))],
                  blk, S((T, DC), F32), scratch=[pltpu.VMEM((T + CONV_PAD, LANES), F32)])(ag, ag, w, b)


def _conv_bwd(dy, ag, w, name):
    T = ag.shape[0]
    DC = ag.shape[1] // 2
    nb = DC // LANES
    tr = _tile(T, 256)

    def body(dy_ref, a_ref, g_ref, w_ref, da_ref, dg_ref, dw_ref, db_ref, pad, dpad):
        av = a_ref[...]
        sg = _sigmoid(g_ref[...])
        pad[0:CONV_PAD, :] = jnp.zeros((CONV_PAD, LANES), F32)
        pad[CONV_PAD:CONV_PAD + T, :] = av * sg
        dpad[0:T, :] = dy_ref[...]
        dpad[T:T + CONV_PAD, :] = jnp.zeros((CONV_PAD, LANES), F32)
        db_ref[...] = _colsum(dy_ref[...])
        dw_ref[...] = jnp.zeros_like(dw_ref)
        for j in range(CONV_WIDTH):
            acc = jnp.zeros((tr, LANES), F32)
            for r in range(T // tr):
                o = r * tr + CONV_PAD - (CONV_WIDTH - 1) + j
                acc = acc + dpad[r * tr:(r + 1) * tr, :] * pad[o:o + tr, :]
            dw_ref[j:j + 1, :] = _colsum(acc)
        for r in range(T // tr):
            acc = jnp.zeros((tr, LANES), F32)
            for j in range(CONV_WIDTH):
                o = r * tr + (CONV_WIDTH - 1) - j
                acc = acc + w_ref[j:j + 1, :] * dpad[o:o + tr, :]
            rows = slice(r * tr, (r + 1) * tr)
            sgr = sg[rows, :]
            da_ref[rows, :] = (acc * sgr).astype(BF16)
            dg_ref[rows, :] = (acc * av[rows, :] * sgr * (1.0 - sgr)).astype(BF16)

    blk = pl.BlockSpec((T, LANES), lambda c: (0, c))
    wblk = pl.BlockSpec((CONV_PAD, LANES), lambda c: (0, c))
    return _pcall(name, body, (nb,), [blk, blk, pl.BlockSpec((T, LANES), lambda c: (0, nb + c)), wblk],
                  [blk, blk, wblk, pl.BlockSpec((1, LANES), lambda c: (0, c))],
                  [S((T, DC), BF16), S((T, DC), BF16), S((CONV_PAD, DC), F32), S((1, DC), F32)],
                  scratch=[pltpu.VMEM((T + CONV_PAD, LANES), F32), pltpu.VMEM((T + CONV_PAD, LANES), F32)])(dy, ag, ag, w)


def _conv_norms(yc, lg, lb):
    mu = jnp.mean(yc, axis=-1, keepdims=True)
    xc = yc - mu
    rs = lax.rsqrt(jnp.mean(xc * xc, axis=-1, keepdims=True) + EPS)
    xh = xc * rs
    z = xh * lg + lb
    sg = _sigmoid(z)
    return rs, xh, z, sg, z * sg


def _mix_post(attn, yc, ag, cg, lg, lb, name):
    T, DA = attn.shape
    DC = yc.shape[1]
    tm = _tile(T, ROW_TILE)

    def body(at_ref, yc_ref, ag_ref, cg_ref, lg_ref, lb_ref, y_ref):
        at = at_ref[...]
        y_ref[:, 0:DA] = (at * _rstd(at) * ag_ref[...]).astype(BF16)
        _, _, _, _, sv = _conv_norms(yc_ref[...], lg_ref[...], lb_ref[...])
        y_ref[:, DA:DA + DC] = (sv * _rstd(sv) * cg_ref[...]).astype(BF16)

    return _pcall(name, body, (T // tm,),
                  [pl.BlockSpec((tm, DA), lambda i: (i, 0)), pl.BlockSpec((tm, DC), lambda i: (i, 0)), _full((1, DA)),
                   _full((1, DC)), _full((1, DC)), _full((1, DC))],
                  pl.BlockSpec((tm, DA + DC), lambda i: (i, 0)), S((T, DA + DC), BF16))(attn, yc, ag, cg, lg, lb)


def _mix_post_bwd(dy, attn, yc, ag, cg, lg, lb, name):
    T, DA = attn.shape
    DC = yc.shape[1]
    tm = _tile(T, ROW_TILE)

    def body(dy_ref, at_ref, yc_ref, ag_ref, cg_ref, lg_ref, lb_ref, dat_ref, dyc_ref, dag_ref, dcg_ref, dlg_ref, dlb_ref):
        dat, dag_rows = _rms_bwd(dy_ref[:, 0:DA], at_ref[...], ag_ref[...])
        dat_ref[...] = dat.astype(BF16)
        lgv = lg_ref[...]
        rs, xh, z, sg, sv = _conv_norms(yc_ref[...], lgv, lb_ref[...])
        dsv, dcg_rows = _rms_bwd(dy_ref[:, DA:DA + DC], sv, cg_ref[...])
        dz = dsv * (sg * (1.0 + z * (1.0 - sg)))
        dxh = dz * lgv
        dyc_ref[...] = rs * (dxh - jnp.mean(dxh, axis=-1, keepdims=True) - xh * jnp.mean(dxh * xh, axis=-1, keepdims=True))

        @pl.when(pl.program_id(0) == 0)
        def _():
            for r in (dag_ref, dcg_ref, dlg_ref, dlb_ref):
                r[...] = jnp.zeros_like(r)

        dag_ref[...] += _colsum(dag_rows)
        dcg_ref[...] += _colsum(dcg_rows)
        dlg_ref[...] += _colsum(dz * xh)
        dlb_ref[...] += _colsum(dz)

    ra = pl.BlockSpec((tm, DA), lambda i: (i, 0))
    rc = pl.BlockSpec((tm, DC), lambda i: (i, 0))
    return _pcall(name, body, (T // tm,),
                  [pl.BlockSpec((tm, DA + DC), lambda i: (i, 0)), ra, rc, _full((1, DA)), _full((1, DC)), _full((1, DC)),
                   _full((1, DC))],
                  [ra, rc, _full((1, DA)), _full((1, DC)), _full((1, DC)), _full((1, DC))],
                  [S((T, DA), BF16), S((T, DC), F32), S((1, DA), F32), S((1, DC), F32), S((1, DC), F32), S((1, DC), F32)])(
        dy, attn, yc, ag, cg, lg, lb)


def _xattn_probs(q, k, xd):
    s = _dot(q, k, NT) * (1.0 / math.sqrt(xd))
    p = jnp.exp(s - jnp.max(s, axis=-1, keepdims=True))
    return p / jnp.sum(p, axis=-1, keepdims=True)


def _xattn_fwd(q, kv, name):
    T, D = q.shape
    M = kv.shape[0]
    xd = D // N_XATTN_HEADS
    tq = _tile(T, ROW_TILE)

    def body(q_ref, kv_ref, o_ref):
        for h in range(N_XATTN_HEADS):
            sl = slice(h * xd, (h + 1) * xd)
            p = _xattn_probs(q_ref[:, sl], kv_ref[:, sl], xd)
            o_ref[:, sl] = _dot(p.astype(BF16), kv_ref[:, D + h * xd:D + (h + 1) * xd], NN).astype(BF16)

    row = pl.BlockSpec((tq, D), lambda i: (i, 0))
    return _pcall(name, body, (T // tq,), [row, _full((M, 2 * D))], row, S((T, D), BF16))(q, kv)


def _xattn_bwd(q, kv, do, name):
    T, D = q.shape
    M = kv.shape[0]
    xd = D // N_XATTN_HEADS
    tq = _tile(T, ROW_TILE)
    scale = 1.0 / math.sqrt(xd)

    def body(q_ref, kv_ref, do_ref, dq_ref, dkv_ref):
        @pl.when(pl.program_id(0) == 0)
        def _():
            dkv_ref[...] = jnp.zeros_like(dkv_ref)

        for h in range(N_XATTN_HEADS):
            sl = slice(h * xd, (h + 1) * xd)
            vsl = slice(D + h * xd, D + (h + 1) * xd)
            qh = q_ref[:, sl]
            kh = kv_ref[:, sl]
            doh = do_ref[:, sl]
            p = _xattn_probs(qh, kh, xd)
            dp = _dot(doh, kv_ref[:, vsl], NT)
            ds = (p * (dp - jnp.sum(p * dp, axis=-1, keepdims=True)) * scale).astype(BF16)
            dq_ref[:, sl] = _dot(ds, kh, NN).astype(BF16)
            dkv_ref[:, sl] += _dot(ds, qh, TN)
            dkv_ref[:, vsl] += _dot(p.astype(BF16), doh, TN)

    row = pl.BlockSpec((tq, D), lambda i: (i, 0))
    return _pcall(name, body, (T // tq,), [row, _full((M, 2 * D)), row], [row, _full((M, 2 * D))],
                  [S((T, D), BF16), S((M, 2 * D), F32)])(q, kv, do)


def _adamw(w, m, v, g, name):
    shape = w.shape
    C = shape[-1]
    R = w.size // C
    tr = R if R <= 512 else _tile(R, 512)

    def body(w_ref, m_ref, v_ref, g_ref, d_ref, nm_ref, nv_ref):
        gv = g_ref[...]
        mv = ADAM_B1 * m_ref[...] + (1.0 - ADAM_B1) * gv
        vv = ADAM_B2 * v_ref[...] + (1.0 - ADAM_B2) * (gv * gv)
        m_hat = mv / (1.0 - ADAM_B1 ** ADAM_STEP)
        v_hat = vv / (1.0 - ADAM_B2 ** ADAM_STEP)
        d_ref[...] = -ADAM_LR * (m_hat / (jnp.sqrt(v_hat) + ADAM_EPS) + ADAM_WD * w_ref[...])
        nm_ref[...] = mv
        nv_ref[...] = vv

    blk = pl.BlockSpec((tr, C), lambda i: (i, 0))
    outs = _pcall(name, body, (R // tr,), [blk] * 4, [blk] * 3, [S((R, C), F32)] * 3)(
        w.reshape(R, C), m.reshape(R, C), v.reshape(R, C), g.reshape(R, C))
    return [o.reshape(shape) for o in outs]


def _place_scalars():
    return jnp.stack([lax.axis_index("c"), 2 * lax.axis_index("x") + lax.axis_index("y")]).astype(jnp.int32)


def _adamw_sum(w, m, v, owns, landed, name):
    L, p, q = w.shape
    qq = owns[0].shape[2]
    tr = next((t for t in range(min(p, ROW_TILE) // 16 * 16, 0, -16) if p % t == 0), p)

    def body(place_ref, w_ref, m_ref, v_ref, *rest):
        own_refs, land_refs = rest[:L], rest[L:2 * L]
        g_ref, d_ref, nm_ref, nv_ref = rest[2 * L:]
        chip = place_ref[1]

        def update(l):
            own = own_refs[l][...].astype(F32)
            gs = None
            for k in range(4):
                term = jnp.where(chip == k, own, land_refs[l][k].astype(F32))
                gs = term if gs is None else gs + term
            gv = gs[:, 0:q]
            mv = ADAM_B1 * m_ref[...] + (1.0 - ADAM_B1) * gv
            vv = ADAM_B2 * v_ref[...] + (1.0 - ADAM_B2) * (gv * gv)
            m_hat = mv / (1.0 - ADAM_B1 ** ADAM_STEP)
            v_hat = vv / (1.0 - ADAM_B2 ** ADAM_STEP)
            g_ref[...] = gv
            d_ref[...] = -ADAM_LR * (m_hat / (jnp.sqrt(v_hat) + ADAM_EPS) + ADAM_WD * w_ref[...])
            nm_ref[...] = mv
            nv_ref[...] = vv

        for l in range(L):
            pl.when(pl.program_id(0) == l)(lambda l=l: update(l))

    def rows_of(layer):
        return lambda l, i, place: jnp.where(l == layer, i, 0)

    blk = pl.BlockSpec((None, tr, q), lambda l, i, place: (l, i, 0))
    in_specs = [blk, blk, blk]
    in_specs += [pl.BlockSpec((None, tr, qq), lambda l, i, place, r=rows_of(k): (0, r(l, i, place), 0)) for k in range(L)]
    in_specs += [pl.BlockSpec((4, None, tr, qq), lambda l, i, place, r=rows_of(k): (0, 0, r(l, i, place), 0)) for k in range(L)]
    gs = pltpu.PrefetchScalarGridSpec(num_scalar_prefetch=1, grid=(L, p // tr), in_specs=in_specs, out_specs=[blk] * 4)
    return pl.pallas_call(body, grid_spec=gs, out_shape=[S((L, p, q), F32)] * 4, name=name,
                          compiler_params=pltpu.CompilerParams(dimension_semantics=("arbitrary", "arbitrary")))(
        _place_scalars(), w, m, v, *owns, *landed)


def _pair_add(g, recv, axis, name):
    _, _, p, q = recv.shape

    def body(place_ref, g_ref, r_ref, o_ref, own_ref):
        s = (g_ref[...].astype(F32) + r_ref[...].astype(F32)).astype(BF16)
        o_ref[...] = s

        @pl.when(pl.program_id(0) == place_ref[1])
        def _():
            own_ref[...] = s

    if axis == 1:
        gspec = pl.BlockSpec((None, p, q), lambda k, place: (0, 2 * k + place[0], 0))
    else:
        gspec = pl.BlockSpec((None, p, q), lambda k, place: (0, 0, 2 * k + place[0]))
    part = pl.BlockSpec((None, None, p, q), lambda k, place: (k, 0, 0, 0))
    own = pl.BlockSpec((None, p, q), lambda k, place: (0, 0, 0))
    gs = pltpu.PrefetchScalarGridSpec(num_scalar_prefetch=1, grid=(4,), in_specs=[gspec, part], out_specs=[part, own])
    return pl.pallas_call(body, grid_spec=gs, out_shape=[S((4, 1, p, q), BF16), S((1, p, q), BF16)], name=name,
                          compiler_params=pltpu.CompilerParams(dimension_semantics=("arbitrary",)))(_place_scalars(), g, recv)


def _win_pieces(n_attn, n_heads, n_conv, shard, chunk):
    bounds = [0, 3 * n_attn, 3 * n_attn + n_heads, 3 * n_attn + n_heads + 2 * n_conv]
    pieces = []
    for j in range(N_DEV):
        lo, hi = shard * j, shard * (j + 1)
        for r in range(3):
            a, b = max(lo, bounds[r]), min(hi, bounds[r + 1])
            if a < b:
                pieces.append((r, a - bounds[r], b - bounds[r], chunk * j + a - lo))
    return pieces


def _win_split(w_in, pieces, widths, name):
    L, D, C = w_in.shape
    tr = _tile(D, 256)

    def body(x_ref, *outs):
        outs[1][...] = jnp.zeros_like(outs[1])
        for r, d0, d1, s0 in pieces:
            outs[r][:, d0:d1] = x_ref[:, s0:s0 + d1 - d0]

    return _pcall(name, body, (L, D // tr), [pl.BlockSpec((None, tr, C), lambda l, i: (l, i, 0))],
                  [pl.BlockSpec((None, tr, wd), lambda l, i: (l, i, 0)) for wd in widths],
                  [S((L, D, wd), BF16) for wd in widths])(w_in)


def _win_merge(parts, pieces, chunked_cols, name):
    L, D, _ = parts[0].shape
    tr = _tile(D, 256)

    def body(a_ref, b_ref, c_ref, o_ref):
        ins = (a_ref, b_ref, c_ref)
        o_ref[...] = jnp.zeros_like(o_ref)
        for r, d0, d1, s0 in pieces:
            o_ref[:, s0:s0 + d1 - d0] = ins[r][:, d0:d1]

    return _pcall(name, body, (L, D // tr), [pl.BlockSpec((None, tr, x.shape[2]), lambda l, i: (l, i, 0)) for x in parts],
                  pl.BlockSpec((None, tr, chunked_cols), lambda l, i: (l, i, 0)), S((L, D, chunked_cols), BF16))(*parts)


def _place():
    return lax.axis_index("x"), lax.axis_index("y"), lax.axis_index("c")


def _flip(v, f):
    return 1 - v if f else v


def _window(ref, axis, size, dev):
    start = dev * size if isinstance(dev, int) else pl.multiple_of(dev * size, LANES if axis == 2 else 16)
    return ref.at[:, pl.ds(start, size), :] if axis == 1 else ref.at[:, :, pl.ds(start, size)]


HBM_SPEC = pl.BlockSpec(memory_space=pltpu.HBM)
SEM_SPEC = pl.BlockSpec(memory_space=pltpu.SEMAPHORE)
SPLIT_COPY_PARAMS = dict(has_side_effects=pltpu.SideEffectType.DATAFLOW_SIDE_EFFECTING)


def _hbm(v):
    return pltpu.with_memory_space_constraint(v, pltpu.HBM)


def _full_shape(shard, axis):
    return tuple(N_DEV * d if i == axis else d for i, d in enumerate(shard.shape))


def _ag_peers(x, y, c):
    return [(x, y, 1 - c), (1 - x, y, c), (x, 1 - y, c), (1 - x, 1 - y, c)]


SIBLING_COLLECTIVE_ID = 0


def _sibling_handshake(x, y, c):
    barrier = pltpu.get_barrier_semaphore()
    pl.semaphore_signal(barrier, inc=1, device_id=(x, y, 1 - c), device_id_type=MESH)
    pl.semaphore_wait(barrier, 1)


def _ag_start(shards, axes, groups, after, name):
    n, ng = len(shards), len(groups)
    sizes = [s.shape[ax] for s, ax in zip(shards, axes)]
    where = {w: (g, i) for g, members in enumerate(groups) for i, w in enumerate(members)}

    def body(*refs):
        xs, fulls = refs[:n], refs[n + 1:2 * n + 1]
        send, recv = refs[3 * n + 1:3 * n + 1 + ng], refs[3 * n + 1 + ng:]
        x, y, c = _place()
        for members in groups:
            for w in members:
                g, i = where[w]
                for k, to in enumerate(_ag_peers(x, y, c)):
                    pltpu.make_async_remote_copy(
                        src_ref=xs[w], dst_ref=_window(fulls[w], axes[w], sizes[w], 4 * x + 2 * y + c),
                        send_sem=send[g].at[4 * i + k], recv_sem=recv[g].at[4 * i + k], device_id=to, device_id_type=MESH).start()

    sems = [pltpu.SemaphoreType.DMA((4 * len(m),)) for m in groups]
    outs = pl.pallas_call(
        body, name=name,
        out_shape=[pltpu.HBM(_full_shape(s, ax), s.dtype) for s, ax in zip(shards, axes)] + [pltpu.HBM(s.shape, s.dtype) for s in shards]
        + sems + sems,
        in_specs=[HBM_SPEC] * n + [pl.BlockSpec(memory_space=pl.ANY)], out_specs=[HBM_SPEC] * (2 * n) + [SEM_SPEC] * (2 * ng),
        input_output_aliases={w: n + w for w in range(n)},
        compiler_params=pltpu.CompilerParams(**SPLIT_COPY_PARAMS))(*[_hbm(s) for s in shards], after)
    return outs[:n], outs[n:2 * n], outs[2 * n:2 * n + ng], outs[2 * n + ng:]


def _ag_wait(shards, fulls, send_sems, recv_sems, axes, after, name):
    n = len(shards)
    sizes = [s.shape[ax] for s, ax in zip(shards, axes)]

    def body(*refs):
        xs = refs[:n]
        send, recv = refs[2 * n], refs[2 * n + 1]
        landed = refs[3 * n + 3:]
        x, y, c = _place()
        for w in range(n):
            for k, frm in enumerate(_ag_peers(x, y, c)):
                copy = pltpu.make_async_remote_copy(
                    src_ref=xs[w], dst_ref=_window(landed[w], axes[w], sizes[w], 4 * frm[0] + 2 * frm[1] + frm[2]),
                    send_sem=send.at[4 * w + k], recv_sem=recv.at[4 * w + k], device_id=frm, device_id_type=MESH)
                copy.wait_send()
                copy.wait_recv()

    outs = pl.pallas_call(
        body, name=name, out_shape=[pltpu.HBM(v.shape, v.dtype) for v in list(shards) + list(fulls)],
        in_specs=[HBM_SPEC] * (2 * n) + [SEM_SPEC, SEM_SPEC, pl.BlockSpec(memory_space=pl.ANY)], out_specs=[HBM_SPEC] * (2 * n),
        input_output_aliases={i: i for i in range(2 * n)},
        compiler_params=pltpu.CompilerParams(**SPLIT_COPY_PARAMS))(*shards, *fulls, send_sems, recv_sems, after)
    return outs[:n], outs[n:]


def _ag_forward(fulls, shards, axes, name):
    n = len(fulls)
    sizes = [s.shape[ax] for s, ax in zip(shards, axes)]

    def body(*refs):
        xs, full_refs = refs[:n], refs[2 * n:3 * n]
        send_sems, recv_sems, local_sems = refs[3 * n:3 * n + 3]
        staged = refs[3 * n + 3:]
        x, y, c = _place()
        _sibling_handshake(x, y, c)
        chips = [(1 - x, y), (x, 1 - y), (1 - x, 1 - y)]
        loads = [pltpu.make_async_copy(xs[w], staged[w], local_sems.at[w]) for w in range(n)]
        for cp in loads:
            cp.start()
        copies = []
        for w in range(n):
            for j, (px, py) in enumerate(chips):
                sent = _window(full_refs[w], axes[w], sizes[w], 4 * px + 2 * py + c)
                got = _window(full_refs[w], axes[w], sizes[w], 4 * px + 2 * py + 1 - c)
                out = pltpu.make_async_remote_copy(src_ref=sent, dst_ref=sent, send_sem=send_sems.at[3 * w + j],
                                                   recv_sem=recv_sems.at[3 * w + j], device_id=(x, y, 1 - c), device_id_type=MESH)
                out.start()
                back = pltpu.make_async_remote_copy(src_ref=got, dst_ref=got, send_sem=send_sems.at[3 * w + j],
                                                    recv_sem=recv_sems.at[3 * w + j], device_id=(x, y, 1 - c), device_id_type=MESH)
                copies.append((out, back))
        stores = []
        for w in range(n):
            loads[w].wait()
            store = pltpu.make_async_copy(staged[w], _window(full_refs[w], axes[w], sizes[w], 4 * x + 2 * y + c), local_sems.at[w])
            store.start()
            stores.append(store)
        for out, back in copies:
            out.wait_send()
            back.wait_recv()
        for cp in stores:
            cp.wait()

    any_spec = pl.BlockSpec(memory_space=pl.ANY)
    outs = pl.pallas_call(
        body, name=name, out_shape=[S(f.shape, f.dtype) for f in fulls], in_specs=[any_spec] * (2 * n), out_specs=[any_spec] * n,
        input_output_aliases={n + w: w for w in range(n)},
        scratch_shapes=[pltpu.SemaphoreType.DMA((3 * n,)), pltpu.SemaphoreType.DMA((3 * n,)), pltpu.SemaphoreType.DMA((n,))]
        + [pltpu.VMEM(s.shape, s.dtype) for s in shards],
        compiler_params=pltpu.CompilerParams(collective_id=SIBLING_COLLECTIVE_ID))(*shards, *fulls)
    return outs


def _sibling_copy(g_ref, land_ref, send_sems, recv_sems, axis, size, w, k, x, y, c):
    return pltpu.make_async_remote_copy(
        src_ref=_window(g_ref, axis, size, 2 * k + 1 - c), dst_ref=land_ref.at[k], send_sem=send_sems.at[4 * w + k],
        recv_sem=recv_sems.at[4 * w + k], device_id=(x, y, 1 - c), device_id_type=MESH)


def _to_sibling_start(grads, axes, sizes, after, name):
    n = len(grads)
    landing = []
    for g, ax, sz in zip(grads, axes, sizes):
        L, K, N = g.shape
        landing.append(pltpu.HBM((4, L, sz, N) if ax == 1 else (4, L, K, sz), g.dtype))
    extra = [] if after is None else [after]

    def body(*refs):
        g_refs = refs[:n]
        land_refs = refs[n + len(extra):2 * n + len(extra)]
        send_sems, recv_sems, token = refs[3 * n + len(extra):]
        x, y, c = _place()
        _sibling_handshake(x, y, c)
        for w in range(n):
            for k in range(4):
                _sibling_copy(g_refs[w], land_refs[w], send_sems, recv_sems, axes[w], sizes[w], w, k, x, y, c).start()
        token[...] = jnp.zeros_like(token)

    sems = pltpu.SemaphoreType.DMA((4 * n,))
    outs = pl.pallas_call(
        body, name=name, out_shape=landing + [pltpu.HBM(g.shape, g.dtype) for g in grads] + [sems, sems, S((8, LANES), F32)],
        in_specs=[HBM_SPEC] * n + [pl.BlockSpec(memory_space=pl.ANY)] * len(extra),
        out_specs=[HBM_SPEC] * (2 * n) + [SEM_SPEC, SEM_SPEC, pl.BlockSpec(memory_space=pltpu.VMEM)],
        input_output_aliases={w: n + w for w in range(n)},
        compiler_params=pltpu.CompilerParams(collective_id=SIBLING_COLLECTIVE_ID, **SPLIT_COPY_PARAMS))(
        *[_hbm(g) for g in grads], *extra)
    return outs[:n], outs[n:2 * n], outs[2 * n], outs[2 * n + 1], outs[2 * n + 2]


def _to_sibling_wait(grads, landing, send_sems, recv_sems, axes, sizes, after, name):
    n = len(grads)

    def body(*refs):
        g_refs = refs[:n]
        send, recv = refs[2 * n], refs[2 * n + 1]
        landed = refs[3 * n + 3:]
        x, y, c = _place()
        for w in range(n):
            for k in range(4):
                copy = _sibling_copy(g_refs[w], landed[w], send, recv, axes[w], sizes[w], w, k, x, y, c)
                copy.wait_send()
                copy.wait_recv()

    outs = pl.pallas_call(
        body, name=name, out_shape=[pltpu.HBM(v.shape, v.dtype) for v in list(grads) + list(landing)],
        in_specs=[HBM_SPEC] * (2 * n) + [SEM_SPEC, SEM_SPEC, pl.BlockSpec(memory_space=pl.ANY)], out_specs=[HBM_SPEC] * (2 * n),
        input_output_aliases={i: i for i in range(2 * n)},
        compiler_params=pltpu.CompilerParams(**SPLIT_COPY_PARAMS))(*grads, *landing, send_sems, recv_sems, after)
    return outs[:n], outs[n:]


def _rs_copy(p_ref, out_ref, send_sems, recv_sems, w, rel, x, y, c):
    tx, ty = _flip(x, rel & 2), _flip(y, rel & 1)
    return pltpu.make_async_remote_copy(
        src_ref=p_ref.at[2 * tx + ty], dst_ref=out_ref.at[2 * x + y], send_sem=send_sems.at[3 * w + rel - 1],
        recv_sem=recv_sems.at[3 * w + rel - 1], device_id=(tx, ty, c), device_id_type=MESH)


def _rs_start(parts, name):
    n = len(parts)

    def body(*refs):
        p_refs, out_refs = refs[:n], refs[n:2 * n]
        send_sems, recv_sems, token = refs[3 * n:]
        x, y, c = _place()
        for w in range(n):
            for rel in (1, 2, 3):
                _rs_copy(p_refs[w], out_refs[w], send_sems, recv_sems, w, rel, x, y, c).start()
        token[...] = jnp.zeros_like(token)

    sems = pltpu.SemaphoreType.DMA((3 * n,))
    outs = pl.pallas_call(
        body, name=name,
        out_shape=[pltpu.HBM(p.shape, p.dtype) for p in parts] * 2 + [sems, sems, S((8, LANES), F32)],
        in_specs=[HBM_SPEC] * n, out_specs=[HBM_SPEC] * (2 * n) + [SEM_SPEC, SEM_SPEC, pl.BlockSpec(memory_space=pltpu.VMEM)],
        input_output_aliases={w: n + w for w in range(n)},
        compiler_params=pltpu.CompilerParams(**SPLIT_COPY_PARAMS))(*[_hbm(p) for p in parts])
    return outs[:n], outs[n:2 * n], outs[2 * n], outs[2 * n + 1], outs[2 * n + 2]


def _rs_wait(parts, landing, send_sems, recv_sems, after, name):
    n = len(parts)

    def body(*refs):
        p_refs = refs[:n]
        send, recv = refs[2 * n], refs[2 * n + 1]
        landed = refs[3 * n + 3:]
        x, y, c = _place()
        for w in range(n):
            for rel in (1, 2, 3):
                copy = _rs_copy(p_refs[w], landed[w], send, recv, w, rel, x, y, c)
                copy.wait_send()
                copy.wait_recv()

    outs = pl.pallas_call(
        body, name=name, out_shape=[pltpu.HBM(v.shape, v.dtype) for v in list(parts) + list(landing)],
        in_specs=[HBM_SPEC] * (2 * n) + [SEM_SPEC, SEM_SPEC, pl.BlockSpec(memory_space=pl.ANY)], out_specs=[HBM_SPEC] * (2 * n),
        input_output_aliases={i: i for i in range(2 * n)},
        compiler_params=pltpu.CompilerParams(**SPLIT_COPY_PARAMS))(*parts, *landing, send_sems, recv_sems, after)
    return outs[n:]


def _exchange_small(v, reduce, name, after=None):
    R, C = v.shape

    def body(v_ref, *rest):
        out_ref, gath, send_sems, recv_sems = rest[-4:]
        x, y, c = _place()
        me = 4 * x + 2 * y + c
        buf = gath if reduce else out_ref
        buf[me] = v_ref[...]
        copies = []
        for rel in range(1, N_DEV):
            peer = (_flip(x, rel & 4), _flip(y, rel & 2), _flip(c, rel & 1))
            copies.append(pltpu.make_async_remote_copy(
                src_ref=v_ref, dst_ref=buf.at[me], send_sem=send_sems.at[rel - 1], recv_sem=recv_sems.at[rel - 1],
                device_id=peer, device_id_type=MESH))
        for cp in copies:
            cp.start()
        for cp in copies:
            cp.wait()
        if reduce:
            acc = gath[0]
            for d in range(1, N_DEV):
                acc = acc + gath[d]
            out_ref[...] = acc

    vm = pl.BlockSpec(memory_space=pltpu.VMEM)
    extra = [] if after is None else [after]
    return pl.pallas_call(
        body, out_shape=S((R, C) if reduce else (N_DEV, R, C), F32), in_specs=[vm] + [pl.BlockSpec(memory_space=pl.ANY)] * len(extra),
        out_specs=vm, name=name,
        scratch_shapes=[pltpu.VMEM((N_DEV, R, C) if reduce else (8, LANES), F32), pltpu.SemaphoreType.DMA((N_DEV - 1,)),
                        pltpu.SemaphoreType.DMA((N_DEV - 1,))])(v, *extra)


def _pad_rows(flat, cols, mult):
    n = flat.shape[-1]
    rows = -(-n // cols)
    rows = -(-rows // mult) * mult
    pad = [(0, 0)] * (flat.ndim - 1) + [(0, rows * cols - n)]
    return jnp.pad(flat, pad).reshape(flat.shape[:-1] + (rows, cols))


def _round_up(n, m):
    return -(-n // m) * m


def _shard_axes(a):
    return [(2, _round_up(a[n].shape[2], LANES)) if kind == 'col' else (1, _round_up(a[n].shape[1 if kind == 'row' else 2], LANES))
            for n, kind in BIG]


def _pack_small(vals):
    rows = [_pad_rows(vals[n].astype(F32).reshape(-1), SMALL_COLS, 1) for n in SMALL]
    m = jnp.concatenate(rows, axis=0)
    return jnp.pad(m, ((0, -m.shape[0] % 8), (0, 0)))


def _unpack_small(m, a):
    out, r = {}, 0
    for n in SMALL:
        nr = -(-a[n].size // SMALL_COLS)
        out[n] = m[r:r + nr].reshape(-1)[:a[n].size].reshape(a[n].shape)
        r += nr
    return out, r


GROUPS = (('ffn1_w_gate', 'ffn1_w_up', 'ffn1_w_down'), ('w_in', 'w_out', 'xattn_w_q', 'xattn_w_kv', 'xattn_w_o'),
          ('ffn2_w_gate', 'ffn2_w_up', 'ffn2_w_down'))
GATHER_GROUPS = GROUPS


def _layer_small(a, conv_w_full, l):
    H = a['b_f'].shape[1]
    return dict(
        bft=jnp.pad(a['b_f'][l].reshape(H, 1), ((0, 16 - H), (0, 0))),
        cw=jnp.pad(conv_w_full[l], ((0, CONV_PAD - CONV_WIDTH), (0, 0))), cb=a['conv_b'][l].reshape(1, -1),
        lg=a['conv_ln_g'][l].reshape(1, -1), lb=a['conv_ln_b'][l].reshape(1, -1),
        ag=a['attn_out_g'][l].reshape(1, -1), cg=a['conv_out_g'][l].reshape(1, -1),
        g1=a['ffn1_norm_g'][l], gm=a['mix_norm_g'][l], gx=a['xattn_norm_g'][l], gmem=a['mem_norm_g'][l], g2=a['ffn2_norm_g'][l])


def _layer_fwd(x0, mem, w, fetch, cfg, l):
    T = x0.shape[0]
    H = cfg['heads']
    sv = {'x0': x0}
    m = fetch(l, 0, x0)
    w.update(wg1=(m['ffn1_w_gate'], 0), wu1=(m['ffn1_w_up'], 0), wd1=(m['ffn1_w_down'], 0))
    sv['h1'] = _rms_fwd(x0, w['g1'], f"l{l}_ffn1_norm")
    sv['G1'], sv['U1'], sv['A1'] = _ffn_up(sv['h1'], w['wg1'], w['wu1'], f"l{l}_ffn1_up")
    x1 = sv['x1'] = _mm_res(sv['A1'], w['wd1'], x0, 0.5, f"l{l}_ffn1_down")
    m = fetch(l, 1, x1)
    wqkv, wf, wag = _win_split(m['w_in'], cfg['pieces'], cfg['widths'], f"l{l}_w_in_split")
    w.update(wqkv=(wqkv, 0), wft=wf[0, :, :16].T, wag=(wag, 0), wout=(m['w_out'], 0), wq=(m['xattn_w_q'], 0),
             wkv=(m['xattn_w_kv'], 0), wo=(m['xattn_w_o'], 0))
    h2 = sv['h2'] = _rms_fwd(x1, w['gm'], f"l{l}_mix_norm")
    sv['qkv'] = _mm(h2, w['wqkv'], BF16, f"l{l}_qkv_proj")
    sv['agv'] = _mm(h2, w['wag'], F32, f"l{l}_glu_proj")
    ct, sv['sg'] = _fox_prep(h2, w['wft'], w['bft'], f"l{l}_fox_prep")
    sv['c_col'] = ct[:H].reshape(H, T, 1)
    sv['c_row'] = ct[:H].reshape(H // 2, 2, T)
    sv['attn'], sv['lse'] = _fox_fwd(sv['qkv'], sv['c_col'], sv['c_row'], f"l{l}_fox_fwd")
    sv['yc'] = _conv_fwd(sv['agv'], w['cw'], w['cb'], f"l{l}_conv_fwd")
    sv['ycat'] = _mix_post(sv['attn'], sv['yc'], w['ag'], w['cg'], w['lg'], w['lb'], f"l{l}_mix_post")
    x2 = sv['x2'] = _mm_res(sv['ycat'], w['wout'], x1, 1.0, f"l{l}_out_proj")
    sv['h3'] = _rms_fwd(x2, w['gx'], f"l{l}_xattn_norm")
    sv['memn'] = _rms_fwd(mem, w['gmem'], f"l{l}_mem_norm")
    sv['q'] = _mm(sv['h3'], w['wq'], BF16, f"l{l}_xattn_q")
    sv['kv'] = _mm_nt(sv['memn'], w['wkv'], BF16, f"l{l}_xattn_kv")
    sv['o'] = _xattn_fwd(sv['q'], sv['kv'], f"l{l}_xattn_fwd")
    x3 = sv['x3'] = _mm_res(sv['o'], w['wo'], x2, 1.0, f"l{l}_xattn_out")
    m = fetch(l, 2, x3)
    w.update(wg2=(m['ffn2_w_gate'], 0), wu2=(m['ffn2_w_up'], 0), wd2=(m['ffn2_w_down'], 0))
    sv['h4'] = _rms_fwd(x3, w['g2'], f"l{l}_ffn2_norm")
    sv['G2'], sv['U2'], sv['A2'] = _ffn_up(sv['h4'], w['wg2'], w['wu2'], f"l{l}_ffn2_up")
    return _mm_res(sv['A2'], w['wd2'], x3, 0.5, f"l{l}_ffn2_down"), sv


def _ffn_bwd(dout, x_in, h, G, U, A, wg, wu, wd, g, tag, put, which, dep, flush, settle):
    dG, dU = _ffn_bwd_act(dout, wd, G, U, 0.5, tag + "_bwd_act", dep)
    settle(dG)
    put(which + '_w_down', A, dout, 0.5, tag + "_dwd")
    put(which + '_w_gate', dG, h, 1.0, tag + "_dwg")
    put(which + '_w_up', dU, h, 1.0, tag + "_dwu")
    dep = flush()
    dx, dg = _bwd_h([(dG, wg, 'nn'), (dU, wu, 'nn')], x_in, g, dout, tag + "_bwd_h", dep)
    token = settle(dx)
    return dx, dg, dep if token is None else token


def _layer_bwd(dx4, mem, w, sv, reduce, settle, cfg, l, dep):
    small, grads = {}, {}
    T = dx4.shape[0]
    H = cfg['heads']
    tokens = []

    def put(key, act, dy, scale, name):
        grads[key] = _wgrad(act, dy, scale, name, (None, 0, 1))

    def put_and_reduce(key, act, dy, scale, name):
        put(key, act, dy, scale, name)
        tokens.append(reduce(l, (key,), {key: grads.pop(key)}))

    dx3, small['ffn2_norm_g'], dep = _ffn_bwd(
        dx4, sv['x3'], sv['h4'], sv['G2'], sv['U2'], sv['A2'], w['wg2'], w['wu2'], w['wd2'], w['g2'], f"l{l}_ffn2", put, 'ffn2',
        dep, lambda: reduce(l, GROUPS[2], {n: grads.pop(n) for n in GROUPS[2]}), settle)
    do = _mm_nt(dx3, w['wo'], BF16, f"l{l}_xattn_do", dep)
    put('xattn_w_o', sv['o'], dx3, 1.0, f"l{l}_dwo")
    dq, dkv = _xattn_bwd(sv['q'], sv['kv'], do, f"l{l}_xattn_bwd")
    put('xattn_w_q', sv['h3'], dq, 1.0, f"l{l}_dwq")
    dx2, small['xattn_norm_g'] = _bwd_h([(dq, w['wq'], 'nt')], sv['x2'], w['gx'], dx3, f"l{l}_xattn_bwd_h")
    dmemn = _mm(dkv, w['wkv'], F32, f"l{l}_dmemn")
    put('xattn_w_kv', dkv, sv['memn'], 1.0, f"l{l}_dwkv")
    small['mem_norm_g'] = _rms_gain_grad(dmemn, mem, w['gmem'], f"l{l}_dgmem")
    dycat = _mm_nt(dx2, w['wout'], F32, f"l{l}_dycat")
    put('w_out', sv['ycat'], dx2, 1.0, f"l{l}_dwout")
    dattn, dyc, small['attn_out_g'], small['conv_out_g'], small['conv_ln_g'], small['conv_ln_b'] = _mix_post_bwd(
        dycat, sv['attn'], sv['yc'], w['ag'], w['cg'], w['lg'], w['lb'], f"l{l}_mix_post_bwd")
    dva, dga, dcw, small['conv_b'] = _conv_bwd(dyc, sv['agv'], w['cw'], f"l{l}_conv_bwd")
    dq_, dk_, dv_, dcs = _fox_bwd(sv['qkv'], sv['c_col'], sv['c_row'], sv['lse'], dattn, f"l{l}_fox_bwd")
    dcs16 = jnp.pad(dcs.reshape(H, T), ((0, 16 - H), (0, 0)))
    dflt, dwft, dbf = _fox_prep_bwd(dcs16, sv['sg'], sv['h2'], f"l{l}_fox_prep_bwd")
    small['b_f'] = dbf[:H].reshape(H)
    dqkv = jnp.concatenate([dq_, dk_, dv_], axis=1)
    dag = jnp.concatenate([dva, dga], axis=1)
    put('wqkv', sv['h2'], dqkv, 1.0, f"l{l}_dwqkv")
    put('wag', sv['h2'], dag, 1.0, f"l{l}_dwag")
    dx1, small['mix_norm_g'] = _bwd_h([(dqkv, w['wqkv'], 'nt'), (dag, w['wag'], 'nt'), (dflt, w['wft'], 'tn')],
                                      sv['x1'], w['gm'], dx2, f"l{l}_mix_bwd_h")
    dwf = jnp.pad(dwft[:H].T, ((0, 0), (0, LANES - H)))[None].astype(BF16)
    grads['w_in'] = _win_merge((grads.pop('wqkv'), dwf, grads.pop('wag')), cfg['pieces'], cfg['chunked_cols'], f"l{l}_w_in_merge")
    dep = reduce(l, GROUPS[1], {n: grads.pop(n) for n in GROUPS[1]})
    if l == 0:
        dx0, small['ffn1_norm_g'], dep = _ffn_bwd(
            dx1, sv['x0'], sv['h1'], sv['G1'], sv['U1'], sv['A1'], w['wg1'], w['wu1'], w['wd1'], w['g1'], f"l{l}_ffn1",
            put_and_reduce, 'ffn1', dep, lambda: tokens[-1], settle)
    else:
        dx0, small['ffn1_norm_g'], dep = _ffn_bwd(
            dx1, sv['x0'], sv['h1'], sv['G1'], sv['U1'], sv['A1'], w['wg1'], w['wu1'], w['wd1'], w['g1'], f"l{l}_ffn1", put, 'ffn1',
            dep, lambda: reduce(l, GROUPS[0], {n: grads.pop(n) for n in GROUPS[0]}), settle)
    small = {k: v.reshape(-1) for k, v in small.items()}
    return dx0, small, dcw[:CONV_WIDTH], dep


def _local_step(x, mem, tgt, a, conv_w_full, fetch, reduce, settle, cfg):
    L = a['b_f'].shape[0]
    ws = [_layer_small(a, conv_w_full, l) for l in range(L)]
    saved = []
    for l in range(L):
        x, sv = _layer_fwd(x, mem, ws[l], fetch, cfg, l)
        saved.append(sv)
    loss, dx, dgf = _loss_head(x, a['final_norm_g'], tgt, "loss_head")
    smalls, dcws, dep = [None] * L, [None] * L, None
    for l in range(L - 1, -1, -1):
        dx, smalls[l], dcws[l], dep = _layer_bwd(dx, mem, ws[l], saved[l], reduce, settle, cfg, l, dep)
    small = {n: jnp.stack([smalls[l][n] for l in range(L)]) for n in SMALL if n != 'final_norm_g'}
    small['final_norm_g'] = dgf.reshape(-1)
    return loss, dx, small, jnp.stack(dcws)


def kernel(x, mem, ffn1_norm_g, ffn1_w_gate, ffn1_w_up, ffn1_w_down, mix_norm_g, w_in, b_f, conv_w, conv_b, conv_ln_g, conv_ln_b, attn_out_g, conv_out_g, w_out, xattn_norm_g, mem_norm_g, xattn_w_q, xattn_w_kv, xattn_w_o, ffn2_norm_g, ffn2_w_gate, ffn2_w_up, ffn2_w_down, final_norm_g, loss_target, m_ffn1_norm_g, m_ffn1_w_gate, m_ffn1_w_up, m_ffn1_w_down, m_mix_norm_g, m_w_in, m_b_f, m_conv_w, m_conv_b, m_conv_ln_g, m_conv_ln_b, m_attn_out_g, m_conv_out_g, m_w_out, m_xattn_norm_g, m_mem_norm_g, m_xattn_w_q, m_xattn_w_kv, m_xattn_w_o, m_ffn2_norm_g, m_ffn2_w_gate, m_ffn2_w_up, m_ffn2_w_down, m_final_norm_g, v_ffn1_norm_g, v_ffn1_w_gate, v_ffn1_w_up, v_ffn1_w_down, v_mix_norm_g, v_w_in, v_b_f, v_conv_w, v_conv_b, v_conv_ln_g, v_conv_ln_b, v_attn_out_g, v_conv_out_g, v_w_out, v_xattn_norm_g, v_mem_norm_g, v_xattn_w_q, v_xattn_w_kv, v_xattn_w_o, v_ffn2_norm_g, v_ffn2_w_gate, v_ffn2_w_up, v_ffn2_w_down, v_final_norm_g):
    args = (x, mem, ffn1_norm_g, ffn1_w_gate, ffn1_w_up, ffn1_w_down, mix_norm_g, w_in, b_f, conv_w, conv_b, conv_ln_g, conv_ln_b, attn_out_g, conv_out_g, w_out, xattn_norm_g, mem_norm_g, xattn_w_q, xattn_w_kv, xattn_w_o, ffn2_norm_g, ffn2_w_gate, ffn2_w_up, ffn2_w_down, final_norm_g)
    moments_m = (m_ffn1_norm_g, m_ffn1_w_gate, m_ffn1_w_up, m_ffn1_w_down, m_mix_norm_g, m_w_in, m_b_f, m_conv_w, m_conv_b, m_conv_ln_g, m_conv_ln_b, m_attn_out_g, m_conv_out_g, m_w_out, m_xattn_norm_g, m_mem_norm_g, m_xattn_w_q, m_xattn_w_kv, m_xattn_w_o, m_ffn2_norm_g, m_ffn2_w_gate, m_ffn2_w_up, m_ffn2_w_down, m_final_norm_g)
    moments_v = (v_ffn1_norm_g, v_ffn1_w_gate, v_ffn1_w_up, v_ffn1_w_down, v_mix_norm_g, v_w_in, v_b_f, v_conv_w, v_conv_b, v_conv_ln_g, v_conv_ln_b, v_attn_out_g, v_conv_out_g, v_w_out, v_xattn_norm_g, v_mem_norm_g, v_xattn_w_q, v_xattn_w_kv, v_xattn_w_o, v_ffn2_norm_g, v_ffn2_w_gate, v_ffn2_w_up, v_ffn2_w_down, v_final_norm_g)
    a = dict(zip(NAMES, args))
    am = dict(zip(WEIGHTS, moments_m))
    av = dict(zip(WEIGHTS, moments_v))
    L, taps, cshard = conv_w.shape
    dev = 4 * lax.axis_index("x") + 2 * lax.axis_index("y") + lax.axis_index("c")

    big_names = [n for n, _ in BIG]
    geometry = dict(zip(big_names, _shard_axes(a)))
    n_attn, n_heads, n_conv = attn_out_g.shape[1], b_f.shape[1], conv_out_g.shape[1]
    chunk = geometry['w_in'][1]
    cfg = dict(heads=n_heads, pieces=_win_pieces(n_attn, n_heads, n_conv, w_in.shape[2], chunk),
               widths=(3 * n_attn, LANES, 2 * n_conv), chunked_cols=N_DEV * chunk)

    cw_rows = _pad_rows(conv_w.reshape(-1), LANES, 8)
    cw_all = _exchange_small(cw_rows, False, "allgather_conv_w")
    conv_w_full = cw_all.reshape(N_DEV, -1)[:, :conv_w.size].reshape(N_DEV, L, taps, cshard).transpose(1, 2, 0, 3).reshape(
        L, taps, N_DEV * cshard)

    keys = [(l, n) for l in range(L) for names in GATHER_GROUPS for n in names]
    members = [[keys.index((l, n)) for n in names] for l in range(L) for names in GATHER_GROUPS]
    shards = []
    for l, n in keys:
        ax, size = geometry[n]
        shard = _as_handled(n, a[n][l:l + 1]).astype(BF16)
        pad = [(0, 0)] * 3
        pad[ax] = (0, size - shard.shape[ax])
        shards.append(jnp.pad(shard, pad))
    key_axes = [geometry[n][0] for _, n in keys]
    fulls, thru, ag_send, ag_recv = _ag_start(shards, key_axes, members, cw_all, "allgather_start")

    def fetch(l, gi, after):
        g = l * len(GATHER_GROUPS) + gi
        axs = [key_axes[i] for i in members[g]]
        own, landed = _ag_wait([thru[i] for i in members[g]], [fulls[i] for i in members[g]], ag_send[g], ag_recv[g], axs, after,
                               f"allgather_wait_l{l}g{gi}")
        return dict(zip(GATHER_GROUPS[gi], _ag_forward(landed, own, axs, f"allgather_forward_l{l}g{gi}")))

    pending, own_part, landed_part, in_flight = [], {}, {}, []

    def finish_exchange(after):
        if not in_flight:
            return None
        l, names, g_thru, landing, send, recv_sems = in_flight.pop()
        tag = f"l{l}_{names[0]}_{len(names)}"
        axs = [geometry[n][0] for n in names]
        g_done, recv = _to_sibling_wait(g_thru, landing, send, recv_sems, axs, [geometry[n][1] for n in names], after,
                                       "reduce_sibling_wait_" + tag)
        parts = []
        for n, g, r, ax in zip(names, g_done, recv, axs):
            part, own_part[(l, n)] = _pair_add(g, r, ax, f"reduce_pair_add_l{l}_{n}")
            parts.append(part)
        landing, parts_thru, send, recv_sems, token = _rs_start(parts, "reduce_start_" + tag)
        pending.append((l, names, parts_thru, landing, send, recv_sems))
        last_token[0] = token
        return token

    last_token = [None]

    def reduce(l, names, grads):
        gl = [grads[n] for n in names]
        finish_exchange(gl[0])
        token = last_token[0]
        landing, g_thru, send, recv_sems, token = _to_sibling_start(
            gl, [geometry[n][0] for n in names], [geometry[n][1] for n in names], token,
            f"reduce_sibling_start_l{l}_{names[0]}_{len(names)}")
        in_flight.append((l, names, g_thru, landing, send, recv_sems))
        return token

    loss, grad_x, gsmall, dcw = _local_step(x[0], mem[0], loss_target[0], a, conv_w_full, fetch, reduce, finish_exchange, cfg)
    finish_exchange(grad_x)

    def wait_group(entry, after):
        l, names, parts_thru, landing, send, recv_sems = entry
        landed = _rs_wait(parts_thru, landing, send, recv_sems, after, f"reduce_wait_l{l}_{names[0]}_{len(names)}")
        for n, arr in zip(names, landed):
            landed_part[(l, n)] = arr

    for entry in pending[:-1]:
        wait_group(entry, grad_x)

    grads, delta, new_m, new_v = {}, {}, {}, {}

    def update(n):
        outs = _adamw_sum(_as_handled(n, a[n]), _as_handled(n, am[n]), _as_handled(n, av[n]),
                          [own_part[(l, n)] for l in range(L)], [landed_part[(l, n)] for l in range(L)], "adamw_" + n)
        grads[n], delta[n], new_m[n], new_v[n] = (_as_handled(n, o) for o in outs)

    last_names = pending[-1][1]
    early = [n for n in big_names if n not in last_names]
    for n in early:
        update(n)

    small_rows = _pack_small(gsmall)
    n_small = small_rows.shape[0]
    dcw_rows = jnp.pad(dcw, ((0, 0), (0, CONV_PAD - taps), (0, 0))).reshape(-1, SMALL_COLS)
    summed = _exchange_small(jnp.concatenate([small_rows, dcw_rows], axis=0), True, "allreduce_small", after=delta[early[-1]])
    g_small, _ = _unpack_small(summed[:n_small], a)
    dcw_sum = summed[n_small:].reshape(L, CONV_PAD, N_DEV * cshard)[:, :taps]
    grads.update(g_small)
    grads['conv_w'] = lax.dynamic_slice_in_dim(dcw_sum, dev * cshard, cshard, axis=2)
    delta['conv_w'], new_m['conv_w'], new_v['conv_w'] = _adamw(conv_w, am['conv_w'], av['conv_w'], grads['conv_w'], "adamw_conv_w")
    pw, pm, pv, pg = (_pack_small(d) for d in (a, am, av, g_small))
    for dst, packed in zip((delta, new_m, new_v), _adamw(pw, pm, pv, pg, "adamw_small")):
        dst.update(_unpack_small(packed, a)[0])

    wait_group(pending[-1], delta['conv_w'])
    for n in last_names:
        update(n)

    total = lax.psum(loss.reshape(()), ("x", "y", "c"))
    return (total, grad_x[None], *[grads[n] for n in WEIGHTS], *[delta[n] for n in WEIGHTS], *[new_m[n] for n in WEIGHTS],
            *[new_v[n] for n in WEIGHTS])
```
